```python
import jax
import jax.numpy as jnp
from jax import lax
import numpy as np

D_MODEL = 4096
BATCH = 2
SEQ = 4096
DEPTH = 2

CTX_LEN = 256
GRID_W = 64
RMS_EPS = 1e-6

NA_HEADS = 8
NA_DIM = 128
NA_W = NA_HEADS * NA_DIM
NA_WIN_R = 8
NA_WIN_C = 16
MLA_HEADS = 12
MLA_Q_LORA = 1024
MLA_KV_LORA = 512
MLA_NOPE = 128
MLA_ROPE = 64
MLA_QK = MLA_NOPE + MLA_ROPE
MLA_V = 128
MLA_W = MLA_HEADS * MLA_V
ROPE_BASE = 10000.0
Q_BLOCK = 128
GLA_HEADS = 6
GLA_DK = 128
GLA_DV = 256
GLA_KW = GLA_HEADS * GLA_DK
GLA_W = GLA_HEADS * GLA_DV
GLA_GATE_RANK = 16
GLA_TAU = 16.0
GLA_CHUNK = 64
MIX_W = NA_W + MLA_W + GLA_W
IN_SPLITS = (NA_W, NA_W, NA_W, MLA_Q_LORA, MLA_KV_LORA, MLA_ROPE, GLA_KW, GLA_KW, GLA_W, GLA_W, GLA_GATE_RANK, GLA_GATE_RANK)
IN_W = 3 * NA_W + MLA_Q_LORA + MLA_KV_LORA + MLA_ROPE + 2 * GLA_KW + 2 * GLA_W + 2 * GLA_GATE_RANK
MOE_EXPERTS = 16
MOE_GROUPS = 4
MOE_PER_GROUP = MOE_EXPERTS // MOE_GROUPS
MOE_TOP_K = 2
MOE_FF = 1024
MOE_BLOCK = 256

kernel_name = 'hybrid_na_mla_gla_grouped_moe_dit'


def rms_norm(a, g):
    af = a.astype(jnp.float32)
    y = af * lax.rsqrt(jnp.mean(af * af, axis=-1, keepdims=True) + RMS_EPS)
    return (y * g.astype(jnp.float32)).astype(a.dtype)


def modulate(h, shift, scale):
    return h * (1 + scale) + shift


def split_heads(a, n_heads):
    return a.reshape(*a.shape[:-1], n_heads, a.shape[-1] // n_heads)


def merge_heads(a):
    return a.reshape(*a.shape[:-2], a.shape[-2] * a.shape[-1])


def rope_1d(a, pos):
    nf = a.shape[-1] // 2
    inv = ROPE_BASE ** (-jnp.arange(nf, dtype=jnp.float32) / nf)
    ang = pos.astype(jnp.float32)[:, None] * inv[None, :]
    cos = jnp.cos(ang)[None, :, None, :]
    sin = jnp.sin(ang)[None, :, None, :]
    a1 = a[..., :nf].astype(jnp.float32)
    a2 = a[..., nf:].astype(jnp.float32)
    return jnp.concatenate([a1 * cos - a2 * sin, a1 * sin + a2 * cos], axis=-1).astype(a.dtype)


def rope_tail(a, pos_row, pos_col):
    half = MLA_ROPE // 2
    rope = a[..., MLA_NOPE:]
    return jnp.concatenate([a[..., :MLA_NOPE], rope_1d(rope[..., :half], pos_row), rope_1d(rope[..., half:], pos_col)], axis=-1)


def dense_attention(q, k, v, scale):
    s = jnp.einsum('bqhd,bkhd->bhqk', q, k).astype(jnp.float32) * scale
    p = jax.nn.softmax(s, axis=-1).astype(v.dtype)
    return jnp.einsum('bhqk,bkhd->bqhd', p, v)


def blocked_attention(q, k, v, scale):
    B, S, H, dq = q.shape
    qb = jnp.moveaxis(q.reshape(B, S // Q_BLOCK, Q_BLOCK, H, dq), 1, 0)
    ob = lax.map(lambda qq: dense_attention(qq, k, v, scale), qb)
    return jnp.moveaxis(ob, 0, 1).reshape(B, S, H * v.shape[-1])


def neighbourhood_attention(q, k, v, k_ctx, v_ctx, rpb):
    B, S, H, Dh = q.shape
    rows = S // GRID_W
    wr = min(NA_WIN_R, rows)
    n_cb = GRID_W // NA_WIN_C
    kcw = 2 * NA_WIN_C
    qg = q.reshape(B, rows, GRID_W, H, Dh)
    kg = k.reshape(B, rows, GRID_W, H, Dh)
    vg = v.reshape(B, rows, GRID_W, H, Dh)
    qcol = np.arange(GRID_W).reshape(n_cb, NA_WIN_C)
    cb0 = np.clip(np.arange(n_cb) * NA_WIN_C - NA_WIN_C // 2, 0, GRID_W - kcw)
    kcol = cb0[:, None] + np.arange(kcw)
    ws = np.clip(qcol - NA_WIN_C // 2, 0, GRID_W - NA_WIN_C)
    col_valid = (kcol[:, None, :] >= ws[:, :, None]) & (kcol[:, None, :] < ws[:, :, None] + NA_WIN_C)
    dcol = np.clip(kcol[:, None, :] - qcol[:, :, None], -(NA_WIN_C - 1), NA_WIN_C - 1) + NA_WIN_C - 1
    rpb_c = rpb[:, :, dcol]
    scale = Dh ** -0.5
    n_loc = wr * kcw

    def row_block(r):
        r0 = jnp.clip(r - wr // 2, 0, rows - wr)
        k_r = lax.dynamic_slice_in_dim(kg, r0, wr, axis=1)[:, :, kcol]
        v_r = lax.dynamic_slice_in_dim(vg, r0, wr, axis=1)[:, :, kcol]
        q_r = lax.dynamic_index_in_dim(qg, r, axis=1, keepdims=False).reshape(B, n_cb, NA_WIN_C, H, Dh)
        s_loc = jnp.einsum('bjqhd,bwjkhd->bhjqwk', q_r, k_r).astype(jnp.float32) * scale
        dr = r0 + jnp.arange(wr) - r + (NA_WIN_R - 1)
        bias = jnp.take(rpb_c, dr, axis=1).transpose(0, 2, 3, 1, 4).astype(jnp.float32)
        s_loc = jnp.where(col_valid[:, :, None, :], s_loc + bias, -jnp.inf)
        s_ctx = jnp.einsum('bjqhd,blhd->bhjql', q_r, k_ctx).astype(jnp.float32) * scale
        s = jnp.concatenate([s_loc.reshape(B, H, n_cb, NA_WIN_C, n_loc), s_ctx], axis=-1)
        p = jax.nn.softmax(s, axis=-1).astype(v.dtype)
        p_loc = p[..., :n_loc].reshape(B, H, n_cb, NA_WIN_C, wr, kcw)
        o = jnp.einsum('bhjqwk,bwjkhd->bjqhd', p_loc, v_r) + jnp.einsum('bhjql,blhd->bjqhd', p[..., n_loc:], v_ctx)
        return o.reshape(B, GRID_W, H, Dh)

    out = lax.map(row_block, jnp.arange(rows, dtype=jnp.int32))
    return jnp.moveaxis(out, 0, 1).reshape(B, S, H * Dh)


def mla_q(cq, qa_g, w_uq, qn_g):
    q = split_heads(rms_norm(cq, qa_g) @ w_uq, MLA_HEADS)
    return rms_norm(q, qn_g)


def mla_kv(ckv, k_rope, kva_g, w_ukv, kn_g):
    kv = split_heads(rms_norm(ckv, kva_g) @ w_ukv, MLA_HEADS)
    kr = jnp.broadcast_to(k_rope[:, :, None, :], k_rope.shape[:2] + (MLA_HEADS, MLA_ROPE))
    k = rms_norm(jnp.concatenate([kv[..., :MLA_NOPE], kr], axis=-1), kn_g)
    return k, kv[..., MLA_NOPE:]


def gla_log_decay(a_low, w2, b):
    z = (a_low @ w2 + b).astype(jnp.float32)
    return split_heads(jax.nn.log_sigmoid(z) / GLA_TAU, GLA_HEADS)


def gla_scan(q, k, v, log_a, s0):
    B, T, H, Dk = q.shape
    Dv = v.shape[-1]
    n = T // GLA_CHUNK

    def chunks(a):
        return a.reshape(B, n, GLA_CHUNK, H, a.shape[-1]).transpose(1, 0, 3, 2, 4).astype(jnp.float32)

    causal = jnp.tril(jnp.ones((GLA_CHUNK, GLA_CHUNK), dtype=bool))

    def step(state, inp):
        qc, kc, vc, gc = inp
        b = jnp.cumsum(gc, axis=2)
        rel = jnp.where(causal[:, :, None], b[:, :, :, None, :] - b[:, :, None, :, :], -jnp.inf)
        att = jnp.einsum('bhid,bhjd,bhijd->bhij', qc, kc, jnp.exp(rel))
        out = jnp.einsum('bhid,bhde->bhie', qc * jnp.exp(b), state) + jnp.einsum('bhij,bhje->bhie', att, vc)
        b_end = b[:, :, -1, :]
        state = jnp.exp(b_end)[..., None] * state + jnp.einsum('bhcd,bhce->bhde', kc * jnp.exp(b_end[:, :, None, :] - b), vc)
        return state, out

    s_fin, o = lax.scan(step, s0, (chunks(q), chunks(k), chunks(v), chunks(log_a)))
    o = o.transpose(1, 0, 3, 2, 4).reshape(B, T, H, Dv)
    return o.astype(v.dtype), s_fin


def gla_bidirectional(q, k, v, a_f, a_b, q_ctx, k_ctx, v_ctx, a_f_ctx, a_b_ctx, w_f, b_f, w_b, b_b):
    def prep(q_, k_, v_):
        return split_heads(q_, GLA_HEADS) * GLA_DK ** -0.5, split_heads(k_, GLA_HEADS), split_heads(v_, GLA_HEADS)

    ql, kl, vl = prep(q, k, v)
    qc, kc, vc = prep(q_ctx, k_ctx, v_ctx)
    s0 = jnp.zeros((q.shape[0], GLA_HEADS, GLA_DK, GLA_DV), jnp.float32)
    flip = lambda a: jnp.flip(a, axis=1)
    o_cf, s_cf = gla_scan(qc, kc, vc, gla_log_decay(a_f_ctx, w_f, b_f), s0)
    o_lf, _ = gla_scan(ql, kl, vl, gla_log_decay(a_f, w_f, b_f), s_cf)
    o_cb, s_cb = gla_scan(flip(qc), flip(kc), flip(vc), flip(gla_log_decay(a_b_ctx, w_b, b_b)), s0)
    o_lb, _ = gla_scan(flip(ql), flip(kl), flip(vl), flip(gla_log_decay(a_b, w_b, b_b)), s_cb)
    return o_lf + flip(o_lb), o_cf + flip(o_cb)


def gla_output(o, gate, on_g):
    B, T = o.shape[:2]
    o = rms_norm(o, on_g) * jax.nn.silu(split_heads(gate, GLA_HEADS))
    return o.reshape(B, T, GLA_W)


def token_mixing(hl, hc, w_in, w_out, na_qn, na_kn, na_rpb, mla_qa_g, mla_kva_g, mla_w_uq, mla_w_ukv,
                 mla_qn, mla_kn, gla_wf, gla_bf, gla_wb, gla_bb, gla_on, pos_row, pos_col, need_ctx):
    split_at = np.cumsum(IN_SPLITS)[:-1].tolist()
    (na_q, na_k, na_v, mla_cq, mla_ckv, mla_kr, gla_q, gla_k, gla_v, gla_g, gla_af, gla_ab) = jnp.split(hl @ w_in, split_at, axis=-1)
    (na_q_ctx, na_k_ctx, na_v_ctx, mla_cq_ctx, mla_ckv_ctx, mla_kr_ctx, gla_q_ctx, gla_k_ctx, gla_v_ctx,
     gla_g_ctx, gla_af_ctx, gla_ab_ctx) = jnp.split(hc @ w_in, split_at, axis=-1)
    ka_ctx = rms_norm(split_heads(na_k_ctx, NA_HEADS), na_kn)
    va_ctx = split_heads(na_v_ctx, NA_HEADS)
    oa = neighbourhood_attention(rms_norm(split_heads(na_q, NA_HEADS), na_qn),
                                 rms_norm(split_heads(na_k, NA_HEADS), na_kn),
                                 split_heads(na_v, NA_HEADS), ka_ctx, va_ctx, na_rpb)
    kb, vb = mla_kv(mla_ckv, mla_kr, mla_kva_g, mla_w_ukv, mla_kn)
    kb = rope_tail(kb, pos_row, pos_col)
    qb = rope_tail(mla_q(mla_cq, mla_qa_g, mla_w_uq, mla_qn), pos_row, pos_col)
    kb_ctx, vb_ctx = mla_kv(mla_ckv_ctx, mla_kr_ctx, mla_kva_g, mla_w_ukv, mla_kn)
    ob = blocked_attention(qb, jnp.concatenate([kb_ctx, kb], axis=1), jnp.concatenate([vb_ctx, vb], axis=1), MLA_QK ** -0.5)
    oc, oc_ctx = gla_bidirectional(gla_q, gla_k, gla_v, gla_af, gla_ab, gla_q_ctx, gla_k_ctx, gla_v_ctx,
                                   gla_af_ctx, gla_ab_ctx, gla_wf, gla_bf, gla_wb, gla_bb)
    yl = jnp.concatenate([oa, ob, gla_output(oc, gla_g, gla_on)], axis=-1) @ w_out
    if not need_ctx:
        return yl, None
    qa_ctx = rms_norm(split_heads(na_q_ctx, NA_HEADS), na_qn)
    oa_ctx = merge_heads(dense_attention(qa_ctx, ka_ctx, va_ctx, NA_DIM ** -0.5))
    qb_ctx = mla_q(mla_cq_ctx, mla_qa_g, mla_w_uq, mla_qn)
    ob_ctx = merge_heads(dense_attention(qb_ctx, kb_ctx, vb_ctx, MLA_QK ** -0.5))
    yc = jnp.concatenate([oa_ctx, ob_ctx, gla_output(oc_ctx, gla_g_ctx, gla_on)], axis=-1) @ w_out
    return yl, yc


def group_limited_route(h, w_router, router_bias):
    N = h.shape[0]
    scores = jax.nn.sigmoid((h @ w_router).astype(jnp.float32))
    grouped = (scores + router_bias.astype(jnp.float32)).reshape(N, MOE_GROUPS, MOE_PER_GROUP)
    group_score = jnp.sum(lax.top_k(grouped, 2)[0], axis=-1)
    grp = jnp.argmax(group_score, axis=-1).astype(jnp.int32)
    in_group = jnp.take_along_axis(grouped, grp[:, None, None], axis=1)[:, 0]
    _, local = lax.top_k(in_group, MOE_TOP_K)
    expert_idx = grp[:, None] * MOE_PER_GROUP + local.astype(jnp.int32)
    w = jnp.take_along_axis(scores, expert_idx, axis=1)
    w = w / jnp.sum(w, axis=-1, keepdims=True)
    return expert_idx, w.astype(h.dtype)


def moe_ffn(h, w_router, router_bias, w1, w3, w2):
    N, D = h.shape
    E = w1.shape[0]
    expert_idx, expert_w = group_limited_route(h, w_router, router_bias)
    NK = N * MOE_TOP_K
    flat_e = expert_idx.reshape(NK)
    order = jnp.argsort(flat_e)
    sorted_e = flat_e[order]
    counts = jnp.bincount(flat_e, length=E).astype(jnp.int32)
    padded = (counts + MOE_BLOCK - 1) // MOE_BLOCK * MOE_BLOCK
    pad_end = jnp.cumsum(padded)
    pad_start = pad_end - padded
    seg_start = jnp.cumsum(counts) - counts
    dest = pad_start[sorted_e] + jnp.arange(NK, dtype=jnp.int32) - seg_start[sorted_e]
    n_blocks = (NK + E * (MOE_BLOCK - 1)) // MOE_BLOCK
    P = n_blocks * MOE_BLOCK
    row_token = jnp.full((P,), N, jnp.int32).at[dest].set((order // MOE_TOP_K).astype(jnp.int32))
    row_gate = jnp.zeros((P,), h.dtype).at[dest].set(expert_w.reshape(NK)[order])
    block_expert = jnp.minimum(jnp.searchsorted(pad_end, jnp.arange(n_blocks, dtype=jnp.int32) * MOE_BLOCK, side='right'), E - 1)
    h_pad = jnp.concatenate([h, jnp.zeros((1, D), h.dtype)], axis=0)
    xb = h_pad[row_token].reshape(n_blocks, MOE_BLOCK, D)

    def expert_block(args):
        xblk, e = args
        return (jax.nn.silu(xblk @ w1[e]) * (xblk @ w3[e])) @ w2[e]

    yb = lax.map(expert_block, (xb, block_expert)).reshape(P, D)
    return jax.ops.segment_sum(yb * row_gate[:, None], row_token, num_segments=N + 1)[:N]


def setup_inputs(seed: int = 0) -> dict:
    key = jax.random.key(seed)
    ks = iter(jax.random.split(key, 32))
    f32 = jnp.float32
    nrm = lambda shape, s: jax.random.normal(next(ks), shape, f32) * s
    gain = lambda shape: 1.0 + nrm(shape, 0.02)
    D = D_MODEL
    return {
        'x': nrm((BATCH, SEQ, D), 1.0),
        'c': nrm((BATCH, D), 1.0),
        'ctx': nrm((BATCH, CTX_LEN, D), 1.0),
        'c_ctx': nrm((D,), 1.0),
        'w_ada': nrm((DEPTH, D, 6 * D), 0.5 * D ** -0.5),
        'b_ada': nrm((DEPTH, 6 * D), 0.02),
        'norm1': gain((DEPTH, D)),
        'norm2': gain((DEPTH, D)),
        'w_in': nrm((DEPTH, D, IN_W), D ** -0.5),
        'w_out': nrm((DEPTH, MIX_W, D), MIX_W ** -0.5),
        'na_q_norm': gain((DEPTH, NA_DIM)),
        'na_k_norm': gain((DEPTH, NA_DIM)),
        'na_rpb': nrm((DEPTH, NA_HEADS, 2 * NA_WIN_R - 1, 2 * NA_WIN_C - 1), 0.2),
        'mla_qa_norm': gain((DEPTH, MLA_Q_LORA)),
        'mla_kva_norm': gain((DEPTH, MLA_KV_LORA)),
        'mla_w_uq': nrm((DEPTH, MLA_Q_LORA, MLA_HEADS * MLA_QK), MLA_Q_LORA ** -0.5),
        'mla_w_ukv': nrm((DEPTH, MLA_KV_LORA, MLA_HEADS * (MLA_NOPE + MLA_V)), MLA_KV_LORA ** -0.5),
        'mla_q_norm': gain((DEPTH, MLA_QK)),
        'mla_k_norm': gain((DEPTH, MLA_QK)),
        'gla_w_gate_f': nrm((DEPTH, GLA_GATE_RANK, GLA_KW), GLA_GATE_RANK ** -0.5),
        'gla_b_gate_f': nrm((DEPTH, GLA_KW), 0.1),
        'gla_w_gate_b': nrm((DEPTH, GLA_GATE_RANK, GLA_KW), GLA_GATE_RANK ** -0.5),
        'gla_b_gate_b': nrm((DEPTH, GLA_KW), 0.1),
        'gla_out_norm': gain((DEPTH, GLA_DV)),
        'w_router': nrm((D, MOE_EXPERTS), D ** -0.5),
        'router_bias': nrm((MOE_EXPERTS,), 0.01),
        'moe_w1': nrm((DEPTH, MOE_EXPERTS, D, MOE_FF), D ** -0.5),
        'moe_w3': nrm((DEPTH, MOE_EXPERTS, D, MOE_FF), D ** -0.5),
        'moe_w2': nrm((DEPTH, MOE_EXPERTS, MOE_FF, D), MOE_FF ** -0.5),
    }


def reference(x, c, ctx, c_ctx, w_ada, b_ada, norm1, norm2, w_in, w_out,
              na_q_norm, na_k_norm, na_rpb, mla_qa_norm, mla_kva_norm, mla_w_uq, mla_w_ukv,
              mla_q_norm, mla_k_norm, gla_w_gate_f, gla_b_gate_f, gla_w_gate_b, gla_b_gate_b,
              gla_out_norm, w_router, router_bias, moe_w1, moe_w3, moe_w2):
    B, S, D = x.shape
    L = ctx.shape[1]
    t = jnp.arange(S, dtype=jnp.int32)
    pos_row, pos_col = t // GRID_W, t % GRID_W
    silu_c = jax.nn.silu(c)
    silu_cc = jax.nn.silu(c_ctx)
    xl, xc = x, ctx
    for l in range(DEPTH):
        need_ctx = l < DEPTH - 1
        mod_l = silu_c @ w_ada[l] + b_ada[l]
        mod_c = silu_cc @ w_ada[l] + b_ada[l]
        sh1, sc1, g1, sh2, sc2, g2 = jnp.split(mod_l[:, None, :], 6, axis=-1)
        csh1, csc1, cg1, csh2, csc2, cg2 = jnp.split(mod_c, 6)
        hl = modulate(rms_norm(xl, norm1[l]), sh1, sc1)
        hc = modulate(rms_norm(xc, norm1[l]), csh1, csc1)
        yl, yc = token_mixing(hl, hc, w_in[l], w_out[l], na_q_norm[l], na_k_norm[l], na_rpb[l],
                              mla_qa_norm[l], mla_kva_norm[l], mla_w_uq[l], mla_w_ukv[l],
                              mla_q_norm[l], mla_k_norm[l], gla_w_gate_f[l], gla_b_gate_f[l],
                              gla_w_gate_b[l], gla_b_gate_b[l], gla_out_norm[l], pos_row, pos_col, need_ctx)
        xl = xl + g1 * yl
        hl = modulate(rms_norm(xl, norm2[l]), sh2, sc2).reshape(B * S, D)
        if need_ctx:
            xc = xc + cg1 * yc
            hc = modulate(rms_norm(xc, norm2[l]), csh2, csc2).reshape(B * L, D)
            h = jnp.concatenate([hl, hc], axis=0)
        else:
            h = hl
        y = moe_ffn(h, w_router, router_bias, moe_w1[l], moe_w3[l], moe_w2[l])
        xl = xl + g2 * y[:B * S].reshape(B, S, D)
        if need_ctx:
            xc = xc + cg2 * y[B * S:].reshape(B, L, D)
    return xl
```

```python
import functools

import numpy as np
import jax
import jax.numpy as jnp
from jax import lax
from jax.experimental import pallas as pl
from jax.experimental.pallas import tpu as pltpu

F32 = jnp.float32
BF16 = jnp.bfloat16

D = 4096
NB = 2
S = 4096
L = 256
DEPTH = 2
GW = 64
EPS = 1e-6
NL = NB * S
NT = NL + NB * L
NA_H, NA_D = 8, 128
NA_W = NA_H * NA_D
NA_R, NA_C = 8, 16
MLA_H = 12
MLA_QL, MLA_KVL = 1024, 512
MLA_NOPE, MLA_ROPE, MLA_V = 128, 64, 128
MLA_QK = MLA_NOPE + MLA_ROPE
MLA_W = MLA_H * MLA_V
GLA_H, GLA_DK, GLA_DV = 6, 128, 256
GLA_KW = GLA_H * GLA_DK
GLA_W = GLA_H * GLA_DV
GLA_RANK = 16
GLA_TAU = 16.0
CH = 64
E = 16
E_GROUPS = 4
E_PER = E // E_GROUPS
TOPK = 2
FF = 1024
ROPE_BASE = 10000.0

_COL_NAQ, _COL_NAK, _COL_NAV = 0, 1024, 2048
_COL_CQ, _COL_CKV = 3072, 4096
_COL_GQ, _COL_GK, _COL_GV, _COL_GG = 4608, 5376, 6144, 7680
_COL_KR = 9216
_COL_GA = 9472
PW = 9728

BLK = 256
NBLK_L = S // BLK
CTX_BLK0 = NL // BLK
MOE_R = 512
MOE_TF = 256
NK = NT * TOPK
MOE_NB = (NK + E * (MOE_R - 1)) // MOE_R
MOE_P = MOE_NB * MOE_R

NEG = -1e30
VMEM_LIMIT = 56 * 1024 * 1024


def _cp(sem, vmem=VMEM_LIMIT):
    return pltpu.CompilerParams(dimension_semantics=sem, vmem_limit_bytes=vmem)


def _dot(a, b):
    return jnp.dot(a, b, preferred_element_type=F32)


def _dot_nt(a, b):
    return lax.dot_general(a, b, (((1,), (1,)), ((), ())), preferred_element_type=F32)


def _dot_tn(a, b):
    return lax.dot_general(a, b, (((0,), (0,)), ((), ())), preferred_element_type=F32)


def _sigmoid(x):
    return 1.0 / (1.0 + jnp.exp(-x))


def _row_group(i, tm):
    r0 = i * tm
    return jnp.where(r0 >= NL, 2, r0 // S)


def _ada_kernel(s_ref, w_ref, b_ref, o_ref):
    s = s_ref[...]
    s = s * _sigmoid(s)
    o_ref[...] = _dot(s.astype(BF16), w_ref[...].astype(BF16)) + b_ref[...]


def ada_modulation(cond, w_ada, b_ada):
    tn = 512
    n = 6 * D
    return pl.pallas_call(
        _ada_kernel,
        grid=(DEPTH, n // tn),
        in_specs=[
            pl.BlockSpec((8, D), lambda l, j: (0, 0)),
            pl.BlockSpec((None, D, tn), lambda l, j: (l, 0, j)),
            pl.BlockSpec((None, 1, tn), lambda l, j: (l, 0, j)),
        ],
        out_specs=pl.BlockSpec((None, 8, tn), lambda l, j: (l, 0, j)),
        out_shape=jax.ShapeDtypeStruct((DEPTH, 8, n), F32),
        compiler_params=_cp(("arbitrary", "arbitrary")),
        name="ada_modulation",
    )(cond, w_ada, b_ada.reshape(DEPTH, 1, n))


def _normmod(x, g, sh, sc):
    y = x * lax.rsqrt(jnp.mean(x * x, axis=-1, keepdims=True) + EPS) * g
    return y * (1.0 + sc) + sh


def _normmod_kernel(x_ref, g_ref, sh_ref, sc_ref, h_ref):
    h_ref[...] = _normmod(x_ref[...], g_ref[...], sh_ref[...], sc_ref[...]).astype(h_ref.dtype)


def _normmod_router_kernel(x_ref, g_ref, sh_ref, sc_ref, wr_ref, h_ref, lg_ref):
    h = _normmod(x_ref[...], g_ref[...], sh_ref[...], sc_ref[...])
    h_ref[...] = h.astype(h_ref.dtype)
    lg_ref[...] = jnp.dot(h, wr_ref[...], preferred_element_type=F32,
                          precision=lax.Precision.HIGHEST)


def norm_modulate(xs, gain, mod, k_shift, k_scale, out_dtype, w_router=None):
    tm = 512
    in_specs = [
        pl.BlockSpec((tm, D), lambda i: (i, 0)),
        pl.BlockSpec((1, D), lambda i: (0, 0)),
        pl.BlockSpec((None, 1, D), lambda i: (_row_group(i, tm), 0, k_shift)),
        pl.BlockSpec((None, 1, D), lambda i: (_row_group(i, tm), 0, k_scale)),
    ]
    h_spec = pl.BlockSpec((tm, D), lambda i: (i, 0))
    h_shape = jax.ShapeDtypeStruct((NT, D), out_dtype)
    if w_router is None:
        return pl.pallas_call(
            _normmod_kernel, grid=(NT // tm,), in_specs=in_specs, out_specs=h_spec,
            out_shape=h_shape, compiler_params=_cp(("arbitrary",)), name="norm_modulate",
        )(xs, gain, mod, mod)
    return pl.pallas_call(
        _normmod_router_kernel, grid=(NT // tm,),
        in_specs=in_specs + [pl.BlockSpec((D, 128), lambda i: (0, 0))],
        out_specs=[h_spec, pl.BlockSpec((tm, 128), lambda i: (i, 0))],
        out_shape=[h_shape, jax.ShapeDtypeStruct((NT, 128), F32)],
        compiler_params=_cp(("arbitrary",)), name="norm_modulate_router",
    )(xs, gain, mod, mod, w_router)


def _mm_kernel(a_ref, b_ref, o_ref):
    o_ref[...] = _dot(a_ref[...], b_ref[...]).astype(o_ref.dtype)


def matmul(a, b, tm, tn, out_dtype):
    m, k = a.shape
    n = b.shape[1]
    return pl.pallas_call(
        _mm_kernel,
        grid=(m // tm, n // tn),
        in_specs=[pl.BlockSpec((tm, k), lambda i, j: (i, 0)),
                  pl.BlockSpec((k, tn), lambda i, j: (0, j))],
        out_specs=pl.BlockSpec((tm, tn), lambda i, j: (i, j)),
        out_shape=jax.ShapeDtypeStruct((m, n), out_dtype),
        compiler_params=_cp(("arbitrary", "arbitrary")),
        name="matmul",
    )(a, b)


def _qblk(b, i):
    return jnp.where(i == 0, CTX_BLK0 + b, b * NBLK_L + i - 1)


def _rms_rows(x, g):
    return x * lax.rsqrt(jnp.mean(x * x, axis=-1, keepdims=True) + EPS) * g


NA_KROWS = 12
NA_KW = NA_KROWS * GW


def _na_kernel(q_ref, kl_ref, kc_ref, vl_ref, vc_ref, bias_ref, qn_ref, kn_ref, o_ref, ks_ref):
    i = pl.program_id(2)

    @pl.when(i == 0)
    def _():
        ks_ref[0:L, :] = _rms_rows(kc_ref[...].astype(F32), kn_ref[...]).astype(BF16)
        ks_ref[L:, :] = _rms_rows(kl_ref[...].astype(F32), kn_ref[...]).astype(BF16)

    q = (_rms_rows(q_ref[...].astype(F32), qn_ref[...]) * (NA_D ** -0.5)).astype(BF16)
    s_c = _dot_nt(q, ks_ref[0:L, :])
    m_c = jnp.max(s_c, axis=-1, keepdims=True)

    @pl.when(i == 0)
    def _():
        p = jnp.exp(s_c - m_c)
        l = jnp.sum(p, axis=-1, keepdims=True)
        o_ref[...] = (_dot(p.astype(BF16), vc_ref[...]) / l).astype(o_ref.dtype)

    @pl.when(i > 0)
    def _():
        row0 = jnp.clip(4 * (i - 1) - 4, 0, GW - NA_KROWS)
        k0 = pl.multiple_of(row0 * GW, BLK)
        s_l = _dot_nt(q, ks_ref[pl.ds(L + k0, NA_KW), :]) + bias_ref[...]
        m = jnp.maximum(m_c, jnp.max(s_l, axis=-1, keepdims=True))
        p_c = jnp.exp(s_c - m)
        p_l = jnp.exp(s_l - m)
        l = jnp.sum(p_c, axis=-1, keepdims=True) + jnp.sum(p_l, axis=-1, keepdims=True)
        o = _dot(p_l.astype(BF16), vl_ref[pl.ds(k0, NA_KW), :]) + _dot(p_c.astype(BF16), vc_ref[...])
        o_ref[...] = (o / l).astype(o_ref.dtype)


def _na_bias_index():
    dr = np.zeros((3, BLK, NA_KW), np.int32)
    dc = np.zeros((3, BLK, NA_KW), np.int32)
    ok = np.zeros((3, BLK, NA_KW), bool)
    rows = S // GW
    for p, blk in enumerate((0, 5, NBLK_L - 1)):
        ks = int(np.clip(4 * blk - 4, 0, GW - NA_KROWS))
        r = 4 * blk + np.arange(BLK) // GW
        c = np.arange(BLK) % GW
        rk = ks + np.arange(NA_KW) // GW
        ck = np.arange(NA_KW) % GW
        r0 = np.clip(r - NA_R // 2, 0, rows - NA_R)
        ws = np.clip(c - NA_C // 2, 0, GW - NA_C)
        row_ok = (rk[None, :] >= r0[:, None]) & (rk[None, :] < r0[:, None] + NA_R)
        col_ok = (ck[None, :] >= ws[:, None]) & (ck[None, :] < ws[:, None] + NA_C)
        ok[p] = row_ok & col_ok
        dr[p] = np.clip(rk[None, :] - r[:, None] + NA_R - 1, 0, 2 * NA_R - 2)
        dc[p] = np.clip(ck[None, :] - c[:, None], -(NA_C - 1), NA_C - 1) + NA_C - 1
    return dr, dc, ok


def na_attention(proj, rpb, qn, kn):
    dr, dc, ok = _na_bias_index()
    bias = jnp.where(ok[None], rpb[:, dr, dc].astype(F32), NEG)

    def pat(i):
        return jnp.where(i <= 1, 0, jnp.where(i == NBLK_L, 2, 1))

    cq, ck, cv = _COL_NAQ // NA_D, _COL_NAK // NA_D, _COL_NAV // NA_D
    return pl.pallas_call(
        _na_kernel,
        grid=(NB, NA_H, NBLK_L + 1),
        in_specs=[
            pl.BlockSpec((BLK, NA_D), lambda b, h, i: (_qblk(b, i), cq + h)),
            pl.BlockSpec((S, NA_D), lambda b, h, i: (b, ck + h)),
            pl.BlockSpec((L, NA_D), lambda b, h, i: (CTX_BLK0 + b, ck + h)),
            pl.BlockSpec((S, NA_D), lambda b, h, i: (b, cv + h)),
            pl.BlockSpec((L, NA_D), lambda b, h, i: (CTX_BLK0 + b, cv + h)),
            pl.BlockSpec((None, None, BLK, NA_KW), lambda b, h, i: (h, pat(i), 0, 0)),
            pl.BlockSpec((1, NA_D), lambda b, h, i: (0, 0)),
            pl.BlockSpec((1, NA_D), lambda b, h, i: (0, 0)),
        ],
        out_specs=pl.BlockSpec((BLK, NA_D), lambda b, h, i: (_qblk(b, i), h)),
        out_shape=jax.ShapeDtypeStruct((NT, NA_W), BF16),
        scratch_shapes=[pltpu.VMEM((L + S, NA_D), BF16)],
        compiler_params=_cp(("arbitrary", "arbitrary", "arbitrary")),
        name="na_attention",
    )(proj, proj, proj, proj, proj, bias, qn.reshape(1, NA_D), kn.reshape(1, NA_D))


MLA_HW = 256


def _half_mask(h, width=128):
    lane = lax.broadcasted_iota(jnp.int32, (1, width), 1)
    return (lane < 64) if h % 2 == 0 else (lane >= 64)


def _mla_q_kernel(cq_ref, w_ref, qa_ref, gn_ref, g2_ref, gs2_ref, c2_ref, s2_ref, q_ref):
    x = _rms_rows(cq_ref[...].astype(F32), qa_ref[...]).astype(BF16)
    y = _dot(x, w_ref[...])
    nw = MLA_H * MLA_NOPE
    gc = g2_ref[...] * c2_ref[...]
    gs = gs2_ref[...] * s2_ref[...]
    for p in range(MLA_H // 2):
        r1 = y[:, nw + 128 * p: nw + 128 * (p + 1)]
        r2 = y[:, nw + 768 + 128 * p: nw + 768 + 128 * (p + 1)]
        rot = r1 * gc + r2 * gs
        sq = r1 * r1
        for h in (2 * p, 2 * p + 1):
            msk = _half_mask(h)
            nope = y[:, 128 * h: 128 * (h + 1)]
            ss = jnp.sum(nope * nope, axis=-1, keepdims=True) + jnp.sum(
                jnp.where(msk, sq, 0.0), axis=-1, keepdims=True)
            inv = lax.rsqrt(ss / MLA_QK + EPS) * (MLA_QK ** -0.5)
            q_ref[h, :, 0:128] = (nope * gn_ref[...] * inv).astype(BF16)
            q_ref[h, :, 128:256] = (jnp.where(msk, rot, 0.0) * inv).astype(BF16)


def _mla_kv_kernel(ckv_ref, kr_ref, w_ref, kva_ref, gn_ref, g2_ref, gs2_ref, c2_ref, s2_ref,
                   k_ref, v_ref):
    x = _rms_rows(ckv_ref[...].astype(F32), kva_ref[...]).astype(BF16)
    y = _dot(x, w_ref[...])
    kr = kr_ref[...].astype(F32)
    r1 = kr[:, 0:128]
    r2 = kr[:, 128:256]
    rot = r1 * (g2_ref[...] * c2_ref[...]) + r2 * (gs2_ref[...] * s2_ref[...])
    ss_r = jnp.sum(jnp.where(_half_mask(0), r1 * r1, 0.0), axis=-1, keepdims=True)
    nw = MLA_H * MLA_NOPE
    for h in range(MLA_H):
        nope = y[:, 128 * h: 128 * (h + 1)]
        ss = jnp.sum(nope * nope, axis=-1, keepdims=True) + ss_r
        inv = lax.rsqrt(ss / MLA_QK + EPS)
        k_ref[h, :, 0:128] = (nope * gn_ref[...] * inv).astype(BF16)
        k_ref[h, :, 128:256] = (jnp.where(_half_mask(h), rot, 0.0) * inv).astype(BF16)
        v_ref[h] = y[:, nw + 128 * h: nw + 128 * (h + 1)].astype(BF16)


def _rope_tables():
    t = jnp.arange(S, dtype=jnp.int32)
    pos = (t // GW, t % GW)
    nf = MLA_ROPE // 4
    inv = ROPE_BASE ** (-jnp.arange(nf, dtype=F32) / nf)
    cs, sn = [], []
    for ax in range(2):
        ang = pos[ax].astype(F32)[:, None] * inv[None, :]
        c, s = jnp.cos(ang), jnp.sin(ang)
        cs += [c, c]
        sn += [-s, s]
    c64 = jnp.tile(jnp.concatenate(cs, axis=1), (NB, 1))
    s64 = jnp.tile(jnp.concatenate(sn, axis=1), (NB, 1))
    c64 = jnp.concatenate([c64, jnp.ones((NB * L, MLA_ROPE), F32)], axis=0)
    s64 = jnp.concatenate([s64, jnp.zeros((NB * L, MLA_ROPE), F32)], axis=0)
    return jnp.tile(c64, (1, 2)), jnp.tile(s64, (1, 2))


_ROPE_SWAP = np.concatenate([np.arange(16, 32), np.arange(0, 16), np.arange(48, 64), np.arange(32, 48)])


def _rope_gains(g):
    gr = g[MLA_NOPE:]
    return (g[:MLA_NOPE].reshape(1, 128), jnp.tile(gr, 2).reshape(1, 128),
            jnp.tile(gr[_ROPE_SWAP], 2).reshape(1, 128))


def mla_q_prep(proj, w_uq_r, qa_g, qn_g, c2, s2):
    tm = 512
    gn, g2, gs2 = _rope_gains(qn_g)
    vec = lambda w: pl.BlockSpec((1, w), lambda i: (0, 0))
    return pl.pallas_call(
        _mla_q_kernel,
        grid=(NT // tm,),
        in_specs=[
            pl.BlockSpec((tm, MLA_QL), lambda i: (i, _COL_CQ // MLA_QL)),
            pl.BlockSpec((MLA_QL, 3072), lambda i: (0, 0)),
            vec(MLA_QL), vec(128), vec(128), vec(128),
            pl.BlockSpec((tm, 128), lambda i: (i, 0)),
            pl.BlockSpec((tm, 128), lambda i: (i, 0)),
        ],
        out_specs=pl.BlockSpec((MLA_H, tm, MLA_HW), lambda i: (0, i, 0)),
        out_shape=jax.ShapeDtypeStruct((MLA_H, NT, MLA_HW), BF16),
        compiler_params=_cp(("arbitrary",)),
        name="mla_q_prep",
    )(proj, w_uq_r, qa_g.reshape(1, MLA_QL), gn, g2, gs2, c2, s2)


def mla_kv_prep(proj, w_ukv_r, kva_g, kn_g, c2, s2):
    tm = 512
    gn, g2, gs2 = _rope_gains(kn_g)
    vec = lambda w: pl.BlockSpec((1, w), lambda i: (0, 0))
    return pl.pallas_call(
        _mla_kv_kernel,
        grid=(NT // tm,),
        in_specs=[
            pl.BlockSpec((tm, MLA_KVL), lambda i: (i, _COL_CKV // MLA_KVL)),
            pl.BlockSpec((tm, 256), lambda i: (i, _COL_KR // 256)),
            pl.BlockSpec((MLA_KVL, 3072), lambda i: (0, 0)),
            vec(MLA_KVL), vec(128), vec(128), vec(128),
            pl.BlockSpec((tm, 128), lambda i: (i, 0)),
            pl.BlockSpec((tm, 128), lambda i: (i, 0)),
        ],
        out_specs=[pl.BlockSpec((MLA_H, tm, MLA_HW), lambda i: (0, i, 0)),
                   pl.BlockSpec((MLA_H, tm, MLA_V), lambda i: (0, i, 0))],
        out_shape=[jax.ShapeDtypeStruct((MLA_H, NT, MLA_HW), BF16),
                   jax.ShapeDtypeStruct((MLA_H, NT, MLA_V), BF16)],
        compiler_params=_cp(("arbitrary",)),
        name="mla_kv_prep",
    )(proj, proj, w_ukv_r, kva_g.reshape(1, MLA_KVL), gn, g2, gs2, c2, s2)


def _mla_attn_kernel(q_ref, kl_ref, kc_ref, vl_ref, vc_ref, o_ref):
    i = pl.program_id(2)
    q = q_ref[...]
    s_c = _dot_nt(q, kc_ref[...])
    m_c = jnp.max(s_c, axis=-1, keepdims=True)

    @pl.when(i == 0)
    def _():
        p = jnp.exp(s_c - m_c)
        l = jnp.sum(p, axis=-1, keepdims=True)
        o_ref[...] = (_dot(p.astype(BF16), vc_ref[...]) / l).astype(o_ref.dtype)

    @pl.when(i > 0)
    def _():
        s_l = _dot_nt(q, kl_ref[...])
        m = jnp.maximum(m_c, jnp.max(s_l, axis=-1, keepdims=True))
        p_c = jnp.exp(s_c - m)
        p_l = jnp.exp(s_l - m)
        l = jnp.sum(p_c, axis=-1, keepdims=True) + jnp.sum(p_l, axis=-1, keepdims=True)
        o = _dot(p_l.astype(BF16), vl_ref[...]) + _dot(p_c.astype(BF16), vc_ref[...])
        o_ref[...] = (o / l).astype(o_ref.dtype)


def mla_attention(q, k, v):
    return pl.pallas_call(
        _mla_attn_kernel,
        grid=(NB, MLA_H, NBLK_L + 1),
        in_specs=[
            pl.BlockSpec((None, BLK, MLA_HW), lambda b, h, i: (h, _qblk(b, i), 0)),
            pl.BlockSpec((None, S, MLA_HW), lambda b, h, i: (h, b, 0)),
            pl.BlockSpec((None, L, MLA_HW), lambda b, h, i: (h, CTX_BLK0 + b, 0)),
            pl.BlockSpec((None, S, MLA_V), lambda b, h, i: (h, b, 0)),
            pl.BlockSpec((None, L, MLA_V), lambda b, h, i: (h, CTX_BLK0 + b, 0)),
        ],
        out_specs=pl.BlockSpec((BLK, MLA_V), lambda b, h, i: (_qblk(b, i), h)),
        out_shape=jax.ShapeDtypeStruct((NT, MLA_W), BF16),
        compiler_params=_cp(("arbitrary", "arbitrary", "arbitrary")),
        name="mla_attention",
    )(q, k, k, v, v)


N_CH_C = L // CH
N_CH_L = S // CH
N_CH = N_CH_C + N_CH_L


def _gla_kernel(qf_ref, kf_ref, vf_ref, af_ref, qb_ref, kb_ref, vb_ref, ab_ref,
                wg_ref, bg_ref, of_ref, ob_ref, st_ref):
    t = pl.program_id(1)

    @pl.when(t == 0)
    def _():
        st_ref[...] = jnp.zeros_like(st_ref)

    row = lax.broadcasted_iota(jnp.int32, (CH, CH), 0)
    col = lax.broadcasted_iota(jnp.int32, (CH, CH), 1)
    dirs = (
        (qf_ref, kf_ref, vf_ref, af_ref, of_ref, col <= row, CH - 1),
        (qb_ref, kb_ref, vb_ref, ab_ref, ob_ref, col >= row, 0),
    )
    for d, (q_ref, k_ref, v_ref, a_ref, o_ref, keep, last) in enumerate(dirs):
        z = _dot(a_ref[...], wg_ref[d]) + bg_ref[d]
        g = (jnp.minimum(z, 0.0) - jnp.log(1.0 + jnp.exp(-jnp.abs(z)))) / GLA_TAU
        b = jnp.dot(keep.astype(F32), g, preferred_element_type=F32,
                    precision=lax.Precision.HIGHEST)
        b_end = b[last:last + 1, :]
        q = q_ref[...].astype(F32) * (GLA_DK ** -0.5)
        k = k_ref[...].astype(F32)
        qd = (q * jnp.exp(b)).astype(BF16)
        kd = (k * jnp.exp(-b)).astype(BF16)
        ke = (k * jnp.exp(b_end - b)).astype(BF16)
        e_end = jnp.exp(b_end)
        v = v_ref[...]
        for h in range(GLA_H):
            ksl = slice(GLA_DK * h, GLA_DK * (h + 1))
            vsl = slice(GLA_DV * h, GLA_DV * (h + 1))
            att = jnp.where(keep, _dot_nt(qd[:, ksl], kd[:, ksl]), 0.0)
            st = st_ref[d, h]
            o = _dot_nt(qd[:, ksl], st.astype(BF16)) + _dot(att.astype(BF16), v[:, vsl])
            o_ref[:, vsl] = o
            st_ref[d, h] = st * e_end[:, ksl] + _dot_tn(v[:, vsl], ke[:, ksl])


def gla_scan(proj, wg_f, bg_f, wg_b, bg_b):
    wg = jnp.zeros((2, 128, GLA_KW), F32)
    wg = wg.at[0, 0:GLA_RANK].set(wg_f).at[1, GLA_RANK:2 * GLA_RANK].set(wg_b).astype(BF16)
    bg = jnp.stack([bg_f, bg_b]).reshape(2, 1, GLA_KW)

    ctx0 = NL // CH

    def fwd(b, t):
        return jnp.where(t < N_CH_C, ctx0 + N_CH_C * b + t, N_CH_L * b + t - N_CH_C)

    def bwd(b, t):
        return jnp.where(t < N_CH_C, ctx0 + N_CH_C * b + N_CH_C - 1 - t, N_CH_L * b + N_CH - 1 - t)

    def specs(rowfn):
        return [
            pl.BlockSpec((CH, GLA_KW), lambda b, t: (rowfn(b, t), _COL_GQ // GLA_KW)),
            pl.BlockSpec((CH, GLA_KW), lambda b, t: (rowfn(b, t), _COL_GK // GLA_KW)),
            pl.BlockSpec((CH, GLA_W), lambda b, t: (rowfn(b, t), _COL_GV // GLA_W)),
            pl.BlockSpec((CH, 128), lambda b, t: (rowfn(b, t), _COL_GA // 128)),
        ]

    return pl.pallas_call(
        _gla_kernel,
        grid=(NB, N_CH),
        in_specs=specs(fwd) + specs(bwd) + [
            pl.BlockSpec((2, 128, GLA_KW), lambda b, t: (0, 0, 0)),
            pl.BlockSpec((2, 1, GLA_KW), lambda b, t: (0, 0, 0)),
        ],
        out_specs=[pl.BlockSpec((CH, GLA_W), lambda b, t: (fwd(b, t), 0)),
                   pl.BlockSpec((CH, GLA_W), lambda b, t: (bwd(b, t), 0))],
        out_shape=[jax.ShapeDtypeStruct((NT, GLA_W), F32)] * 2,
        scratch_shapes=[pltpu.VMEM((2, GLA_H, GLA_DV, GLA_DK), F32)],
        compiler_params=_cp(("arbitrary", "arbitrary")),
        name="gla_scan",
    )(proj, proj, proj, proj, proj, proj, proj, proj, wg, bg)


def _gla_out_kernel(of_ref, ob_ref, g_ref, gn_ref, o_ref):
    for h in range(GLA_H):
        sl = slice(GLA_DV * h, GLA_DV * (h + 1))
        o = _rms_rows(of_ref[:, sl] + ob_ref[:, sl], gn_ref[...])
        gate = g_ref[:, sl].astype(F32)
        o_ref[:, sl] = (o * (gate * _sigmoid(gate))).astype(o_ref.dtype)


def gla_output(o_f, o_b, proj, on_g):
    tm = 512
    return pl.pallas_call(
        _gla_out_kernel,
        grid=(NT // tm,),
        in_specs=[
            pl.BlockSpec((tm, GLA_W), lambda i: (i, 0)),
            pl.BlockSpec((tm, GLA_W), lambda i: (i, 0)),
            pl.BlockSpec((tm, GLA_W), lambda i: (i, _COL_GG // GLA_W)),
            pl.BlockSpec((1, GLA_DV), lambda i: (0, 0)),
        ],
        out_specs=pl.BlockSpec((tm, GLA_W), lambda i: (i, 0)),
        out_shape=jax.ShapeDtypeStruct((NT, GLA_W), BF16),
        compiler_params=_cp(("arbitrary",)),
        name="gla_output",
    )(o_f, o_b, proj, on_g.reshape(1, GLA_DV))


def _out_proj_kernel(oa_ref, ob_ref, oc_ref, wa_ref, wb_ref, wc_ref, x_ref, g_ref, o_ref):
    y = _dot(oa_ref[...], wa_ref[...]) + _dot(ob_ref[...], wb_ref[...]) + _dot(oc_ref[...], wc_ref[...])
    o_ref[...] = x_ref[...] + g_ref[...] * y


def out_projection(oa, ob, oc, wa, wb, wc, xs, mod, k_gate):
    tm, tn = 512, 1024
    nj = D // tn
    return pl.pallas_call(
        _out_proj_kernel,
        grid=(nj, NT // tm),
        in_specs=[
            pl.BlockSpec((tm, NA_W), lambda j, i: (i, 0)),
            pl.BlockSpec((tm, MLA_W), lambda j, i: (i, 0)),
            pl.BlockSpec((tm, GLA_W), lambda j, i: (i, 0)),
            pl.BlockSpec((NA_W, tn), lambda j, i: (0, j)),
            pl.BlockSpec((MLA_W, tn), lambda j, i: (0, j)),
            pl.BlockSpec((GLA_W, tn), lambda j, i: (0, j)),
            pl.BlockSpec((tm, tn), lambda j, i: (i, j)),
            pl.BlockSpec((None, 1, tn), lambda j, i: (_row_group(i, tm), 0, k_gate * nj + j)),
        ],
        out_specs=pl.BlockSpec((tm, tn), lambda j, i: (i, j)),
        out_shape=jax.ShapeDtypeStruct((NT, D), F32),
        compiler_params=_cp(("arbitrary", "arbitrary")),
        name="out_projection",
    )(oa, ob, oc, wa, wb, wc, xs, mod)


def _route(logits, router_bias):
    scores = jax.nn.sigmoid(logits)
    grouped = (scores + router_bias.astype(F32)).reshape(NT, E_GROUPS, E_PER)
    group_score = jnp.sum(lax.top_k(grouped, 2)[0], axis=-1)
    grp = jnp.argmax(group_score, axis=-1).astype(jnp.int32)
    in_group = jnp.take_along_axis(grouped, grp[:, None, None], axis=1)[:, 0]
    _, local = lax.top_k(in_group, TOPK)
    expert_idx = grp[:, None] * E_PER + local.astype(jnp.int32)
    w = jnp.take_along_axis(scores, expert_idx, axis=1)
    return expert_idx, w / jnp.sum(w, axis=-1, keepdims=True)


def _dispatch_plan(expert_idx):
    flat_e = expert_idx.reshape(NK)
    onehot = (flat_e[:, None] == jnp.arange(E, dtype=jnp.int32)[None, :]).astype(jnp.int32)
    csum = jnp.cumsum(onehot, axis=0)
    counts = csum[-1]
    rank = jnp.take_along_axis(csum, flat_e[:, None], axis=1)[:, 0] - 1
    padded = (counts + MOE_R - 1) // MOE_R * MOE_R
    pad_end = jnp.cumsum(padded)
    pad_start = pad_end - padded
    dest = (pad_start[flat_e] + rank).astype(jnp.int32)
    row_token = jnp.zeros((MOE_P,), jnp.int32).at[dest].set(jnp.arange(NK, dtype=jnp.int32) // TOPK)
    block_expert = jnp.minimum(
        jnp.searchsorted(pad_end, jnp.arange(MOE_NB, dtype=jnp.int32) * MOE_R, side='right'),
        E - 1).astype(jnp.int32)
    n_active = (pad_end[-1] // MOE_R).astype(jnp.int32).reshape(1)
    return dest, row_token, block_expert, n_active


def _gather_rows_kernel(tok_ref, h_ref, o_ref, buf_ref, sem):
    base = pl.program_id(0) * MOE_R

    def copy(r):
        return pltpu.make_async_copy(h_ref.at[pl.ds(tok_ref[base + r], 1)], buf_ref.at[pl.ds(r, 1)], sem)

    def start(r, c):
        copy(r).start()
        return c

    def wait(r, c):
        copy(r).wait()
        return c

    lax.fori_loop(0, MOE_R, start, 0)
    lax.fori_loop(0, MOE_R, wait, 0)
    o_ref[...] = buf_ref[...].astype(o_ref.dtype)


def gather_rows(row_token, h):
    return pl.pallas_call(
        _gather_rows_kernel,
        grid_spec=pltpu.PrefetchScalarGridSpec(
            num_scalar_prefetch=1,
            grid=(MOE_NB,),
            in_specs=[pl.BlockSpec(memory_space=pl.ANY)],
            out_specs=pl.BlockSpec((MOE_R, D), lambda i, tok: (i, 0)),
            scratch_shapes=[pltpu.VMEM((MOE_R, D), F32), pltpu.SemaphoreType.DMA(())],
        ),
        out_shape=jax.ShapeDtypeStruct((MOE_P, D), BF16),
        compiler_params=_cp(("arbitrary",)),
        name="moe_gather_rows",
    )(row_token, h)


def _expert_ffn_kernel(be_ref, na_ref, x_ref, w1_ref, w3_ref, w2_ref, o_ref):
    i = pl.program_id(0)
    j = pl.program_id(1)

    @pl.when(i < na_ref[0])
    def _():
        x = x_ref[...]
        a = _dot(x, w1_ref[...])
        b = _dot(x, w3_ref[...])
        y = _dot((a * _sigmoid(a) * b).astype(BF16), w2_ref[...])

        @pl.when(j == 0)
        def _():
            o_ref[...] = y

        @pl.when(j > 0)
        def _():
            o_ref[...] += y

    @pl.when(jnp.logical_and(i >= na_ref[0], j == 0))
    def _():
        o_ref[...] = jnp.zeros_like(o_ref)


def expert_ffn(block_expert, n_active, xb, w1, w3, w2):
    return pl.pallas_call(
        _expert_ffn_kernel,
        grid_spec=pltpu.PrefetchScalarGridSpec(
            num_scalar_prefetch=2,
            grid=(MOE_NB, FF // MOE_TF),
            in_specs=[
                pl.BlockSpec((MOE_R, D), lambda i, j, be, na: (i, 0)),
                pl.BlockSpec((None, D, MOE_TF), lambda i, j, be, na: (be[i], 0, j)),
                pl.BlockSpec((None, D, MOE_TF), lambda i, j, be, na: (be[i], 0, j)),
                pl.BlockSpec((None, MOE_TF, D), lambda i, j, be, na: (be[i], j, 0)),
            ],
            out_specs=pl.BlockSpec((MOE_R, D), lambda i, j, be, na: (i, 0)),
        ),
        out_shape=jax.ShapeDtypeStruct((MOE_P, D), F32),
        compiler_params=_cp(("arbitrary", "arbitrary")),
        name="moe_expert_ffn",
    )(block_expert, n_active, xb, w1, w3, w2)


CMB_T = 256


def _combine_kernel(pos_ref, yb_ref, gate_ref, x_ref, g_ref, o_ref, buf_ref, sem):
    base = pl.program_id(0) * CMB_T

    def copy(r, k):
        p = pos_ref[(base + r) * TOPK + k]
        return pltpu.make_async_copy(yb_ref.at[pl.ds(p, 1)], buf_ref.at[k, pl.ds(r, 1)], sem.at[k])

    def start(r, c):
        copy(r, 0).start()
        copy(r, 1).start()
        return c

    def wait(r, c):
        copy(r, 0).wait()
        copy(r, 1).wait()
        return c

    lax.fori_loop(0, CMB_T, start, 0)
    lax.fori_loop(0, CMB_T, wait, 0)
    gate = gate_ref[...]
    y = buf_ref[0] * gate[:, 0:1] + buf_ref[1] * gate[:, 1:2]
    o_ref[...] = x_ref[...] + g_ref[...] * y


def moe_combine(dest, yb, gates, xs, mod, k_gate):
    return pl.pallas_call(
        _combine_kernel,
        grid_spec=pltpu.PrefetchScalarGridSpec(
            num_scalar_prefetch=1,
            grid=(NT // CMB_T,),
            in_specs=[
                pl.BlockSpec(memory_space=pl.ANY),
                pl.BlockSpec((CMB_T, TOPK), lambda i, pos: (i, 0)),
                pl.BlockSpec((CMB_T, D), lambda i, pos: (i, 0)),
                pl.BlockSpec((None, 1, D), lambda i, pos: (_row_group(i, CMB_T), 0, k_gate)),
            ],
            out_specs=pl.BlockSpec((CMB_T, D), lambda i, pos: (i, 0)),
            scratch_shapes=[pltpu.VMEM((TOPK, CMB_T, D), F32), pltpu.SemaphoreType.DMA((TOPK,))],
        ),
        out_shape=jax.ShapeDtypeStruct((NT, D), F32),
        compiler_params=_cp(("arbitrary",)),
        name="moe_combine",
    )(dest, yb, gates, xs, mod)


def _w_in_layout(w):
    kr = w[:, 4608:4672]
    krs = kr[:, _ROPE_SWAP]
    return jnp.concatenate(
        [w[:, 0:4608], w[:, 4672:9280], kr, kr, krs, krs, w[:, 9280:9312],
         jnp.zeros((D, PW - _COL_GA - 2 * GLA_RANK), w.dtype)], axis=1).astype(BF16)


def _w_uq_cols():
    nope = [MLA_QK * h + j for h in range(MLA_H) for j in range(MLA_NOPE)]
    rope = [MLA_QK * h + MLA_NOPE + d for h in range(MLA_H) for d in range(MLA_ROPE)]
    rope_s = [MLA_QK * h + MLA_NOPE + int(d) for h in range(MLA_H) for d in _ROPE_SWAP]
    return np.array(nope + rope + rope_s, np.int32)


def _w_ukv_cols():
    kn = [(MLA_NOPE + MLA_V) * h + j for h in range(MLA_H) for j in range(MLA_NOPE)]
    vv = [(MLA_NOPE + MLA_V) * h + MLA_NOPE + j for h in range(MLA_H) for j in range(MLA_V)]
    return np.array(kn + vv, np.int32)


def token_mixing_layer(xs, mod, norm1, w_in, w_out, na_qn, na_kn, na_rpb, mla_qa, mla_kva, w_uq, w_ukv,
                       mla_qn, mla_kn, gwf, gbf, gwb, gbb, gla_on, c2, s2):
    h = norm_modulate(xs, norm1.reshape(1, D), mod, 0, 1, BF16)
    proj = matmul(h, _w_in_layout(w_in), NT // 8, 512, BF16)
    oa = na_attention(proj, na_rpb, na_qn, na_kn)
    q = mla_q_prep(proj, w_uq[:, _w_uq_cols()].astype(BF16), mla_qa, mla_qn, c2, s2)
    k, v = mla_kv_prep(proj, w_ukv[:, _w_ukv_cols()].astype(BF16), mla_kva, mla_kn, c2, s2)
    ob = mla_attention(q, k, v)
    o_f, o_b = gla_scan(proj, gwf, gbf, gwb, gbb)
    oc = gla_output(o_f, o_b, proj, gla_on)
    wo = w_out.astype(BF16)
    return out_projection(oa, ob, oc, wo[:NA_W], wo[NA_W:NA_W + MLA_W], wo[NA_W + MLA_W:], xs, mod, 2)


def moe_layer(xs, mod, norm2, w_router_p, router_bias, w1, w3, w2):
    h, logits = norm_modulate(xs, norm2.reshape(1, D), mod, 3, 4, F32, w_router_p)
    expert_idx, gates = _route(logits[:, :E], router_bias)
    dest, row_token, block_expert, n_active = _dispatch_plan(expert_idx)
    xb = gather_rows(row_token, h)
    yb = expert_ffn(block_expert, n_active, xb, w1.astype(BF16), w3.astype(BF16), w2.astype(BF16))
    return moe_combine(dest, yb, gates, xs, mod, 5)


def kernel(x, c, ctx, c_ctx, w_ada, b_ada, norm1, norm2, w_in, w_out, na_q_norm, na_k_norm, na_rpb, mla_qa_norm, mla_kva_norm, mla_w_uq, mla_w_ukv, mla_q_norm, mla_k_norm, gla_w_gate_f, gla_b_gate_f, gla_w_gate_b, gla_b_gate_b, gla_out_norm, w_router, router_bias, moe_w1, moe_w3, moe_w2):
    cond = jnp.concatenate([c, c_ctx[None, :], jnp.zeros((8 - NB - 1, D), F32)], axis=0)
    mods = ada_modulation(cond, w_ada, b_ada)
    xs = jnp.concatenate([x.reshape(NL, D), ctx.reshape(NB * L, D)], axis=0)
    c2, s2 = _rope_tables()
    w_router_p = jnp.concatenate([w_router, jnp.zeros((D, 128 - E), F32)], axis=1)
    for l in range(DEPTH):
        mod = mods[l].reshape(8, 1, 6 * D)
        xs = token_mixing_layer(xs, mod, norm1[l], w_in[l], w_out[l], na_q_norm[l], na_k_norm[l], na_rpb[l],
                                mla_qa_norm[l], mla_kva_norm[l], mla_w_uq[l], mla_w_ukv[l],
                                mla_q_norm[l], mla_k_norm[l], gla_w_gate_f[l], gla_b_gate_f[l],
                                gla_w_gate_b[l], gla_b_gate_b[l], gla_out_norm[l], c2, s2)
        xs = moe_layer(xs, mod, norm2[l], w_router_p, router_bias, moe_w1[l], moe_w3[l], moe_w2[l])
    return xs[:NL].reshape(NB, S, D)
```

```python
import functools

import numpy as np
import jax
import jax.numpy as jnp
from jax import lax
from jax.experimental import pallas as pl
from jax.experimental.pallas import tpu as pltpu

F32 = jnp.float32
BF16 = jnp.bfloat16

D = 4096
NB = 2
S = 4096
L = 256
DEPTH = 2
GW = 64
EPS = 1e-6
NL = NB * S
NT = NL + NB * L
NA_H, NA_D = 8, 128
NA_W = NA_H * NA_D
NA_R, NA_C = 8, 16
MLA_H = 12
MLA_QL, MLA_KVL = 1024, 512
MLA_NOPE, MLA_ROPE, MLA_V = 128, 64, 128
MLA_QK = MLA_NOPE + MLA_ROPE
MLA_W = MLA_H * MLA_V
GLA_H, GLA_DK, GLA_DV = 6, 128, 256
GLA_KW = GLA_H * GLA_DK
GLA_W = GLA_H * GLA_DV
GLA_RANK = 16
GLA_TAU = 16.0
CH = 64
E = 16
E_GROUPS = 4
E_PER = E // E_GROUPS
TOPK = 2
FF = 1024
ROPE_BASE = 10000.0

_COL_NAQ, _COL_NAK, _COL_NAV = 0, 1024, 2048
_COL_CQ, _COL_CKV = 3072, 4096
_COL_GQ, _COL_GK, _COL_GV, _COL_GG = 4608, 5376, 6144, 7680
_COL_KR = 9216
_COL_GA = 9472
PW = 9728

BLK = 256
NBLK_L = S // BLK
CTX_BLK0 = NL // BLK
MOE_R = 512
MOE_TF = 256
NK = NT * TOPK
MOE_NB = (NK + E * (MOE_R - 1)) // MOE_R
MOE_P = MOE_NB * MOE_R

NEG = -1e30
VMEM_LIMIT = 56 * 1024 * 1024


def _cp(sem, vmem=VMEM_LIMIT):
    return pltpu.CompilerParams(dimension_semantics=sem, vmem_limit_bytes=vmem)


def _dot(a, b):
    return jnp.dot(a, b, preferred_element_type=F32)


def _dot_nt(a, b):
    return lax.dot_general(a, b, (((1,), (1,)), ((), ())), preferred_element_type=F32)


def _dot_tn(a, b):
    return lax.dot_general(a, b, (((0,), (0,)), ((), ())), preferred_element_type=F32)


def _sigmoid(x):
    return 1.0 / (1.0 + jnp.exp(-x))


def _row_group(i, tm):
    r0 = i * tm
    return jnp.where(r0 >= NL, 2, r0 // S)


def _ada_kernel(s_ref, w_ref, b_ref, o_ref):
    s = s_ref[...]
    s = s * _sigmoid(s)
    o_ref[...] = _dot(s.astype(BF16), w_ref[...].astype(BF16)) + b_ref[...]


def ada_modulation(cond, w_ada, b_ada):
    tn = 512
    n = 6 * D
    return pl.pallas_call(
        _ada_kernel,
        grid=(DEPTH, n // tn),
        in_specs=[
            pl.BlockSpec((8, D), lambda l, j: (0, 0)),
            pl.BlockSpec((None, D, tn), lambda l, j: (l, 0, j)),
            pl.BlockSpec((None, 1, tn), lambda l, j: (l, 0, j)),
        ],
        out_specs=pl.BlockSpec((None, 8, tn), lambda l, j: (l, 0, j)),
        out_shape=jax.ShapeDtypeStruct((DEPTH, 8, n), F32),
        compiler_params=_cp(("arbitrary", "arbitrary")),
        name="ada_modulation",
    )(cond, w_ada, b_ada.reshape(DEPTH, 1, n))


def _normmod(x, g, sh, sc):
    y = x * lax.rsqrt(jnp.mean(x * x, axis=-1, keepdims=True) + EPS) * g
    return y * (1.0 + sc) + sh


def _normmod_kernel(x_ref, g_ref, sh_ref, sc_ref, h_ref):
    h_ref[...] = _normmod(x_ref[...], g_ref[...], sh_ref[...], sc_ref[...]).astype(h_ref.dtype)


def _normmod_router_kernel(x_ref, g_ref, sh_ref, sc_ref, wr_ref, h_ref, lg_ref):
    h = _normmod(x_ref[...], g_ref[...], sh_ref[...], sc_ref[...])
    h_ref[...] = h.astype(h_ref.dtype)
    lg_ref[...] = jnp.dot(h, wr_ref[...], preferred_element_type=F32,
                          precision=lax.Precision.HIGHEST)


def norm_modulate(xs, gain, mod, k_shift, k_scale, out_dtype, w_router=None):
    tm = 512
    in_specs = [
        pl.BlockSpec((tm, D), lambda i: (i, 0)),
        pl.BlockSpec((1, D), lambda i: (0, 0)),
        pl.BlockSpec((None, 1, D), lambda i: (_row_group(i, tm), 0, k_shift)),
        pl.BlockSpec((None, 1, D), lambda i: (_row_group(i, tm), 0, k_scale)),
    ]
    h_spec = pl.BlockSpec((tm, D), lambda i: (i, 0))
    h_shape = jax.ShapeDtypeStruct((NT, D), out_dtype)
    if w_router is None:
        return pl.pallas_call(
            _normmod_kernel, grid=(NT // tm,), in_specs=in_specs, out_specs=h_spec,
            out_shape=h_shape, compiler_params=_cp(("arbitrary",)), name="norm_modulate",
        )(xs, gain, mod, mod)
    return pl.pallas_call(
        _normmod_router_kernel, grid=(NT // tm,),
        in_specs=in_specs + [pl.BlockSpec((D, 128), lambda i: (0, 0))],
        out_specs=[h_spec, pl.BlockSpec((tm, 128), lambda i: (i, 0))],
        out_shape=[h_shape, jax.ShapeDtypeStruct((NT, 128), F32)],
        compiler_params=_cp(("arbitrary",)), name="norm_modulate_router",
    )(xs, gain, mod, mod, w_router)


def _mm_kernel(a_ref, b_ref, o_ref):
    o_ref[...] = _dot(a_ref[...], b_ref[...]).astype(o_ref.dtype)


def matmul(a, b, tm, tn, out_dtype):
    m, k = a.shape
    n = b.shape[1]
    return pl.pallas_call(
        _mm_kernel,
        grid=(m // tm, n // tn),
        in_specs=[pl.BlockSpec((tm, k), lambda i, j: (i, 0)),
                  pl.BlockSpec((k, tn), lambda i, j: (0, j))],
        out_specs=pl.BlockSpec((tm, tn), lambda i, j: (i, j)),
        out_shape=jax.ShapeDtypeStruct((m, n), out_dtype),
        compiler_params=_cp(("arbitrary", "arbitrary")),
        name="matmul",
    )(a, b)


def _qblk(b, i):
    return jnp.where(i == 0, CTX_BLK0 + b, b * NBLK_L + i - 1)


def _rms_rows(x, g):
    return x * lax.rsqrt(jnp.mean(x * x, axis=-1, keepdims=True) + EPS) * g


NA_KROWS = 12
NA_KW = NA_KROWS * GW


def _na_kernel(q_ref, kl_ref, kc_ref, vl_ref, vc_ref, bias_ref, qn_ref, kn_ref, o_ref, ks_ref):
    i = pl.program_id(2)

    @pl.when(i == 0)
    def _():
        ks_ref[0:L, :] = _rms_rows(kc_ref[...].astype(F32), kn_ref[...]).astype(BF16)
        ks_ref[L:, :] = _rms_rows(kl_ref[...].astype(F32), kn_ref[...]).astype(BF16)

    q = (_rms_rows(q_ref[...].astype(F32), qn_ref[...]) * (NA_D ** -0.5)).astype(BF16)
    s_c = _dot_nt(q, ks_ref[0:L, :])
    m_c = jnp.max(s_c, axis=-1, keepdims=True)

    @pl.when(i == 0)
    def _():
        p = jnp.exp(s_c - m_c)
        l = jnp.sum(p, axis=-1, keepdims=True)
        o_ref[...] = (_dot(p.astype(BF16), vc_ref[...]) / l).astype(o_ref.dtype)

    @pl.when(i > 0)
    def _():
        row0 = jnp.clip(4 * (i - 1) - 4, 0, GW - NA_KROWS)
        k0 = pl.multiple_of(row0 * GW, BLK)
        s_l = _dot_nt(q, ks_ref[pl.ds(L + k0, NA_KW), :]) + bias_ref[...]
        m = jnp.maximum(m_c, jnp.max(s_l, axis=-1, keepdims=True))
        p_c = jnp.exp(s_c - m)
        p_l = jnp.exp(s_l - m)
        l = jnp.sum(p_c, axis=-1, keepdims=True) + jnp.sum(p_l, axis=-1, keepdims=True)
        o = _dot(p_l.astype(BF16), vl_ref[pl.ds(k0, NA_KW), :]) + _dot(p_c.astype(BF16), vc_ref[...])
        o_ref[...] = (o / l).astype(o_ref.dtype)


def _na_bias_index():
    dr = np.zeros((3, BLK, NA_KW), np.int32)
    dc = np.zeros((3, BLK, NA_KW), np.int32)
    ok = np.zeros((3, BLK, NA_KW), bool)
    rows = S // GW
    for p, blk in enumerate((0, 5, NBLK_L - 1)):
        ks = int(np.clip(4 * blk - 4, 0, GW - NA_KROWS))
        r = 4 * blk + np.arange(BLK) // GW
        c = np.arange(BLK) % GW
        rk = ks + np.arange(NA_KW) // GW
        ck = np.arange(NA_KW) % GW
        r0 = np.clip(r - NA_R // 2, 0, rows - NA_R)
        ws = np.clip(c - NA_C // 2, 0, GW - NA_C)
        row_ok = (rk[None, :] >= r0[:, None]) & (rk[None, :] < r0[:, None] + NA_R)
        col_ok = (ck[None, :] >= ws[:, None]) & (ck[None, :] < ws[:, None] + NA_C)
        ok[p] = row_ok & col_ok
        dr[p] = np.clip(rk[None, :] - r[:, None] + NA_R - 1, 0, 2 * NA_R - 2)
        dc[p] = np.clip(ck[None, :] - c[:, None], -(NA_C - 1), NA_C - 1) + NA_C - 1
    return dr, dc, ok


def _na_bias_table(rpb):
    dr, dc, ok = _na_bias_index()
    qr, kr = BLK // GW, NA_KROWS
    dr_t = dr.reshape(3, qr, GW, kr, GW)[:, :, 0, :, 0].reshape(3 * qr * kr)
    dc_t = dc[0].reshape(qr, GW, kr, GW)[0, :, 0, :].reshape(GW * GW)
    oh_c = jnp.asarray(np.eye(2 * NA_C - 1, dtype=np.float32)[:, dc_t])
    oh_r = jnp.asarray(np.eye(2 * NA_R - 1, dtype=np.float32)[dr_t])
    hi = lax.Precision.HIGHEST
    t1 = jnp.einsum('hrd,dx->hrx', rpb.astype(F32), oh_c, precision=hi)
    t2 = jnp.einsum('yr,hrx->hyx', oh_r, t1, precision=hi)
    t2 = t2.reshape(NA_H, 3, qr, kr, GW, GW).transpose(0, 1, 2, 4, 3, 5).reshape(NA_H, 3, BLK, NA_KW)
    return jnp.where(ok[None], t2, NEG)


def na_attention(proj, rpb, qn, kn):
    bias = _na_bias_table(rpb)

    def pat(i):
        return jnp.where(i <= 1, 0, jnp.where(i == NBLK_L, 2, 1))

    cq, ck, cv = _COL_NAQ // NA_D, _COL_NAK // NA_D, _COL_NAV // NA_D
    return pl.pallas_call(
        _na_kernel,
        grid=(NB, NA_H, NBLK_L + 1),
        in_specs=[
            pl.BlockSpec((BLK, NA_D), lambda b, h, i: (_qblk(b, i), cq + h)),
            pl.BlockSpec((S, NA_D), lambda b, h, i: (b, ck + h)),
            pl.BlockSpec((L, NA_D), lambda b, h, i: (CTX_BLK0 + b, ck + h)),
            pl.BlockSpec((S, NA_D), lambda b, h, i: (b, cv + h)),
            pl.BlockSpec((L, NA_D), lambda b, h, i: (CTX_BLK0 + b, cv + h)),
            pl.BlockSpec((None, None, BLK, NA_KW), lambda b, h, i: (h, pat(i), 0, 0)),
            pl.BlockSpec((1, NA_D), lambda b, h, i: (0, 0)),
            pl.BlockSpec((1, NA_D), lambda b, h, i: (0, 0)),
        ],
        out_specs=pl.BlockSpec((BLK, NA_D), lambda b, h, i: (_qblk(b, i), h)),
        out_shape=jax.ShapeDtypeStruct((NT, NA_W), BF16),
        scratch_shapes=[pltpu.VMEM((L + S, NA_D), BF16)],
        compiler_params=_cp(("arbitrary", "arbitrary", "arbitrary")),
        name="na_attention",
    )(proj, proj, proj, proj, proj, bias, qn.reshape(1, NA_D), kn.reshape(1, NA_D))


MLA_HW = 256


def _half_mask(h, width=128):
    lane = lax.broadcasted_iota(jnp.int32, (1, width), 1)
    return (lane < 64) if h % 2 == 0 else (lane >= 64)


def _mla_q_kernel(cq_ref, w_ref, qa_ref, gn_ref, g2_ref, gs2_ref, c2_ref, s2_ref, q_ref):
    x = _rms_rows(cq_ref[...].astype(F32), qa_ref[...]).astype(BF16)
    y = _dot(x, w_ref[...])
    nw = MLA_H * MLA_NOPE
    gc = g2_ref[...] * c2_ref[...]
    gs = gs2_ref[...] * s2_ref[...]
    for p in range(MLA_H // 2):
        r1 = y[:, nw + 128 * p: nw + 128 * (p + 1)]
        r2 = y[:, nw + 768 + 128 * p: nw + 768 + 128 * (p + 1)]
        rot = r1 * gc + r2 * gs
        sq = r1 * r1
        for h in (2 * p, 2 * p + 1):
            msk = _half_mask(h)
            nope = y[:, 128 * h: 128 * (h + 1)]
            ss = jnp.sum(nope * nope, axis=-1, keepdims=True) + jnp.sum(
                jnp.where(msk, sq, 0.0), axis=-1, keepdims=True)
            inv = lax.rsqrt(ss / MLA_QK + EPS) * (MLA_QK ** -0.5)
            q_ref[h, :, 0:128] = (nope * gn_ref[...] * inv).astype(BF16)
            q_ref[h, :, 128:256] = (jnp.where(msk, rot, 0.0) * inv).astype(BF16)


def _mla_kv_kernel(ckv_ref, kr_ref, w_ref, kva_ref, gn_ref, g2_ref, gs2_ref, c2_ref, s2_ref,
                   k_ref, v_ref):
    x = _rms_rows(ckv_ref[...].astype(F32), kva_ref[...]).astype(BF16)
    y = _dot(x, w_ref[...])
    kr = kr_ref[...].astype(F32)
    r1 = kr[:, 0:128]
    r2 = kr[:, 128:256]
    rot = r1 * (g2_ref[...] * c2_ref[...]) + r2 * (gs2_ref[...] * s2_ref[...])
    ss_r = jnp.sum(jnp.where(_half_mask(0), r1 * r1, 0.0), axis=-1, keepdims=True)
    nw = MLA_H * MLA_NOPE
    for h in range(MLA_H):
        nope = y[:, 128 * h: 128 * (h + 1)]
        ss = jnp.sum(nope * nope, axis=-1, keepdims=True) + ss_r
        inv = lax.rsqrt(ss / MLA_QK + EPS)
        k_ref[h, :, 0:128] = (nope * gn_ref[...] * inv).astype(BF16)
        k_ref[h, :, 128:256] = (jnp.where(_half_mask(h), rot, 0.0) * inv).astype(BF16)
        v_ref[h] = y[:, nw + 128 * h: nw + 128 * (h + 1)].astype(BF16)


def _rope_tables():
    t = jnp.arange(S, dtype=jnp.int32)
    pos = (t // GW, t % GW)
    nf = MLA_ROPE // 4
    inv = ROPE_BASE ** (-jnp.arange(nf, dtype=F32) / nf)
    cs, sn = [], []
    for ax in range(2):
        ang = pos[ax].astype(F32)[:, None] * inv[None, :]
        c, s = jnp.cos(ang), jnp.sin(ang)
        cs += [c, c]
        sn += [-s, s]
    c64 = jnp.tile(jnp.concatenate(cs, axis=1), (NB, 1))
    s64 = jnp.tile(jnp.concatenate(sn, axis=1), (NB, 1))
    c64 = jnp.concatenate([c64, jnp.ones((NB * L, MLA_ROPE), F32)], axis=0)
    s64 = jnp.concatenate([s64, jnp.zeros((NB * L, MLA_ROPE), F32)], axis=0)
    return jnp.tile(c64, (1, 2)), jnp.tile(s64, (1, 2))


_ROPE_SWAP = np.concatenate([np.arange(16, 32), np.arange(0, 16), np.arange(48, 64), np.arange(32, 48)])


def _rope_gains(g):
    gr = g[MLA_NOPE:]
    return (g[:MLA_NOPE].reshape(1, 128), jnp.tile(gr, 2).reshape(1, 128),
            jnp.tile(gr[_ROPE_SWAP], 2).reshape(1, 128))


def mla_q_prep(proj, w_uq_r, qa_g, qn_g, c2, s2):
    tm = 512
    gn, g2, gs2 = _rope_gains(qn_g)
    vec = lambda w: pl.BlockSpec((1, w), lambda i: (0, 0))
    return pl.pallas_call(
        _mla_q_kernel,
        grid=(NT // tm,),
        in_specs=[
            pl.BlockSpec((tm, MLA_QL), lambda i: (i, _COL_CQ // MLA_QL)),
            pl.BlockSpec((MLA_QL, 3072), lambda i: (0, 0)),
            vec(MLA_QL), vec(128), vec(128), vec(128),
            pl.BlockSpec((tm, 128), lambda i: (i, 0)),
            pl.BlockSpec((tm, 128), lambda i: (i, 0)),
        ],
        out_specs=pl.BlockSpec((MLA_H, tm, MLA_HW), lambda i: (0, i, 0)),
        out_shape=jax.ShapeDtypeStruct((MLA_H, NT, MLA_HW), BF16),
        compiler_params=_cp(("arbitrary",)),
        name="mla_q_prep",
    )(proj, w_uq_r, qa_g.reshape(1, MLA_QL), gn, g2, gs2, c2, s2)


def mla_kv_prep(proj, w_ukv_r, kva_g, kn_g, c2, s2):
    tm = 512
    gn, g2, gs2 = _rope_gains(kn_g)
    vec = lambda w: pl.BlockSpec((1, w), lambda i: (0, 0))
    return pl.pallas_call(
        _mla_kv_kernel,
        grid=(NT // tm,),
        in_specs=[
            pl.BlockSpec((tm, MLA_KVL), lambda i: (i, _COL_CKV // MLA_KVL)),
            pl.BlockSpec((tm, 256), lambda i: (i, _COL_KR // 256)),
            pl.BlockSpec((MLA_KVL, 3072), lambda i: (0, 0)),
            vec(MLA_KVL), vec(128), vec(128), vec(128),
            pl.BlockSpec((tm, 128), lambda i: (i, 0)),
            pl.BlockSpec((tm, 128), lambda i: (i, 0)),
        ],
        out_specs=[pl.BlockSpec((MLA_H, tm, MLA_HW), lambda i: (0, i, 0)),
                   pl.BlockSpec((MLA_H, tm, MLA_V), lambda i: (0, i, 0))],
        out_shape=[jax.ShapeDtypeStruct((MLA_H, NT, MLA_HW), BF16),
                   jax.ShapeDtypeStruct((MLA_H, NT, MLA_V), BF16)],
        compiler_params=_cp(("arbitrary",)),
        name="mla_kv_prep",
    )(proj, proj, w_ukv_r, kva_g.reshape(1, MLA_KVL), gn, g2, gs2, c2, s2)


def _mla_attn_kernel(q_ref, kl_ref, kc_ref, vl_ref, vc_ref, o_ref):
    i = pl.program_id(2)
    q = q_ref[...]
    s_c = _dot_nt(q, kc_ref[...])
    m_c = jnp.max(s_c, axis=-1, keepdims=True)

    @pl.when(i == 0)
    def _():
        p = jnp.exp(s_c - m_c)
        l = jnp.sum(p, axis=-1, keepdims=True)
        o_ref[...] = (_dot(p.astype(BF16), vc_ref[...]) / l).astype(o_ref.dtype)

    @pl.when(i > 0)
    def _():
        s_l = _dot_nt(q, kl_ref[...])
        m = jnp.maximum(m_c, jnp.max(s_l, axis=-1, keepdims=True))
        p_c = jnp.exp(s_c - m)
        p_l = jnp.exp(s_l - m)
        l = jnp.sum(p_c, axis=-1, keepdims=True) + jnp.sum(p_l, axis=-1, keepdims=True)
        o = _dot(p_l.astype(BF16), vl_ref[...]) + _dot(p_c.astype(BF16), vc_ref[...])
        o_ref[...] = (o / l).astype(o_ref.dtype)


def mla_attention(q, k, v):
    return pl.pallas_call(
        _mla_attn_kernel,
        grid=(NB, MLA_H, NBLK_L + 1),
        in_specs=[
            pl.BlockSpec((None, BLK, MLA_HW), lambda b, h, i: (h, _qblk(b, i), 0)),
            pl.BlockSpec((None, S, MLA_HW), lambda b, h, i: (h, b, 0)),
            pl.BlockSpec((None, L, MLA_HW), lambda b, h, i: (h, CTX_BLK0 + b, 0)),
            pl.BlockSpec((None, S, MLA_V), lambda b, h, i: (h, b, 0)),
            pl.BlockSpec((None, L, MLA_V), lambda b, h, i: (h, CTX_BLK0 + b, 0)),
        ],
        out_specs=pl.BlockSpec((BLK, MLA_V), lambda b, h, i: (_qblk(b, i), h)),
        out_shape=jax.ShapeDtypeStruct((NT, MLA_W), BF16),
        compiler_params=_cp(("arbitrary", "arbitrary", "arbitrary")),
        name="mla_attention",
    )(q, k, k, v, v)


N_CH_C = L // CH
N_CH_L = S // CH
N_CH = N_CH_C + N_CH_L


def _gla_kernel(qf_ref, kf_ref, vf_ref, af_ref, qb_ref, kb_ref, vb_ref, ab_ref,
                wg_ref, bg_ref, of_ref, ob_ref, st_ref):
    t = pl.program_id(1)

    @pl.when(t == 0)
    def _():
        st_ref[...] = jnp.zeros_like(st_ref)

    row = lax.broadcasted_iota(jnp.int32, (CH, CH), 0)
    col = lax.broadcasted_iota(jnp.int32, (CH, CH), 1)
    dirs = (
        (qf_ref, kf_ref, vf_ref, af_ref, of_ref, col <= row, CH - 1),
        (qb_ref, kb_ref, vb_ref, ab_ref, ob_ref, col >= row, 0),
    )
    for d, (q_ref, k_ref, v_ref, a_ref, o_ref, keep, last) in enumerate(dirs):
        z = _dot(a_ref[...], wg_ref[d]) + bg_ref[d]
        g = (jnp.minimum(z, 0.0) - jnp.log(1.0 + jnp.exp(-jnp.abs(z)))) / GLA_TAU
        b = jnp.dot(keep.astype(F32), g, preferred_element_type=F32,
                    precision=lax.Precision.HIGHEST)
        b_end = b[last:last + 1, :]
        q = q_ref[...].astype(F32) * (GLA_DK ** -0.5)
        k = k_ref[...].astype(F32)
        qd = (q * jnp.exp(b)).astype(BF16)
        kd = (k * jnp.exp(-b)).astype(BF16)
        ke = (k * jnp.exp(b_end - b)).astype(BF16)
        e_end = jnp.exp(b_end)
        v = v_ref[...]
        for h in range(GLA_H):
            ksl = slice(GLA_DK * h, GLA_DK * (h + 1))
            vsl = slice(GLA_DV * h, GLA_DV * (h + 1))
            att = jnp.where(keep, _dot_nt(qd[:, ksl], kd[:, ksl]), 0.0)
            st = st_ref[d, h]
            o = _dot_nt(qd[:, ksl], st.astype(BF16)) + _dot(att.astype(BF16), v[:, vsl])
            o_ref[:, vsl] = o
            st_ref[d, h] = st * e_end[:, ksl] + _dot_tn(v[:, vsl], ke[:, ksl])


def gla_scan(proj, wg_f, bg_f, wg_b, bg_b):
    wg = jnp.zeros((2, 128, GLA_KW), F32)
    wg = wg.at[0, 0:GLA_RANK].set(wg_f).at[1, GLA_RANK:2 * GLA_RANK].set(wg_b).astype(BF16)
    bg = jnp.stack([bg_f, bg_b]).reshape(2, 1, GLA_KW)

    ctx0 = NL // CH

    def fwd(b, t):
        return jnp.where(t < N_CH_C, ctx0 + N_CH_C * b + t, N_CH_L * b + t - N_CH_C)

    def bwd(b, t):
        return jnp.where(t < N_CH_C, ctx0 + N_CH_C * b + N_CH_C - 1 - t, N_CH_L * b + N_CH - 1 - t)

    def specs(rowfn):
        return [
            pl.BlockSpec((CH, GLA_KW), lambda b, t: (rowfn(b, t), _COL_GQ // GLA_KW)),
            pl.BlockSpec((CH, GLA_KW), lambda b, t: (rowfn(b, t), _COL_GK // GLA_KW)),
            pl.BlockSpec((CH, GLA_W), lambda b, t: (rowfn(b, t), _COL_GV // GLA_W)),
            pl.BlockSpec((CH, 128), lambda b, t: (rowfn(b, t), _COL_GA // 128)),
        ]

    return pl.pallas_call(
        _gla_kernel,
        grid=(NB, N_CH),
        in_specs=specs(fwd) + specs(bwd) + [
            pl.BlockSpec((2, 128, GLA_KW), lambda b, t: (0, 0, 0)),
            pl.BlockSpec((2, 1, GLA_KW), lambda b, t: (0, 0, 0)),
        ],
        out_specs=[pl.BlockSpec((CH, GLA_W), lambda b, t: (fwd(b, t), 0)),
                   pl.BlockSpec((CH, GLA_W), lambda b, t: (bwd(b, t), 0))],
        out_shape=[jax.ShapeDtypeStruct((NT, GLA_W), F32)] * 2,
        scratch_shapes=[pltpu.VMEM((2, GLA_H, GLA_DV, GLA_DK), F32)],
        compiler_params=_cp(("arbitrary", "arbitrary")),
        name="gla_scan",
    )(proj, proj, proj, proj, proj, proj, proj, proj, wg, bg)


def _gla_out_kernel(of_ref, ob_ref, g_ref, gn_ref, o_ref):
    for h in range(GLA_H):
        sl = slice(GLA_DV * h, GLA_DV * (h + 1))
        o = _rms_rows(of_ref[:, sl] + ob_ref[:, sl], gn_ref[...])
        gate = g_ref[:, sl].astype(F32)
        o_ref[:, sl] = (o * (gate * _sigmoid(gate))).astype(o_ref.dtype)


def gla_output(o_f, o_b, proj, on_g):
    tm = 512
    return pl.pallas_call(
        _gla_out_kernel,
        grid=(NT // tm,),
        in_specs=[
            pl.BlockSpec((tm, GLA_W), lambda i: (i, 0)),
            pl.BlockSpec((tm, GLA_W), lambda i: (i, 0)),
            pl.BlockSpec((tm, GLA_W), lambda i: (i, _COL_GG // GLA_W)),
            pl.BlockSpec((1, GLA_DV), lambda i: (0, 0)),
        ],
        out_specs=pl.BlockSpec((tm, GLA_W), lambda i: (i, 0)),
        out_shape=jax.ShapeDtypeStruct((NT, GLA_W), BF16),
        compiler_params=_cp(("arbitrary",)),
        name="gla_output",
    )(o_f, o_b, proj, on_g.reshape(1, GLA_DV))


def _out_proj_kernel(oa_ref, ob_ref, oc_ref, wa_ref, wb_ref, wc_ref, x_ref, g_ref, o_ref):
    y = _dot(oa_ref[...], wa_ref[...]) + _dot(ob_ref[...], wb_ref[...]) + _dot(oc_ref[...], wc_ref[...])
    o_ref[...] = x_ref[...] + g_ref[...] * y


def out_projection(oa, ob, oc, wa, wb, wc, xs, mod, k_gate):
    tm, tn = 512, 1024
    nj = D // tn
    return pl.pallas_call(
        _out_proj_kernel,
        grid=(nj, NT // tm),
        in_specs=[
            pl.BlockSpec((tm, NA_W), lambda j, i: (i, 0)),
            pl.BlockSpec((tm, MLA_W), lambda j, i: (i, 0)),
            pl.BlockSpec((tm, GLA_W), lambda j, i: (i, 0)),
            pl.BlockSpec((NA_W, tn), lambda j, i: (0, j)),
            pl.BlockSpec((MLA_W, tn), lambda j, i: (0, j)),
            pl.BlockSpec((GLA_W, tn), lambda j, i: (0, j)),
            pl.BlockSpec((tm, tn), lambda j, i: (i, j)),
            pl.BlockSpec((None, 1, tn), lambda j, i: (_row_group(i, tm), 0, k_gate * nj + j)),
        ],
        out_specs=pl.BlockSpec((tm, tn), lambda j, i: (i, j)),
        out_shape=jax.ShapeDtypeStruct((NT, D), F32),
        compiler_params=_cp(("arbitrary", "arbitrary")),
        name="out_projection",
    )(oa, ob, oc, wa, wb, wc, xs, mod)


def _route(logits, router_bias):
    scores = jax.nn.sigmoid(logits)
    grouped = (scores + router_bias.astype(F32)).reshape(NT, E_GROUPS, E_PER)

    def top2(a):
        idx = lax.broadcasted_iota(jnp.int32, a.shape, a.ndim - 1)
        i1 = jnp.argmax(a, axis=-1).astype(jnp.int32)
        rest = jnp.where(idx == i1[..., None], -jnp.inf, a)
        i2 = jnp.argmax(rest, axis=-1).astype(jnp.int32)
        return jnp.max(a, axis=-1), jnp.max(rest, axis=-1), i1, i2

    m1, m2, _, _ = top2(grouped)
    grp = jnp.argmax(m1 + m2, axis=-1).astype(jnp.int32)
    gsel = lax.broadcasted_iota(jnp.int32, (NT, E_GROUPS, E_PER), 1) == grp[:, None, None]
    in_group = jnp.sum(jnp.where(gsel, grouped, 0.0), axis=1)
    _, _, l1, l2 = top2(in_group)
    expert_idx = grp[:, None] * E_PER + jnp.stack([l1, l2], axis=-1)
    esel = lax.broadcasted_iota(jnp.int32, (NT, TOPK, E), 2) == expert_idx[:, :, None]
    w = jnp.sum(jnp.where(esel, scores[:, None, :], 0.0), axis=-1)
    return expert_idx, w / jnp.sum(w, axis=-1, keepdims=True)


def _dispatch_plan(expert_idx):
    flat_e = expert_idx.reshape(NK)
    onehot = (flat_e[:, None] == jnp.arange(E, dtype=jnp.int32)[None, :]).astype(jnp.int32)
    csum = jnp.cumsum(onehot, axis=0)
    counts = csum[-1]
    rank = jnp.take_along_axis(csum, flat_e[:, None], axis=1)[:, 0] - 1
    padded = (counts + MOE_R - 1) // MOE_R * MOE_R
    pad_end = jnp.cumsum(padded)
    pad_start = pad_end - padded
    dest = (pad_start[flat_e] + rank).astype(jnp.int32)
    row_token = jnp.zeros((MOE_P,), jnp.int32).at[dest].set(jnp.arange(NK, dtype=jnp.int32) // TOPK)
    block_expert = jnp.minimum(
        jnp.searchsorted(pad_end, jnp.arange(MOE_NB, dtype=jnp.int32) * MOE_R, side='right'),
        E - 1).astype(jnp.int32)
    n_active = (pad_end[-1] // MOE_R).astype(jnp.int32).reshape(1)
    return dest, row_token, block_expert, n_active


def _gather_rows_kernel(tok_ref, h_ref, o_ref, buf_ref, sem):
    base = pl.program_id(0) * MOE_R

    def copy(r):
        return pltpu.make_async_copy(h_ref.at[pl.ds(tok_ref[base + r], 1)], buf_ref.at[pl.ds(r, 1)], sem)

    def start(r, c):
        copy(r).start()
        return c

    def wait(r, c):
        copy(r).wait()
        return c

    lax.fori_loop(0, MOE_R, start, 0)
    lax.fori_loop(0, MOE_R, wait, 0)
    o_ref[...] = buf_ref[...].astype(o_ref.dtype)


def gather_rows(row_token, h):
    return pl.pallas_call(
        _gather_rows_kernel,
        grid_spec=pltpu.PrefetchScalarGridSpec(
            num_scalar_prefetch=1,
            grid=(MOE_NB,),
            in_specs=[pl.BlockSpec(memory_space=pl.ANY)],
            out_specs=pl.BlockSpec((MOE_R, D), lambda i, tok: (i, 0)),
            scratch_shapes=[pltpu.VMEM((MOE_R, D), F32), pltpu.SemaphoreType.DMA(())],
        ),
        out_shape=jax.ShapeDtypeStruct((MOE_P, D), BF16),
        compiler_params=_cp(("arbitrary",)),
        name="moe_gather_rows",
    )(row_token, h)


def _expert_ffn_kernel(be_ref, na_ref, x_ref, w1_ref, w3_ref, w2_ref, o_ref):
    i = pl.program_id(0)
    j = pl.program_id(1)

    @pl.when(i < na_ref[0])
    def _():
        x = x_ref[...]
        a = _dot(x, w1_ref[...])
        b = _dot(x, w3_ref[...])
        y = _dot((a * _sigmoid(a) * b).astype(BF16), w2_ref[...])

        @pl.when(j == 0)
        def _():
            o_ref[...] = y

        @pl.when(j > 0)
        def _():
            o_ref[...] += y

    @pl.when(jnp.logical_and(i >= na_ref[0], j == 0))
    def _():
        o_ref[...] = jnp.zeros_like(o_ref)


def expert_ffn(block_expert, n_active, xb, w1, w3, w2):
    return pl.pallas_call(
        _expert_ffn_kernel,
        grid_spec=pltpu.PrefetchScalarGridSpec(
            num_scalar_prefetch=2,
            grid=(MOE_NB, FF // MOE_TF),
            in_specs=[
                pl.BlockSpec((MOE_R, D), lambda i, j, be, na: (i, 0)),
                pl.BlockSpec((None, D, MOE_TF), lambda i, j, be, na: (be[i], 0, j)),
                pl.BlockSpec((None, D, MOE_TF), lambda i, j, be, na: (be[i], 0, j)),
                pl.BlockSpec((None, MOE_TF, D), lambda i, j, be, na: (be[i], j, 0)),
            ],
            out_specs=pl.BlockSpec((MOE_R, D), lambda i, j, be, na: (i, 0)),
        ),
        out_shape=jax.ShapeDtypeStruct((MOE_P, D), F32),
        compiler_params=_cp(("arbitrary", "arbitrary")),
        name="moe_expert_ffn",
    )(block_expert, n_active, xb, w1, w3, w2)


CMB_T = 256


def _combine_kernel(pos_ref, yb_ref, gate_ref, x_ref, g_ref, o_ref, buf_ref, sem):
    base = pl.program_id(0) * CMB_T

    def copy(r, k):
        p = pos_ref[(base + r) * TOPK + k]
        return pltpu.make_async_copy(yb_ref.at[pl.ds(p, 1)], buf_ref.at[k, pl.ds(r, 1)], sem.at[k])

    def start(r, c):
        copy(r, 0).start()
        copy(r, 1).start()
        return c

    def wait(r, c):
        copy(r, 0).wait()
        copy(r, 1).wait()
        return c

    lax.fori_loop(0, CMB_T, start, 0)
    lax.fori_loop(0, CMB_T, wait, 0)
    gate = gate_ref[...]
    y = buf_ref[0] * gate[:, 0:1] + buf_ref[1] * gate[:, 1:2]
    o_ref[...] = x_ref[...] + g_ref[...] * y


def moe_combine(dest, yb, gates, xs, mod, k_gate):
    return pl.pallas_call(
        _combine_kernel,
        grid_spec=pltpu.PrefetchScalarGridSpec(
            num_scalar_prefetch=1,
            grid=(NT // CMB_T,),
            in_specs=[
                pl.BlockSpec(memory_space=pl.ANY),
                pl.BlockSpec((CMB_T, TOPK), lambda i, pos: (i, 0)),
                pl.BlockSpec((CMB_T, D), lambda i, pos: (i, 0)),
                pl.BlockSpec((None, 1, D), lambda i, pos: (_row_group(i, CMB_T), 0, k_gate)),
            ],
            out_specs=pl.BlockSpec((CMB_T, D), lambda i, pos: (i, 0)),
            scratch_shapes=[pltpu.VMEM((TOPK, CMB_T, D), F32), pltpu.SemaphoreType.DMA((TOPK,))],
        ),
        out_shape=jax.ShapeDtypeStruct((NT, D), F32),
        compiler_params=_cp(("arbitrary",)),
        name="moe_combine",
    )(dest, yb, gates, xs, mod)


def _w_in_layout(w):
    kr = w[:, 4608:4672]
    krs = kr[:, _ROPE_SWAP]
    return jnp.concatenate(
        [w[:, 0:4608], w[:, 4672:9280], kr, kr, krs, krs, w[:, 9280:9312],
         jnp.zeros((D, PW - _COL_GA - 2 * GLA_RANK), w.dtype)], axis=1).astype(BF16)


def _w_uq_cols():
    nope = [MLA_QK * h + j for h in range(MLA_H) for j in range(MLA_NOPE)]
    rope = [MLA_QK * h + MLA_NOPE + d for h in range(MLA_H) for d in range(MLA_ROPE)]
    rope_s = [MLA_QK * h + MLA_NOPE + int(d) for h in range(MLA_H) for d in _ROPE_SWAP]
    return np.array(nope + rope + rope_s, np.int32)


def _w_ukv_cols():
    kn = [(MLA_NOPE + MLA_V) * h + j for h in range(MLA_H) for j in range(MLA_NOPE)]
    vv = [(MLA_NOPE + MLA_V) * h + MLA_NOPE + j for h in range(MLA_H) for j in range(MLA_V)]
    return np.array(kn + vv, np.int32)


def token_mixing_layer(xs, mod, norm1, w_in, w_out, na_qn, na_kn, na_rpb, mla_qa, mla_kva, w_uq, w_ukv,
                       mla_qn, mla_kn, gwf, gbf, gwb, gbb, gla_on, c2, s2):
    h = norm_modulate(xs, norm1.reshape(1, D), mod, 0, 1, BF16)
    proj = matmul(h, _w_in_layout(w_in), NT // 8, 512, BF16)
    oa = na_attention(proj, na_rpb, na_qn, na_kn)
    q = mla_q_prep(proj, w_uq[:, _w_uq_cols()].astype(BF16), mla_qa, mla_qn, c2, s2)
    k, v = mla_kv_prep(proj, w_ukv[:, _w_ukv_cols()].astype(BF16), mla_kva, mla_kn, c2, s2)
    ob = mla_attention(q, k, v)
    o_f, o_b = gla_scan(proj, gwf, gbf, gwb, gbb)
    oc = gla_output(o_f, o_b, proj, gla_on)
    wo = w_out.astype(BF16)
    return out_projection(oa, ob, oc, wo[:NA_W], wo[NA_W:NA_W + MLA_W], wo[NA_W + MLA_W:], xs, mod, 2)


def moe_layer(xs, mod, norm2, w_router_p, router_bias, w1, w3, w2):
    h, logits = norm_modulate(xs, norm2.reshape(1, D), mod, 3, 4, F32, w_router_p)
    expert_idx, gates = _route(logits[:, :E], router_bias)
    dest, row_token, block_expert, n_active = _dispatch_plan(expert_idx)
    xb = gather_rows(row_token, h)
    yb = expert_ffn(block_expert, n_active, xb, w1.astype(BF16), w3.astype(BF16), w2.astype(BF16))
    return moe_combine(dest, yb, gates, xs, mod, 5)


def kernel(x, c, ctx, c_ctx, w_ada, b_ada, norm1, norm2, w_in, w_out, na_q_norm, na_k_norm, na_rpb, mla_qa_norm, mla_kva_norm, mla_w_uq, mla_w_ukv, mla_q_norm, mla_k_norm, gla_w_gate_f, gla_b_gate_f, gla_w_gate_b, gla_b_gate_b, gla_out_norm, w_router, router_bias, moe_w1, moe_w3, moe_w2):
    cond = jnp.concatenate([c, c_ctx[None, :], jnp.zeros((8 - NB - 1, D), F32)], axis=0)
    mods = ada_modulation(cond, w_ada, b_ada)
    xs = jnp.concatenate([x.reshape(NL, D), ctx.reshape(NB * L, D)], axis=0)
    c2, s2 = _rope_tables()
    w_router_p = jnp.concatenate([w_router, jnp.zeros((D, 128 - E), F32)], axis=1)
    for l in range(DEPTH):
        mod = mods[l].reshape(8, 1, 6 * D)
        xs = token_mixing_layer(xs, mod, norm1[l], w_in[l], w_out[l], na_q_norm[l], na_k_norm[l], na_rpb[l],
                                mla_qa_norm[l], mla_kva_norm[l], mla_w_uq[l], mla_w_ukv[l],
                                mla_q_norm[l], mla_k_norm[l], gla_w_gate_f[l], gla_b_gate_f[l],
                                gla_w_gate_b[l], gla_b_gate_b[l], gla_out_norm[l], c2, s2)
        xs = moe_layer(xs, mod, norm2[l], w_router_p, router_bias, moe_w1[l], moe_w3[l], moe_w2[l])
    return xs[:NL].reshape(NB, S, D)
```

```python
import functools

import numpy as np
import jax
import jax.numpy as jnp
from jax import lax
from jax.experimental import pallas as pl
from jax.experimental.pallas import tpu as pltpu

F32 = jnp.float32
BF16 = jnp.bfloat16

D = 4096
NB = 2
S = 4096
L = 256
DEPTH = 2
GW = 64
EPS = 1e-6
NL = NB * S
NT = NL + NB * L
NA_H, NA_D = 8, 128
NA_W = NA_H * NA_D
NA_R, NA_C = 8, 16
MLA_H = 12
MLA_QL, MLA_KVL = 1024, 512
MLA_NOPE, MLA_ROPE, MLA_V = 128, 64, 128
MLA_QK = MLA_NOPE + MLA_ROPE
MLA_W = MLA_H * MLA_V
GLA_H, GLA_DK, GLA_DV = 6, 128, 256
GLA_KW = GLA_H * GLA_DK
GLA_W = GLA_H * GLA_DV
GLA_RANK = 16
GLA_TAU = 16.0
CH = 64
E = 16
E_GROUPS = 4
E_PER = E // E_GROUPS
TOPK = 2
FF = 1024
ROPE_BASE = 10000.0

_COL_NAQ, _COL_NAK, _COL_NAV = 0, 1024, 2048
_COL_CQ, _COL_CKV = 3072, 4096
_COL_GQ, _COL_GK, _COL_GV, _COL_GG = 4608, 5376, 6144, 7680
_COL_KR = 9216
_COL_GA = 9472
PW = 9728

BLK = 256
NBLK_L = S // BLK
CTX_BLK0 = NL // BLK
MOE_R = 512
MOE_TF = 256
NK = NT * TOPK
MOE_NB = (NK + E * (MOE_R - 1)) // MOE_R
MOE_P = MOE_NB * MOE_R

NEG = -1e30
VMEM_LIMIT = 56 * 1024 * 1024


def _cp(sem, vmem=VMEM_LIMIT):
    return pltpu.CompilerParams(dimension_semantics=sem, vmem_limit_bytes=vmem)


def _dot(a, b):
    return jnp.dot(a, b, preferred_element_type=F32)


def _dot_nt(a, b):
    return lax.dot_general(a, b, (((1,), (1,)), ((), ())), preferred_element_type=F32)


def _dot_tn(a, b):
    return lax.dot_general(a, b, (((0,), (0,)), ((), ())), preferred_element_type=F32)


def _sigmoid(x):
    return 1.0 / (1.0 + jnp.exp(-x))


def _pack_pairs(lo, hi):
    lo_w = lax.bitcast_convert_type(lo.astype(BF16).astype(F32), jnp.uint32)
    hi_w = lax.bitcast_convert_type(hi.astype(BF16).astype(F32), jnp.uint32)
    return lax.shift_right_logical(lo_w, jnp.uint32(16)) | (hi_w & jnp.uint32(0xFFFF0000))


def _unpack_pairs(w):
    lo = lax.bitcast_convert_type(lax.shift_left(w, jnp.uint32(16)), F32)
    hi = lax.bitcast_convert_type(w & jnp.uint32(0xFFFF0000), F32)
    return lo, hi


def _row_group(i, tm):
    r0 = i * tm
    return jnp.where(r0 >= NL, 2, r0 // S)


def _ada_kernel(s_ref, w_ref, b_ref, o_ref):
    s = s_ref[...]
    s = s * _sigmoid(s)
    o_ref[...] = _dot(s.astype(BF16), w_ref[...].astype(BF16)) + b_ref[...]


def ada_modulation(cond, w_ada, b_ada):
    tn = 512
    n = 6 * D
    return pl.pallas_call(
        _ada_kernel,
        grid=(DEPTH, n // tn),
        in_specs=[
            pl.BlockSpec((8, D), lambda l, j: (0, 0)),
            pl.BlockSpec((None, D, tn), lambda l, j: (l, 0, j)),
            pl.BlockSpec((None, 1, tn), lambda l, j: (l, 0, j)),
        ],
        out_specs=pl.BlockSpec((None, 8, tn), lambda l, j: (l, 0, j)),
        out_shape=jax.ShapeDtypeStruct((DEPTH, 8, n), F32),
        compiler_params=_cp(("arbitrary", "arbitrary")),
        name="ada_modulation",
    )(cond, w_ada, b_ada.reshape(DEPTH, 1, n))


def _normmod(x, g, sh, sc):
    y = x * lax.rsqrt(jnp.mean(x * x, axis=-1, keepdims=True) + EPS) * g
    return y * (1.0 + sc) + sh


def _normmod_kernel(x_ref, g_ref, sh_ref, sc_ref, h_ref):
    h_ref[...] = _normmod(x_ref[...], g_ref[...], sh_ref[...], sc_ref[...]).astype(h_ref.dtype)


def _normmod_router_kernel(x_ref, g_ref, sh_ref, sc_ref, wr_ref, h_ref, lg_ref):
    h = _normmod(x_ref[...], g_ref[...], sh_ref[...], sc_ref[...])
    h_ref[...] = _pack_pairs(h[:, :D // 2], h[:, D // 2:])
    lg_ref[...] = jnp.dot(h, wr_ref[...], preferred_element_type=F32,
                          precision=lax.Precision.HIGHEST)


def norm_modulate(xs, gain, mod, k_shift, k_scale, out_dtype, w_router=None):
    tm = 512
    in_specs = [
        pl.BlockSpec((tm, D), lambda i: (i, 0)),
        pl.BlockSpec((1, D), lambda i: (0, 0)),
        pl.BlockSpec((None, 1, D), lambda i: (_row_group(i, tm), 0, k_shift)),
        pl.BlockSpec((None, 1, D), lambda i: (_row_group(i, tm), 0, k_scale)),
    ]
    if w_router is None:
        return pl.pallas_call(
            _normmod_kernel, grid=(NT // tm,), in_specs=in_specs,
            out_specs=pl.BlockSpec((tm, D), lambda i: (i, 0)),
            out_shape=jax.ShapeDtypeStruct((NT, D), out_dtype),
            compiler_params=_cp(("arbitrary",)), name="norm_modulate",
        )(xs, gain, mod, mod)
    return pl.pallas_call(
        _normmod_router_kernel, grid=(NT // tm,),
        in_specs=in_specs + [pl.BlockSpec((D, 128), lambda i: (0, 0))],
        out_specs=[pl.BlockSpec((tm, D // 2), lambda i: (i, 0)), pl.BlockSpec((tm, 128), lambda i: (i, 0))],
        out_shape=[jax.ShapeDtypeStruct((NT, D // 2), jnp.uint32), jax.ShapeDtypeStruct((NT, 128), F32)],
        compiler_params=_cp(("arbitrary",)), name="norm_modulate_router",
    )(xs, gain, mod, mod, w_router)


def _mm_kernel(a_ref, b_ref, o_ref):
    o_ref[...] = _dot(a_ref[...], b_ref[...]).astype(o_ref.dtype)


def matmul(a, b, layer, tm, tn, out_dtype):
    m, k = a.shape
    n = b.shape[2]
    return pl.pallas_call(
        _mm_kernel,
        grid=(m // tm, n // tn),
        in_specs=[pl.BlockSpec((tm, k), lambda i, j: (i, 0)),
                  pl.BlockSpec((None, k, tn), lambda i, j: (layer, 0, j))],
        out_specs=pl.BlockSpec((tm, tn), lambda i, j: (i, j)),
        out_shape=jax.ShapeDtypeStruct((m, n), out_dtype),
        compiler_params=_cp(("arbitrary", "arbitrary")),
        name="matmul",
    )(a, b)


def _qblk(b, i):
    return jnp.where(i == 0, CTX_BLK0 + b, b * NBLK_L + i - 1)


def _rms_rows(x, g):
    return x * lax.rsqrt(jnp.mean(x * x, axis=-1, keepdims=True) + EPS) * g


NA_KROWS = 12
NA_KW = NA_KROWS * GW


def _na_kernel(q_ref, kl_ref, kc_ref, vl_ref, vc_ref, bias_ref, qn_ref, kn_ref, o_ref, ks_ref):
    i = pl.program_id(2)

    @pl.when(i == 0)
    def _():
        ks_ref[0:L, :] = _rms_rows(kc_ref[...].astype(F32), kn_ref[...]).astype(BF16)
        ks_ref[L:, :] = _rms_rows(kl_ref[...].astype(F32), kn_ref[...]).astype(BF16)

    q = (_rms_rows(q_ref[...].astype(F32), qn_ref[...]) * (NA_D ** -0.5)).astype(BF16)
    s_c = _dot_nt(q, ks_ref[0:L, :])
    m_c = jnp.max(s_c, axis=-1, keepdims=True)

    @pl.when(i == 0)
    def _():
        p = jnp.exp(s_c - m_c)
        l = jnp.sum(p, axis=-1, keepdims=True)
        o_ref[...] = (_dot(p.astype(BF16), vc_ref[...]) / l).astype(o_ref.dtype)

    @pl.when(i > 0)
    def _():
        row0 = jnp.clip(4 * (i - 1) - 4, 0, GW - NA_KROWS)
        k0 = pl.multiple_of(row0 * GW, BLK)
        s_l = _dot_nt(q, ks_ref[pl.ds(L + k0, NA_KW), :]) + bias_ref[...]
        m = jnp.maximum(m_c, jnp.max(s_l, axis=-1, keepdims=True))
        p_c = jnp.exp(s_c - m)
        p_l = jnp.exp(s_l - m)
        l = jnp.sum(p_c, axis=-1, keepdims=True) + jnp.sum(p_l, axis=-1, keepdims=True)
        o = _dot(p_l.astype(BF16), vl_ref[pl.ds(k0, NA_KW), :]) + _dot(p_c.astype(BF16), vc_ref[...])
        o_ref[...] = (o / l).astype(o_ref.dtype)


def _na_bias_index():
    dr = np.zeros((3, BLK, NA_KW), np.int32)
    dc = np.zeros((3, BLK, NA_KW), np.int32)
    ok = np.zeros((3, BLK, NA_KW), bool)
    rows = S // GW
    for p, blk in enumerate((0, 5, NBLK_L - 1)):
        ks = int(np.clip(4 * blk - 4, 0, GW - NA_KROWS))
        r = 4 * blk + np.arange(BLK) // GW
        c = np.arange(BLK) % GW
        rk = ks + np.arange(NA_KW) // GW
        ck = np.arange(NA_KW) % GW
        r0 = np.clip(r - NA_R // 2, 0, rows - NA_R)
        ws = np.clip(c - NA_C // 2, 0, GW - NA_C)
        row_ok = (rk[None, :] >= r0[:, None]) & (rk[None, :] < r0[:, None] + NA_R)
        col_ok = (ck[None, :] >= ws[:, None]) & (ck[None, :] < ws[:, None] + NA_C)
        ok[p] = row_ok & col_ok
        dr[p] = np.clip(rk[None, :] - r[:, None] + NA_R - 1, 0, 2 * NA_R - 2)
        dc[p] = np.clip(ck[None, :] - c[:, None], -(NA_C - 1), NA_C - 1) + NA_C - 1
    return dr, dc, ok


def _na_bias_table(rpb):
    dr, dc, ok = _na_bias_index()
    qr, kr = BLK // GW, NA_KROWS
    dr_t = dr.reshape(3, qr, GW, kr, GW)[:, :, 0, :, 0].reshape(3 * qr * kr)
    dc_t = dc[0].reshape(qr, GW, kr, GW)[0, :, 0, :].reshape(GW * GW)
    oh_c = jnp.asarray(np.eye(2 * NA_C - 1, dtype=np.float32)[:, dc_t])
    oh_r = jnp.asarray(np.eye(2 * NA_R - 1, dtype=np.float32)[dr_t])
    hi = lax.Precision.HIGHEST
    t1 = jnp.einsum('hrd,dx->hrx', rpb.astype(F32), oh_c, precision=hi)
    t2 = jnp.einsum('yr,hrx->hyx', oh_r, t1, precision=hi)
    t2 = t2.reshape(NA_H, 3, qr, kr, GW, GW).transpose(0, 1, 2, 4, 3, 5).reshape(NA_H, 3, BLK, NA_KW)
    return jnp.where(ok[None], t2, NEG)


def na_attention(proj, rpb, qn, kn):
    bias = _na_bias_table(rpb)

    def pat(i):
        return jnp.where(i <= 1, 0, jnp.where(i == NBLK_L, 2, 1))

    cq, ck, cv = _COL_NAQ // NA_D, _COL_NAK // NA_D, _COL_NAV // NA_D
    return pl.pallas_call(
        _na_kernel,
        grid=(NB, NA_H, NBLK_L + 1),
        in_specs=[
            pl.BlockSpec((BLK, NA_D), lambda b, h, i: (_qblk(b, i), cq + h)),
            pl.BlockSpec((S, NA_D), lambda b, h, i: (b, ck + h)),
            pl.BlockSpec((L, NA_D), lambda b, h, i: (CTX_BLK0 + b, ck + h)),
            pl.BlockSpec((S, NA_D), lambda b, h, i: (b, cv + h)),
            pl.BlockSpec((L, NA_D), lambda b, h, i: (CTX_BLK0 + b, cv + h)),
            pl.BlockSpec((None, None, BLK, NA_KW), lambda b, h, i: (h, pat(i), 0, 0)),
            pl.BlockSpec((1, NA_D), lambda b, h, i: (0, 0)),
            pl.BlockSpec((1, NA_D), lambda b, h, i: (0, 0)),
        ],
        out_specs=pl.BlockSpec((BLK, NA_D), lambda b, h, i: (_qblk(b, i), h)),
        out_shape=jax.ShapeDtypeStruct((NT, NA_W), BF16),
        scratch_shapes=[pltpu.VMEM((L + S, NA_D), BF16)],
        compiler_params=_cp(("arbitrary", "arbitrary", "arbitrary")),
        name="na_attention",
    )(proj, proj, proj, proj, proj, bias, qn.reshape(1, NA_D), kn.reshape(1, NA_D))


MLA_HW = 256


def _half_mask(h, width=128):
    lane = lax.broadcasted_iota(jnp.int32, (1, width), 1)
    return (lane < 64) if h % 2 == 0 else (lane >= 64)


def _mla_q_kernel(cq_ref, w_ref, qa_ref, gn_ref, g2_ref, gs2_ref, c2_ref, s2_ref, q_ref):
    x = _rms_rows(cq_ref[...].astype(F32), qa_ref[...]).astype(BF16)
    y = _dot(x, w_ref[...])
    nw = MLA_H * MLA_NOPE
    gc = g2_ref[...] * c2_ref[...]
    gs = gs2_ref[...] * s2_ref[...]
    for p in range(MLA_H // 2):
        r1 = y[:, nw + 128 * p: nw + 128 * (p + 1)]
        r2 = y[:, nw + 768 + 128 * p: nw + 768 + 128 * (p + 1)]
        rot = r1 * gc + r2 * gs
        sq = r1 * r1
        for h in (2 * p, 2 * p + 1):
            msk = _half_mask(h)
            nope = y[:, 128 * h: 128 * (h + 1)]
            ss = jnp.sum(nope * nope, axis=-1, keepdims=True) + jnp.sum(
                jnp.where(msk, sq, 0.0), axis=-1, keepdims=True)
            inv = lax.rsqrt(ss / MLA_QK + EPS) * (MLA_QK ** -0.5)
            q_ref[h, :, 0:128] = (nope * gn_ref[...] * inv).astype(BF16)
            q_ref[h, :, 128:256] = (jnp.where(msk, rot, 0.0) * inv).astype(BF16)


def _mla_kv_kernel(ckv_ref, kr_ref, w_ref, kva_ref, gn_ref, g2_ref, gs2_ref, c2_ref, s2_ref,
                   k_ref, v_ref):
    x = _rms_rows(ckv_ref[...].astype(F32), kva_ref[...]).astype(BF16)
    y = _dot(x, w_ref[...])
    kr = kr_ref[...].astype(F32)
    r1 = kr[:, 0:128]
    r2 = kr[:, 128:256]
    rot = r1 * (g2_ref[...] * c2_ref[...]) + r2 * (gs2_ref[...] * s2_ref[...])
    ss_r = jnp.sum(jnp.where(_half_mask(0), r1 * r1, 0.0), axis=-1, keepdims=True)
    nw = MLA_H * MLA_NOPE
    for h in range(MLA_H):
        nope = y[:, 128 * h: 128 * (h + 1)]
        ss = jnp.sum(nope * nope, axis=-1, keepdims=True) + ss_r
        inv = lax.rsqrt(ss / MLA_QK + EPS)
        k_ref[h, :, 0:128] = (nope * gn_ref[...] * inv).astype(BF16)
        k_ref[h, :, 128:256] = (jnp.where(_half_mask(h), rot, 0.0) * inv).astype(BF16)
        v_ref[h] = y[:, nw + 128 * h: nw + 128 * (h + 1)].astype(BF16)


def _rope_tables():
    t = jnp.arange(S, dtype=jnp.int32)
    pos = (t // GW, t % GW)
    nf = MLA_ROPE // 4
    inv = ROPE_BASE ** (-jnp.arange(nf, dtype=F32) / nf)
    cs, sn = [], []
    for ax in range(2):
        ang = pos[ax].astype(F32)[:, None] * inv[None, :]
        c, s = jnp.cos(ang), jnp.sin(ang)
        cs += [c, c]
        sn += [-s, s]
    c64 = jnp.tile(jnp.concatenate(cs, axis=1), (NB, 1))
    s64 = jnp.tile(jnp.concatenate(sn, axis=1), (NB, 1))
    c64 = jnp.concatenate([c64, jnp.ones((NB * L, MLA_ROPE), F32)], axis=0)
    s64 = jnp.concatenate([s64, jnp.zeros((NB * L, MLA_ROPE), F32)], axis=0)
    return jnp.tile(c64, (1, 2)), jnp.tile(s64, (1, 2))


_ROPE_SWAP = np.concatenate([np.arange(16, 32), np.arange(0, 16), np.arange(48, 64), np.arange(32, 48)])


def _rope_gains(g):
    gr = g[MLA_NOPE:]
    return (g[:MLA_NOPE].reshape(1, 128), jnp.tile(gr, 2).reshape(1, 128),
            jnp.tile(gr[_ROPE_SWAP], 2).reshape(1, 128))


def mla_q_prep(proj, w_uq_r, qa_g, qn_g, c2, s2):
    tm = 512
    gn, g2, gs2 = _rope_gains(qn_g)
    vec = lambda w: pl.BlockSpec((1, w), lambda i: (0, 0))
    return pl.pallas_call(
        _mla_q_kernel,
        grid=(NT // tm,),
        in_specs=[
            pl.BlockSpec((tm, MLA_QL), lambda i: (i, _COL_CQ // MLA_QL)),
            pl.BlockSpec((MLA_QL, 3072), lambda i: (0, 0)),
            vec(MLA_QL), vec(128), vec(128), vec(128),
            pl.BlockSpec((tm, 128), lambda i: (i, 0)),
            pl.BlockSpec((tm, 128), lambda i: (i, 0)),
        ],
        out_specs=pl.BlockSpec((MLA_H, tm, MLA_HW), lambda i: (0, i, 0)),
        out_shape=jax.ShapeDtypeStruct((MLA_H, NT, MLA_HW), BF16),
        compiler_params=_cp(("arbitrary",)),
        name="mla_q_prep",
    )(proj, w_uq_r, qa_g.reshape(1, MLA_QL), gn, g2, gs2, c2, s2)


def mla_kv_prep(proj, w_ukv_r, kva_g, kn_g, c2, s2):
    tm = 512
    gn, g2, gs2 = _rope_gains(kn_g)
    vec = lambda w: pl.BlockSpec((1, w), lambda i: (0, 0))
    return pl.pallas_call(
        _mla_kv_kernel,
        grid=(NT // tm,),
        in_specs=[
            pl.BlockSpec((tm, MLA_KVL), lambda i: (i, _COL_CKV // MLA_KVL)),
            pl.BlockSpec((tm, 256), lambda i: (i, _COL_KR // 256)),
            pl.BlockSpec((MLA_KVL, 3072), lambda i: (0, 0)),
            vec(MLA_KVL), vec(128), vec(128), vec(128),
            pl.BlockSpec((tm, 128), lambda i: (i, 0)),
            pl.BlockSpec((tm, 128), lambda i: (i, 0)),
        ],
        out_specs=[pl.BlockSpec((MLA_H, tm, MLA_HW), lambda i: (0, i, 0)),
                   pl.BlockSpec((MLA_H, tm, MLA_V), lambda i: (0, i, 0))],
        out_shape=[jax.ShapeDtypeStruct((MLA_H, NT, MLA_HW), BF16),
                   jax.ShapeDtypeStruct((MLA_H, NT, MLA_V), BF16)],
        compiler_params=_cp(("arbitrary",)),
        name="mla_kv_prep",
    )(proj, proj, w_ukv_r, kva_g.reshape(1, MLA_KVL), gn, g2, gs2, c2, s2)


MLA_TQ = 512


def _mla_attn_kernel(q_ref, kl_ref, kc_ref, vl_ref, vc_ref, o_ref):
    for a in range(MLA_TQ // BLK):
        rows = slice(BLK * a, BLK * (a + 1))
        q = q_ref[rows, :]
        s_c = _dot_nt(q, kc_ref[...])
        s_l = _dot_nt(q, kl_ref[...])
        m = jnp.maximum(jnp.max(s_c, axis=-1, keepdims=True), jnp.max(s_l, axis=-1, keepdims=True))
        p_c = jnp.exp(s_c - m)
        p_l = jnp.exp(s_l - m)
        l = jnp.sum(p_c, axis=-1, keepdims=True) + jnp.sum(p_l, axis=-1, keepdims=True)
        o = _dot(p_l.astype(BF16), vl_ref[...]) + _dot(p_c.astype(BF16), vc_ref[...])
        o_ref[rows, :] = (o / l).astype(o_ref.dtype)


def _mla_ctx_kernel(q_ref, kc_ref, vc_ref, o_ref):
    s = _dot_nt(q_ref[...], kc_ref[...])
    p = jnp.exp(s - jnp.max(s, axis=-1, keepdims=True))
    l = jnp.sum(p, axis=-1, keepdims=True)
    o_ref[...] = (_dot(p.astype(BF16), vc_ref[...]) / l).astype(o_ref.dtype)


def mla_attention(q, k, v):
    ob = pl.pallas_call(
        _mla_attn_kernel,
        grid=(NB, MLA_H, S // MLA_TQ),
        in_specs=[
            pl.BlockSpec((None, MLA_TQ, MLA_HW), lambda b, h, i: (h, b * (S // MLA_TQ) + i, 0)),
            pl.BlockSpec((None, S, MLA_HW), lambda b, h, i: (h, b, 0)),
            pl.BlockSpec((None, L, MLA_HW), lambda b, h, i: (h, CTX_BLK0 + b, 0)),
            pl.BlockSpec((None, S, MLA_V), lambda b, h, i: (h, b, 0)),
            pl.BlockSpec((None, L, MLA_V), lambda b, h, i: (h, CTX_BLK0 + b, 0)),
        ],
        out_specs=pl.BlockSpec((MLA_TQ, MLA_V), lambda b, h, i: (b * (S // MLA_TQ) + i, h)),
        out_shape=jax.ShapeDtypeStruct((NL, MLA_W), BF16),
        compiler_params=_cp(("arbitrary", "arbitrary", "arbitrary")),
        name="mla_attention",
    )(q, k, k, v, v)
    ob_ctx = pl.pallas_call(
        _mla_ctx_kernel,
        grid=(NB, MLA_H),
        in_specs=[
            pl.BlockSpec((None, L, MLA_HW), lambda b, h: (h, CTX_BLK0 + b, 0)),
            pl.BlockSpec((None, L, MLA_HW), lambda b, h: (h, CTX_BLK0 + b, 0)),
            pl.BlockSpec((None, L, MLA_V), lambda b, h: (h, CTX_BLK0 + b, 0)),
        ],
        out_specs=pl.BlockSpec((L, MLA_V), lambda b, h: (b, h)),
        out_shape=jax.ShapeDtypeStruct((NB * L, MLA_W), BF16),
        compiler_params=_cp(("arbitrary", "arbitrary")),
        name="mla_attention_ctx",
    )(q, k, v)
    return ob, ob_ctx


N_CH_C = L // CH
N_CH_L = S // CH
N_CH = N_CH_C + N_CH_L


def _gla_kernel(qf_ref, kf_ref, vf_ref, af_ref, qb_ref, kb_ref, vb_ref, ab_ref,
                wg_ref, bg_ref, of_ref, ob_ref, st_ref):
    t = pl.program_id(1)

    @pl.when(t == 0)
    def _():
        st_ref[...] = jnp.zeros_like(st_ref)

    row = lax.broadcasted_iota(jnp.int32, (CH, CH), 0)
    col = lax.broadcasted_iota(jnp.int32, (CH, CH), 1)
    dirs = (
        (qf_ref, kf_ref, vf_ref, af_ref, of_ref, col <= row, CH - 1),
        (qb_ref, kb_ref, vb_ref, ab_ref, ob_ref, col >= row, 0),
    )
    for d, (q_ref, k_ref, v_ref, a_ref, o_ref, keep, last) in enumerate(dirs):
        z = _dot(a_ref[...], wg_ref[d]) + bg_ref[d]
        g = (jnp.minimum(z, 0.0) - jnp.log(1.0 + jnp.exp(-jnp.abs(z)))) / GLA_TAU
        b = jnp.dot(keep.astype(F32), g, preferred_element_type=F32,
                    precision=lax.Precision.HIGHEST)
        b_end = b[last:last + 1, :]
        q = q_ref[...].astype(F32) * (GLA_DK ** -0.5)
        k = k_ref[...].astype(F32)
        qd = (q * jnp.exp(b)).astype(BF16)
        kd = (k * jnp.exp(-b)).astype(BF16)
        ke = (k * jnp.exp(b_end - b)).astype(BF16)
        e_end = jnp.exp(b_end)
        v = v_ref[...]
        for h in range(GLA_H):
            ksl = slice(GLA_DK * h, GLA_DK * (h + 1))
            vsl = slice(GLA_DV * h, GLA_DV * (h + 1))
            att = jnp.where(keep, _dot_nt(qd[:, ksl], kd[:, ksl]), 0.0)
            st = st_ref[d, h]
            o = _dot_nt(qd[:, ksl], st.astype(BF16)) + _dot(att.astype(BF16), v[:, vsl])
            o_ref[:, vsl] = o
            st_ref[d, h] = st * e_end[:, ksl] + _dot_tn(v[:, vsl], ke[:, ksl])


def gla_scan(proj, wg_f, bg_f, wg_b, bg_b):
    wg = jnp.zeros((2, 128, GLA_KW), F32)
    wg = wg.at[0, 0:GLA_RANK].set(wg_f).at[1, GLA_RANK:2 * GLA_RANK].set(wg_b).astype(BF16)
    bg = jnp.stack([bg_f, bg_b]).reshape(2, 1, GLA_KW)

    ctx0 = NL // CH

    def fwd(b, t):
        return jnp.where(t < N_CH_C, ctx0 + N_CH_C * b + t, N_CH_L * b + t - N_CH_C)

    def bwd(b, t):
        return jnp.where(t < N_CH_C, ctx0 + N_CH_C * b + N_CH_C - 1 - t, N_CH_L * b + N_CH - 1 - t)

    def specs(rowfn):
        return [
            pl.BlockSpec((CH, GLA_KW), lambda b, t: (rowfn(b, t), _COL_GQ // GLA_KW)),
            pl.BlockSpec((CH, GLA_KW), lambda b, t: (rowfn(b, t), _COL_GK // GLA_KW)),
            pl.BlockSpec((CH, GLA_W), lambda b, t: (rowfn(b, t), _COL_GV // GLA_W)),
            pl.BlockSpec((CH, 128), lambda b, t: (rowfn(b, t), _COL_GA // 128)),
        ]

    return pl.pallas_call(
        _gla_kernel,
        grid=(NB, N_CH),
        in_specs=specs(fwd) + specs(bwd) + [
            pl.BlockSpec((2, 128, GLA_KW), lambda b, t: (0, 0, 0)),
            pl.BlockSpec((2, 1, GLA_KW), lambda b, t: (0, 0, 0)),
        ],
        out_specs=[pl.BlockSpec((CH, GLA_W), lambda b, t: (fwd(b, t), 0)),
                   pl.BlockSpec((CH, GLA_W), lambda b, t: (bwd(b, t), 0))],
        out_shape=[jax.ShapeDtypeStruct((NT, GLA_W), F32)] * 2,
        scratch_shapes=[pltpu.VMEM((2, GLA_H, GLA_DV, GLA_DK), F32)],
        compiler_params=_cp(("arbitrary", "arbitrary")),
        name="gla_scan",
    )(proj, proj, proj, proj, proj, proj, proj, proj, wg, bg)


def _gla_out_kernel(of_ref, ob_ref, g_ref, gn_ref, o_ref):
    for h in range(GLA_H):
        sl = slice(GLA_DV * h, GLA_DV * (h + 1))
        o = _rms_rows(of_ref[:, sl] + ob_ref[:, sl], gn_ref[...])
        gate = g_ref[:, sl].astype(F32)
        o_ref[:, sl] = (o * (gate * _sigmoid(gate))).astype(o_ref.dtype)


def gla_output(o_f, o_b, proj, on_g):
    tm = 512
    return pl.pallas_call(
        _gla_out_kernel,
        grid=(NT // tm,),
        in_specs=[
            pl.BlockSpec((tm, GLA_W), lambda i: (i, 0)),
            pl.BlockSpec((tm, GLA_W), lambda i: (i, 0)),
            pl.BlockSpec((tm, GLA_W), lambda i: (i, _COL_GG // GLA_W)),
            pl.BlockSpec((1, GLA_DV), lambda i: (0, 0)),
        ],
        out_specs=pl.BlockSpec((tm, GLA_W), lambda i: (i, 0)),
        out_shape=jax.ShapeDtypeStruct((NT, GLA_W), BF16),
        compiler_params=_cp(("arbitrary",)),
        name="gla_output",
    )(o_f, o_b, proj, on_g.reshape(1, GLA_DV))


OUT_TM = NB * L


def _out_proj_kernel(oa_ref, obl_ref, obc_ref, oc_ref, w_ref, x_ref, g_ref, o_ref):
    ob = jnp.where(pl.program_id(1) < NL // OUT_TM, obl_ref[...], obc_ref[...])
    a = jnp.concatenate([oa_ref[...], ob, oc_ref[...]], axis=1)
    o_ref[...] = x_ref[...] + g_ref[...] * _dot(a, w_ref[...])


def out_projection(oa, ob, ob_ctx, oc, w_out, layer, xs, mod, k_gate):
    tm, tn = OUT_TM, 1024
    nj = D // tn
    return pl.pallas_call(
        _out_proj_kernel,
        grid=(nj, NT // tm),
        in_specs=[
            pl.BlockSpec((tm, NA_W), lambda j, i: (i, 0)),
            pl.BlockSpec((tm, MLA_W), lambda j, i: (jnp.minimum(i, NL // tm - 1), 0)),
            pl.BlockSpec((tm, MLA_W), lambda j, i: (0, 0)),
            pl.BlockSpec((tm, GLA_W), lambda j, i: (i, 0)),
            pl.BlockSpec((None, D, tn), lambda j, i: (layer, 0, j)),
            pl.BlockSpec((tm, tn), lambda j, i: (i, j)),
            pl.BlockSpec((None, 1, tn), lambda j, i: (_row_group(i, tm), 0, k_gate * nj + j)),
        ],
        out_specs=pl.BlockSpec((tm, tn), lambda j, i: (i, j)),
        out_shape=jax.ShapeDtypeStruct((NT, D), F32),
        compiler_params=_cp(("arbitrary", "arbitrary")),
        name="out_projection",
    )(oa, ob, ob_ctx, oc, w_out, xs, mod)


def _route(logits, router_bias):
    scores = jax.nn.sigmoid(logits)
    grouped = (scores + router_bias.astype(F32)).reshape(NT, E_GROUPS, E_PER)

    def top2(a):
        idx = lax.broadcasted_iota(jnp.int32, a.shape, a.ndim - 1)
        i1 = jnp.argmax(a, axis=-1).astype(jnp.int32)
        rest = jnp.where(idx == i1[..., None], -jnp.inf, a)
        i2 = jnp.argmax(rest, axis=-1).astype(jnp.int32)
        return jnp.max(a, axis=-1), jnp.max(rest, axis=-1), i1, i2

    m1, m2, _, _ = top2(grouped)
    grp = jnp.argmax(m1 + m2, axis=-1).astype(jnp.int32)
    gsel = lax.broadcasted_iota(jnp.int32, (NT, E_GROUPS, E_PER), 1) == grp[:, None, None]
    in_group = jnp.sum(jnp.where(gsel, grouped, 0.0), axis=1)
    _, _, l1, l2 = top2(in_group)
    expert_idx = grp[:, None] * E_PER + jnp.stack([l1, l2], axis=-1)
    esel = lax.broadcasted_iota(jnp.int32, (NT, TOPK, E), 2) == expert_idx[:, :, None]
    w = jnp.sum(jnp.where(esel, scores[:, None, :], 0.0), axis=-1)
    return expert_idx, w / jnp.sum(w, axis=-1, keepdims=True)


def _dispatch_plan(expert_idx):
    flat_e = expert_idx.reshape(NK)
    onehot = (flat_e[:, None] == jnp.arange(E, dtype=jnp.int32)[None, :]).astype(jnp.int32)
    csum = jnp.cumsum(onehot, axis=0)
    counts = csum[-1]
    rank = jnp.take_along_axis(csum, flat_e[:, None], axis=1)[:, 0] - 1
    padded = (counts + MOE_R - 1) // MOE_R * MOE_R
    pad_end = jnp.cumsum(padded)
    pad_start = pad_end - padded
    dest = (pad_start[flat_e] + rank).astype(jnp.int32)
    row_token = jnp.zeros((MOE_P,), jnp.int32).at[dest].set(jnp.arange(NK, dtype=jnp.int32) // TOPK)
    block_expert = jnp.minimum(
        jnp.searchsorted(pad_end, jnp.arange(MOE_NB, dtype=jnp.int32) * MOE_R, side='right'),
        E - 1).astype(jnp.int32)
    n_active = (pad_end[-1] // MOE_R).astype(jnp.int32).reshape(1)
    blk0 = jnp.arange(MOE_NB, dtype=jnp.int32) * MOE_R
    n_valid = jnp.clip(counts[block_expert] - (blk0 - pad_start[block_expert]), 0, MOE_R).astype(jnp.int32)
    return dest, row_token, block_expert, n_active, n_valid


DH = D // 2


def _gather_rows_kernel(tok_ref, nv_ref, h_ref, o_ref, sem):
    i = pl.program_id(0)
    base = i * MOE_R
    nv = nv_ref[i]

    @pl.when(nv < MOE_R)
    def _():
        o_ref[...] = jnp.zeros_like(o_ref)

    def copy(r):
        return pltpu.make_async_copy(h_ref.at[pl.ds(tok_ref[base + r], 1)], o_ref.at[pl.ds(r, 1)], sem)

    def start(r, c):
        copy(r).start()
        return c

    def wait(r, c):
        copy(r).wait()
        return c

    lax.fori_loop(0, nv, start, 0)
    lax.fori_loop(0, nv, wait, 0)


def gather_rows(row_token, n_valid, h):
    return pl.pallas_call(
        _gather_rows_kernel,
        grid_spec=pltpu.PrefetchScalarGridSpec(
            num_scalar_prefetch=2,
            grid=(MOE_NB,),
            in_specs=[pl.BlockSpec(memory_space=pl.ANY)],
            out_specs=pl.BlockSpec((MOE_R, DH), lambda i, tok, nv: (i, 0)),
            scratch_shapes=[pltpu.SemaphoreType.DMA(())],
        ),
        out_shape=jax.ShapeDtypeStruct((MOE_P, DH), jnp.uint32),
        compiler_params=_cp(("arbitrary",)),
        name="moe_gather_rows",
    )(row_token, n_valid, h)


def _new_expert(be_ref, i):
    return jnp.logical_or(i == 0, be_ref[i] != be_ref[jnp.maximum(i - 1, 0)])


def _ffn_up_kernel(be_ref, na_ref, x_ref, w1_ref, w3_ref, o_ref, w1b_ref, w3b_ref):
    i = pl.program_id(1)
    active = i < na_ref[0]

    @pl.when(jnp.logical_and(active, _new_expert(be_ref, i)))
    def _():
        w1b_ref[...] = w1_ref[...].astype(BF16)
        w3b_ref[...] = w3_ref[...].astype(BF16)

    @pl.when(active)
    def _():
        lo, hi = _unpack_pairs(x_ref[...])
        lo, hi = lo.astype(BF16), hi.astype(BF16)
        a = _dot(lo, w1b_ref[0:DH, :]) + _dot(hi, w1b_ref[DH:, :])
        b = _dot(lo, w3b_ref[0:DH, :]) + _dot(hi, w3b_ref[DH:, :])
        o_ref[...] = (a * _sigmoid(a) * b).astype(o_ref.dtype)

    @pl.when(jnp.logical_not(active))
    def _():
        o_ref[...] = jnp.zeros_like(o_ref)


def ffn_up(block_expert, n_active, xb, w1, w3, layer):
    last = lambda i, na: jnp.minimum(i, na[0] - 1)
    return pl.pallas_call(
        _ffn_up_kernel,
        grid_spec=pltpu.PrefetchScalarGridSpec(
            num_scalar_prefetch=2,
            grid=(FF // MOE_TF, MOE_NB),
            in_specs=[
                pl.BlockSpec((MOE_R, DH), lambda j, i, be, na: (last(i, na), 0)),
                pl.BlockSpec((None, None, D, MOE_TF), lambda j, i, be, na: (layer, be[i], 0, j)),
                pl.BlockSpec((None, None, D, MOE_TF), lambda j, i, be, na: (layer, be[i], 0, j)),
            ],
            out_specs=pl.BlockSpec((MOE_R, MOE_TF), lambda j, i, be, na: (i, j)),
            scratch_shapes=[pltpu.VMEM((D, MOE_TF), BF16), pltpu.VMEM((D, MOE_TF), BF16)],
        ),
        out_shape=jax.ShapeDtypeStruct((MOE_P, FF), BF16),
        compiler_params=_cp(("arbitrary", "arbitrary")),
        name="moe_ffn_up",
    )(block_expert, n_active, xb, w1, w3)


MOE_TN = 1024


def _ffn_down_kernel(be_ref, na_ref, h_ref, w2_ref, o_ref, w2b_ref):
    i = pl.program_id(1)
    active = i < na_ref[0]

    @pl.when(jnp.logical_and(active, _new_expert(be_ref, i)))
    def _():
        w2b_ref[...] = w2_ref[...].astype(BF16)

    @pl.when(active)
    def _():
        y = _dot(h_ref[...], w2b_ref[...])
        o_ref[...] = _pack_pairs(y[:, :MOE_TN // 2], y[:, MOE_TN // 2:])

    @pl.when(jnp.logical_not(active))
    def _():
        o_ref[...] = jnp.zeros_like(o_ref)


def ffn_down(block_expert, n_active, hmid, w2, layer):
    last = lambda i, na: jnp.minimum(i, na[0] - 1)
    return pl.pallas_call(
        _ffn_down_kernel,
        grid_spec=pltpu.PrefetchScalarGridSpec(
            num_scalar_prefetch=2,
            grid=(D // MOE_TN, MOE_NB),
            in_specs=[
                pl.BlockSpec((MOE_R, FF), lambda n, i, be, na: (last(i, na), 0)),
                pl.BlockSpec((None, None, FF, MOE_TN), lambda n, i, be, na: (layer, be[i], 0, n)),
            ],
            out_specs=pl.BlockSpec((MOE_R, MOE_TN // 2), lambda n, i, be, na: (i, n)),
            scratch_shapes=[pltpu.VMEM((FF, MOE_TN), BF16)],
        ),
        out_shape=jax.ShapeDtypeStruct((MOE_P, DH), jnp.uint32),
        compiler_params=_cp(("arbitrary", "arbitrary")),
        name="moe_ffn_down",
    )(block_expert, n_active, hmid, w2)


CMB_T = 256


def _combine_kernel(pos_ref, yb_ref, gate_ref, x_ref, g_ref, o_ref, buf_ref, sem):
    base = pl.program_id(0) * CMB_T

    def copy(r, k):
        p = pos_ref[(base + r) * TOPK + k]
        return pltpu.make_async_copy(yb_ref.at[pl.ds(p, 1)], buf_ref.at[k, pl.ds(r, 1)], sem.at[k])

    def start(r, c):
        copy(r, 0).start()
        copy(r, 1).start()
        return c

    def wait(r, c):
        copy(r, 0).wait()
        copy(r, 1).wait()
        return c

    lax.fori_loop(0, CMB_T, start, 0)
    lax.fori_loop(0, CMB_T, wait, 0)
    gate = gate_ref[...]
    g0, g1 = gate[:, 0:1], gate[:, 1:2]
    half = MOE_TN // 2
    for n in range(D // MOE_TN):
        lo0, hi0 = _unpack_pairs(buf_ref[0, :, half * n: half * (n + 1)])
        lo1, hi1 = _unpack_pairs(buf_ref[1, :, half * n: half * (n + 1)])
        for part, y in enumerate((lo0 * g0 + lo1 * g1, hi0 * g0 + hi1 * g1)):
            sl = slice(MOE_TN * n + half * part, MOE_TN * n + half * (part + 1))
            o_ref[:, sl] = x_ref[:, sl] + g_ref[:, sl] * y


def moe_combine(dest, yb, gates, xs, mod, k_gate, out_rows):
    return pl.pallas_call(
        _combine_kernel,
        grid_spec=pltpu.PrefetchScalarGridSpec(
            num_scalar_prefetch=1,
            grid=(out_rows // CMB_T,),
            in_specs=[
                pl.BlockSpec(memory_space=pl.ANY),
                pl.BlockSpec((CMB_T, TOPK), lambda i, pos: (i, 0)),
                pl.BlockSpec((CMB_T, D), lambda i, pos: (i, 0)),
                pl.BlockSpec((None, 1, D), lambda i, pos: (_row_group(i, CMB_T), 0, k_gate)),
            ],
            out_specs=pl.BlockSpec((CMB_T, D), lambda i, pos: (i, 0)),
            scratch_shapes=[pltpu.VMEM((TOPK, CMB_T, DH), jnp.uint32), pltpu.SemaphoreType.DMA((TOPK,))],
        ),
        out_shape=jax.ShapeDtypeStruct((out_rows, D), F32),
        compiler_params=_cp(("arbitrary",)),
        name="moe_combine",
    )(dest, yb, gates, xs, mod)


def _w_in_layout(w):
    bf = lambda a: a.astype(BF16)
    kr = bf(w[:, :, 4608:4672])
    krs = kr[:, :, _ROPE_SWAP]
    return jnp.concatenate(
        [bf(w[:, :, 0:4608]), bf(w[:, :, 4672:9280]), kr, kr, krs, krs, bf(w[:, :, 9280:9312]),
         jnp.zeros((DEPTH, D, PW - _COL_GA - 2 * GLA_RANK), BF16)], axis=2)


def _w_uq_cols():
    nope = [MLA_QK * h + j for h in range(MLA_H) for j in range(MLA_NOPE)]
    rope = [MLA_QK * h + MLA_NOPE + d for h in range(MLA_H) for d in range(MLA_ROPE)]
    rope_s = [MLA_QK * h + MLA_NOPE + int(d) for h in range(MLA_H) for d in _ROPE_SWAP]
    return np.array(nope + rope + rope_s, np.int32)


def _w_ukv_cols():
    kn = [(MLA_NOPE + MLA_V) * h + j for h in range(MLA_H) for j in range(MLA_NOPE)]
    vv = [(MLA_NOPE + MLA_V) * h + MLA_NOPE + j for h in range(MLA_H) for j in range(MLA_V)]
    return np.array(kn + vv, np.int32)


def token_mixing_layer(xs, mod, layer, norm1, w_in_b, w_out_b, na_qn, na_kn, na_rpb, mla_qa, mla_kva, w_uq, w_ukv,
                       mla_qn, mla_kn, gwf, gbf, gwb, gbb, gla_on, c2, s2):
    h = norm_modulate(xs, norm1.reshape(1, D), mod, 0, 1, BF16)
    proj = matmul(h, w_in_b, layer, NT // 8, 512, BF16)
    oa = na_attention(proj, na_rpb, na_qn, na_kn)
    q = mla_q_prep(proj, w_uq[:, _w_uq_cols()].astype(BF16), mla_qa, mla_qn, c2, s2)
    k, v = mla_kv_prep(proj, w_ukv[:, _w_ukv_cols()].astype(BF16), mla_kva, mla_kn, c2, s2)
    ob, ob_ctx = mla_attention(q, k, v)
    o_f, o_b = gla_scan(proj, gwf, gbf, gwb, gbb)
    oc = gla_output(o_f, o_b, proj, gla_on)
    return out_projection(oa, ob, ob_ctx, oc, w_out_b, layer, xs, mod, 2)


def moe_layer(xs, mod, norm2, w_router_p, router_bias, w1, w3, w2, layer, out_rows):
    h, logits = norm_modulate(xs, norm2.reshape(1, D), mod, 3, 4, None, w_router_p)
    expert_idx, gates = _route(logits[:, :E], router_bias)
    dest, row_token, block_expert, n_active, n_valid = _dispatch_plan(expert_idx)
    xb = gather_rows(row_token, n_valid, h)
    hmid = ffn_up(block_expert, n_active, xb, w1, w3, layer)
    yb = ffn_down(block_expert, n_active, hmid, w2, layer)
    return moe_combine(dest, yb, gates, xs, mod, 5, out_rows)


def kernel(x, c, ctx, c_ctx, w_ada, b_ada, norm1, norm2, w_in, w_out, na_q_norm, na_k_norm, na_rpb, mla_qa_norm, mla_kva_norm, mla_w_uq, mla_w_ukv, mla_q_norm, mla_k_norm, gla_w_gate_f, gla_b_gate_f, gla_w_gate_b, gla_b_gate_b, gla_out_norm, w_router, router_bias, moe_w1, moe_w3, moe_w2):
    cond = jnp.concatenate([c, c_ctx[None, :], jnp.zeros((8 - NB - 1, D), F32)], axis=0)
    mods = ada_modulation(cond, w_ada, b_ada)
    xs = jnp.concatenate([x.reshape(NL, D), ctx.reshape(NB * L, D)], axis=0)
    c2, s2 = _rope_tables()
    w_router_p = jnp.concatenate([w_router, jnp.zeros((D, 128 - E), F32)], axis=1)
    w_in_b = _w_in_layout(w_in)
    w_out_b = w_out.astype(BF16)
    for l in range(DEPTH):
        mod = mods[l].reshape(8, 1, 6 * D)
        xs = token_mixing_layer(xs, mod, l, norm1[l], w_in_b, w_out_b, na_q_norm[l], na_k_norm[l], na_rpb[l],
                                mla_qa_norm[l], mla_kva_norm[l], mla_w_uq[l], mla_w_ukv[l],
                                mla_q_norm[l], mla_k_norm[l], gla_w_gate_f[l], gla_b_gate_f[l],
                                gla_w_gate_b[l], gla_b_gate_b[l], gla_out_norm[l], c2, s2)
        xs = moe_layer(xs, mod, norm2[l], w_router_p, router_bias, moe_w1, moe_w3, moe_w2, l,
                       NT if l < DEPTH - 1 else NL)
    return xs.reshape(NB, S, D)
```

```python
import functools

import numpy as np
import jax
import jax.numpy as jnp
from jax import lax
from jax.experimental import pallas as pl
from jax.experimental.pallas import tpu as pltpu

F32 = jnp.float32
BF16 = jnp.bfloat16

D = 4096
NB = 2
S = 4096
L = 256
DEPTH = 2
GW = 64
EPS = 1e-6
NL = NB * S
NT = NL + NB * L
NA_H, NA_D = 8, 128
NA_W = NA_H * NA_D
NA_R, NA_C = 8, 16
MLA_H = 12
MLA_QL, MLA_KVL = 1024, 512
MLA_NOPE, MLA_ROPE, MLA_V = 128, 64, 128
MLA_QK = MLA_NOPE + MLA_ROPE
MLA_W = MLA_H * MLA_V
GLA_H, GLA_DK, GLA_DV = 6, 128, 256
GLA_KW = GLA_H * GLA_DK
GLA_W = GLA_H * GLA_DV
GLA_RANK = 16
GLA_TAU = 16.0
CH = 64
E = 16
E_GROUPS = 4
E_PER = E // E_GROUPS
TOPK = 2
FF = 1024
ROPE_BASE = 10000.0

_COL_NAQ, _COL_NAK, _COL_NAV = 0, 1024, 2048
_COL_CQ, _COL_CKV = 3072, 4096
_COL_GQ, _COL_GK, _COL_GV, _COL_GG = 4608, 5376, 6144, 7680
_COL_KR = 9216
_COL_GA = 9472
PW = 9728

BLK = 256
NBLK_L = S // BLK
CTX_BLK0 = NL // BLK
MOE_R = 512
MOE_TF = 256
NK = NT * TOPK
MOE_NB = (NK + E * (MOE_R - 1)) // MOE_R
MOE_P = MOE_NB * MOE_R

NEG = -1e30
VMEM_LIMIT = 56 * 1024 * 1024


def _cp(sem, vmem=VMEM_LIMIT):
    return pltpu.CompilerParams(dimension_semantics=sem, vmem_limit_bytes=vmem)


def _dot(a, b):
    return jnp.dot(a, b, preferred_element_type=F32)


def _dot_nt(a, b):
    return lax.dot_general(a, b, (((1,), (1,)), ((), ())), preferred_element_type=F32)


def _dot_tn(a, b):
    return lax.dot_general(a, b, (((0,), (0,)), ((), ())), preferred_element_type=F32)


def _sigmoid(x):
    return 1.0 / (1.0 + jnp.exp(-x))


def _pack_pairs(lo, hi):
    lo_w = lax.bitcast_convert_type(lo.astype(BF16).astype(F32), jnp.uint32)
    hi_w = lax.bitcast_convert_type(hi.astype(BF16).astype(F32), jnp.uint32)
    return lax.shift_right_logical(lo_w, jnp.uint32(16)) | (hi_w & jnp.uint32(0xFFFF0000))


def _unpack_pairs(w):
    lo = lax.bitcast_convert_type(lax.shift_left(w, jnp.uint32(16)), F32)
    hi = lax.bitcast_convert_type(w & jnp.uint32(0xFFFF0000), F32)
    return lo, hi


def _row_group(i, tm):
    r0 = i * tm
    return jnp.where(r0 >= NL, 2, r0 // S)


def _ada_kernel(s_ref, w_ref, b_ref, o_ref):
    s = s_ref[...]
    s = s * _sigmoid(s)
    o_ref[...] = _dot(s.astype(BF16), w_ref[...].astype(BF16)) + b_ref[...]


def ada_modulation(cond, w_ada, b_ada):
    tn = 512
    n = 6 * D
    return pl.pallas_call(
        _ada_kernel,
        grid=(DEPTH, n // tn),
        in_specs=[
            pl.BlockSpec((8, D), lambda l, j: (0, 0)),
            pl.BlockSpec((None, D, tn), lambda l, j: (l, 0, j)),
            pl.BlockSpec((None, 1, tn), lambda l, j: (l, 0, j)),
        ],
        out_specs=pl.BlockSpec((None, 8, tn), lambda l, j: (l, 0, j)),
        out_shape=jax.ShapeDtypeStruct((DEPTH, 8, n), F32),
        compiler_params=_cp(("arbitrary", "arbitrary")),
        name="ada_modulation",
    )(cond, w_ada, b_ada.reshape(DEPTH, 1, n))


def _normmod(x, g, sh, sc):
    y = x * lax.rsqrt(jnp.mean(x * x, axis=-1, keepdims=True) + EPS) * g
    return y * (1.0 + sc) + sh


def _normmod_kernel(x_ref, g_ref, sh_ref, sc_ref, h_ref):
    h_ref[...] = _normmod(x_ref[...], g_ref[...], sh_ref[...], sc_ref[...]).astype(h_ref.dtype)


def _normmod_router_kernel(x_ref, g_ref, sh_ref, sc_ref, wr_ref, h_ref, lg_ref):
    h = _normmod(x_ref[...], g_ref[...], sh_ref[...], sc_ref[...])
    h_ref[...] = _pack_pairs(h[:, :D // 2], h[:, D // 2:])
    lg_ref[...] = jnp.dot(h, wr_ref[...], preferred_element_type=F32,
                          precision=lax.Precision.HIGHEST)


def norm_modulate(xs, gain, mod, k_shift, k_scale, out_dtype, w_router=None):
    tm = 512
    in_specs = [
        pl.BlockSpec((tm, D), lambda i: (i, 0)),
        pl.BlockSpec((1, D), lambda i: (0, 0)),
        pl.BlockSpec((None, 1, D), lambda i: (_row_group(i, tm), 0, k_shift)),
        pl.BlockSpec((None, 1, D), lambda i: (_row_group(i, tm), 0, k_scale)),
    ]
    if w_router is None:
        return pl.pallas_call(
            _normmod_kernel, grid=(NT // tm,), in_specs=in_specs,
            out_specs=pl.BlockSpec((tm, D), lambda i: (i, 0)),
            out_shape=jax.ShapeDtypeStruct((NT, D), out_dtype),
            compiler_params=_cp(("arbitrary",)), name="norm_modulate",
        )(xs, gain, mod, mod)
    return pl.pallas_call(
        _normmod_router_kernel, grid=(NT // tm,),
        in_specs=in_specs + [pl.BlockSpec((D, 128), lambda i: (0, 0))],
        out_specs=[pl.BlockSpec((tm, D // 2), lambda i: (i, 0)), pl.BlockSpec((tm, 128), lambda i: (i, 0))],
        out_shape=[jax.ShapeDtypeStruct((NT, D // 2), jnp.uint32), jax.ShapeDtypeStruct((NT, 128), F32)],
        compiler_params=_cp(("arbitrary",)), name="norm_modulate_router",
    )(xs, gain, mod, mod, w_router)


def _mm_kernel(a_ref, b_ref, o_ref):
    o_ref[...] = _dot(a_ref[...], b_ref[...]).astype(o_ref.dtype)


def matmul(a, b, layer, tm, tn, out_dtype):
    m, k = a.shape
    n = b.shape[2]
    return pl.pallas_call(
        _mm_kernel,
        grid=(m // tm, n // tn),
        in_specs=[pl.BlockSpec((tm, k), lambda i, j: (i, 0)),
                  pl.BlockSpec((None, k, tn), lambda i, j: (layer, 0, j))],
        out_specs=pl.BlockSpec((tm, tn), lambda i, j: (i, j)),
        out_shape=jax.ShapeDtypeStruct((m, n), out_dtype),
        compiler_params=_cp(("arbitrary", "arbitrary")),
        name="matmul",
    )(a, b)


def _qblk(b, i):
    return jnp.where(i == 0, CTX_BLK0 + b, b * NBLK_L + i - 1)


def _rms_rows(x, g):
    return x * lax.rsqrt(jnp.mean(x * x, axis=-1, keepdims=True) + EPS) * g


NA_KROWS = 12
NA_KW = NA_KROWS * GW


def _na_kernel(q_ref, kl_ref, kc_ref, vl_ref, vc_ref, bias_ref, qn_ref, kn_ref, o_ref, ks_ref):
    i = pl.program_id(2)

    @pl.when(i == 0)
    def _():
        ks_ref[0:L, :] = _rms_rows(kc_ref[...].astype(F32), kn_ref[...]).astype(BF16)
        ks_ref[L:, :] = _rms_rows(kl_ref[...].astype(F32), kn_ref[...]).astype(BF16)

    q = (_rms_rows(q_ref[...].astype(F32), qn_ref[...]) * (NA_D ** -0.5)).astype(BF16)
    s_c = _dot_nt(q, ks_ref[0:L, :])
    m_c = jnp.max(s_c, axis=-1, keepdims=True)

    @pl.when(i == 0)
    def _():
        p = jnp.exp(s_c - m_c)
        l = jnp.sum(p, axis=-1, keepdims=True)
        o_ref[...] = (_dot(p.astype(BF16), vc_ref[...]) / l).astype(o_ref.dtype)

    @pl.when(i > 0)
    def _():
        row0 = jnp.clip(4 * (i - 1) - 4, 0, GW - NA_KROWS)
        k0 = pl.multiple_of(row0 * GW, BLK)
        s_l = _dot_nt(q, ks_ref[pl.ds(L + k0, NA_KW), :]) + bias_ref[...]
        m = jnp.maximum(m_c, jnp.max(s_l, axis=-1, keepdims=True))
        p_c = jnp.exp(s_c - m)
        p_l = jnp.exp(s_l - m)
        l = jnp.sum(p_c, axis=-1, keepdims=True) + jnp.sum(p_l, axis=-1, keepdims=True)
        o = _dot(p_l.astype(BF16), vl_ref[pl.ds(k0, NA_KW), :]) + _dot(p_c.astype(BF16), vc_ref[...])
        o_ref[...] = (o / l).astype(o_ref.dtype)


def _na_bias_index():
    dr = np.zeros((3, BLK, NA_KW), np.int32)
    dc = np.zeros((3, BLK, NA_KW), np.int32)
    ok = np.zeros((3, BLK, NA_KW), bool)
    rows = S // GW
    for p, blk in enumerate((0, 5, NBLK_L - 1)):
        ks = int(np.clip(4 * blk - 4, 0, GW - NA_KROWS))
        r = 4 * blk + np.arange(BLK) // GW
        c = np.arange(BLK) % GW
        rk = ks + np.arange(NA_KW) // GW
        ck = np.arange(NA_KW) % GW
        r0 = np.clip(r - NA_R // 2, 0, rows - NA_R)
        ws = np.clip(c - NA_C // 2, 0, GW - NA_C)
        row_ok = (rk[None, :] >= r0[:, None]) & (rk[None, :] < r0[:, None] + NA_R)
        col_ok = (ck[None, :] >= ws[:, None]) & (ck[None, :] < ws[:, None] + NA_C)
        ok[p] = row_ok & col_ok
        dr[p] = np.clip(rk[None, :] - r[:, None] + NA_R - 1, 0, 2 * NA_R - 2)
        dc[p] = np.clip(ck[None, :] - c[:, None], -(NA_C - 1), NA_C - 1) + NA_C - 1
    return dr, dc, ok


def _na_bias_table(rpb):
    dr, dc, ok = _na_bias_index()
    qr, kr = BLK // GW, NA_KROWS
    dr_t = dr.reshape(3, qr, GW, kr, GW)[:, :, 0, :, 0].reshape(3 * qr * kr)
    dc_t = dc[0].reshape(qr, GW, kr, GW)[0, :, 0, :].reshape(GW * GW)
    oh_c = jnp.asarray(np.eye(2 * NA_C - 1, dtype=np.float32)[:, dc_t])
    oh_r = jnp.asarray(np.eye(2 * NA_R - 1, dtype=np.float32)[dr_t])
    hi = lax.Precision.HIGHEST
    t1 = jnp.einsum('hrd,dx->hrx', rpb.astype(F32), oh_c, precision=hi)
    t2 = jnp.einsum('yr,hrx->hyx', oh_r, t1, precision=hi)
    t2 = t2.reshape(NA_H, 3, qr, kr, GW, GW).transpose(0, 1, 2, 4, 3, 5).reshape(NA_H, 3, BLK, NA_KW)
    return jnp.where(ok[None], t2, NEG)


def na_attention(proj, rpb, qn, kn):
    bias = _na_bias_table(rpb)

    def pat(i):
        return jnp.where(i <= 1, 0, jnp.where(i == NBLK_L, 2, 1))

    cq, ck, cv = _COL_NAQ // NA_D, _COL_NAK // NA_D, _COL_NAV // NA_D
    return pl.pallas_call(
        _na_kernel,
        grid=(NB, NA_H, NBLK_L + 1),
        in_specs=[
            pl.BlockSpec((BLK, NA_D), lambda b, h, i: (_qblk(b, i), cq + h)),
            pl.BlockSpec((S, NA_D), lambda b, h, i: (b, ck + h)),
            pl.BlockSpec((L, NA_D), lambda b, h, i: (CTX_BLK0 + b, ck + h)),
            pl.BlockSpec((S, NA_D), lambda b, h, i: (b, cv + h)),
            pl.BlockSpec((L, NA_D), lambda b, h, i: (CTX_BLK0 + b, cv + h)),
            pl.BlockSpec((None, None, BLK, NA_KW), lambda b, h, i: (h, pat(i), 0, 0)),
            pl.BlockSpec((1, NA_D), lambda b, h, i: (0, 0)),
            pl.BlockSpec((1, NA_D), lambda b, h, i: (0, 0)),
        ],
        out_specs=pl.BlockSpec((BLK, NA_D), lambda b, h, i: (_qblk(b, i), h)),
        out_shape=jax.ShapeDtypeStruct((NT, NA_W), BF16),
        scratch_shapes=[pltpu.VMEM((L + S, NA_D), BF16)],
        compiler_params=_cp(("arbitrary", "arbitrary", "arbitrary")),
        name="na_attention",
    )(proj, proj, proj, proj, proj, bias, qn.reshape(1, NA_D), kn.reshape(1, NA_D))


MLA_HW = 256


def _half_mask(h, width=128):
    lane = lax.broadcasted_iota(jnp.int32, (1, width), 1)
    return (lane < 64) if h % 2 == 0 else (lane >= 64)


def _mla_q_kernel(cq_ref, w_ref, qa_ref, gn_ref, g2_ref, gs2_ref, c2_ref, s2_ref, q_ref):
    x = _rms_rows(cq_ref[...].astype(F32), qa_ref[...]).astype(BF16)
    y = _dot(x, w_ref[...])
    nw = MLA_H * MLA_NOPE
    gc = g2_ref[...] * c2_ref[...]
    gs = gs2_ref[...] * s2_ref[...]
    for p in range(MLA_H // 2):
        r1 = y[:, nw + 128 * p: nw + 128 * (p + 1)]
        r2 = y[:, nw + 768 + 128 * p: nw + 768 + 128 * (p + 1)]
        rot = r1 * gc + r2 * gs
        sq = r1 * r1
        for h in (2 * p, 2 * p + 1):
            msk = _half_mask(h)
            nope = y[:, 128 * h: 128 * (h + 1)]
            ss = jnp.sum(nope * nope, axis=-1, keepdims=True) + jnp.sum(
                jnp.where(msk, sq, 0.0), axis=-1, keepdims=True)
            inv = lax.rsqrt(ss / MLA_QK + EPS) * (MLA_QK ** -0.5)
            q_ref[h, :, 0:128] = (nope * gn_ref[...] * inv).astype(BF16)
            q_ref[h, :, 128:256] = (jnp.where(msk, rot, 0.0) * inv).astype(BF16)


def _mla_kv_kernel(ckv_ref, kr_ref, w_ref, kva_ref, gn_ref, g2_ref, gs2_ref, c2_ref, s2_ref,
                   k_ref, v_ref):
    x = _rms_rows(ckv_ref[...].astype(F32), kva_ref[...]).astype(BF16)
    y = _dot(x, w_ref[...])
    kr = kr_ref[...].astype(F32)
    r1 = kr[:, 0:128]
    r2 = kr[:, 128:256]
    rot = r1 * (g2_ref[...] * c2_ref[...]) + r2 * (gs2_ref[...] * s2_ref[...])
    ss_r = jnp.sum(jnp.where(_half_mask(0), r1 * r1, 0.0), axis=-1, keepdims=True)
    nw = MLA_H * MLA_NOPE
    lane0 = lax.broadcasted_iota(jnp.int32, (kr.shape[0], 128), 1) == 0
    for h in range(MLA_H):
        nope = y[:, 128 * h: 128 * (h + 1)]
        ss = jnp.sum(nope * nope, axis=-1, keepdims=True) + ss_r
        inv = lax.rsqrt(ss / MLA_QK + EPS)
        k_ref[h, :, 0:128] = (nope * gn_ref[...] * inv).astype(BF16)
        k_ref[h, :, 128:256] = (jnp.where(_half_mask(h), rot, 0.0) * inv).astype(BF16)
        v_ref[h, :, 0:128] = y[:, nw + 128 * h: nw + 128 * (h + 1)].astype(BF16)
        v_ref[h, :, 128:256] = jnp.where(lane0, 1.0, 0.0).astype(BF16)


def _rope_tables():
    t = jnp.arange(S, dtype=jnp.int32)
    pos = (t // GW, t % GW)
    nf = MLA_ROPE // 4
    inv = ROPE_BASE ** (-jnp.arange(nf, dtype=F32) / nf)
    cs, sn = [], []
    for ax in range(2):
        ang = pos[ax].astype(F32)[:, None] * inv[None, :]
        c, s = jnp.cos(ang), jnp.sin(ang)
        cs += [c, c]
        sn += [-s, s]
    c64 = jnp.tile(jnp.concatenate(cs, axis=1), (NB, 1))
    s64 = jnp.tile(jnp.concatenate(sn, axis=1), (NB, 1))
    c64 = jnp.concatenate([c64, jnp.ones((NB * L, MLA_ROPE), F32)], axis=0)
    s64 = jnp.concatenate([s64, jnp.zeros((NB * L, MLA_ROPE), F32)], axis=0)
    return jnp.tile(c64, (1, 2)), jnp.tile(s64, (1, 2))


_ROPE_SWAP = np.concatenate([np.arange(16, 32), np.arange(0, 16), np.arange(48, 64), np.arange(32, 48)])


def _rope_gains(g):
    gr = g[MLA_NOPE:]
    return (g[:MLA_NOPE].reshape(1, 128), jnp.tile(gr, 2).reshape(1, 128),
            jnp.tile(gr[_ROPE_SWAP], 2).reshape(1, 128))


def mla_q_prep(proj, w_uq_r, qa_g, qn_g, c2, s2):
    tm = 512
    gn, g2, gs2 = _rope_gains(qn_g)
    vec = lambda w: pl.BlockSpec((1, w), lambda i: (0, 0))
    return pl.pallas_call(
        _mla_q_kernel,
        grid=(NT // tm,),
        in_specs=[
            pl.BlockSpec((tm, MLA_QL), lambda i: (i, _COL_CQ // MLA_QL)),
            pl.BlockSpec((MLA_QL, 3072), lambda i: (0, 0)),
            vec(MLA_QL), vec(128), vec(128), vec(128),
            pl.BlockSpec((tm, 128), lambda i: (i, 0)),
            pl.BlockSpec((tm, 128), lambda i: (i, 0)),
        ],
        out_specs=pl.BlockSpec((MLA_H, tm, MLA_HW), lambda i: (0, i, 0)),
        out_shape=jax.ShapeDtypeStruct((MLA_H, NT, MLA_HW), BF16),
        compiler_params=_cp(("arbitrary",)),
        name="mla_q_prep",
    )(proj, w_uq_r, qa_g.reshape(1, MLA_QL), gn, g2, gs2, c2, s2)


def mla_kv_prep(proj, w_ukv_r, kva_g, kn_g, c2, s2):
    tm = 512
    gn, g2, gs2 = _rope_gains(kn_g)
    vec = lambda w: pl.BlockSpec((1, w), lambda i: (0, 0))
    return pl.pallas_call(
        _mla_kv_kernel,
        grid=(NT // tm,),
        in_specs=[
            pl.BlockSpec((tm, MLA_KVL), lambda i: (i, _COL_CKV // MLA_KVL)),
            pl.BlockSpec((tm, 256), lambda i: (i, _COL_KR // 256)),
            pl.BlockSpec((MLA_KVL, 3072), lambda i: (0, 0)),
            vec(MLA_KVL), vec(128), vec(128), vec(128),
            pl.BlockSpec((tm, 128), lambda i: (i, 0)),
            pl.BlockSpec((tm, 128), lambda i: (i, 0)),
        ],
        out_specs=[pl.BlockSpec((MLA_H, tm, MLA_HW), lambda i: (0, i, 0)),
                   pl.BlockSpec((MLA_H, tm, MLA_HW), lambda i: (0, i, 0))],
        out_shape=[jax.ShapeDtypeStruct((MLA_H, NT, MLA_HW), BF16),
                   jax.ShapeDtypeStruct((MLA_H, NT, MLA_HW), BF16)],
        compiler_params=_cp(("arbitrary",)),
        name="mla_kv_prep",
    )(proj, proj, w_ukv_r, kva_g.reshape(1, MLA_KVL), gn, g2, gs2, c2, s2)


MLA_TQ = 512


def _mla_attn_kernel(q_ref, kl_ref, kc_ref, vl_ref, vc_ref, o_ref, s0_ref, s1_ref, p0_ref, p1_ref, acc_ref):
    g = pl.program_id(0)

    @pl.when(g == 0)
    def _():
        s0_ref[...] = jnp.zeros_like(s0_ref)
        s1_ref[...] = jnp.zeros_like(s1_ref)
        p0_ref[...] = jnp.ones_like(p0_ref)
        p1_ref[...] = jnp.ones_like(p1_ref)

    n_kc = (L + S) // L
    rows = MLA_TQ // (n_kc - 1)

    def tie(x, dep):
        if dep is None:
            return x
        return jnp.concatenate([x[0:16, :] + dep, x[16:, :]], axis=0)

    def stages(s_new, s_old, p_new, p_old):
        acc, dep = None, None
        for c in range(n_kc):
            keys = slice(L * c, L * (c + 1))
            k = kc_ref[...] if c == 0 else kl_ref[L * (c - 1): L * c, :]
            v = vc_ref[...] if c == 0 else vl_ref[L * (c - 1): L * c, :]
            if c < n_kc - 1:
                r = slice(rows * c, rows * (c + 1))
                s = s_old[r, :]
                p = jnp.exp((s - jnp.max(s, axis=-1, keepdims=True)).astype(BF16))
                p_old[r, :] = p
                dep = (p[0:16, 0:MLA_HW].astype(F32) * 0.0).astype(BF16)
            part = _dot(tie(p_new[:, keys], dep), v)
            acc = part if acc is None else acc + part
            s_new[:, keys] = _dot_nt(tie(q_ref[...], dep), k)
        o_ref[...] = (acc[:, :MLA_V] / acc[:, MLA_V:MLA_V + 1]).astype(o_ref.dtype)

    @pl.when(g % 2 == 0)
    def _():
        stages(s0_ref, s1_ref, p0_ref, p1_ref)

    @pl.when(g % 2 == 1)
    def _():
        stages(s1_ref, s0_ref, p1_ref, p0_ref)


def _mla_ctx_kernel(q_ref, kc_ref, vc_ref, o_ref):
    s = _dot_nt(q_ref[...], kc_ref[...])
    p = jnp.exp((s - jnp.max(s, axis=-1, keepdims=True)).astype(BF16))
    acc = _dot(p, vc_ref[...])
    o_ref[...] = (acc[:, :MLA_V] / acc[:, MLA_V:MLA_V + 1]).astype(o_ref.dtype)


def mla_attention(q, k, v):
    nq = S // MLA_TQ
    n_steps = NB * MLA_H * nq

    def bhi(g):
        return g // (MLA_H * nq), (g // nq) % MLA_H, g % nq

    def cur(g):
        return bhi(jnp.minimum(g, n_steps - 1))

    def prev(g):
        return bhi(jnp.maximum(g - 2, 0))

    ob = pl.pallas_call(
        _mla_attn_kernel,
        grid=(n_steps + 2,),
        in_specs=[
            pl.BlockSpec((None, MLA_TQ, MLA_HW), lambda g: (cur(g)[1], cur(g)[0] * nq + cur(g)[2], 0)),
            pl.BlockSpec((None, S, MLA_HW), lambda g: (cur(g)[1], cur(g)[0], 0)),
            pl.BlockSpec((None, L, MLA_HW), lambda g: (cur(g)[1], CTX_BLK0 + cur(g)[0], 0)),
            pl.BlockSpec((None, S, MLA_HW), lambda g: (prev(g)[1], prev(g)[0], 0)),
            pl.BlockSpec((None, L, MLA_HW), lambda g: (prev(g)[1], CTX_BLK0 + prev(g)[0], 0)),
        ],
        out_specs=pl.BlockSpec((MLA_TQ, MLA_V), lambda g: (prev(g)[0] * nq + prev(g)[2], prev(g)[1])),
        out_shape=jax.ShapeDtypeStruct((NL, MLA_W), BF16),
        scratch_shapes=[pltpu.VMEM((MLA_TQ, L + S), F32), pltpu.VMEM((MLA_TQ, L + S), F32),
                        pltpu.VMEM((MLA_TQ, L + S), BF16), pltpu.VMEM((MLA_TQ, L + S), BF16),
                        pltpu.VMEM((MLA_TQ, MLA_HW), F32)],
        compiler_params=_cp(("arbitrary",)),
        name="mla_attention",
    )(q, k, k, v, v)
    ob_ctx = pl.pallas_call(
        _mla_ctx_kernel,
        grid=(NB, MLA_H),
        in_specs=[
            pl.BlockSpec((None, L, MLA_HW), lambda b, h: (h, CTX_BLK0 + b, 0)),
            pl.BlockSpec((None, L, MLA_HW), lambda b, h: (h, CTX_BLK0 + b, 0)),
            pl.BlockSpec((None, L, MLA_HW), lambda b, h: (h, CTX_BLK0 + b, 0)),
        ],
        out_specs=pl.BlockSpec((L, MLA_V), lambda b, h: (b, h)),
        out_shape=jax.ShapeDtypeStruct((NB * L, MLA_W), BF16),
        compiler_params=_cp(("arbitrary", "arbitrary")),
        name="mla_attention_ctx",
    )(q, k, v)
    return ob, ob_ctx


N_CH_C = L // CH
N_CH_L = S // CH
N_CH = N_CH_C + N_CH_L


def _gla_kernel(qf_ref, kf_ref, vf_ref, af_ref, qb_ref, kb_ref, vb_ref, ab_ref,
                wg_ref, bg_ref, of_ref, ob_ref, st_ref):
    t = pl.program_id(1)

    @pl.when(t == 0)
    def _():
        st_ref[...] = jnp.zeros_like(st_ref)

    row = lax.broadcasted_iota(jnp.int32, (CH, CH), 0)
    col = lax.broadcasted_iota(jnp.int32, (CH, CH), 1)
    dirs = (
        (qf_ref, kf_ref, vf_ref, af_ref, of_ref, col <= row, CH - 1),
        (qb_ref, kb_ref, vb_ref, ab_ref, ob_ref, col >= row, 0),
    )
    for d, (q_ref, k_ref, v_ref, a_ref, o_ref, keep, last) in enumerate(dirs):
        z = _dot(a_ref[...], wg_ref[d]) + bg_ref[d]
        g = (jnp.minimum(z, 0.0) - jnp.log(1.0 + jnp.exp(-jnp.abs(z)))) / GLA_TAU
        b = jnp.dot(keep.astype(F32), g, preferred_element_type=F32,
                    precision=lax.Precision.HIGHEST)
        b_end = b[last:last + 1, :]
        q = q_ref[...].astype(F32) * (GLA_DK ** -0.5)
        k = k_ref[...].astype(F32)
        qd = (q * jnp.exp(b)).astype(BF16)
        kd = (k * jnp.exp(-b)).astype(BF16)
        ke = (k * jnp.exp(b_end - b)).astype(BF16)
        e_end = jnp.exp(b_end)
        v = v_ref[...]
        for h in range(GLA_H):
            ksl = slice(GLA_DK * h, GLA_DK * (h + 1))
            vsl = slice(GLA_DV * h, GLA_DV * (h + 1))
            att = jnp.where(keep, _dot_nt(qd[:, ksl], kd[:, ksl]), 0.0)
            st = st_ref[d, h]
            o = _dot_nt(qd[:, ksl], st.astype(BF16)) + _dot(att.astype(BF16), v[:, vsl])
            o_ref[:, vsl] = o
            st_ref[d, h] = st * e_end[:, ksl] + _dot_tn(v[:, vsl], ke[:, ksl])


def gla_scan(proj, wg_f, bg_f, wg_b, bg_b):
    wg = jnp.zeros((2, 128, GLA_KW), F32)
    wg = wg.at[0, 0:GLA_RANK].set(wg_f).at[1, GLA_RANK:2 * GLA_RANK].set(wg_b).astype(BF16)
    bg = jnp.stack([bg_f, bg_b]).reshape(2, 1, GLA_KW)

    ctx0 = NL // CH

    def fwd(b, t):
        return jnp.where(t < N_CH_C, ctx0 + N_CH_C * b + t, N_CH_L * b + t - N_CH_C)

    def bwd(b, t):
        return jnp.where(t < N_CH_C, ctx0 + N_CH_C * b + N_CH_C - 1 - t, N_CH_L * b + N_CH - 1 - t)

    def specs(rowfn):
        return [
            pl.BlockSpec((CH, GLA_KW), lambda b, t: (rowfn(b, t), _COL_GQ // GLA_KW)),
            pl.BlockSpec((CH, GLA_KW), lambda b, t: (rowfn(b, t), _COL_GK // GLA_KW)),
            pl.BlockSpec((CH, GLA_W), lambda b, t: (rowfn(b, t), _COL_GV // GLA_W)),
            pl.BlockSpec((CH, 128), lambda b, t: (rowfn(b, t), _COL_GA // 128)),
        ]

    return pl.pallas_call(
        _gla_kernel,
        grid=(NB, N_CH),
        in_specs=specs(fwd) + specs(bwd) + [
            pl.BlockSpec((2, 128, GLA_KW), lambda b, t: (0, 0, 0)),
            pl.BlockSpec((2, 1, GLA_KW), lambda b, t: (0, 0, 0)),
        ],
        out_specs=[pl.BlockSpec((CH, GLA_W), lambda b, t: (fwd(b, t), 0)),
                   pl.BlockSpec((CH, GLA_W), lambda b, t: (bwd(b, t), 0))],
        out_shape=[jax.ShapeDtypeStruct((NT, GLA_W), F32)] * 2,
        scratch_shapes=[pltpu.VMEM((2, GLA_H, GLA_DV, GLA_DK), F32)],
        compiler_params=_cp(("arbitrary", "arbitrary")),
        name="gla_scan",
    )(proj, proj, proj, proj, proj, proj, proj, proj, wg, bg)


def _gla_out_kernel(of_ref, ob_ref, g_ref, gn_ref, o_ref):
    for h in range(GLA_H):
        sl = slice(GLA_DV * h, GLA_DV * (h + 1))
        o = _rms_rows(of_ref[:, sl] + ob_ref[:, sl], gn_ref[...])
        gate = g_ref[:, sl].astype(F32)
        o_ref[:, sl] = (o * (gate * _sigmoid(gate))).astype(o_ref.dtype)


def gla_output(o_f, o_b, proj, on_g):
    tm = 512
    return pl.pallas_call(
        _gla_out_kernel,
        grid=(NT // tm,),
        in_specs=[
            pl.BlockSpec((tm, GLA_W), lambda i: (i, 0)),
            pl.BlockSpec((tm, GLA_W), lambda i: (i, 0)),
            pl.BlockSpec((tm, GLA_W), lambda i: (i, _COL_GG // GLA_W)),
            pl.BlockSpec((1, GLA_DV), lambda i: (0, 0)),
        ],
        out_specs=pl.BlockSpec((tm, GLA_W), lambda i: (i, 0)),
        out_shape=jax.ShapeDtypeStruct((NT, GLA_W), BF16),
        compiler_params=_cp(("arbitrary",)),
        name="gla_output",
    )(o_f, o_b, proj, on_g.reshape(1, GLA_DV))


OUT_TM = NB * L


def _out_proj_kernel(oa_ref, obl_ref, obc_ref, oc_ref, w_ref, x_ref, g_ref, o_ref):
    ob = jnp.where(pl.program_id(1) < NL // OUT_TM, obl_ref[...], obc_ref[...])
    a = jnp.concatenate([oa_ref[...], ob, oc_ref[...]], axis=1)
    o_ref[...] = x_ref[...] + g_ref[...] * _dot(a, w_ref[...])


def out_projection(oa, ob, ob_ctx, oc, w_out, layer, xs, mod, k_gate):
    tm, tn = OUT_TM, 1024
    nj = D // tn
    return pl.pallas_call(
        _out_proj_kernel,
        grid=(nj, NT // tm),
        in_specs=[
            pl.BlockSpec((tm, NA_W), lambda j, i: (i, 0)),
            pl.BlockSpec((tm, MLA_W), lambda j, i: (jnp.minimum(i, NL // tm - 1), 0)),
            pl.BlockSpec((tm, MLA_W), lambda j, i: (0, 0)),
            pl.BlockSpec((tm, GLA_W), lambda j, i: (i, 0)),
            pl.BlockSpec((None, D, tn), lambda j, i: (layer, 0, j)),
            pl.BlockSpec((tm, tn), lambda j, i: (i, j)),
            pl.BlockSpec((None, 1, tn), lambda j, i: (_row_group(i, tm), 0, k_gate * nj + j)),
        ],
        out_specs=pl.BlockSpec((tm, tn), lambda j, i: (i, j)),
        out_shape=jax.ShapeDtypeStruct((NT, D), F32),
        compiler_params=_cp(("arbitrary", "arbitrary")),
        name="out_projection",
    )(oa, ob, ob_ctx, oc, w_out, xs, mod)


def _route(logits, router_bias):
    scores = jax.nn.sigmoid(logits)
    grouped = (scores + router_bias.astype(F32)).reshape(NT, E_GROUPS, E_PER)

    def top2(a):
        idx = lax.broadcasted_iota(jnp.int32, a.shape, a.ndim - 1)
        i1 = jnp.argmax(a, axis=-1).astype(jnp.int32)
        rest = jnp.where(idx == i1[..., None], -jnp.inf, a)
        i2 = jnp.argmax(rest, axis=-1).astype(jnp.int32)
        return jnp.max(a, axis=-1), jnp.max(rest, axis=-1), i1, i2

    m1, m2, _, _ = top2(grouped)
    grp = jnp.argmax(m1 + m2, axis=-1).astype(jnp.int32)
    gsel = lax.broadcasted_iota(jnp.int32, (NT, E_GROUPS, E_PER), 1) == grp[:, None, None]
    in_group = jnp.sum(jnp.where(gsel, grouped, 0.0), axis=1)
    _, _, l1, l2 = top2(in_group)
    expert_idx = grp[:, None] * E_PER + jnp.stack([l1, l2], axis=-1)
    esel = lax.broadcasted_iota(jnp.int32, (NT, TOPK, E), 2) == expert_idx[:, :, None]
    w = jnp.sum(jnp.where(esel, scores[:, None, :], 0.0), axis=-1)
    return expert_idx, w / jnp.sum(w, axis=-1, keepdims=True)


def _dispatch_plan(expert_idx):
    flat_e = expert_idx.reshape(NK)
    onehot = (flat_e[:, None] == jnp.arange(E, dtype=jnp.int32)[None, :]).astype(jnp.int32)
    csum = jnp.cumsum(onehot, axis=0)
    counts = csum[-1]
    rank = jnp.take_along_axis(csum, flat_e[:, None], axis=1)[:, 0] - 1
    padded = (counts + MOE_R - 1) // MOE_R * MOE_R
    pad_end = jnp.cumsum(padded)
    pad_start = pad_end - padded
    dest = (pad_start[flat_e] + rank).astype(jnp.int32)
    row_token = jnp.zeros((MOE_P,), jnp.int32).at[dest].set(jnp.arange(NK, dtype=jnp.int32) // TOPK)
    block_expert = jnp.minimum(
        jnp.searchsorted(pad_end, jnp.arange(MOE_NB, dtype=jnp.int32) * MOE_R, side='right'),
        E - 1).astype(jnp.int32)
    n_active = (pad_end[-1] // MOE_R).astype(jnp.int32).reshape(1)
    blk0 = jnp.arange(MOE_NB, dtype=jnp.int32) * MOE_R
    n_valid = jnp.clip(counts[block_expert] - (blk0 - pad_start[block_expert]), 0, MOE_R).astype(jnp.int32)
    return dest, row_token, block_expert, n_active, n_valid


DH = D // 2


def _gather_rows_kernel(tok_ref, nv_ref, h_ref, o_ref, buf_ref, sem):
    i = pl.program_id(0)

    def copy(blk, slot, r):
        return pltpu.make_async_copy(h_ref.at[pl.ds(tok_ref[blk * MOE_R + r], 1)],
                                     buf_ref.at[slot, pl.ds(r, 1)], sem.at[slot])

    def issue(blk, slot):
        @pl.when(nv_ref[blk] < MOE_R)
        def _():
            buf_ref[slot] = jnp.zeros((MOE_R, DH), jnp.uint32)

        def start(r, c):
            copy(blk, slot, r).start()
            return c

        lax.fori_loop(0, nv_ref[blk], start, 0)

    @pl.when(i == 0)
    def _():
        issue(0, 0)

    @pl.when(i + 1 < MOE_NB)
    def _():
        issue(i + 1, (i + 1) % 2)

    slot = i % 2

    def wait(r, c):
        copy(i, slot, r).wait()
        return c

    lax.fori_loop(0, nv_ref[i], wait, 0)
    o_ref[...] = buf_ref[slot]


def gather_rows(row_token, n_valid, h):
    return pl.pallas_call(
        _gather_rows_kernel,
        grid_spec=pltpu.PrefetchScalarGridSpec(
            num_scalar_prefetch=2,
            grid=(MOE_NB,),
            in_specs=[pl.BlockSpec(memory_space=pl.ANY)],
            out_specs=pl.BlockSpec((MOE_R, DH), lambda i, tok, nv: (i, 0)),
            scratch_shapes=[pltpu.VMEM((2, MOE_R, DH), jnp.uint32), pltpu.SemaphoreType.DMA((2,))],
        ),
        out_shape=jax.ShapeDtypeStruct((MOE_P, DH), jnp.uint32),
        compiler_params=_cp(("arbitrary",)),
        name="moe_gather_rows",
    )(row_token, n_valid, h)


def _new_expert(be_ref, i):
    return jnp.logical_or(i == 0, be_ref[i] != be_ref[jnp.maximum(i - 1, 0)])


def _ffn_up_kernel(be_ref, na_ref, x_ref, w1_ref, w3_ref, o_ref, w1b_ref, w3b_ref):
    i = pl.program_id(1)
    active = i < na_ref[0]

    @pl.when(jnp.logical_and(active, _new_expert(be_ref, i)))
    def _():
        w1b_ref[...] = w1_ref[...].astype(BF16)
        w3b_ref[...] = w3_ref[...].astype(BF16)

    @pl.when(active)
    def _():
        lo, hi = _unpack_pairs(x_ref[...])
        lo, hi = lo.astype(BF16), hi.astype(BF16)
        a = _dot(lo, w1b_ref[0:DH, :]) + _dot(hi, w1b_ref[DH:, :])
        b = _dot(lo, w3b_ref[0:DH, :]) + _dot(hi, w3b_ref[DH:, :])
        o_ref[...] = (a * _sigmoid(a) * b).astype(o_ref.dtype)

    @pl.when(jnp.logical_not(active))
    def _():
        o_ref[...] = jnp.zeros_like(o_ref)


def ffn_up(block_expert, n_active, xb, w1, w3, layer):
    last = lambda i, na: jnp.minimum(i, na[0] - 1)
    return pl.pallas_call(
        _ffn_up_kernel,
        grid_spec=pltpu.PrefetchScalarGridSpec(
            num_scalar_prefetch=2,
            grid=(FF // MOE_TF, MOE_NB),
            in_specs=[
                pl.BlockSpec((MOE_R, DH), lambda j, i, be, na: (last(i, na), 0)),
                pl.BlockSpec((None, None, D, MOE_TF), lambda j, i, be, na: (layer, be[i], 0, j)),
                pl.BlockSpec((None, None, D, MOE_TF), lambda j, i, be, na: (layer, be[i], 0, j)),
            ],
            out_specs=pl.BlockSpec((MOE_R, MOE_TF), lambda j, i, be, na: (i, j)),
            scratch_shapes=[pltpu.VMEM((D, MOE_TF), BF16), pltpu.VMEM((D, MOE_TF), BF16)],
        ),
        out_shape=jax.ShapeDtypeStruct((MOE_P, FF), BF16),
        compiler_params=_cp(("arbitrary", "arbitrary")),
        name="moe_ffn_up",
    )(block_expert, n_active, xb, w1, w3)


MOE_TN = 1024


def _ffn_down_kernel(be_ref, na_ref, h_ref, w2_ref, o_ref, w2b_ref):
    i = pl.program_id(1)
    active = i < na_ref[0]

    @pl.when(jnp.logical_and(active, _new_expert(be_ref, i)))
    def _():
        w2b_ref[...] = w2_ref[...].astype(BF16)

    @pl.when(active)
    def _():
        y = _dot(h_ref[...], w2b_ref[...])
        o_ref[...] = _pack_pairs(y[:, :MOE_TN // 2], y[:, MOE_TN // 2:])

    @pl.when(jnp.logical_not(active))
    def _():
        o_ref[...] = jnp.zeros_like(o_ref)


def ffn_down(block_expert, n_active, hmid, w2, layer):
    last = lambda i, na: jnp.minimum(i, na[0] - 1)
    return pl.pallas_call(
        _ffn_down_kernel,
        grid_spec=pltpu.PrefetchScalarGridSpec(
            num_scalar_prefetch=2,
            grid=(D // MOE_TN, MOE_NB),
            in_specs=[
                pl.BlockSpec((MOE_R, FF), lambda n, i, be, na: (last(i, na), 0)),
                pl.BlockSpec((None, None, FF, MOE_TN), lambda n, i, be, na: (layer, be[i], 0, n)),
            ],
            out_specs=pl.BlockSpec((MOE_R, MOE_TN // 2), lambda n, i, be, na: (i, n)),
            scratch_shapes=[pltpu.VMEM((FF, MOE_TN), BF16)],
        ),
        out_shape=jax.ShapeDtypeStruct((MOE_P, DH), jnp.uint32),
        compiler_params=_cp(("arbitrary", "arbitrary")),
        name="moe_ffn_down",
    )(block_expert, n_active, hmid, w2)


CMB_T = 256


def _combine_kernel(n_blocks, pos_ref, yb_ref, gate_ref, x_ref, g_ref, o_ref, buf_ref, sem):
    i = pl.program_id(0)

    def copy(blk, slot, r, k):
        p = pos_ref[(blk * CMB_T + r) * TOPK + k]
        return pltpu.make_async_copy(yb_ref.at[pl.ds(p, 1)], buf_ref.at[slot, k, pl.ds(r, 1)], sem.at[slot, k])

    def issue(blk, slot):
        def start(r, c):
            copy(blk, slot, r, 0).start()
            copy(blk, slot, r, 1).start()
            return c

        lax.fori_loop(0, CMB_T, start, 0)

    @pl.when(i == 0)
    def _():
        issue(0, 0)

    @pl.when(i + 1 < n_blocks)
    def _():
        issue(i + 1, (i + 1) % 2)

    slot = i % 2

    def wait(r, c):
        copy(i, slot, r, 0).wait()
        copy(i, slot, r, 1).wait()
        return c

    lax.fori_loop(0, CMB_T, wait, 0)
    gate = gate_ref[...]
    g0, g1 = gate[:, 0:1], gate[:, 1:2]
    half = MOE_TN // 2
    for n in range(D // MOE_TN):
        lo0, hi0 = _unpack_pairs(buf_ref[slot, 0, :, half * n: half * (n + 1)])
        lo1, hi1 = _unpack_pairs(buf_ref[slot, 1, :, half * n: half * (n + 1)])
        for part, y in enumerate((lo0 * g0 + lo1 * g1, hi0 * g0 + hi1 * g1)):
            sl = slice(MOE_TN * n + half * part, MOE_TN * n + half * (part + 1))
            o_ref[:, sl] = x_ref[:, sl] + g_ref[:, sl] * y


def moe_combine(dest, yb, gates, xs, mod, k_gate, out_rows):
    return pl.pallas_call(
        functools.partial(_combine_kernel, out_rows // CMB_T),
        grid_spec=pltpu.PrefetchScalarGridSpec(
            num_scalar_prefetch=1,
            grid=(out_rows // CMB_T,),
            in_specs=[
                pl.BlockSpec(memory_space=pl.ANY),
                pl.BlockSpec((CMB_T, TOPK), lambda i, pos: (i, 0)),
                pl.BlockSpec((CMB_T, D), lambda i, pos: (i, 0)),
                pl.BlockSpec((None, 1, D), lambda i, pos: (_row_group(i, CMB_T), 0, k_gate)),
            ],
            out_specs=pl.BlockSpec((CMB_T, D), lambda i, pos: (i, 0)),
            scratch_shapes=[pltpu.VMEM((2, TOPK, CMB_T, DH), jnp.uint32), pltpu.SemaphoreType.DMA((2, TOPK))],
        ),
        out_shape=jax.ShapeDtypeStruct((out_rows, D), F32),
        compiler_params=_cp(("arbitrary",)),
        name="moe_combine",
    )(dest, yb, gates, xs, mod)


IN_W = 9312


def _w_in_tail_select():
    sel = np.zeros((256, 512), np.float32)
    for d in range(MLA_ROPE):
        sel[d, d] = sel[d, 64 + d] = 1.0
        sel[_ROPE_SWAP[d], 128 + d] = sel[_ROPE_SWAP[d], 192 + d] = 1.0
    for t in range(2 * GLA_RANK):
        sel[128 + 64 + t, 256 + t] = 1.0
    return jnp.asarray(sel, BF16)


def _w_in_layout_kernel(w_ref, sel_ref, o_ref):
    o_ref[:, 0:_COL_GQ] = w_ref[:, 0:_COL_GQ].astype(BF16)
    o_ref[:, _COL_GQ:_COL_KR] = w_ref[:, _COL_GQ + MLA_ROPE:9280].astype(BF16)
    src = jnp.concatenate([w_ref[:, 4608:4736], w_ref[:, 9216:IN_W],
                           jnp.zeros((w_ref.shape[0], 32), F32)], axis=1).astype(BF16)
    o_ref[:, _COL_KR:PW] = _dot(src, sel_ref[...]).astype(BF16)


def _w_in_layout(w):
    tr = 256
    return pl.pallas_call(
        _w_in_layout_kernel,
        grid=(DEPTH, D // tr),
        in_specs=[pl.BlockSpec((None, tr, IN_W), lambda l, i: (l, i, 0)),
                  pl.BlockSpec((256, 512), lambda l, i: (0, 0))],
        out_specs=pl.BlockSpec((None, tr, PW), lambda l, i: (l, i, 0)),
        out_shape=jax.ShapeDtypeStruct((DEPTH, D, PW), BF16),
        compiler_params=_cp(("arbitrary", "arbitrary")),
        name="w_in_layout",
    )(w, _w_in_tail_select())


def _w_uq_cols():
    nope = [MLA_QK * h + j for h in range(MLA_H) for j in range(MLA_NOPE)]
    rope = [MLA_QK * h + MLA_NOPE + d for h in range(MLA_H) for d in range(MLA_ROPE)]
    rope_s = [MLA_QK * h + MLA_NOPE + int(d) for h in range(MLA_H) for d in _ROPE_SWAP]
    return np.array(nope + rope + rope_s, np.int32)


def _w_ukv_cols():
    kn = [(MLA_NOPE + MLA_V) * h + j for h in range(MLA_H) for j in range(MLA_NOPE)]
    vv = [(MLA_NOPE + MLA_V) * h + MLA_NOPE + j for h in range(MLA_H) for j in range(MLA_V)]
    return np.array(kn + vv, np.int32)


def token_mixing_layer(xs, mod, layer, norm1, w_in_b, w_out_b, na_qn, na_kn, na_rpb, mla_qa, mla_kva, w_uq, w_ukv,
                       mla_qn, mla_kn, gwf, gbf, gwb, gbb, gla_on, c2, s2):
    h = norm_modulate(xs, norm1.reshape(1, D), mod, 0, 1, BF16)
    proj = matmul(h, w_in_b, layer, NT // 8, 512, BF16)
    oa = na_attention(proj, na_rpb, na_qn, na_kn)
    q = mla_q_prep(proj, w_uq[:, _w_uq_cols()].astype(BF16), mla_qa, mla_qn, c2, s2)
    k, v = mla_kv_prep(proj, w_ukv[:, _w_ukv_cols()].astype(BF16), mla_kva, mla_kn, c2, s2)
    ob, ob_ctx = mla_attention(q, k, v)
    o_f, o_b = gla_scan(proj, gwf, gbf, gwb, gbb)
    oc = gla_output(o_f, o_b, proj, gla_on)
    return out_projection(oa, ob, ob_ctx, oc, w_out_b, layer, xs, mod, 2)


def moe_layer(xs, mod, norm2, w_router_p, router_bias, w1, w3, w2, layer, out_rows):
    h, logits = norm_modulate(xs, norm2.reshape(1, D), mod, 3, 4, None, w_router_p)
    expert_idx, gates = _route(logits[:, :E], router_bias)
    dest, row_token, block_expert, n_active, n_valid = _dispatch_plan(expert_idx)
    xb = gather_rows(row_token, n_valid, h)
    hmid = ffn_up(block_expert, n_active, xb, w1, w3, layer)
    yb = ffn_down(block_expert, n_active, hmid, w2, layer)
    return moe_combine(dest, yb, gates, xs, mod, 5, out_rows)


def kernel(x, c, ctx, c_ctx, w_ada, b_ada, norm1, norm2, w_in, w_out, na_q_norm, na_k_norm, na_rpb, mla_qa_norm, mla_kva_norm, mla_w_uq, mla_w_ukv, mla_q_norm, mla_k_norm, gla_w_gate_f, gla_b_gate_f, gla_w_gate_b, gla_b_gate_b, gla_out_norm, w_router, router_bias, moe_w1, moe_w3, moe_w2):
    cond = jnp.concatenate([c, c_ctx[None, :], jnp.zeros((8 - NB - 1, D), F32)], axis=0)
    mods = ada_modulation(cond, w_ada, b_ada)
    xs = jnp.concatenate([x.reshape(NL, D), ctx.reshape(NB * L, D)], axis=0)
    c2, s2 = _rope_tables()
    w_router_p = jnp.concatenate([w_router, jnp.zeros((D, 128 - E), F32)], axis=1)
    w_in_b = _w_in_layout(w_in)
    w_out_b = w_out.astype(BF16)
    for l in range(DEPTH):
        mod = mods[l].reshape(8, 1, 6 * D)
        xs = token_mixing_layer(xs, mod, l, norm1[l], w_in_b, w_out_b, na_q_norm[l], na_k_norm[l], na_rpb[l],
                                mla_qa_norm[l], mla_kva_norm[l], mla_w_uq[l], mla_w_ukv[l],
                                mla_q_norm[l], mla_k_norm[l], gla_w_gate_f[l], gla_b_gate_f[l],
                                gla_w_gate_b[l], gla_b_gate_b[l], gla_out_norm[l], c2, s2)
        xs = moe_layer(xs, mod, norm2[l], w_router_p, router_bias, moe_w1, moe_w3, moe_w2, l,
                       NT if l < DEPTH - 1 else NL)
    return xs.reshape(NB, S, D)
```

```python
import functools

import numpy as np
import jax
import jax.numpy as jnp
from jax import lax
from jax.experimental import pallas as pl
from jax.experimental.pallas import tpu as pltpu

F32 = jnp.float32
BF16 = jnp.bfloat16

D = 4096
NB = 2
S = 4096
L = 256
DEPTH = 2
GW = 64
EPS = 1e-6
NL = NB * S
NT = NL + NB * L
NA_H, NA_D = 8, 128
NA_W = NA_H * NA_D
NA_R, NA_C = 8, 16
MLA_H = 12
MLA_QL, MLA_KVL = 1024, 512
MLA_NOPE, MLA_ROPE, MLA_V = 128, 64, 128
MLA_QK = MLA_NOPE + MLA_ROPE
MLA_W = MLA_H * MLA_V
GLA_H, GLA_DK, GLA_DV = 6, 128, 256
GLA_KW = GLA_H * GLA_DK
GLA_W = GLA_H * GLA_DV
GLA_RANK = 16
GLA_TAU = 16.0
CH = 64
E = 16
E_GROUPS = 4
E_PER = E // E_GROUPS
TOPK = 2
FF = 1024
ROPE_BASE = 10000.0

_COL_NAQ, _COL_NAK, _COL_NAV = 0, 1024, 2048
_COL_CQ, _COL_CKV = 3072, 4096
_COL_GQ, _COL_GK, _COL_GV, _COL_GG = 4608, 5376, 6144, 7680
_COL_KR = 9216
_COL_GA = 9472
PW = 9728

BLK = 256
NBLK_L = S // BLK
CTX_BLK0 = NL // BLK
MOE_R = 256
MOE_TF = 512
NK = NT * TOPK
MOE_NB = (NK + E * (MOE_R - 1)) // MOE_R
MOE_P = MOE_NB * MOE_R

NEG = -1e30
VMEM_LIMIT = 56 * 1024 * 1024


def _cp(sem, vmem=VMEM_LIMIT):
    return pltpu.CompilerParams(dimension_semantics=sem, vmem_limit_bytes=vmem)


def _dot(a, b):
    return jnp.dot(a, b, preferred_element_type=F32)


def _dot_nt(a, b):
    return lax.dot_general(a, b, (((1,), (1,)), ((), ())), preferred_element_type=F32)


def _dot_tn(a, b):
    return lax.dot_general(a, b, (((0,), (0,)), ((), ())), preferred_element_type=F32)


def _sigmoid(x):
    return 1.0 / (1.0 + jnp.exp(-x))


def _pack_pairs(lo, hi):
    lo_w = lax.bitcast_convert_type(lo.astype(BF16).astype(F32), jnp.uint32)
    hi_w = lax.bitcast_convert_type(hi.astype(BF16).astype(F32), jnp.uint32)
    return lax.shift_right_logical(lo_w, jnp.uint32(16)) | (hi_w & jnp.uint32(0xFFFF0000))


def _unpack_pairs(w):
    lo = lax.bitcast_convert_type(lax.shift_left(w, jnp.uint32(16)), F32)
    hi = lax.bitcast_convert_type(w & jnp.uint32(0xFFFF0000), F32)
    return lo, hi


def _row_group(i, tm):
    r0 = i * tm
    return jnp.where(r0 >= NL, 2, r0 // S)


def _ada_kernel(s_ref, w_ref, b_ref, o_ref):
    s = s_ref[...]
    s = s * _sigmoid(s)
    o_ref[...] = _dot(s.astype(BF16), w_ref[...].astype(BF16)) + b_ref[...]


def ada_modulation(cond, w_ada, b_ada):
    tn = 512
    n = 6 * D
    return pl.pallas_call(
        _ada_kernel,
        grid=(DEPTH, n // tn),
        in_specs=[
            pl.BlockSpec((8, D), lambda l, j: (0, 0)),
            pl.BlockSpec((None, D, tn), lambda l, j: (l, 0, j)),
            pl.BlockSpec((None, 1, tn), lambda l, j: (l, 0, j)),
        ],
        out_specs=pl.BlockSpec((None, 8, tn), lambda l, j: (l, 0, j)),
        out_shape=jax.ShapeDtypeStruct((DEPTH, 8, n), F32),
        compiler_params=_cp(("arbitrary", "arbitrary")),
        name="ada_modulation",
    )(cond, w_ada, b_ada.reshape(DEPTH, 1, n))


def _normmod(x, g, sh, sc):
    y = x * lax.rsqrt(jnp.mean(x * x, axis=-1, keepdims=True) + EPS) * g
    return y * (1.0 + sc) + sh


def _normmod_kernel(x_ref, g_ref, sh_ref, sc_ref, h_ref):
    h_ref[...] = _normmod(x_ref[...], g_ref[...], sh_ref[...], sc_ref[...]).astype(h_ref.dtype)


def _normmod_router_kernel(x_ref, g_ref, sh_ref, sc_ref, wr_ref, h_ref, lg_ref):
    h = _normmod(x_ref[...], g_ref[...], sh_ref[...], sc_ref[...])
    h_ref[...] = _pack_pairs(h[:, :D // 2], h[:, D // 2:])
    lg_ref[...] = jnp.dot(h, wr_ref[...], preferred_element_type=F32,
                          precision=lax.Precision.HIGHEST)


def norm_modulate(xs, gain, mod, k_shift, k_scale, out_dtype, w_router=None):
    tm = 512
    in_specs = [
        pl.BlockSpec((tm, D), lambda i: (i, 0)),
        pl.BlockSpec((1, D), lambda i: (0, 0)),
        pl.BlockSpec((None, 1, D), lambda i: (_row_group(i, tm), 0, k_shift)),
        pl.BlockSpec((None, 1, D), lambda i: (_row_group(i, tm), 0, k_scale)),
    ]
    if w_router is None:
        return pl.pallas_call(
            _normmod_kernel, grid=(NT // tm,), in_specs=in_specs,
            out_specs=pl.BlockSpec((tm, D), lambda i: (i, 0)),
            out_shape=jax.ShapeDtypeStruct((NT, D), out_dtype),
            compiler_params=_cp(("arbitrary",)), name="norm_modulate",
        )(xs, gain, mod, mod)
    return pl.pallas_call(
        _normmod_router_kernel, grid=(NT // tm,),
        in_specs=in_specs + [pl.BlockSpec((D, 128), lambda i: (0, 0))],
        out_specs=[pl.BlockSpec((tm, D // 2), lambda i: (i, 0)), pl.BlockSpec((tm, 128), lambda i: (i, 0))],
        out_shape=[jax.ShapeDtypeStruct((NT, D // 2), jnp.uint32), jax.ShapeDtypeStruct((NT, 128), F32)],
        compiler_params=_cp(("arbitrary",)), name="norm_modulate_router",
    )(xs, gain, mod, mod, w_router)


def _mm_kernel(a_ref, b_ref, o_ref):
    o_ref[...] = _dot_nt(a_ref[...], b_ref[...]).astype(o_ref.dtype)


def matmul(a, b, layer, tm, tn, out_dtype):
    m, k = a.shape
    n = b.shape[1]
    return pl.pallas_call(
        _mm_kernel,
        grid=(m // tm, n // tn),
        in_specs=[pl.BlockSpec((tm, k), lambda i, j: (i, 0)),
                  pl.BlockSpec((None, tn, k), lambda i, j: (layer, j, 0))],
        out_specs=pl.BlockSpec((tm, tn), lambda i, j: (i, j)),
        out_shape=jax.ShapeDtypeStruct((m, n), out_dtype),
        compiler_params=_cp(("arbitrary", "arbitrary")),
        name="matmul",
    )(a, b)


def _qblk(b, i):
    return jnp.where(i == 0, CTX_BLK0 + b, b * NBLK_L + i - 1)


def _rms_rows(x, g):
    return x * lax.rsqrt(jnp.mean(x * x, axis=-1, keepdims=True) + EPS) * g


NA_KROWS = 12
NA_KW = NA_KROWS * GW


def _na_kernel(q_ref, kl_ref, kc_ref, vl_ref, vc_ref, bias_ref, qn_ref, kn_ref, o_ref, ks_ref):
    i = pl.program_id(2)

    @pl.when(i == 0)
    def _():
        ks_ref[0:L, :] = _rms_rows(kc_ref[...].astype(F32), kn_ref[...]).astype(BF16)
        ks_ref[L:, :] = _rms_rows(kl_ref[...].astype(F32), kn_ref[...]).astype(BF16)

    q = (_rms_rows(q_ref[...].astype(F32), qn_ref[...]) * (NA_D ** -0.5)).astype(BF16)
    s_c = _dot_nt(q, ks_ref[0:L, :])
    m_c = jnp.max(s_c, axis=-1, keepdims=True)

    @pl.when(i == 0)
    def _():
        p = jnp.exp(s_c - m_c)
        l = jnp.sum(p, axis=-1, keepdims=True)
        o_ref[...] = (_dot(p.astype(BF16), vc_ref[...]) / l).astype(o_ref.dtype)

    @pl.when(i > 0)
    def _():
        row0 = jnp.clip(4 * (i - 1) - 4, 0, GW - NA_KROWS)
        k0 = pl.multiple_of(row0 * GW, BLK)
        s_l = _dot_nt(q, ks_ref[pl.ds(L + k0, NA_KW), :]) + bias_ref[...]
        m = jnp.maximum(m_c, jnp.max(s_l, axis=-1, keepdims=True))
        p_c = jnp.exp(s_c - m)
        p_l = jnp.exp(s_l - m)
        l = jnp.sum(p_c, axis=-1, keepdims=True) + jnp.sum(p_l, axis=-1, keepdims=True)
        o = _dot(p_l.astype(BF16), vl_ref[pl.ds(k0, NA_KW), :]) + _dot(p_c.astype(BF16), vc_ref[...])
        o_ref[...] = (o / l).astype(o_ref.dtype)


def _na_bias_index():
    dr = np.zeros((3, BLK, NA_KW), np.int32)
    dc = np.zeros((3, BLK, NA_KW), np.int32)
    ok = np.zeros((3, BLK, NA_KW), bool)
    rows = S // GW
    for p, blk in enumerate((0, 5, NBLK_L - 1)):
        ks = int(np.clip(4 * blk - 4, 0, GW - NA_KROWS))
        r = 4 * blk + np.arange(BLK) // GW
        c = np.arange(BLK) % GW
        rk = ks + np.arange(NA_KW) // GW
        ck = np.arange(NA_KW) % GW
        r0 = np.clip(r - NA_R // 2, 0, rows - NA_R)
        ws = np.clip(c - NA_C // 2, 0, GW - NA_C)
        row_ok = (rk[None, :] >= r0[:, None]) & (rk[None, :] < r0[:, None] + NA_R)
        col_ok = (ck[None, :] >= ws[:, None]) & (ck[None, :] < ws[:, None] + NA_C)
        ok[p] = row_ok & col_ok
        dr[p] = np.clip(rk[None, :] - r[:, None] + NA_R - 1, 0, 2 * NA_R - 2)
        dc[p] = np.clip(ck[None, :] - c[:, None], -(NA_C - 1), NA_C - 1) + NA_C - 1
    return dr, dc, ok


def _na_bias_table(rpb):
    dr, dc, ok = _na_bias_index()
    qr, kr = BLK // GW, NA_KROWS
    dr_t = dr.reshape(3, qr, GW, kr, GW)[:, :, 0, :, 0].reshape(3 * qr * kr)
    dc_t = dc[0].reshape(qr, GW, kr, GW)[0, :, 0, :].reshape(GW * GW)
    oh_c = jnp.asarray(np.eye(2 * NA_C - 1, dtype=np.float32)[:, dc_t])
    oh_r = jnp.asarray(np.eye(2 * NA_R - 1, dtype=np.float32)[dr_t])
    hi = lax.Precision.HIGHEST
    t1 = jnp.einsum('hrd,dx->hrx', rpb.astype(F32), oh_c, precision=hi)
    t2 = jnp.einsum('yr,hrx->hyx', oh_r, t1, precision=hi)
    t2 = t2.reshape(NA_H, 3, qr, kr, GW, GW).transpose(0, 1, 2, 4, 3, 5).reshape(NA_H, 3, BLK, NA_KW)
    return jnp.where(ok[None], t2, NEG)


def na_attention(proj, rpb, qn, kn):
    bias = _na_bias_table(rpb)

    def pat(i):
        return jnp.where(i <= 1, 0, jnp.where(i == NBLK_L, 2, 1))

    cq, ck, cv = _COL_NAQ // NA_D, _COL_NAK // NA_D, _COL_NAV // NA_D
    return pl.pallas_call(
        _na_kernel,
        grid=(NB, NA_H, NBLK_L + 1),
        in_specs=[
            pl.BlockSpec((BLK, NA_D), lambda b, h, i: (_qblk(b, i), cq + h)),
            pl.BlockSpec((S, NA_D), lambda b, h, i: (b, ck + h)),
            pl.BlockSpec((L, NA_D), lambda b, h, i: (CTX_BLK0 + b, ck + h)),
            pl.BlockSpec((S, NA_D), lambda b, h, i: (b, cv + h)),
            pl.BlockSpec((L, NA_D), lambda b, h, i: (CTX_BLK0 + b, cv + h)),
            pl.BlockSpec((None, None, BLK, NA_KW), lambda b, h, i: (h, pat(i), 0, 0)),
            pl.BlockSpec((1, NA_D), lambda b, h, i: (0, 0)),
            pl.BlockSpec((1, NA_D), lambda b, h, i: (0, 0)),
        ],
        out_specs=pl.BlockSpec((BLK, NA_D), lambda b, h, i: (_qblk(b, i), h)),
        out_shape=jax.ShapeDtypeStruct((NT, NA_W), BF16),
        scratch_shapes=[pltpu.VMEM((L + S, NA_D), BF16)],
        compiler_params=_cp(("arbitrary", "arbitrary", "arbitrary")),
        name="na_attention",
    )(proj, proj, proj, proj, proj, bias, qn.reshape(1, NA_D), kn.reshape(1, NA_D))


MLA_HW = 256


def _half_mask(h, width=128):
    lane = lax.broadcasted_iota(jnp.int32, (1, width), 1)
    return (lane < 64) if h % 2 == 0 else (lane >= 64)


def _mla_q_kernel(cq_ref, w_ref, qa_ref, gn_ref, g2_ref, gs2_ref, c2_ref, s2_ref, q_ref):
    x = _rms_rows(cq_ref[...].astype(F32), qa_ref[...]).astype(BF16)
    y = _dot(x, w_ref[...])
    nw = MLA_H * MLA_NOPE
    gc = g2_ref[...] * c2_ref[...]
    gs = gs2_ref[...] * s2_ref[...]
    for p in range(MLA_H // 2):
        r1 = y[:, nw + 128 * p: nw + 128 * (p + 1)]
        r2 = y[:, nw + 768 + 128 * p: nw + 768 + 128 * (p + 1)]
        rot = r1 * gc + r2 * gs
        sq = r1 * r1
        for h in (2 * p, 2 * p + 1):
            msk = _half_mask(h)
            nope = y[:, 128 * h: 128 * (h + 1)]
            ss = jnp.sum(nope * nope, axis=-1, keepdims=True) + jnp.sum(
                jnp.where(msk, sq, 0.0), axis=-1, keepdims=True)
            inv = lax.rsqrt(ss / MLA_QK + EPS) * (MLA_QK ** -0.5)
            q_ref[h, :, 0:128] = (nope * gn_ref[...] * inv).astype(BF16)
            q_ref[h, :, 128:256] = (jnp.where(msk, rot, 0.0) * inv).astype(BF16)


def _mla_kv_kernel(ckv_ref, kr_ref, w_ref, kva_ref, gn_ref, g2_ref, gs2_ref, c2_ref, s2_ref,
                   k_ref, v_ref):
    x = _rms_rows(ckv_ref[...].astype(F32), kva_ref[...]).astype(BF16)
    y = _dot(x, w_ref[...])
    kr = kr_ref[...].astype(F32)
    r1 = kr[:, 0:128]
    r2 = kr[:, 128:256]
    rot = r1 * (g2_ref[...] * c2_ref[...]) + r2 * (gs2_ref[...] * s2_ref[...])
    ss_r = jnp.sum(jnp.where(_half_mask(0), r1 * r1, 0.0), axis=-1, keepdims=True)
    nw = MLA_H * MLA_NOPE
    lane0 = lax.broadcasted_iota(jnp.int32, (kr.shape[0], 128), 1) == 0
    for h in range(MLA_H):
        nope = y[:, 128 * h: 128 * (h + 1)]
        ss = jnp.sum(nope * nope, axis=-1, keepdims=True) + ss_r
        inv = lax.rsqrt(ss / MLA_QK + EPS)
        k_ref[h, :, 0:128] = (nope * gn_ref[...] * inv).astype(BF16)
        k_ref[h, :, 128:256] = (jnp.where(_half_mask(h), rot, 0.0) * inv).astype(BF16)
        v_ref[h, :, 0:128] = y[:, nw + 128 * h: nw + 128 * (h + 1)].astype(BF16)
        v_ref[h, :, 128:256] = jnp.where(lane0, 1.0, 0.0).astype(BF16)


def _rope_tables():
    t = jnp.arange(S, dtype=jnp.int32)
    pos = (t // GW, t % GW)
    nf = MLA_ROPE // 4
    inv = ROPE_BASE ** (-jnp.arange(nf, dtype=F32) / nf)
    cs, sn = [], []
    for ax in range(2):
        ang = pos[ax].astype(F32)[:, None] * inv[None, :]
        c, s = jnp.cos(ang), jnp.sin(ang)
        cs += [c, c]
        sn += [-s, s]
    c64 = jnp.tile(jnp.concatenate(cs, axis=1), (NB, 1))
    s64 = jnp.tile(jnp.concatenate(sn, axis=1), (NB, 1))
    c64 = jnp.concatenate([c64, jnp.ones((NB * L, MLA_ROPE), F32)], axis=0)
    s64 = jnp.concatenate([s64, jnp.zeros((NB * L, MLA_ROPE), F32)], axis=0)
    return jnp.tile(c64, (1, 2)), jnp.tile(s64, (1, 2))


_ROPE_SWAP = np.concatenate([np.arange(16, 32), np.arange(0, 16), np.arange(48, 64), np.arange(32, 48)])


def _rope_gains(g):
    gr = g[MLA_NOPE:]
    return (g[:MLA_NOPE].reshape(1, 128), jnp.tile(gr, 2).reshape(1, 128),
            jnp.tile(gr[_ROPE_SWAP], 2).reshape(1, 128))


def mla_q_prep(proj, w_uq_r, qa_g, qn_g, c2, s2):
    tm = 512
    gn, g2, gs2 = _rope_gains(qn_g)
    vec = lambda w: pl.BlockSpec((1, w), lambda i: (0, 0))
    return pl.pallas_call(
        _mla_q_kernel,
        grid=(NT // tm,),
        in_specs=[
            pl.BlockSpec((tm, MLA_QL), lambda i: (i, _COL_CQ // MLA_QL)),
            pl.BlockSpec((MLA_QL, 3072), lambda i: (0, 0)),
            vec(MLA_QL), vec(128), vec(128), vec(128),
            pl.BlockSpec((tm, 128), lambda i: (i, 0)),
            pl.BlockSpec((tm, 128), lambda i: (i, 0)),
        ],
        out_specs=pl.BlockSpec((MLA_H, tm, MLA_HW), lambda i: (0, i, 0)),
        out_shape=jax.ShapeDtypeStruct((MLA_H, NT, MLA_HW), BF16),
        compiler_params=_cp(("arbitrary",)),
        name="mla_q_prep",
    )(proj, w_uq_r, qa_g.reshape(1, MLA_QL), gn, g2, gs2, c2, s2)


def mla_kv_prep(proj, w_ukv_r, kva_g, kn_g, c2, s2):
    tm = 512
    gn, g2, gs2 = _rope_gains(kn_g)
    vec = lambda w: pl.BlockSpec((1, w), lambda i: (0, 0))
    return pl.pallas_call(
        _mla_kv_kernel,
        grid=(NT // tm,),
        in_specs=[
            pl.BlockSpec((tm, MLA_KVL), lambda i: (i, _COL_CKV // MLA_KVL)),
            pl.BlockSpec((tm, 256), lambda i: (i, _COL_KR // 256)),
            pl.BlockSpec((MLA_KVL, 3072), lambda i: (0, 0)),
            vec(MLA_KVL), vec(128), vec(128), vec(128),
            pl.BlockSpec((tm, 128), lambda i: (i, 0)),
            pl.BlockSpec((tm, 128), lambda i: (i, 0)),
        ],
        out_specs=[pl.BlockSpec((MLA_H, tm, MLA_HW), lambda i: (0, i, 0)),
                   pl.BlockSpec((MLA_H, tm, MLA_HW), lambda i: (0, i, 0))],
        out_shape=[jax.ShapeDtypeStruct((MLA_H, NT, MLA_HW), BF16),
                   jax.ShapeDtypeStruct((MLA_H, NT, MLA_HW), BF16)],
        compiler_params=_cp(("arbitrary",)),
        name="mla_kv_prep",
    )(proj, proj, w_ukv_r, kva_g.reshape(1, MLA_KVL), gn, g2, gs2, c2, s2)


MLA_TQ = 512


def _mla_attn_kernel(q_ref, kl_ref, kc_ref, vl_ref, vc_ref, o_ref, s0_ref, s1_ref, p0_ref, p1_ref, acc_ref):
    g = pl.program_id(0)

    @pl.when(g == 0)
    def _():
        s0_ref[...] = jnp.zeros_like(s0_ref)
        s1_ref[...] = jnp.zeros_like(s1_ref)
        p0_ref[...] = jnp.ones_like(p0_ref)
        p1_ref[...] = jnp.ones_like(p1_ref)

    n_kc = (L + S) // L
    rows = MLA_TQ // (n_kc - 1)

    def tie(x, dep):
        if dep is None:
            return x
        return jnp.concatenate([x[0:16, :] + dep, x[16:, :]], axis=0)

    def stages(s_new, s_old, p_new, p_old):
        acc, dep = None, None
        for c in range(n_kc):
            keys = slice(L * c, L * (c + 1))
            k = kc_ref[...] if c == 0 else kl_ref[L * (c - 1): L * c, :]
            v = vc_ref[...] if c == 0 else vl_ref[L * (c - 1): L * c, :]
            if c < n_kc - 1:
                r = slice(rows * c, rows * (c + 1))
                s = s_old[r, :]
                p = jnp.exp((s - jnp.max(s, axis=-1, keepdims=True)).astype(BF16))
                p_old[r, :] = p
                dep = (p[0:16, 0:MLA_HW].astype(F32) * 0.0).astype(BF16)
            part = _dot(tie(p_new[:, keys], dep), v)
            acc = part if acc is None else acc + part
            s_new[:, keys] = _dot_nt(tie(q_ref[...], dep), k)
        o_ref[...] = (acc[:, :MLA_V] / acc[:, MLA_V:MLA_V + 1]).astype(o_ref.dtype)

    @pl.when(g % 2 == 0)
    def _():
        stages(s0_ref, s1_ref, p0_ref, p1_ref)

    @pl.when(g % 2 == 1)
    def _():
        stages(s1_ref, s0_ref, p1_ref, p0_ref)


def _mla_ctx_kernel(q_ref, kc_ref, vc_ref, o_ref):
    s = _dot_nt(q_ref[...], kc_ref[...])
    p = jnp.exp((s - jnp.max(s, axis=-1, keepdims=True)).astype(BF16))
    acc = _dot(p, vc_ref[...])
    o_ref[...] = (acc[:, :MLA_V] / acc[:, MLA_V:MLA_V + 1]).astype(o_ref.dtype)


def mla_attention(q, k, v):
    nq = S // MLA_TQ
    n_steps = NB * MLA_H * nq

    def bhi(g):
        return g // (MLA_H * nq), (g // nq) % MLA_H, g % nq

    def cur(g):
        return bhi(jnp.minimum(g, n_steps - 1))

    def prev(g):
        return bhi(jnp.maximum(g - 2, 0))

    ob = pl.pallas_call(
        _mla_attn_kernel,
        grid=(n_steps + 2,),
        in_specs=[
            pl.BlockSpec((None, MLA_TQ, MLA_HW), lambda g: (cur(g)[1], cur(g)[0] * nq + cur(g)[2], 0)),
            pl.BlockSpec((None, S, MLA_HW), lambda g: (cur(g)[1], cur(g)[0], 0)),
            pl.BlockSpec((None, L, MLA_HW), lambda g: (cur(g)[1], CTX_BLK0 + cur(g)[0], 0)),
            pl.BlockSpec((None, S, MLA_HW), lambda g: (prev(g)[1], prev(g)[0], 0)),
            pl.BlockSpec((None, L, MLA_HW), lambda g: (prev(g)[1], CTX_BLK0 + prev(g)[0], 0)),
        ],
        out_specs=pl.BlockSpec((MLA_TQ, MLA_V), lambda g: (prev(g)[0] * nq + prev(g)[2], prev(g)[1])),
        out_shape=jax.ShapeDtypeStruct((NL, MLA_W), BF16),
        scratch_shapes=[pltpu.VMEM((MLA_TQ, L + S), F32), pltpu.VMEM((MLA_TQ, L + S), F32),
                        pltpu.VMEM((MLA_TQ, L + S), BF16), pltpu.VMEM((MLA_TQ, L + S), BF16),
                        pltpu.VMEM((MLA_TQ, MLA_HW), F32)],
        compiler_params=_cp(("arbitrary",)),
        name="mla_attention",
    )(q, k, k, v, v)
    ob_ctx = pl.pallas_call(
        _mla_ctx_kernel,
        grid=(NB, MLA_H),
        in_specs=[
            pl.BlockSpec((None, L, MLA_HW), lambda b, h: (h, CTX_BLK0 + b, 0)),
            pl.BlockSpec((None, L, MLA_HW), lambda b, h: (h, CTX_BLK0 + b, 0)),
            pl.BlockSpec((None, L, MLA_HW), lambda b, h: (h, CTX_BLK0 + b, 0)),
        ],
        out_specs=pl.BlockSpec((L, MLA_V), lambda b, h: (b, h)),
        out_shape=jax.ShapeDtypeStruct((NB * L, MLA_W), BF16),
        compiler_params=_cp(("arbitrary", "arbitrary")),
        name="mla_attention_ctx",
    )(q, k, v)
    return ob, ob_ctx


N_CH_C = L // CH
N_CH_L = S // CH
N_CH = N_CH_C + N_CH_L


def _gla_kernel(qf_ref, kf_ref, vf_ref, af_ref, qb_ref, kb_ref, vb_ref, ab_ref,
                wg_ref, bg_ref, of_ref, ob_ref, st_ref):
    t = pl.program_id(1)

    @pl.when(t == 0)
    def _():
        st_ref[...] = jnp.zeros_like(st_ref)

    row = lax.broadcasted_iota(jnp.int32, (CH, CH), 0)
    col = lax.broadcasted_iota(jnp.int32, (CH, CH), 1)
    dirs = (
        (qf_ref, kf_ref, vf_ref, af_ref, of_ref, col <= row, CH - 1),
        (qb_ref, kb_ref, vb_ref, ab_ref, ob_ref, col >= row, 0),
    )
    for d, (q_ref, k_ref, v_ref, a_ref, o_ref, keep, last) in enumerate(dirs):
        z = _dot(a_ref[...], wg_ref[d]) + bg_ref[d]
        g = (jnp.minimum(z, 0.0) - jnp.log(1.0 + jnp.exp(-jnp.abs(z)))) / GLA_TAU
        b = jnp.dot(keep.astype(F32), g, preferred_element_type=F32,
                    precision=lax.Precision.HIGHEST)
        b_end = b[last:last + 1, :]
        q = q_ref[...].astype(F32) * (GLA_DK ** -0.5)
        k = k_ref[...].astype(F32)
        qd = (q * jnp.exp(b)).astype(BF16)
        kd = (k * jnp.exp(-b)).astype(BF16)
        ke = (k * jnp.exp(b_end - b)).astype(BF16)
        e_end = jnp.exp(b_end)
        v = v_ref[...]
        for h in range(GLA_H):
            ksl = slice(GLA_DK * h, GLA_DK * (h + 1))
            vsl = slice(GLA_DV * h, GLA_DV * (h + 1))
            att = jnp.where(keep, _dot_nt(qd[:, ksl], kd[:, ksl]), 0.0)
            st = st_ref[d, h]
            o = _dot_nt(qd[:, ksl], st.astype(BF16)) + _dot(att.astype(BF16), v[:, vsl])
            o_ref[:, vsl] = o
            st_ref[d, h] = st * e_end[:, ksl] + _dot_tn(v[:, vsl], ke[:, ksl])


def gla_scan(proj, wg_f, bg_f, wg_b, bg_b):
    wg = jnp.zeros((2, 128, GLA_KW), F32)
    wg = wg.at[0, 0:GLA_RANK].set(wg_f).at[1, GLA_RANK:2 * GLA_RANK].set(wg_b).astype(BF16)
    bg = jnp.stack([bg_f, bg_b]).reshape(2, 1, GLA_KW)

    ctx0 = NL // CH

    def fwd(b, t):
        return jnp.where(t < N_CH_C, ctx0 + N_CH_C * b + t, N_CH_L * b + t - N_CH_C)

    def bwd(b, t):
        return jnp.where(t < N_CH_C, ctx0 + N_CH_C * b + N_CH_C - 1 - t, N_CH_L * b + N_CH - 1 - t)

    def specs(rowfn):
        return [
            pl.BlockSpec((CH, GLA_KW), lambda b, t: (rowfn(b, t), _COL_GQ // GLA_KW)),
            pl.BlockSpec((CH, GLA_KW), lambda b, t: (rowfn(b, t), _COL_GK // GLA_KW)),
            pl.BlockSpec((CH, GLA_W), lambda b, t: (rowfn(b, t), _COL_GV // GLA_W)),
            pl.BlockSpec((CH, 128), lambda b, t: (rowfn(b, t), _COL_GA // 128)),
        ]

    return pl.pallas_call(
        _gla_kernel,
        grid=(NB, N_CH),
        in_specs=specs(fwd) + specs(bwd) + [
            pl.BlockSpec((2, 128, GLA_KW), lambda b, t: (0, 0, 0)),
            pl.BlockSpec((2, 1, GLA_KW), lambda b, t: (0, 0, 0)),
        ],
        out_specs=[pl.BlockSpec((CH, GLA_W), lambda b, t: (fwd(b, t), 0)),
                   pl.BlockSpec((CH, GLA_W), lambda b, t: (bwd(b, t), 0))],
        out_shape=[jax.ShapeDtypeStruct((NT, GLA_W), F32)] * 2,
        scratch_shapes=[pltpu.VMEM((2, GLA_H, GLA_DV, GLA_DK), F32)],
        compiler_params=_cp(("arbitrary", "arbitrary")),
        name="gla_scan",
    )(proj, proj, proj, proj, proj, proj, proj, proj, wg, bg)


def _gla_out_kernel(of_ref, ob_ref, g_ref, gn_ref, o_ref):
    for h in range(GLA_H):
        sl = slice(GLA_DV * h, GLA_DV * (h + 1))
        o = _rms_rows(of_ref[:, sl] + ob_ref[:, sl], gn_ref[...])
        gate = g_ref[:, sl].astype(F32)
        o_ref[:, sl] = (o * (gate * _sigmoid(gate))).astype(o_ref.dtype)


def gla_output(o_f, o_b, proj, on_g):
    tm = 512
    return pl.pallas_call(
        _gla_out_kernel,
        grid=(NT // tm,),
        in_specs=[
            pl.BlockSpec((tm, GLA_W), lambda i: (i, 0)),
            pl.BlockSpec((tm, GLA_W), lambda i: (i, 0)),
            pl.BlockSpec((tm, GLA_W), lambda i: (i, _COL_GG // GLA_W)),
            pl.BlockSpec((1, GLA_DV), lambda i: (0, 0)),
        ],
        out_specs=pl.BlockSpec((tm, GLA_W), lambda i: (i, 0)),
        out_shape=jax.ShapeDtypeStruct((NT, GLA_W), BF16),
        compiler_params=_cp(("arbitrary",)),
        name="gla_output",
    )(o_f, o_b, proj, on_g.reshape(1, GLA_DV))


OUT_TM = NB * L


def _out_proj_kernel(oa_ref, obl_ref, obc_ref, oc_ref, w_ref, x_ref, g_ref, o_ref):
    ob = jnp.where(pl.program_id(1) < NL // OUT_TM, obl_ref[...], obc_ref[...])
    a = jnp.concatenate([oa_ref[...], ob, oc_ref[...]], axis=1)
    o_ref[...] = x_ref[...] + g_ref[...] * _dot(a, w_ref[...])


def out_projection(oa, ob, ob_ctx, oc, w_out, layer, xs, mod, k_gate):
    tm, tn = OUT_TM, 1024
    nj = D // tn
    return pl.pallas_call(
        _out_proj_kernel,
        grid=(nj, NT // tm),
        in_specs=[
            pl.BlockSpec((tm, NA_W), lambda j, i: (i, 0)),
            pl.BlockSpec((tm, MLA_W), lambda j, i: (jnp.minimum(i, NL // tm - 1), 0)),
            pl.BlockSpec((tm, MLA_W), lambda j, i: (0, 0)),
            pl.BlockSpec((tm, GLA_W), lambda j, i: (i, 0)),
            pl.BlockSpec((None, D, tn), lambda j, i: (layer, 0, j)),
            pl.BlockSpec((tm, tn), lambda j, i: (i, j)),
            pl.BlockSpec((None, 1, tn), lambda j, i: (_row_group(i, tm), 0, k_gate * nj + j)),
        ],
        out_specs=pl.BlockSpec((tm, tn), lambda j, i: (i, j)),
        out_shape=jax.ShapeDtypeStruct((NT, D), F32),
        compiler_params=_cp(("arbitrary", "arbitrary")),
        name="out_projection",
    )(oa, ob, ob_ctx, oc, w_out, xs, mod)


def _route(logits, router_bias):
    scores = jax.nn.sigmoid(logits)
    grouped = (scores + router_bias.astype(F32)).reshape(NT, E_GROUPS, E_PER)

    def top2(a):
        idx = lax.broadcasted_iota(jnp.int32, a.shape, a.ndim - 1)
        i1 = jnp.argmax(a, axis=-1).astype(jnp.int32)
        rest = jnp.where(idx == i1[..., None], -jnp.inf, a)
        i2 = jnp.argmax(rest, axis=-1).astype(jnp.int32)
        return jnp.max(a, axis=-1), jnp.max(rest, axis=-1), i1, i2

    m1, m2, _, _ = top2(grouped)
    grp = jnp.argmax(m1 + m2, axis=-1).astype(jnp.int32)
    gsel = lax.broadcasted_iota(jnp.int32, (NT, E_GROUPS, E_PER), 1) == grp[:, None, None]
    in_group = jnp.sum(jnp.where(gsel, grouped, 0.0), axis=1)
    _, _, l1, l2 = top2(in_group)
    expert_idx = grp[:, None] * E_PER + jnp.stack([l1, l2], axis=-1)
    esel = lax.broadcasted_iota(jnp.int32, (NT, TOPK, E), 2) == expert_idx[:, :, None]
    w = jnp.sum(jnp.where(esel, scores[:, None, :], 0.0), axis=-1)
    return expert_idx, w / jnp.sum(w, axis=-1, keepdims=True)


def _dispatch_plan(expert_idx):
    flat_e = expert_idx.reshape(NK)
    onehot = (flat_e[:, None] == jnp.arange(E, dtype=jnp.int32)[None, :]).astype(jnp.int32)
    csum = jnp.cumsum(onehot, axis=0)
    counts = csum[-1]
    rank = jnp.take_along_axis(csum, flat_e[:, None], axis=1)[:, 0] - 1
    padded = (counts + MOE_R - 1) // MOE_R * MOE_R
    pad_end = jnp.cumsum(padded)
    pad_start = pad_end - padded
    dest = (pad_start[flat_e] + rank).astype(jnp.int32)
    row_token = jnp.zeros((MOE_P,), jnp.int32).at[dest].set(jnp.arange(NK, dtype=jnp.int32) // TOPK)
    block_expert = jnp.minimum(
        jnp.searchsorted(pad_end, jnp.arange(MOE_NB, dtype=jnp.int32) * MOE_R, side='right'),
        E - 1).astype(jnp.int32)
    n_active = (pad_end[-1] // MOE_R).astype(jnp.int32).reshape(1)
    blk0 = jnp.arange(MOE_NB, dtype=jnp.int32) * MOE_R
    n_valid = jnp.clip(counts[block_expert] - (blk0 - pad_start[block_expert]), 0, MOE_R).astype(jnp.int32)
    return dest, row_token, block_expert, n_active, n_valid


DH = D // 2


DMA_UNROLL = 8


def _for_rows(n, body):
    full = n // DMA_UNROLL

    def group(t, c):
        for u in range(DMA_UNROLL):
            body(t * DMA_UNROLL + u)
        return c

    def single(r, c):
        body(r)
        return c

    lax.fori_loop(0, full, group, 0)
    lax.fori_loop(full * DMA_UNROLL, n, single, 0)


def _gather_rows_kernel(tok_ref, nv_ref, h_ref, o_ref, sem):
    i = pl.program_id(0)
    base = i * MOE_R
    nv = nv_ref[i]

    @pl.when(nv < MOE_R)
    def _():
        o_ref[...] = jnp.zeros_like(o_ref)

    def copy(r):
        return pltpu.make_async_copy(h_ref.at[pl.ds(tok_ref[base + r], 1)], o_ref.at[pl.ds(r, 1)], sem)

    _for_rows(nv, lambda r: copy(r).start())
    _for_rows(nv, lambda r: copy(r).wait())


def gather_rows(row_token, n_valid, h):
    return pl.pallas_call(
        _gather_rows_kernel,
        grid_spec=pltpu.PrefetchScalarGridSpec(
            num_scalar_prefetch=2,
            grid=(MOE_NB,),
            in_specs=[pl.BlockSpec(memory_space=pl.ANY)],
            out_specs=pl.BlockSpec((MOE_R, DH), lambda i, tok, nv: (i, 0)),
            scratch_shapes=[pltpu.SemaphoreType.DMA(())],
        ),
        out_shape=jax.ShapeDtypeStruct((MOE_P, DH), jnp.uint32),
        compiler_params=_cp(("arbitrary",)),
        name="moe_gather_rows",
    )(row_token, n_valid, h)


def _new_expert(be_ref, i):
    return jnp.logical_or(i == 0, be_ref[i] != be_ref[jnp.maximum(i - 1, 0)])


def _ffn_up_kernel(be_ref, na_ref, x_ref, w1_ref, w3_ref, o_ref, w1b_ref, w3b_ref):
    i = pl.program_id(1)
    active = i < na_ref[0]

    @pl.when(jnp.logical_and(active, _new_expert(be_ref, i)))
    def _():
        w1b_ref[...] = w1_ref[...].astype(BF16)
        w3b_ref[...] = w3_ref[...].astype(BF16)

    @pl.when(active)
    def _():
        lo, hi = _unpack_pairs(x_ref[...])
        lo, hi = lo.astype(BF16), hi.astype(BF16)
        a = _dot(lo, w1b_ref[0:DH, :]) + _dot(hi, w1b_ref[DH:, :])
        b = _dot(lo, w3b_ref[0:DH, :]) + _dot(hi, w3b_ref[DH:, :])
        o_ref[...] = (a * _sigmoid(a) * b).astype(o_ref.dtype)

    @pl.when(jnp.logical_not(active))
    def _():
        o_ref[...] = jnp.zeros_like(o_ref)


def ffn_up(block_expert, n_active, xb, w1, w3, layer):
    last = lambda i, na: jnp.minimum(i, na[0] - 1)
    return pl.pallas_call(
        _ffn_up_kernel,
        grid_spec=pltpu.PrefetchScalarGridSpec(
            num_scalar_prefetch=2,
            grid=(FF // MOE_TF, MOE_NB),
            in_specs=[
                pl.BlockSpec((MOE_R, DH), lambda j, i, be, na: (last(i, na), 0)),
                pl.BlockSpec((None, None, D, MOE_TF), lambda j, i, be, na: (layer, be[i], 0, j)),
                pl.BlockSpec((None, None, D, MOE_TF), lambda j, i, be, na: (layer, be[i], 0, j)),
            ],
            out_specs=pl.BlockSpec((MOE_R, MOE_TF), lambda j, i, be, na: (i, j)),
            scratch_shapes=[pltpu.VMEM((D, MOE_TF), BF16), pltpu.VMEM((D, MOE_TF), BF16)],
        ),
        out_shape=jax.ShapeDtypeStruct((MOE_P, FF), BF16),
        compiler_params=_cp(("arbitrary", "arbitrary")),
        name="moe_ffn_up",
    )(block_expert, n_active, xb, w1, w3)


MOE_TN = 2048


def _ffn_down_kernel(be_ref, na_ref, h_ref, w2_ref, o_ref, w2b_ref):
    i = pl.program_id(1)
    active = i < na_ref[0]

    @pl.when(jnp.logical_and(active, _new_expert(be_ref, i)))
    def _():
        w2b_ref[...] = w2_ref[...].astype(BF16)

    @pl.when(active)
    def _():
        y = _dot(h_ref[...], w2b_ref[...])
        o_ref[...] = _pack_pairs(y[:, :MOE_TN // 2], y[:, MOE_TN // 2:])

    @pl.when(jnp.logical_not(active))
    def _():
        o_ref[...] = jnp.zeros_like(o_ref)


def ffn_down(block_expert, n_active, hmid, w2, layer):
    last = lambda i, na: jnp.minimum(i, na[0] - 1)
    return pl.pallas_call(
        _ffn_down_kernel,
        grid_spec=pltpu.PrefetchScalarGridSpec(
            num_scalar_prefetch=2,
            grid=(D // MOE_TN, MOE_NB),
            in_specs=[
                pl.BlockSpec((MOE_R, FF), lambda n, i, be, na: (last(i, na), 0)),
                pl.BlockSpec((None, None, FF, MOE_TN), lambda n, i, be, na: (layer, be[i], 0, n)),
            ],
            out_specs=pl.BlockSpec((MOE_R, MOE_TN // 2), lambda n, i, be, na: (i, n)),
            scratch_shapes=[pltpu.VMEM((FF, MOE_TN), BF16)],
        ),
        out_shape=jax.ShapeDtypeStruct((MOE_P, DH), jnp.uint32),
        compiler_params=_cp(("arbitrary", "arbitrary")),
        name="moe_ffn_down",
    )(block_expert, n_active, hmid, w2)


CMB_T = 256


def _combine_kernel(pos_ref, yb_ref, gate_ref, x_ref, g_ref, o_ref, buf_ref, sem):
    base = pl.program_id(0) * CMB_T

    def copy(r, k):
        p = pos_ref[(base + r) * TOPK + k]
        return pltpu.make_async_copy(yb_ref.at[pl.ds(p, 1)], buf_ref.at[k, pl.ds(r, 1)], sem.at[k])

    def start(r, c):
        copy(r, 0).start()
        copy(r, 1).start()
        return c

    def wait(r, c):
        copy(r, 0).wait()
        copy(r, 1).wait()
        return c

    lax.fori_loop(0, CMB_T, start, 0, unroll=DMA_UNROLL)
    lax.fori_loop(0, CMB_T, wait, 0, unroll=DMA_UNROLL)
    gate = gate_ref[...]
    g0, g1 = gate[:, 0:1], gate[:, 1:2]
    half = MOE_TN // 2
    for n in range(D // MOE_TN):
        lo0, hi0 = _unpack_pairs(buf_ref[0, :, half * n: half * (n + 1)])
        lo1, hi1 = _unpack_pairs(buf_ref[1, :, half * n: half * (n + 1)])
        for part, y in enumerate((lo0 * g0 + lo1 * g1, hi0 * g0 + hi1 * g1)):
            sl = slice(MOE_TN * n + half * part, MOE_TN * n + half * (part + 1))
            o_ref[:, sl] = x_ref[:, sl] + g_ref[:, sl] * y


def moe_combine(dest, yb, gates, xs, mod, k_gate, out_rows):
    return pl.pallas_call(
        _combine_kernel,
        grid_spec=pltpu.PrefetchScalarGridSpec(
            num_scalar_prefetch=1,
            grid=(out_rows // CMB_T,),
            in_specs=[
                pl.BlockSpec(memory_space=pl.ANY),
                pl.BlockSpec((CMB_T, TOPK), lambda i, pos: (i, 0)),
                pl.BlockSpec((CMB_T, D), lambda i, pos: (i, 0)),
                pl.BlockSpec((None, 1, D), lambda i, pos: (_row_group(i, CMB_T), 0, k_gate)),
            ],
            out_specs=pl.BlockSpec((CMB_T, D), lambda i, pos: (i, 0)),
            scratch_shapes=[pltpu.VMEM((TOPK, CMB_T, DH), jnp.uint32), pltpu.SemaphoreType.DMA((TOPK,))],
        ),
        out_shape=jax.ShapeDtypeStruct((out_rows, D), F32),
        compiler_params=_cp(("arbitrary",)),
        name="moe_combine",
    )(dest, yb, gates, xs, mod)


IN_W = 9312


W_RB = 64
_RB_KR_SRC = 4608 // W_RB
_RB_GA_SRC = 9280 // W_RB
_RB_KR = _COL_KR // W_RB
_RB_GA = _COL_GA // W_RB


def _w_in_layout_kernel(w_ref, o_ref):
    j = pl.program_id(1)

    @pl.when(j < _RB_KR + 2)
    def _():
        o_ref[...] = w_ref[...].astype(BF16)

    @pl.when(jnp.logical_and(j >= _RB_KR + 2, j < _RB_GA))
    def _():
        x = w_ref[...]
        o_ref[...] = jnp.concatenate([x[16:32], x[0:16], x[48:64], x[32:48]], axis=0).astype(BF16)

    @pl.when(j == _RB_GA)
    def _():
        row = lax.broadcasted_iota(jnp.int32, (W_RB, D), 0)
        o_ref[...] = jnp.where(row < 2 * GLA_RANK, w_ref[...], 0.0).astype(BF16)

    @pl.when(j > _RB_GA)
    def _():
        o_ref[...] = jnp.zeros_like(o_ref)


def _w_in_layout(w):
    w_t = jnp.swapaxes(w, 1, 2)

    def src(j):
        seg2 = j + MLA_ROPE // W_RB
        return jnp.where(j < _RB_KR_SRC, j, jnp.where(j < _RB_KR, seg2, jnp.where(j < _RB_GA, _RB_KR_SRC, _RB_GA_SRC)))

    return pl.pallas_call(
        _w_in_layout_kernel,
        grid=(DEPTH, PW // W_RB),
        in_specs=[pl.BlockSpec((None, W_RB, D), lambda l, j: (l, src(j), 0))],
        out_specs=pl.BlockSpec((None, W_RB, D), lambda l, j: (l, j, 0)),
        out_shape=jax.ShapeDtypeStruct((DEPTH, PW, D), BF16),
        compiler_params=_cp(("arbitrary", "arbitrary")),
        name="w_in_layout",
    )(w_t)


def _w_uq_cols():
    nope = [MLA_QK * h + j for h in range(MLA_H) for j in range(MLA_NOPE)]
    rope = [MLA_QK * h + MLA_NOPE + d for h in range(MLA_H) for d in range(MLA_ROPE)]
    rope_s = [MLA_QK * h + MLA_NOPE + int(d) for h in range(MLA_H) for d in _ROPE_SWAP]
    return np.array(nope + rope + rope_s, np.int32)


def _w_ukv_cols():
    kn = [(MLA_NOPE + MLA_V) * h + j for h in range(MLA_H) for j in range(MLA_NOPE)]
    vv = [(MLA_NOPE + MLA_V) * h + MLA_NOPE + j for h in range(MLA_H) for j in range(MLA_V)]
    return np.array(kn + vv, np.int32)


def token_mixing_layer(xs, mod, layer, norm1, w_in_b, w_out_b, na_qn, na_kn, na_rpb, mla_qa, mla_kva, w_uq, w_ukv,
                       mla_qn, mla_kn, gwf, gbf, gwb, gbb, gla_on, c2, s2):
    h = norm_modulate(xs, norm1.reshape(1, D), mod, 0, 1, BF16)
    proj = matmul(h, w_in_b, layer, NT // 8, 512, BF16)
    oa = na_attention(proj, na_rpb, na_qn, na_kn)
    q = mla_q_prep(proj, w_uq[:, _w_uq_cols()].astype(BF16), mla_qa, mla_qn, c2, s2)
    k, v = mla_kv_prep(proj, w_ukv[:, _w_ukv_cols()].astype(BF16), mla_kva, mla_kn, c2, s2)
    ob, ob_ctx = mla_attention(q, k, v)
    o_f, o_b = gla_scan(proj, gwf, gbf, gwb, gbb)
    oc = gla_output(o_f, o_b, proj, gla_on)
    return out_projection(oa, ob, ob_ctx, oc, w_out_b, layer, xs, mod, 2)


def moe_layer(xs, mod, norm2, w_router_p, router_bias, w1, w3, w2, layer, out_rows):
    h, logits = norm_modulate(xs, norm2.reshape(1, D), mod, 3, 4, None, w_router_p)
    expert_idx, gates = _route(logits[:, :E], router_bias)
    dest, row_token, block_expert, n_active, n_valid = _dispatch_plan(expert_idx)
    xb = gather_rows(row_token, n_valid, h)
    hmid = ffn_up(block_expert, n_active, xb, w1, w3, layer)
    yb = ffn_down(block_expert, n_active, hmid, w2, layer)
    return moe_combine(dest, yb, gates, xs, mod, 5, out_rows)


def kernel(x, c, ctx, c_ctx, w_ada, b_ada, norm1, norm2, w_in, w_out, na_q_norm, na_k_norm, na_rpb, mla_qa_norm, mla_kva_norm, mla_w_uq, mla_w_ukv, mla_q_norm, mla_k_norm, gla_w_gate_f, gla_b_gate_f, gla_w_gate_b, gla_b_gate_b, gla_out_norm, w_router, router_bias, moe_w1, moe_w3, moe_w2):
    cond = jnp.concatenate([c, c_ctx[None, :], jnp.zeros((8 - NB - 1, D), F32)], axis=0)
    mods = ada_modulation(cond, w_ada, b_ada)
    xs = jnp.concatenate([x.reshape(NL, D), ctx.reshape(NB * L, D)], axis=0)
    c2, s2 = _rope_tables()
    w_router_p = jnp.concatenate([w_router, jnp.zeros((D, 128 - E), F32)], axis=1)
    w_in_b = _w_in_layout(w_in)
    w_out_b = w_out.astype(BF16)
    for l in range(DEPTH):
        mod = mods[l].reshape(8, 1, 6 * D)
        xs = token_mixing_layer(xs, mod, l, norm1[l], w_in_b, w_out_b, na_q_norm[l], na_k_norm[l], na_rpb[l],
                                mla_qa_norm[l], mla_kva_norm[l], mla_w_uq[l], mla_w_ukv[l],
                                mla_q_norm[l], mla_k_norm[l], gla_w_gate_f[l], gla_b_gate_f[l],
                                gla_w_gate_b[l], gla_b_gate_b[l], gla_out_norm[l], c2, s2)
        xs = moe_layer(xs, mod, norm2[l], w_router_p, router_bias, moe_w1, moe_w3, moe_w2, l,
                       NT if l < DEPTH - 1 else NL)
    return xs.reshape(NB, S, D)
```

```python
import functools

import numpy as np
import jax
import jax.numpy as jnp
from jax import lax
from jax.experimental import pallas as pl
from jax.experimental.pallas import tpu as pltpu

F32 = jnp.float32
BF16 = jnp.bfloat16

D = 4096
NB = 2
S = 4096
L = 256
DEPTH = 2
GW = 64
EPS = 1e-6
NL = NB * S
NT = NL + NB * L
NA_H, NA_D = 8, 128
NA_W = NA_H * NA_D
NA_R, NA_C = 8, 16
MLA_H = 12
MLA_QL, MLA_KVL = 1024, 512
MLA_NOPE, MLA_ROPE, MLA_V = 128, 64, 128
MLA_QK = MLA_NOPE + MLA_ROPE
MLA_W = MLA_H * MLA_V
GLA_H, GLA_DK, GLA_DV = 6, 128, 256
GLA_KW = GLA_H * GLA_DK
GLA_W = GLA_H * GLA_DV
GLA_RANK = 16
GLA_TAU = 16.0
CH = 128
E = 16
E_GROUPS = 4
E_PER = E // E_GROUPS
TOPK = 2
FF = 1024
ROPE_BASE = 10000.0

_COL_NAQ, _COL_NAK, _COL_NAV = 0, 1024, 2048
_COL_CQ, _COL_CKV = 3072, 4096
_COL_GQ, _COL_GK, _COL_GV, _COL_GG = 4608, 5376, 6144, 7680
_COL_KR = 9216
_COL_GA = 9472
PW = 9728

BLK = 256
NBLK_L = S // BLK
CTX_BLK0 = NL // BLK
MOE_R = 256
MOE_TF = 512
NK = NT * TOPK
MOE_NB = (NK + E * (MOE_R - 1)) // MOE_R
MOE_P = MOE_NB * MOE_R

NEG = -1e30
VMEM_LIMIT = 56 * 1024 * 1024


def _cp(sem, vmem=VMEM_LIMIT):
    return pltpu.CompilerParams(dimension_semantics=sem, vmem_limit_bytes=vmem)


def _dot(a, b):
    return jnp.dot(a, b, preferred_element_type=F32)


def _dot_nt(a, b):
    return lax.dot_general(a, b, (((1,), (1,)), ((), ())), preferred_element_type=F32)


def _dot_tn(a, b):
    return lax.dot_general(a, b, (((0,), (0,)), ((), ())), preferred_element_type=F32)


def _sigmoid(x):
    return 1.0 / (1.0 + jnp.exp(-x))


def _pack_pairs(lo, hi):
    lo_w = lax.bitcast_convert_type(lo.astype(BF16).astype(F32), jnp.uint32)
    hi_w = lax.bitcast_convert_type(hi.astype(BF16).astype(F32), jnp.uint32)
    return lax.shift_right_logical(lo_w, jnp.uint32(16)) | (hi_w & jnp.uint32(0xFFFF0000))


def _unpack_pairs(w):
    lo = lax.bitcast_convert_type(lax.shift_left(w, jnp.uint32(16)), F32)
    hi = lax.bitcast_convert_type(w & jnp.uint32(0xFFFF0000), F32)
    return lo, hi


def _row_group(i, tm):
    r0 = i * tm
    return jnp.where(r0 >= NL, 2, r0 // S)


def _ada_kernel(s_ref, w_ref, b_ref, o_ref):
    s = s_ref[...]
    s = s * _sigmoid(s)
    o_ref[...] = _dot(s.astype(BF16), w_ref[...].astype(BF16)) + b_ref[...]


def ada_modulation(cond, w_ada, b_ada):
    tn = 512
    n = 6 * D
    return pl.pallas_call(
        _ada_kernel,
        grid=(DEPTH, n // tn),
        in_specs=[
            pl.BlockSpec((8, D), lambda l, j: (0, 0)),
            pl.BlockSpec((None, D, tn), lambda l, j: (l, 0, j)),
            pl.BlockSpec((None, 1, tn), lambda l, j: (l, 0, j)),
        ],
        out_specs=pl.BlockSpec((None, 8, tn), lambda l, j: (l, 0, j)),
        out_shape=jax.ShapeDtypeStruct((DEPTH, 8, n), F32),
        compiler_params=_cp(("arbitrary", "arbitrary")),
        name="ada_modulation",
    )(cond, w_ada, b_ada.reshape(DEPTH, 1, n))


def _normmod(x, g, sh, sc):
    y = x * lax.rsqrt(jnp.mean(x * x, axis=-1, keepdims=True) + EPS) * g
    return y * (1.0 + sc) + sh


def _normmod_kernel(x_ref, g_ref, sh_ref, sc_ref, h_ref):
    h_ref[...] = _normmod(x_ref[...], g_ref[...], sh_ref[...], sc_ref[...]).astype(h_ref.dtype)


def _normmod_router_kernel(x_ref, g_ref, sh_ref, sc_ref, wr_ref, h_ref, lg_ref):
    h = _normmod(x_ref[...], g_ref[...], sh_ref[...], sc_ref[...])
    h_ref[...] = _pack_pairs(h[:, :D // 2], h[:, D // 2:])
    lg_ref[...] = jnp.dot(h, wr_ref[...], preferred_element_type=F32,
                          precision=lax.Precision.HIGHEST)


def norm_modulate(xs, gain, mod, k_shift, k_scale, out_dtype, w_router=None):
    tm = 512
    in_specs = [
        pl.BlockSpec((tm, D), lambda i: (i, 0)),
        pl.BlockSpec((1, D), lambda i: (0, 0)),
        pl.BlockSpec((None, 1, D), lambda i: (_row_group(i, tm), 0, k_shift)),
        pl.BlockSpec((None, 1, D), lambda i: (_row_group(i, tm), 0, k_scale)),
    ]
    if w_router is None:
        return pl.pallas_call(
            _normmod_kernel, grid=(NT // tm,), in_specs=in_specs,
            out_specs=pl.BlockSpec((tm, D), lambda i: (i, 0)),
            out_shape=jax.ShapeDtypeStruct((NT, D), out_dtype),
            compiler_params=_cp(("arbitrary",)), name="norm_modulate",
        )(xs, gain, mod, mod)
    return pl.pallas_call(
        _normmod_router_kernel, grid=(NT // tm,),
        in_specs=in_specs + [pl.BlockSpec((D, 128), lambda i: (0, 0))],
        out_specs=[pl.BlockSpec((tm, D // 2), lambda i: (i, 0)), pl.BlockSpec((tm, 128), lambda i: (i, 0))],
        out_shape=[jax.ShapeDtypeStruct((NT, D // 2), jnp.uint32), jax.ShapeDtypeStruct((NT, 128), F32)],
        compiler_params=_cp(("arbitrary",)), name="norm_modulate_router",
    )(xs, gain, mod, mod, w_router)


def _mm_kernel(a_ref, b_ref, o_ref):
    o_ref[...] = _dot_nt(a_ref[...], b_ref[...]).astype(o_ref.dtype)


def matmul(a, b, layer, tm, tn, out_dtype):
    m, k = a.shape
    n = b.shape[1]
    return pl.pallas_call(
        _mm_kernel,
        grid=(m // tm, n // tn),
        in_specs=[pl.BlockSpec((tm, k), lambda i, j: (i, 0)),
                  pl.BlockSpec((None, tn, k), lambda i, j: (layer, j, 0))],
        out_specs=pl.BlockSpec((tm, tn), lambda i, j: (i, j)),
        out_shape=jax.ShapeDtypeStruct((m, n), out_dtype),
        compiler_params=_cp(("arbitrary", "arbitrary")),
        name="matmul",
    )(a, b)


def _qblk(b, i):
    return jnp.where(i == 0, CTX_BLK0 + b, b * NBLK_L + i - 1)


def _rms_rows(x, g):
    return x * lax.rsqrt(jnp.mean(x * x, axis=-1, keepdims=True) + EPS) * g


NA_KROWS = 12
NA_KW = NA_KROWS * GW


NA_HG = 4
NA_GW = NA_HG * NA_D


def _na_kernel(q_ref, kl_ref, kc_ref, vl_ref, vc_ref, bias_ref, qn_ref, kn_ref, o_ref, ks_ref):
    i = pl.program_id(2)
    heads = [slice(NA_D * h, NA_D * (h + 1)) for h in range(NA_HG)]

    @pl.when(i == 0)
    def _():
        for hs in heads:
            ks_ref[0:L, hs] = _rms_rows(kc_ref[:, hs].astype(F32), kn_ref[...]).astype(BF16)
            ks_ref[L:, hs] = _rms_rows(kl_ref[:, hs].astype(F32), kn_ref[...]).astype(BF16)

    def query(hs):
        return (_rms_rows(q_ref[:, hs].astype(F32), qn_ref[...]) * (NA_D ** -0.5)).astype(BF16)

    @pl.when(i == 0)
    def _():
        for hs in heads:
            s_c = _dot_nt(query(hs), ks_ref[0:L, hs])
            p = jnp.exp(s_c - jnp.max(s_c, axis=-1, keepdims=True))
            l = jnp.sum(p, axis=-1, keepdims=True)
            o_ref[:, hs] = (_dot(p.astype(BF16), vc_ref[:, hs]) / l).astype(o_ref.dtype)

    @pl.when(i > 0)
    def _():
        row0 = jnp.clip(4 * (i - 1) - 4, 0, GW - NA_KROWS)
        k0 = pl.multiple_of(row0 * GW, BLK)
        for h, hs in enumerate(heads):
            q = query(hs)
            s_c = _dot_nt(q, ks_ref[0:L, hs])
            s_l = _dot_nt(q, ks_ref[pl.ds(L + k0, NA_KW), hs]) + bias_ref[h]
            m = jnp.maximum(jnp.max(s_c, axis=-1, keepdims=True), jnp.max(s_l, axis=-1, keepdims=True))
            p_c = jnp.exp(s_c - m)
            p_l = jnp.exp(s_l - m)
            l = jnp.sum(p_c, axis=-1, keepdims=True) + jnp.sum(p_l, axis=-1, keepdims=True)
            o = _dot(p_l.astype(BF16), vl_ref[pl.ds(k0, NA_KW), hs]) + _dot(p_c.astype(BF16), vc_ref[:, hs])
            o_ref[:, hs] = (o / l).astype(o_ref.dtype)


def _na_bias_index():
    dr = np.zeros((3, BLK, NA_KW), np.int32)
    dc = np.zeros((3, BLK, NA_KW), np.int32)
    ok = np.zeros((3, BLK, NA_KW), bool)
    rows = S // GW
    for p, blk in enumerate((0, 5, NBLK_L - 1)):
        ks = int(np.clip(4 * blk - 4, 0, GW - NA_KROWS))
        r = 4 * blk + np.arange(BLK) // GW
        c = np.arange(BLK) % GW
        rk = ks + np.arange(NA_KW) // GW
        ck = np.arange(NA_KW) % GW
        r0 = np.clip(r - NA_R // 2, 0, rows - NA_R)
        ws = np.clip(c - NA_C // 2, 0, GW - NA_C)
        row_ok = (rk[None, :] >= r0[:, None]) & (rk[None, :] < r0[:, None] + NA_R)
        col_ok = (ck[None, :] >= ws[:, None]) & (ck[None, :] < ws[:, None] + NA_C)
        ok[p] = row_ok & col_ok
        dr[p] = np.clip(rk[None, :] - r[:, None] + NA_R - 1, 0, 2 * NA_R - 2)
        dc[p] = np.clip(ck[None, :] - c[:, None], -(NA_C - 1), NA_C - 1) + NA_C - 1
    return dr, dc, ok


def _na_bias_table(rpb):
    dr, dc, ok = _na_bias_index()
    qr, kr = BLK // GW, NA_KROWS
    dr_t = dr.reshape(3, qr, GW, kr, GW)[:, :, 0, :, 0].reshape(3 * qr * kr)
    dc_t = dc[0].reshape(qr, GW, kr, GW)[0, :, 0, :].reshape(GW * GW)
    oh_c = jnp.asarray(np.eye(2 * NA_C - 1, dtype=np.float32)[:, dc_t])
    oh_r = jnp.asarray(np.eye(2 * NA_R - 1, dtype=np.float32)[dr_t])
    hi = lax.Precision.HIGHEST
    t1 = jnp.einsum('hrd,dx->hrx', rpb.astype(F32), oh_c, precision=hi)
    t2 = jnp.einsum('yr,hrx->hyx', oh_r, t1, precision=hi)
    t2 = t2.reshape(NA_H, 3, qr, kr, GW, GW).transpose(0, 1, 2, 4, 3, 5).reshape(NA_H, 3, BLK, NA_KW)
    return jnp.where(ok[None], t2, NEG)


def na_attention(proj, rpb, qn, kn):
    bias = _na_bias_table(rpb)

    def pat(i):
        return jnp.where(i <= 1, 0, jnp.where(i == NBLK_L, 2, 1))

    cq, ck, cv = _COL_NAQ // NA_GW, _COL_NAK // NA_GW, _COL_NAV // NA_GW
    return pl.pallas_call(
        _na_kernel,
        grid=(NB, NA_H // NA_HG, NBLK_L + 1),
        in_specs=[
            pl.BlockSpec((BLK, NA_GW), lambda b, h, i: (_qblk(b, i), cq + h)),
            pl.BlockSpec((S, NA_GW), lambda b, h, i: (b, ck + h)),
            pl.BlockSpec((L, NA_GW), lambda b, h, i: (CTX_BLK0 + b, ck + h)),
            pl.BlockSpec((S, NA_GW), lambda b, h, i: (b, cv + h)),
            pl.BlockSpec((L, NA_GW), lambda b, h, i: (CTX_BLK0 + b, cv + h)),
            pl.BlockSpec((NA_HG, None, BLK, NA_KW), lambda b, h, i: (h, pat(i), 0, 0)),
            pl.BlockSpec((1, NA_D), lambda b, h, i: (0, 0)),
            pl.BlockSpec((1, NA_D), lambda b, h, i: (0, 0)),
        ],
        out_specs=pl.BlockSpec((BLK, NA_GW), lambda b, h, i: (_qblk(b, i), h)),
        out_shape=jax.ShapeDtypeStruct((NT, NA_W), BF16),
        scratch_shapes=[pltpu.VMEM((L + S, NA_GW), BF16)],
        compiler_params=_cp(("arbitrary", "arbitrary", "arbitrary")),
        name="na_attention",
    )(proj, proj, proj, proj, proj, bias, qn.reshape(1, NA_D), kn.reshape(1, NA_D))


MLA_HW = 256


def _half_mask(h, width=128):
    lane = lax.broadcasted_iota(jnp.int32, (1, width), 1)
    return (lane < 64) if h % 2 == 0 else (lane >= 64)


def _mla_q_kernel(cq_ref, w_ref, qa_ref, gn_ref, g2_ref, gs2_ref, c2_ref, s2_ref, q_ref):
    x = _rms_rows(cq_ref[...].astype(F32), qa_ref[...]).astype(BF16)
    y = _dot(x, w_ref[...])
    nw = MLA_H * MLA_NOPE
    gc = g2_ref[...] * c2_ref[...]
    gs = gs2_ref[...] * s2_ref[...]
    for p in range(MLA_H // 2):
        r1 = y[:, nw + 128 * p: nw + 128 * (p + 1)]
        r2 = y[:, nw + 768 + 128 * p: nw + 768 + 128 * (p + 1)]
        rot = r1 * gc + r2 * gs
        sq = r1 * r1
        for h in (2 * p, 2 * p + 1):
            msk = _half_mask(h)
            nope = y[:, 128 * h: 128 * (h + 1)]
            ss = jnp.sum(nope * nope, axis=-1, keepdims=True) + jnp.sum(
                jnp.where(msk, sq, 0.0), axis=-1, keepdims=True)
            inv = lax.rsqrt(ss / MLA_QK + EPS) * (MLA_QK ** -0.5)
            q_ref[h, :, 0:128] = (nope * gn_ref[...] * inv).astype(BF16)
            q_ref[h, :, 128:256] = (jnp.where(msk, rot, 0.0) * inv).astype(BF16)


def _mla_kv_kernel(ckv_ref, kr_ref, w_ref, kva_ref, gn_ref, g2_ref, gs2_ref, c2_ref, s2_ref,
                   k_ref, v_ref):
    x = _rms_rows(ckv_ref[...].astype(F32), kva_ref[...]).astype(BF16)
    y = _dot(x, w_ref[...])
    kr = kr_ref[...].astype(F32)
    r1 = kr[:, 0:128]
    r2 = kr[:, 128:256]
    rot = r1 * (g2_ref[...] * c2_ref[...]) + r2 * (gs2_ref[...] * s2_ref[...])
    ss_r = jnp.sum(jnp.where(_half_mask(0), r1 * r1, 0.0), axis=-1, keepdims=True)
    nw = MLA_H * MLA_NOPE
    lane0 = lax.broadcasted_iota(jnp.int32, (kr.shape[0], 128), 1) == 0
    for h in range(MLA_H):
        nope = y[:, 128 * h: 128 * (h + 1)]
        ss = jnp.sum(nope * nope, axis=-1, keepdims=True) + ss_r
        inv = lax.rsqrt(ss / MLA_QK + EPS)
        k_ref[h, :, 0:128] = (nope * gn_ref[...] * inv).astype(BF16)
        k_ref[h, :, 128:256] = (jnp.where(_half_mask(h), rot, 0.0) * inv).astype(BF16)
        v_ref[h, :, 0:128] = y[:, nw + 128 * h: nw + 128 * (h + 1)].astype(BF16)
        v_ref[h, :, 128:256] = jnp.where(lane0, 1.0, 0.0).astype(BF16)


def _rope_tables():
    t = jnp.arange(S, dtype=jnp.int32)
    pos = (t // GW, t % GW)
    nf = MLA_ROPE // 4
    inv = ROPE_BASE ** (-jnp.arange(nf, dtype=F32) / nf)
    cs, sn = [], []
    for ax in range(2):
        ang = pos[ax].astype(F32)[:, None] * inv[None, :]
        c, s = jnp.cos(ang), jnp.sin(ang)
        cs += [c, c]
        sn += [-s, s]
    c64 = jnp.tile(jnp.concatenate(cs, axis=1), (NB, 1))
    s64 = jnp.tile(jnp.concatenate(sn, axis=1), (NB, 1))
    c64 = jnp.concatenate([c64, jnp.ones((NB * L, MLA_ROPE), F32)], axis=0)
    s64 = jnp.concatenate([s64, jnp.zeros((NB * L, MLA_ROPE), F32)], axis=0)
    return jnp.tile(c64, (1, 2)), jnp.tile(s64, (1, 2))


_ROPE_SWAP = np.concatenate([np.arange(16, 32), np.arange(0, 16), np.arange(48, 64), np.arange(32, 48)])


def _rope_gains(g):
    gr = g[MLA_NOPE:]
    return (g[:MLA_NOPE].reshape(1, 128), jnp.tile(gr, 2).reshape(1, 128),
            jnp.tile(gr[_ROPE_SWAP], 2).reshape(1, 128))


def mla_q_prep(proj, w_uq_r, qa_g, qn_g, c2, s2):
    tm = 512
    gn, g2, gs2 = _rope_gains(qn_g)
    vec = lambda w: pl.BlockSpec((1, w), lambda i: (0, 0))
    return pl.pallas_call(
        _mla_q_kernel,
        grid=(NT // tm,),
        in_specs=[
            pl.BlockSpec((tm, MLA_QL), lambda i: (i, _COL_CQ // MLA_QL)),
            pl.BlockSpec((MLA_QL, 3072), lambda i: (0, 0)),
            vec(MLA_QL), vec(128), vec(128), vec(128),
            pl.BlockSpec((tm, 128), lambda i: (i, 0)),
            pl.BlockSpec((tm, 128), lambda i: (i, 0)),
        ],
        out_specs=pl.BlockSpec((MLA_H, tm, MLA_HW), lambda i: (0, i, 0)),
        out_shape=jax.ShapeDtypeStruct((MLA_H, NT, MLA_HW), BF16),
        compiler_params=_cp(("arbitrary",)),
        name="mla_q_prep",
    )(proj, w_uq_r, qa_g.reshape(1, MLA_QL), gn, g2, gs2, c2, s2)


def mla_kv_prep(proj, w_ukv_r, kva_g, kn_g, c2, s2):
    tm = 512
    gn, g2, gs2 = _rope_gains(kn_g)
    vec = lambda w: pl.BlockSpec((1, w), lambda i: (0, 0))
    return pl.pallas_call(
        _mla_kv_kernel,
        grid=(NT // tm,),
        in_specs=[
            pl.BlockSpec((tm, MLA_KVL), lambda i: (i, _COL_CKV // MLA_KVL)),
            pl.BlockSpec((tm, 256), lambda i: (i, _COL_KR // 256)),
            pl.BlockSpec((MLA_KVL, 3072), lambda i: (0, 0)),
            vec(MLA_KVL), vec(128), vec(128), vec(128),
            pl.BlockSpec((tm, 128), lambda i: (i, 0)),
            pl.BlockSpec((tm, 128), lambda i: (i, 0)),
        ],
        out_specs=[pl.BlockSpec((MLA_H, tm, MLA_HW), lambda i: (0, i, 0)),
                   pl.BlockSpec((MLA_H, tm, MLA_HW), lambda i: (0, i, 0))],
        out_shape=[jax.ShapeDtypeStruct((MLA_H, NT, MLA_HW), BF16),
                   jax.ShapeDtypeStruct((MLA_H, NT, MLA_HW), BF16)],
        compiler_params=_cp(("arbitrary",)),
        name="mla_kv_prep",
    )(proj, proj, w_ukv_r, kva_g.reshape(1, MLA_KVL), gn, g2, gs2, c2, s2)


MLA_TQ = 512


def _mla_attn_kernel(q_ref, kl_ref, kc_ref, vl_ref, vc_ref, o_ref, s0_ref, s1_ref, p0_ref, p1_ref, acc_ref):
    g = pl.program_id(0)

    @pl.when(g == 0)
    def _():
        s0_ref[...] = jnp.zeros_like(s0_ref)
        s1_ref[...] = jnp.zeros_like(s1_ref)
        p0_ref[...] = jnp.ones_like(p0_ref)
        p1_ref[...] = jnp.ones_like(p1_ref)

    n_kc = (L + S) // L
    rows = MLA_TQ // (n_kc - 1)

    def tie(x, dep):
        if dep is None:
            return x
        return jnp.concatenate([x[0:16, :] + dep, x[16:, :]], axis=0)

    def stages(s_new, s_old, p_new, p_old):
        acc, dep = None, None
        for c in range(n_kc):
            keys = slice(L * c, L * (c + 1))
            k = kc_ref[...] if c == 0 else kl_ref[L * (c - 1): L * c, :]
            v = vc_ref[...] if c == 0 else vl_ref[L * (c - 1): L * c, :]
            if c < n_kc - 1:
                r = slice(rows * c, rows * (c + 1))
                s = s_old[r, :]
                p = jnp.exp((s - jnp.max(s, axis=-1, keepdims=True)).astype(BF16))
                p_old[r, :] = p
                dep = (p[0:16, 0:MLA_HW].astype(F32) * 0.0).astype(BF16)
            part = _dot(tie(p_new[:, keys], dep), v)
            acc = part if acc is None else acc + part
            s_new[:, keys] = _dot_nt(tie(q_ref[...], dep), k)
        o_ref[...] = (acc[:, :MLA_V] / acc[:, MLA_V:MLA_V + 1]).astype(o_ref.dtype)

    @pl.when(g % 2 == 0)
    def _():
        stages(s0_ref, s1_ref, p0_ref, p1_ref)

    @pl.when(g % 2 == 1)
    def _():
        stages(s1_ref, s0_ref, p1_ref, p0_ref)


def _mla_ctx_kernel(q_ref, kc_ref, vc_ref, o_ref):
    s = _dot_nt(q_ref[...], kc_ref[...])
    p = jnp.exp((s - jnp.max(s, axis=-1, keepdims=True)).astype(BF16))
    acc = _dot(p, vc_ref[...])
    o_ref[...] = (acc[:, :MLA_V] / acc[:, MLA_V:MLA_V + 1]).astype(o_ref.dtype)


def mla_attention(q, k, v):
    nq = S // MLA_TQ
    n_steps = NB * MLA_H * nq

    def bhi(g):
        return g // (MLA_H * nq), (g // nq) % MLA_H, g % nq

    def cur(g):
        return bhi(jnp.minimum(g, n_steps - 1))

    def prev(g):
        return bhi(jnp.maximum(g - 2, 0))

    ob = pl.pallas_call(
        _mla_attn_kernel,
        grid=(n_steps + 2,),
        in_specs=[
            pl.BlockSpec((None, MLA_TQ, MLA_HW), lambda g: (cur(g)[1], cur(g)[0] * nq + cur(g)[2], 0)),
            pl.BlockSpec((None, S, MLA_HW), lambda g: (cur(g)[1], cur(g)[0], 0)),
            pl.BlockSpec((None, L, MLA_HW), lambda g: (cur(g)[1], CTX_BLK0 + cur(g)[0], 0)),
            pl.BlockSpec((None, S, MLA_HW), lambda g: (prev(g)[1], prev(g)[0], 0)),
            pl.BlockSpec((None, L, MLA_HW), lambda g: (prev(g)[1], CTX_BLK0 + prev(g)[0], 0)),
        ],
        out_specs=pl.BlockSpec((MLA_TQ, MLA_V), lambda g: (prev(g)[0] * nq + prev(g)[2], prev(g)[1])),
        out_shape=jax.ShapeDtypeStruct((NL, MLA_W), BF16),
        scratch_shapes=[pltpu.VMEM((MLA_TQ, L + S), F32), pltpu.VMEM((MLA_TQ, L + S), F32),
                        pltpu.VMEM((MLA_TQ, L + S), BF16), pltpu.VMEM((MLA_TQ, L + S), BF16),
                        pltpu.VMEM((MLA_TQ, MLA_HW), F32)],
        compiler_params=_cp(("arbitrary",)),
        name="mla_attention",
    )(q, k, k, v, v)
    ob_ctx = pl.pallas_call(
        _mla_ctx_kernel,
        grid=(NB, MLA_H),
        in_specs=[
            pl.BlockSpec((None, L, MLA_HW), lambda b, h: (h, CTX_BLK0 + b, 0)),
            pl.BlockSpec((None, L, MLA_HW), lambda b, h: (h, CTX_BLK0 + b, 0)),
            pl.BlockSpec((None, L, MLA_HW), lambda b, h: (h, CTX_BLK0 + b, 0)),
        ],
        out_specs=pl.BlockSpec((L, MLA_V), lambda b, h: (b, h)),
        out_shape=jax.ShapeDtypeStruct((NB * L, MLA_W), BF16),
        compiler_params=_cp(("arbitrary", "arbitrary")),
        name="mla_attention_ctx",
    )(q, k, v)
    return ob, ob_ctx


N_CH_C = L // CH
N_CH_L = S // CH
N_CH = N_CH_C + N_CH_L


def _gla_kernel(qf_ref, kf_ref, vf_ref, af_ref, qb_ref, kb_ref, vb_ref, ab_ref,
                wg_ref, bg_ref, of_ref, ob_ref, st_ref):
    t = pl.program_id(1)

    @pl.when(t == 0)
    def _():
        st_ref[...] = jnp.zeros_like(st_ref)

    row = lax.broadcasted_iota(jnp.int32, (CH, CH), 0)
    col = lax.broadcasted_iota(jnp.int32, (CH, CH), 1)
    dirs = (
        (qf_ref, kf_ref, vf_ref, af_ref, of_ref, col <= row, CH - 1),
        (qb_ref, kb_ref, vb_ref, ab_ref, ob_ref, col >= row, 0),
    )
    for d, (q_ref, k_ref, v_ref, a_ref, o_ref, keep, last) in enumerate(dirs):
        z = _dot(a_ref[...], wg_ref[d]) + bg_ref[d]
        g = (jnp.minimum(z, 0.0) - jnp.log(1.0 + jnp.exp(-jnp.abs(z)))) / GLA_TAU
        b = jnp.dot(keep.astype(F32), g, preferred_element_type=F32,
                    precision=lax.Precision.HIGHEST)
        b_end = b[last:last + 1, :]
        b_mid = b[CH // 2:CH // 2 + 1, :]
        q = q_ref[...].astype(F32) * (GLA_DK ** -0.5)
        k = k_ref[...].astype(F32)
        qa = (q * jnp.exp(b - b_mid)).astype(BF16)
        ka = (k * jnp.exp(b_mid - b)).astype(BF16)
        qd = (q * jnp.exp(b)).astype(BF16)
        ke = (k * jnp.exp(b_end - b)).astype(BF16)
        e_end = jnp.exp(b_end)
        v = v_ref[...]
        for h in range(GLA_H):
            ksl = slice(GLA_DK * h, GLA_DK * (h + 1))
            vsl = slice(GLA_DV * h, GLA_DV * (h + 1))
            att = jnp.where(keep, _dot_nt(qa[:, ksl], ka[:, ksl]), 0.0)
            st = st_ref[d, h]
            o = _dot_nt(qd[:, ksl], st.astype(BF16)) + _dot(att.astype(BF16), v[:, vsl])
            o_ref[:, vsl] = o
            st_ref[d, h] = st * e_end[:, ksl] + _dot_tn(v[:, vsl], ke[:, ksl])


def gla_scan(proj, wg_f, bg_f, wg_b, bg_b):
    wg = jnp.zeros((2, 128, GLA_KW), F32)
    wg = wg.at[0, 0:GLA_RANK].set(wg_f).at[1, GLA_RANK:2 * GLA_RANK].set(wg_b).astype(BF16)
    bg = jnp.stack([bg_f, bg_b]).reshape(2, 1, GLA_KW)

    ctx0 = NL // CH

    def fwd(b, t):
        return jnp.where(t < N_CH_C, ctx0 + N_CH_C * b + t, N_CH_L * b + t - N_CH_C)

    def bwd(b, t):
        return jnp.where(t < N_CH_C, ctx0 + N_CH_C * b + N_CH_C - 1 - t, N_CH_L * b + N_CH - 1 - t)

    def specs(rowfn):
        return [
            pl.BlockSpec((CH, GLA_KW), lambda b, t: (rowfn(b, t), _COL_GQ // GLA_KW)),
            pl.BlockSpec((CH, GLA_KW), lambda b, t: (rowfn(b, t), _COL_GK // GLA_KW)),
            pl.BlockSpec((CH, GLA_W), lambda b, t: (rowfn(b, t), _COL_GV // GLA_W)),
            pl.BlockSpec((CH, 128), lambda b, t: (rowfn(b, t), _COL_GA // 128)),
        ]

    return pl.pallas_call(
        _gla_kernel,
        grid=(NB, N_CH),
        in_specs=specs(fwd) + specs(bwd) + [
            pl.BlockSpec((2, 128, GLA_KW), lambda b, t: (0, 0, 0)),
            pl.BlockSpec((2, 1, GLA_KW), lambda b, t: (0, 0, 0)),
        ],
        out_specs=[pl.BlockSpec((CH, GLA_W), lambda b, t: (fwd(b, t), 0)),
                   pl.BlockSpec((CH, GLA_W), lambda b, t: (bwd(b, t), 0))],
        out_shape=[jax.ShapeDtypeStruct((NT, GLA_W), F32)] * 2,
        scratch_shapes=[pltpu.VMEM((2, GLA_H, GLA_DV, GLA_DK), F32)],
        compiler_params=_cp(("arbitrary", "arbitrary")),
        name="gla_scan",
    )(proj, proj, proj, proj, proj, proj, proj, proj, wg, bg)


def _gla_out_kernel(of_ref, ob_ref, g_ref, gn_ref, o_ref):
    for h in range(GLA_H):
        sl = slice(GLA_DV * h, GLA_DV * (h + 1))
        o = _rms_rows(of_ref[:, sl] + ob_ref[:, sl], gn_ref[...])
        gate = g_ref[:, sl].astype(F32)
        o_ref[:, sl] = (o * (gate * _sigmoid(gate))).astype(o_ref.dtype)


def gla_output(o_f, o_b, proj, on_g):
    tm = 512
    return pl.pallas_call(
        _gla_out_kernel,
        grid=(NT // tm,),
        in_specs=[
            pl.BlockSpec((tm, GLA_W), lambda i: (i, 0)),
            pl.BlockSpec((tm, GLA_W), lambda i: (i, 0)),
            pl.BlockSpec((tm, GLA_W), lambda i: (i, _COL_GG // GLA_W)),
            pl.BlockSpec((1, GLA_DV), lambda i: (0, 0)),
        ],
        out_specs=pl.BlockSpec((tm, GLA_W), lambda i: (i, 0)),
        out_shape=jax.ShapeDtypeStruct((NT, GLA_W), BF16),
        compiler_params=_cp(("arbitrary",)),
        name="gla_output",
    )(o_f, o_b, proj, on_g.reshape(1, GLA_DV))


OUT_TM = NB * L


def _out_proj_kernel(oa_ref, obl_ref, obc_ref, oc_ref, w_ref, x_ref, g_ref, o_ref):
    ob = jnp.where(pl.program_id(1) < NL // OUT_TM, obl_ref[...], obc_ref[...])
    a = jnp.concatenate([oa_ref[...], ob, oc_ref[...]], axis=1)
    o_ref[...] = x_ref[...] + g_ref[...] * _dot(a, w_ref[...])


def out_projection(oa, ob, ob_ctx, oc, w_out, layer, xs, mod, k_gate):
    tm, tn = OUT_TM, 1024
    nj = D // tn
    return pl.pallas_call(
        _out_proj_kernel,
        grid=(nj, NT // tm),
        in_specs=[
            pl.BlockSpec((tm, NA_W), lambda j, i: (i, 0)),
            pl.BlockSpec((tm, MLA_W), lambda j, i: (jnp.minimum(i, NL // tm - 1), 0)),
            pl.BlockSpec((tm, MLA_W), lambda j, i: (0, 0)),
            pl.BlockSpec((tm, GLA_W), lambda j, i: (i, 0)),
            pl.BlockSpec((None, D, tn), lambda j, i: (layer, 0, j)),
            pl.BlockSpec((tm, tn), lambda j, i: (i, j)),
            pl.BlockSpec((None, 1, tn), lambda j, i: (_row_group(i, tm), 0, k_gate * nj + j)),
        ],
        out_specs=pl.BlockSpec((tm, tn), lambda j, i: (i, j)),
        out_shape=jax.ShapeDtypeStruct((NT, D), F32),
        compiler_params=_cp(("arbitrary", "arbitrary")),
        name="out_projection",
    )(oa, ob, ob_ctx, oc, w_out, xs, mod)


def _route(logits, router_bias):
    scores = jax.nn.sigmoid(logits)
    grouped = (scores + router_bias.astype(F32)).reshape(NT, E_GROUPS, E_PER)

    def top2(a):
        idx = lax.broadcasted_iota(jnp.int32, a.shape, a.ndim - 1)
        i1 = jnp.argmax(a, axis=-1).astype(jnp.int32)
        rest = jnp.where(idx == i1[..., None], -jnp.inf, a)
        i2 = jnp.argmax(rest, axis=-1).astype(jnp.int32)
        return jnp.max(a, axis=-1), jnp.max(rest, axis=-1), i1, i2

    m1, m2, _, _ = top2(grouped)
    grp = jnp.argmax(m1 + m2, axis=-1).astype(jnp.int32)
    gsel = lax.broadcasted_iota(jnp.int32, (NT, E_GROUPS, E_PER), 1) == grp[:, None, None]
    in_group = jnp.sum(jnp.where(gsel, grouped, 0.0), axis=1)
    _, _, l1, l2 = top2(in_group)
    expert_idx = grp[:, None] * E_PER + jnp.stack([l1, l2], axis=-1)
    esel = lax.broadcasted_iota(jnp.int32, (NT, TOPK, E), 2) == expert_idx[:, :, None]
    w = jnp.sum(jnp.where(esel, scores[:, None, :], 0.0), axis=-1)
    return expert_idx, w / jnp.sum(w, axis=-1, keepdims=True)


def _dispatch_plan(expert_idx):
    flat_e = expert_idx.reshape(NK)
    onehot = (flat_e[:, None] == jnp.arange(E, dtype=jnp.int32)[None, :]).astype(jnp.int32)
    csum = jnp.cumsum(onehot, axis=0)
    counts = csum[-1]
    rank = jnp.take_along_axis(csum, flat_e[:, None], axis=1)[:, 0] - 1
    padded = (counts + MOE_R - 1) // MOE_R * MOE_R
    pad_end = jnp.cumsum(padded)
    pad_start = pad_end - padded
    dest = (pad_start[flat_e] + rank).astype(jnp.int32)
    row_token = jnp.zeros((MOE_P,), jnp.int32).at[dest].set(jnp.arange(NK, dtype=jnp.int32) // TOPK)
    blk0 = jnp.arange(MOE_NB, dtype=jnp.int32) * MOE_R
    block_expert = jnp.minimum(jnp.sum((pad_end[None, :] <= blk0[:, None]).astype(jnp.int32), axis=1), E - 1)
    n_active = (pad_end[-1] // MOE_R).astype(jnp.int32).reshape(1)
    n_valid = jnp.clip(counts[block_expert] - (blk0 - pad_start[block_expert]), 0, MOE_R).astype(jnp.int32)
    return dest, row_token, block_expert, n_active, n_valid


DH = D // 2


DMA_UNROLL = 8


def _for_rows(n, body):
    full = n // DMA_UNROLL

    def group(t, c):
        for u in range(DMA_UNROLL):
            body(t * DMA_UNROLL + u)
        return c

    def single(r, c):
        body(r)
        return c

    lax.fori_loop(0, full, group, 0)
    lax.fori_loop(full * DMA_UNROLL, n, single, 0)


def _gather_rows_kernel(tok_ref, nv_ref, h_ref, o_ref, sem):
    i = pl.program_id(0)
    base = i * MOE_R
    nv = nv_ref[i]

    @pl.when(nv < MOE_R)
    def _():
        o_ref[...] = jnp.zeros_like(o_ref)

    def copy(r):
        return pltpu.make_async_copy(h_ref.at[pl.ds(tok_ref[base + r], 1)], o_ref.at[pl.ds(r, 1)], sem)

    _for_rows(nv, lambda r: copy(r).start())
    _for_rows(nv, lambda r: copy(r).wait())


def gather_rows(row_token, n_valid, h):
    return pl.pallas_call(
        _gather_rows_kernel,
        grid_spec=pltpu.PrefetchScalarGridSpec(
            num_scalar_prefetch=2,
            grid=(MOE_NB,),
            in_specs=[pl.BlockSpec(memory_space=pl.ANY)],
            out_specs=pl.BlockSpec((MOE_R, DH), lambda i, tok, nv: (i, 0)),
            scratch_shapes=[pltpu.SemaphoreType.DMA(())],
        ),
        out_shape=jax.ShapeDtypeStruct((MOE_P, DH), jnp.uint32),
        compiler_params=_cp(("arbitrary",)),
        name="moe_gather_rows",
    )(row_token, n_valid, h)


def _new_expert(be_ref, i):
    return jnp.logical_or(i == 0, be_ref[i] != be_ref[jnp.maximum(i - 1, 0)])


def _ffn_up_kernel(be_ref, na_ref, x_ref, w1_ref, w3_ref, o_ref, w1b_ref, w3b_ref):
    i = pl.program_id(1)
    active = i < na_ref[0]

    @pl.when(jnp.logical_and(active, _new_expert(be_ref, i)))
    def _():
        w1b_ref[...] = w1_ref[...].astype(BF16)
        w3b_ref[...] = w3_ref[...].astype(BF16)

    @pl.when(active)
    def _():
        lo, hi = _unpack_pairs(x_ref[...])
        lo, hi = lo.astype(BF16), hi.astype(BF16)
        a = _dot(lo, w1b_ref[0:DH, :]) + _dot(hi, w1b_ref[DH:, :])
        b = _dot(lo, w3b_ref[0:DH, :]) + _dot(hi, w3b_ref[DH:, :])
        o_ref[...] = (a * _sigmoid(a) * b).astype(o_ref.dtype)

    @pl.when(jnp.logical_not(active))
    def _():
        o_ref[...] = jnp.zeros_like(o_ref)


def ffn_up(block_expert, n_active, xb, w1, w3, layer):
    last = lambda i, na: jnp.minimum(i, na[0] - 1)
    return pl.pallas_call(
        _ffn_up_kernel,
        grid_spec=pltpu.PrefetchScalarGridSpec(
            num_scalar_prefetch=2,
            grid=(FF // MOE_TF, MOE_NB),
            in_specs=[
                pl.BlockSpec((MOE_R, DH), lambda j, i, be, na: (last(i, na), 0)),
                pl.BlockSpec((None, None, D, MOE_TF), lambda j, i, be, na: (layer, be[i], 0, j)),
                pl.BlockSpec((None, None, D, MOE_TF), lambda j, i, be, na: (layer, be[i], 0, j)),
            ],
            out_specs=pl.BlockSpec((MOE_R, MOE_TF), lambda j, i, be, na: (i, j)),
            scratch_shapes=[pltpu.VMEM((D, MOE_TF), BF16), pltpu.VMEM((D, MOE_TF), BF16)],
        ),
        out_shape=jax.ShapeDtypeStruct((MOE_P, FF), BF16),
        compiler_params=_cp(("arbitrary", "arbitrary")),
        name="moe_ffn_up",
    )(block_expert, n_active, xb, w1, w3)


MOE_TN = 2048


def _ffn_down_kernel(be_ref, na_ref, h_ref, w2_ref, o_ref, w2b_ref):
    i = pl.program_id(1)
    active = i < na_ref[0]

    @pl.when(jnp.logical_and(active, _new_expert(be_ref, i)))
    def _():
        w2b_ref[...] = w2_ref[...].astype(BF16)

    @pl.when(active)
    def _():
        y = _dot(h_ref[...], w2b_ref[...])
        o_ref[...] = _pack_pairs(y[:, :MOE_TN // 2], y[:, MOE_TN // 2:])

    @pl.when(jnp.logical_not(active))
    def _():
        o_ref[...] = jnp.zeros_like(o_ref)


def ffn_down(block_expert, n_active, hmid, w2, layer):
    last = lambda i, na: jnp.minimum(i, na[0] - 1)
    return pl.pallas_call(
        _ffn_down_kernel,
        grid_spec=pltpu.PrefetchScalarGridSpec(
            num_scalar_prefetch=2,
            grid=(D // MOE_TN, MOE_NB),
            in_specs=[
                pl.BlockSpec((MOE_R, FF), lambda n, i, be, na: (last(i, na), 0)),
                pl.BlockSpec((None, None, FF, MOE_TN), lambda n, i, be, na: (layer, be[i], 0, n)),
            ],
            out_specs=pl.BlockSpec((MOE_R, MOE_TN // 2), lambda n, i, be, na: (i, n)),
            scratch_shapes=[pltpu.VMEM((FF, MOE_TN), BF16)],
        ),
        out_shape=jax.ShapeDtypeStruct((MOE_P, DH), jnp.uint32),
        compiler_params=_cp(("arbitrary", "arbitrary")),
        name="moe_ffn_down",
    )(block_expert, n_active, hmid, w2)


CMB_T = 256


def _combine_kernel(pos_ref, yb_ref, gate_ref, x_ref, g_ref, o_ref, buf_ref, sem):
    base = pl.program_id(0) * CMB_T

    def copy(r, k):
        p = pos_ref[(base + r) * TOPK + k]
        return pltpu.make_async_copy(yb_ref.at[pl.ds(p, 1)], buf_ref.at[k, pl.ds(r, 1)], sem.at[k])

    def start(r, c):
        copy(r, 0).start()
        copy(r, 1).start()
        return c

    def wait(r, c):
        copy(r, 0).wait()
        copy(r, 1).wait()
        return c

    lax.fori_loop(0, CMB_T, start, 0, unroll=DMA_UNROLL)
    lax.fori_loop(0, CMB_T, wait, 0, unroll=DMA_UNROLL)
    gate = gate_ref[...]
    g0, g1 = gate[:, 0:1], gate[:, 1:2]
    half = MOE_TN // 2
    for n in range(D // MOE_TN):
        lo0, hi0 = _unpack_pairs(buf_ref[0, :, half * n: half * (n + 1)])
        lo1, hi1 = _unpack_pairs(buf_ref[1, :, half * n: half * (n + 1)])
        for part, y in enumerate((lo0 * g0 + lo1 * g1, hi0 * g0 + hi1 * g1)):
            sl = slice(MOE_TN * n + half * part, MOE_TN * n + half * (part + 1))
            o_ref[:, sl] = x_ref[:, sl] + g_ref[:, sl] * y


def moe_combine(dest, yb, gates, xs, mod, k_gate, out_rows):
    return pl.pallas_call(
        _combine_kernel,
        grid_spec=pltpu.PrefetchScalarGridSpec(
            num_scalar_prefetch=1,
            grid=(out_rows // CMB_T,),
            in_specs=[
                pl.BlockSpec(memory_space=pl.ANY),
                pl.BlockSpec((CMB_T, TOPK), lambda i, pos: (i, 0)),
                pl.BlockSpec((CMB_T, D), lambda i, pos: (i, 0)),
                pl.BlockSpec((None, 1, D), lambda i, pos: (_row_group(i, CMB_T), 0, k_gate)),
            ],
            out_specs=pl.BlockSpec((CMB_T, D), lambda i, pos: (i, 0)),
            scratch_shapes=[pltpu.VMEM((TOPK, CMB_T, DH), jnp.uint32), pltpu.SemaphoreType.DMA((TOPK,))],
        ),
        out_shape=jax.ShapeDtypeStruct((out_rows, D), F32),
        compiler_params=_cp(("arbitrary",)),
        name="moe_combine",
    )(dest, yb, gates, xs, mod)


IN_W = 9312


W_RB = 64
_RB_KR_SRC = 4608 // W_RB
_RB_GA_SRC = 9280 // W_RB
_RB_KR = _COL_KR // W_RB
_RB_GA = _COL_GA // W_RB


def _w_in_layout_kernel(w_ref, o_ref):
    j = pl.program_id(1)

    @pl.when(j < _RB_KR + 2)
    def _():
        o_ref[...] = w_ref[...].astype(BF16)

    @pl.when(jnp.logical_and(j >= _RB_KR + 2, j < _RB_GA))
    def _():
        x = w_ref[...]
        o_ref[...] = jnp.concatenate([x[16:32], x[0:16], x[48:64], x[32:48]], axis=0).astype(BF16)

    @pl.when(j == _RB_GA)
    def _():
        row = lax.broadcasted_iota(jnp.int32, (W_RB, D), 0)
        o_ref[...] = jnp.where(row < 2 * GLA_RANK, w_ref[...], 0.0).astype(BF16)

    @pl.when(j > _RB_GA)
    def _():
        o_ref[...] = jnp.zeros_like(o_ref)


def _w_in_layout(w):
    w_t = jnp.swapaxes(w, 1, 2)

    def src(j):
        seg2 = j + MLA_ROPE // W_RB
        return jnp.where(j < _RB_KR_SRC, j, jnp.where(j < _RB_KR, seg2, jnp.where(j < _RB_GA, _RB_KR_SRC, _RB_GA_SRC)))

    return pl.pallas_call(
        _w_in_layout_kernel,
        grid=(DEPTH, PW // W_RB),
        in_specs=[pl.BlockSpec((None, W_RB, D), lambda l, j: (l, src(j), 0))],
        out_specs=pl.BlockSpec((None, W_RB, D), lambda l, j: (l, j, 0)),
        out_shape=jax.ShapeDtypeStruct((DEPTH, PW, D), BF16),
        compiler_params=_cp(("arbitrary", "arbitrary")),
        name="w_in_layout",
    )(w_t)


def _w_uq_cols():
    nope = [MLA_QK * h + j for h in range(MLA_H) for j in range(MLA_NOPE)]
    rope = [MLA_QK * h + MLA_NOPE + d for h in range(MLA_H) for d in range(MLA_ROPE)]
    rope_s = [MLA_QK * h + MLA_NOPE + int(d) for h in range(MLA_H) for d in _ROPE_SWAP]
    return np.array(nope + rope + rope_s, np.int32)


def _w_ukv_cols():
    kn = [(MLA_NOPE + MLA_V) * h + j for h in range(MLA_H) for j in range(MLA_NOPE)]
    vv = [(MLA_NOPE + MLA_V) * h + MLA_NOPE + j for h in range(MLA_H) for j in range(MLA_V)]
    return np.array(kn + vv, np.int32)


def token_mixing_layer(xs, mod, layer, norm1, w_in_b, w_out_b, na_qn, na_kn, na_rpb, mla_qa, mla_kva, w_uq, w_ukv,
                       mla_qn, mla_kn, gwf, gbf, gwb, gbb, gla_on, c2, s2):
    h = norm_modulate(xs, norm1.reshape(1, D), mod, 0, 1, BF16)
    proj = matmul(h, w_in_b, layer, NT // 8, 512, BF16)
    oa = na_attention(proj, na_rpb, na_qn, na_kn)
    q = mla_q_prep(proj, w_uq[:, _w_uq_cols()].astype(BF16), mla_qa, mla_qn, c2, s2)
    k, v = mla_kv_prep(proj, w_ukv[:, _w_ukv_cols()].astype(BF16), mla_kva, mla_kn, c2, s2)
    ob, ob_ctx = mla_attention(q, k, v)
    o_f, o_b = gla_scan(proj, gwf, gbf, gwb, gbb)
    oc = gla_output(o_f, o_b, proj, gla_on)
    return out_projection(oa, ob, ob_ctx, oc, w_out_b, layer, xs, mod, 2)


def moe_layer(xs, mod, norm2, w_router_p, router_bias, w1, w3, w2, layer, out_rows):
    h, logits = norm_modulate(xs, norm2.reshape(1, D), mod, 3, 4, None, w_router_p)
    expert_idx, gates = _route(logits[:, :E], router_bias)
    dest, row_token, block_expert, n_active, n_valid = _dispatch_plan(expert_idx)
    xb = gather_rows(row_token, n_valid, h)
    hmid = ffn_up(block_expert, n_active, xb, w1, w3, layer)
    yb = ffn_down(block_expert, n_active, hmid, w2, layer)
    return moe_combine(dest, yb, gates, xs, mod, 5, out_rows)


def kernel(x, c, ctx, c_ctx, w_ada, b_ada, norm1, norm2, w_in, w_out, na_q_norm, na_k_norm, na_rpb, mla_qa_norm, mla_kva_norm, mla_w_uq, mla_w_ukv, mla_q_norm, mla_k_norm, gla_w_gate_f, gla_b_gate_f, gla_w_gate_b, gla_b_gate_b, gla_out_norm, w_router, router_bias, moe_w1, moe_w3, moe_w2):
    cond = jnp.concatenate([c, c_ctx[None, :], jnp.zeros((8 - NB - 1, D), F32)], axis=0)
    mods = ada_modulation(cond, w_ada, b_ada)
    xs = jnp.concatenate([x.reshape(NL, D), ctx.reshape(NB * L, D)], axis=0)
    c2, s2 = _rope_tables()
    w_router_p = jnp.concatenate([w_router, jnp.zeros((D, 128 - E), F32)], axis=1)
    w_in_b = _w_in_layout(w_in)
    w_out_b = w_out.astype(BF16)
    for l in range(DEPTH):
        mod = mods[l].reshape(8, 1, 6 * D)
        xs = token_mixing_layer(xs, mod, l, norm1[l], w_in_b, w_out_b, na_q_norm[l], na_k_norm[l], na_rpb[l],
                                mla_qa_norm[l], mla_kva_norm[l], mla_w_uq[l], mla_w_ukv[l],
                                mla_q_norm[l], mla_k_norm[l], gla_w_gate_f[l], gla_b_gate_f[l],
                                gla_w_gate_b[l], gla_b_gate_b[l], gla_out_norm[l], c2, s2)
        xs = moe_layer(xs, mod, norm2[l], w_router_p, router_bias, moe_w1, moe_w3, moe_w2, l,
                       NT if l < DEPTH - 1 else NL)
    return xs.reshape(NB, S, D)
```

```python
import functools

import numpy as np
import jax
import jax.numpy as jnp
from jax import lax
from jax.experimental import pallas as pl
from jax.experimental.pallas import tpu as pltpu

F32 = jnp.float32
BF16 = jnp.bfloat16

D = 4096
NB = 2
S = 4096
L = 256
DEPTH = 2
GW = 64
EPS = 1e-6
NL = NB * S
NT = NL + NB * L
NA_H, NA_D = 8, 128
NA_W = NA_H * NA_D
NA_R, NA_C = 8, 16
MLA_H = 12
MLA_QL, MLA_KVL = 1024, 512
MLA_NOPE, MLA_ROPE, MLA_V = 128, 64, 128
MLA_QK = MLA_NOPE + MLA_ROPE
MLA_W = MLA_H * MLA_V
GLA_H, GLA_DK, GLA_DV = 6, 128, 256
GLA_KW = GLA_H * GLA_DK
GLA_W = GLA_H * GLA_DV
GLA_RANK = 16
GLA_TAU = 16.0
CH = 128
E = 16
E_GROUPS = 4
E_PER = E // E_GROUPS
TOPK = 2
FF = 1024
ROPE_BASE = 10000.0

_COL_NAQ, _COL_NAK, _COL_NAV = 0, 1024, 2048
_COL_CQ, _COL_CKV = 3072, 4096
_COL_GQ, _COL_GK, _COL_GV, _COL_GG = 4608, 5376, 6144, 7680
_COL_KR = 9216
_COL_GA = 9472
PW = 9728

BLK = 256
NBLK_L = S // BLK
CTX_BLK0 = NL // BLK
MOE_R = 256
MOE_TF = 512
NK = NT * TOPK
MOE_NB = (NK + E * (MOE_R - 1)) // MOE_R
MOE_P = MOE_NB * MOE_R

NEG = -1e30
VMEM_LIMIT = 56 * 1024 * 1024


def _cp(sem, vmem=VMEM_LIMIT):
    return pltpu.CompilerParams(dimension_semantics=sem, vmem_limit_bytes=vmem)


def _dot(a, b):
    return jnp.dot(a, b, preferred_element_type=F32)


def _dot_nt(a, b):
    return lax.dot_general(a, b, (((1,), (1,)), ((), ())), preferred_element_type=F32)


def _dot_tn(a, b):
    return lax.dot_general(a, b, (((0,), (0,)), ((), ())), preferred_element_type=F32)


def _sigmoid(x):
    return 1.0 / (1.0 + jnp.exp(-x))


def _pack_pairs(lo, hi):
    lo_w = lax.bitcast_convert_type(lo.astype(BF16).astype(F32), jnp.uint32)
    hi_w = lax.bitcast_convert_type(hi.astype(BF16).astype(F32), jnp.uint32)
    return lax.shift_right_logical(lo_w, jnp.uint32(16)) | (hi_w & jnp.uint32(0xFFFF0000))


def _unpack_pairs(w):
    lo = lax.bitcast_convert_type(lax.shift_left(w, jnp.uint32(16)), F32)
    hi = lax.bitcast_convert_type(w & jnp.uint32(0xFFFF0000), F32)
    return lo, hi


def _row_group(i, tm):
    r0 = i * tm
    return jnp.where(r0 >= NL, 2, r0 // S)


def _ada_kernel(s_ref, w_ref, b_ref, o_ref):
    s = s_ref[...]
    s = s * _sigmoid(s)
    o_ref[...] = _dot(s.astype(BF16), w_ref[...].astype(BF16)) + b_ref[...]


def ada_modulation(cond, w_ada, b_ada):
    tn = 512
    n = 6 * D
    return pl.pallas_call(
        _ada_kernel,
        grid=(DEPTH, n // tn),
        in_specs=[
            pl.BlockSpec((8, D), lambda l, j: (0, 0)),
            pl.BlockSpec((None, D, tn), lambda l, j: (l, 0, j)),
            pl.BlockSpec((None, 1, tn), lambda l, j: (l, 0, j)),
        ],
        out_specs=pl.BlockSpec((None, 8, tn), lambda l, j: (l, 0, j)),
        out_shape=jax.ShapeDtypeStruct((DEPTH, 8, n), F32),
        compiler_params=_cp(("arbitrary", "arbitrary")),
        name="ada_modulation",
    )(cond, w_ada, b_ada.reshape(DEPTH, 1, n))


def _normmod(x, g, sh, sc):
    y = x * lax.rsqrt(jnp.mean(x * x, axis=-1, keepdims=True) + EPS) * g
    return y * (1.0 + sc) + sh


def _normmod_kernel(x_ref, g_ref, sh_ref, sc_ref, h_ref):
    h_ref[...] = _normmod(x_ref[...], g_ref[...], sh_ref[...], sc_ref[...]).astype(h_ref.dtype)


def _normmod_router_kernel(x_ref, g_ref, sh_ref, sc_ref, wr_ref, h_ref, lg_ref):
    h = _normmod(x_ref[...], g_ref[...], sh_ref[...], sc_ref[...])
    h_ref[...] = _pack_pairs(h[:, :D // 2], h[:, D // 2:])
    lg_ref[...] = jnp.dot(h, wr_ref[...], preferred_element_type=F32,
                          precision=lax.Precision.HIGHEST)


def norm_modulate(xs, gain, mod, k_shift, k_scale, out_dtype, w_router=None):
    tm = 512
    in_specs = [
        pl.BlockSpec((tm, D), lambda i: (i, 0)),
        pl.BlockSpec((1, D), lambda i: (0, 0)),
        pl.BlockSpec((None, 1, D), lambda i: (_row_group(i, tm), 0, k_shift)),
        pl.BlockSpec((None, 1, D), lambda i: (_row_group(i, tm), 0, k_scale)),
    ]
    if w_router is None:
        return pl.pallas_call(
            _normmod_kernel, grid=(NT // tm,), in_specs=in_specs,
            out_specs=pl.BlockSpec((tm, D), lambda i: (i, 0)),
            out_shape=jax.ShapeDtypeStruct((NT, D), out_dtype),
            compiler_params=_cp(("arbitrary",)), name="norm_modulate",
        )(xs, gain, mod, mod)
    return pl.pallas_call(
        _normmod_router_kernel, grid=(NT // tm,),
        in_specs=in_specs + [pl.BlockSpec((D, 128), lambda i: (0, 0))],
        out_specs=[pl.BlockSpec((tm, D // 2), lambda i: (i, 0)), pl.BlockSpec((tm, 128), lambda i: (i, 0))],
        out_shape=[jax.ShapeDtypeStruct((NT, D // 2), jnp.uint32), jax.ShapeDtypeStruct((NT, 128), F32)],
        compiler_params=_cp(("arbitrary",)), name="norm_modulate_router",
    )(xs, gain, mod, mod, w_router)


def _mm_kernel(a_ref, b_ref, o_ref):
    o_ref[...] = _dot_nt(a_ref[...], b_ref[...]).astype(o_ref.dtype)


def matmul(a, b, layer, tm, tn, out_dtype):
    m, k = a.shape
    n = b.shape[1]
    return pl.pallas_call(
        _mm_kernel,
        grid=(m // tm, n // tn),
        in_specs=[pl.BlockSpec((tm, k), lambda i, j: (i, 0)),
                  pl.BlockSpec((None, tn, k), lambda i, j: (layer, j, 0))],
        out_specs=pl.BlockSpec((tm, tn), lambda i, j: (i, j)),
        out_shape=jax.ShapeDtypeStruct((m, n), out_dtype),
        compiler_params=_cp(("arbitrary", "arbitrary")),
        name="matmul",
    )(a, b)


def _qblk(b, i):
    return jnp.where(i == 0, CTX_BLK0 + b, b * NBLK_L + i - 1)


def _rms_rows(x, g):
    return x * lax.rsqrt(jnp.mean(x * x, axis=-1, keepdims=True) + EPS) * g


NA_KROWS = 12
NA_KW = NA_KROWS * GW


NA_HG = 4
NA_GW = NA_HG * NA_D


def _na_kernel(q_ref, kl_ref, kc_ref, vl_ref, vc_ref, bias_ref, qn_ref, kn_ref, o_ref, ks_ref):
    i = pl.program_id(2)
    heads = [slice(NA_D * h, NA_D * (h + 1)) for h in range(NA_HG)]

    @pl.when(i == 0)
    def _():
        for hs in heads:
            ks_ref[0:L, hs] = _rms_rows(kc_ref[:, hs].astype(F32), kn_ref[...]).astype(BF16)
            ks_ref[L:, hs] = _rms_rows(kl_ref[:, hs].astype(F32), kn_ref[...]).astype(BF16)

    def query(hs):
        return (_rms_rows(q_ref[:, hs].astype(F32), qn_ref[...]) * (NA_D ** -0.5)).astype(BF16)

    @pl.when(i == 0)
    def _():
        for hs in heads:
            s_c = _dot_nt(query(hs), ks_ref[0:L, hs])
            p = jnp.exp(s_c - jnp.max(s_c, axis=-1, keepdims=True))
            l = jnp.sum(p, axis=-1, keepdims=True)
            o_ref[:, hs] = (_dot(p.astype(BF16), vc_ref[:, hs]) / l).astype(o_ref.dtype)

    @pl.when(i > 0)
    def _():
        row0 = jnp.clip(4 * (i - 1) - 4, 0, GW - NA_KROWS)
        k0 = pl.multiple_of(row0 * GW, BLK)
        for h, hs in enumerate(heads):
            q = query(hs)
            s_c = _dot_nt(q, ks_ref[0:L, hs])
            s_l = _dot_nt(q, ks_ref[pl.ds(L + k0, NA_KW), hs]) + bias_ref[h]
            m = jnp.maximum(jnp.max(s_c, axis=-1, keepdims=True), jnp.max(s_l, axis=-1, keepdims=True))
            p_c = jnp.exp(s_c - m)
            p_l = jnp.exp(s_l - m)
            l = jnp.sum(p_c, axis=-1, keepdims=True) + jnp.sum(p_l, axis=-1, keepdims=True)
            o = _dot(p_l.astype(BF16), vl_ref[pl.ds(k0, NA_KW), hs]) + _dot(p_c.astype(BF16), vc_ref[:, hs])
            o_ref[:, hs] = (o / l).astype(o_ref.dtype)


def _na_bias_index():
    dr = np.zeros((3, BLK, NA_KW), np.int32)
    dc = np.zeros((3, BLK, NA_KW), np.int32)
    ok = np.zeros((3, BLK, NA_KW), bool)
    rows = S // GW
    for p, blk in enumerate((0, 5, NBLK_L - 1)):
        ks = int(np.clip(4 * blk - 4, 0, GW - NA_KROWS))
        r = 4 * blk + np.arange(BLK) // GW
        c = np.arange(BLK) % GW
        rk = ks + np.arange(NA_KW) // GW
        ck = np.arange(NA_KW) % GW
        r0 = np.clip(r - NA_R // 2, 0, rows - NA_R)
        ws = np.clip(c - NA_C // 2, 0, GW - NA_C)
        row_ok = (rk[None, :] >= r0[:, None]) & (rk[None, :] < r0[:, None] + NA_R)
        col_ok = (ck[None, :] >= ws[:, None]) & (ck[None, :] < ws[:, None] + NA_C)
        ok[p] = row_ok & col_ok
        dr[p] = np.clip(rk[None, :] - r[:, None] + NA_R - 1, 0, 2 * NA_R - 2)
        dc[p] = np.clip(ck[None, :] - c[:, None], -(NA_C - 1), NA_C - 1) + NA_C - 1
    return dr, dc, ok


def _na_bias_table(rpb):
    dr, dc, ok = _na_bias_index()
    qr, kr = BLK // GW, NA_KROWS
    dr_t = dr.reshape(3, qr, GW, kr, GW)[:, :, 0, :, 0].reshape(3 * qr * kr)
    dc_t = dc[0].reshape(qr, GW, kr, GW)[0, :, 0, :].reshape(GW * GW)
    oh_c = jnp.asarray(np.eye(2 * NA_C - 1, dtype=np.float32)[:, dc_t])
    oh_r = jnp.asarray(np.eye(2 * NA_R - 1, dtype=np.float32)[dr_t])
    hi = lax.Precision.HIGHEST
    t1 = jnp.einsum('hrd,dx->hrx', rpb.astype(F32), oh_c, precision=hi)
    t2 = jnp.einsum('yr,hrx->hyx', oh_r, t1, precision=hi)
    t2 = t2.reshape(NA_H, 3, qr, kr, GW, GW).transpose(0, 1, 2, 4, 3, 5).reshape(NA_H, 3, BLK, NA_KW)
    return jnp.where(ok[None], t2, NEG)


def na_attention(proj, rpb, qn, kn):
    bias = _na_bias_table(rpb)

    def pat(i):
        return jnp.where(i <= 1, 0, jnp.where(i == NBLK_L, 2, 1))

    cq, ck, cv = _COL_NAQ // NA_GW, _COL_NAK // NA_GW, _COL_NAV // NA_GW
    return pl.pallas_call(
        _na_kernel,
        grid=(NB, NA_H // NA_HG, NBLK_L + 1),
        in_specs=[
            pl.BlockSpec((BLK, NA_GW), lambda b, h, i: (_qblk(b, i), cq + h)),
            pl.BlockSpec((S, NA_GW), lambda b, h, i: (b, ck + h)),
            pl.BlockSpec((L, NA_GW), lambda b, h, i: (CTX_BLK0 + b, ck + h)),
            pl.BlockSpec((S, NA_GW), lambda b, h, i: (b, cv + h)),
            pl.BlockSpec((L, NA_GW), lambda b, h, i: (CTX_BLK0 + b, cv + h)),
            pl.BlockSpec((NA_HG, None, BLK, NA_KW), lambda b, h, i: (h, pat(i), 0, 0)),
            pl.BlockSpec((1, NA_D), lambda b, h, i: (0, 0)),
            pl.BlockSpec((1, NA_D), lambda b, h, i: (0, 0)),
        ],
        out_specs=pl.BlockSpec((BLK, NA_GW), lambda b, h, i: (_qblk(b, i), h)),
        out_shape=jax.ShapeDtypeStruct((NT, NA_W), BF16),
        scratch_shapes=[pltpu.VMEM((L + S, NA_GW), BF16)],
        compiler_params=_cp(("arbitrary", "arbitrary", "arbitrary")),
        name="na_attention",
    )(proj, proj, proj, proj, proj, bias, qn.reshape(1, NA_D), kn.reshape(1, NA_D))


MLA_HW = 256


def _half_mask(h, width=128):
    lane = lax.broadcasted_iota(jnp.int32, (1, width), 1)
    return (lane < 64) if h % 2 == 0 else (lane >= 64)


def _mla_q_kernel(cq_ref, w_ref, qa_ref, gn_ref, g2_ref, gs2_ref, c2_ref, s2_ref, q_ref):
    x = _rms_rows(cq_ref[...].astype(F32), qa_ref[...]).astype(BF16)
    y = _dot(x, w_ref[...])
    nw = MLA_H * MLA_NOPE
    gc = g2_ref[...] * c2_ref[...]
    gs = gs2_ref[...] * s2_ref[...]
    for p in range(MLA_H // 2):
        r1 = y[:, nw + 128 * p: nw + 128 * (p + 1)]
        r2 = y[:, nw + 768 + 128 * p: nw + 768 + 128 * (p + 1)]
        rot = r1 * gc + r2 * gs
        sq = r1 * r1
        for h in (2 * p, 2 * p + 1):
            msk = _half_mask(h)
            nope = y[:, 128 * h: 128 * (h + 1)]
            ss = jnp.sum(nope * nope, axis=-1, keepdims=True) + jnp.sum(
                jnp.where(msk, sq, 0.0), axis=-1, keepdims=True)
            inv = lax.rsqrt(ss / MLA_QK + EPS) * (MLA_QK ** -0.5)
            q_ref[h, :, 0:128] = (nope * gn_ref[...] * inv).astype(BF16)
            q_ref[h, :, 128:256] = (jnp.where(msk, rot, 0.0) * inv).astype(BF16)


def _mla_kv_kernel(ckv_ref, kr_ref, w_ref, kva_ref, gn_ref, g2_ref, gs2_ref, c2_ref, s2_ref,
                   k_ref, v_ref):
    x = _rms_rows(ckv_ref[...].astype(F32), kva_ref[...]).astype(BF16)
    y = _dot(x, w_ref[...])
    kr = kr_ref[...].astype(F32)
    r1 = kr[:, 0:128]
    r2 = kr[:, 128:256]
    rot = r1 * (g2_ref[...] * c2_ref[...]) + r2 * (gs2_ref[...] * s2_ref[...])
    ss_r = jnp.sum(jnp.where(_half_mask(0), r1 * r1, 0.0), axis=-1, keepdims=True)
    nw = MLA_H * MLA_NOPE
    lane0 = lax.broadcasted_iota(jnp.int32, (kr.shape[0], 128), 1) == 0
    for h in range(MLA_H):
        nope = y[:, 128 * h: 128 * (h + 1)]
        ss = jnp.sum(nope * nope, axis=-1, keepdims=True) + ss_r
        inv = lax.rsqrt(ss / MLA_QK + EPS)
        k_ref[h, :, 0:128] = (nope * gn_ref[...] * inv).astype(BF16)
        k_ref[h, :, 128:256] = (jnp.where(_half_mask(h), rot, 0.0) * inv).astype(BF16)
        v_ref[h, :, 0:128] = y[:, nw + 128 * h: nw + 128 * (h + 1)].astype(BF16)
        v_ref[h, :, 128:256] = jnp.where(lane0, 1.0, 0.0).astype(BF16)


def _rope_tables():
    t = jnp.arange(S, dtype=jnp.int32)
    pos = (t // GW, t % GW)
    nf = MLA_ROPE // 4
    inv = ROPE_BASE ** (-jnp.arange(nf, dtype=F32) / nf)
    cs, sn = [], []
    for ax in range(2):
        ang = pos[ax].astype(F32)[:, None] * inv[None, :]
        c, s = jnp.cos(ang), jnp.sin(ang)
        cs += [c, c]
        sn += [-s, s]
    c64 = jnp.tile(jnp.concatenate(cs, axis=1), (NB, 1))
    s64 = jnp.tile(jnp.concatenate(sn, axis=1), (NB, 1))
    c64 = jnp.concatenate([c64, jnp.ones((NB * L, MLA_ROPE), F32)], axis=0)
    s64 = jnp.concatenate([s64, jnp.zeros((NB * L, MLA_ROPE), F32)], axis=0)
    return jnp.tile(c64, (1, 2)), jnp.tile(s64, (1, 2))


_ROPE_SWAP = np.concatenate([np.arange(16, 32), np.arange(0, 16), np.arange(48, 64), np.arange(32, 48)])


def _rope_gains(g):
    gr = g[MLA_NOPE:]
    return (g[:MLA_NOPE].reshape(1, 128), jnp.tile(gr, 2).reshape(1, 128),
            jnp.tile(gr[_ROPE_SWAP], 2).reshape(1, 128))


def mla_q_prep(proj, w_uq_r, qa_g, qn_g, c2, s2):
    tm = 512
    gn, g2, gs2 = _rope_gains(qn_g)
    vec = lambda w: pl.BlockSpec((1, w), lambda i: (0, 0))
    return pl.pallas_call(
        _mla_q_kernel,
        grid=(NT // tm,),
        in_specs=[
            pl.BlockSpec((tm, MLA_QL), lambda i: (i, _COL_CQ // MLA_QL)),
            pl.BlockSpec((MLA_QL, 3072), lambda i: (0, 0)),
            vec(MLA_QL), vec(128), vec(128), vec(128),
            pl.BlockSpec((tm, 128), lambda i: (i, 0)),
            pl.BlockSpec((tm, 128), lambda i: (i, 0)),
        ],
        out_specs=pl.BlockSpec((MLA_H, tm, MLA_HW), lambda i: (0, i, 0)),
        out_shape=jax.ShapeDtypeStruct((MLA_H, NT, MLA_HW), BF16),
        compiler_params=_cp(("arbitrary",)),
        name="mla_q_prep",
    )(proj, w_uq_r, qa_g.reshape(1, MLA_QL), gn, g2, gs2, c2, s2)


def mla_kv_prep(proj, w_ukv_r, kva_g, kn_g, c2, s2):
    tm = 512
    gn, g2, gs2 = _rope_gains(kn_g)
    vec = lambda w: pl.BlockSpec((1, w), lambda i: (0, 0))
    return pl.pallas_call(
        _mla_kv_kernel,
        grid=(NT // tm,),
        in_specs=[
            pl.BlockSpec((tm, MLA_KVL), lambda i: (i, _COL_CKV // MLA_KVL)),
            pl.BlockSpec((tm, 256), lambda i: (i, _COL_KR // 256)),
            pl.BlockSpec((MLA_KVL, 3072), lambda i: (0, 0)),
            vec(MLA_KVL), vec(128), vec(128), vec(128),
            pl.BlockSpec((tm, 128), lambda i: (i, 0)),
            pl.BlockSpec((tm, 128), lambda i: (i, 0)),
        ],
        out_specs=[pl.BlockSpec((MLA_H, tm, MLA_HW), lambda i: (0, i, 0)),
                   pl.BlockSpec((MLA_H, tm, MLA_HW), lambda i: (0, i, 0))],
        out_shape=[jax.ShapeDtypeStruct((MLA_H, NT, MLA_HW), BF16),
                   jax.ShapeDtypeStruct((MLA_H, NT, MLA_HW), BF16)],
        compiler_params=_cp(("arbitrary",)),
        name="mla_kv_prep",
    )(proj, proj, w_ukv_r, kva_g.reshape(1, MLA_KVL), gn, g2, gs2, c2, s2)


MLA_TQ = 512


def _mla_attn_kernel(q_ref, kl_ref, kc_ref, vl_ref, vc_ref, o_ref, s0_ref, s1_ref, p0_ref, p1_ref, acc_ref):
    g = pl.program_id(0)

    @pl.when(g == 0)
    def _():
        s0_ref[...] = jnp.zeros_like(s0_ref)
        s1_ref[...] = jnp.zeros_like(s1_ref)
        p0_ref[...] = jnp.ones_like(p0_ref)
        p1_ref[...] = jnp.ones_like(p1_ref)

    n_kc = (L + S) // L
    rows = MLA_TQ // (n_kc - 1)

    def tie(x, dep):
        if dep is None:
            return x
        return jnp.concatenate([x[0:16, :] + dep, x[16:, :]], axis=0)

    def stages(s_new, s_old, p_new, p_old):
        acc, dep = None, None
        for c in range(n_kc):
            keys = slice(L * c, L * (c + 1))
            k = kc_ref[...] if c == 0 else kl_ref[L * (c - 1): L * c, :]
            v = vc_ref[...] if c == 0 else vl_ref[L * (c - 1): L * c, :]
            if c < n_kc - 1:
                r = slice(rows * c, rows * (c + 1))
                s = s_old[r, :]
                p = jnp.exp((s - jnp.max(s, axis=-1, keepdims=True)).astype(BF16))
                p_old[r, :] = p
                dep = (p[0:16, 0:MLA_HW].astype(F32) * 0.0).astype(BF16)
            part = _dot(tie(p_new[:, keys], dep), v)
            acc = part if acc is None else acc + part
            s_new[:, keys] = _dot_nt(tie(q_ref[...], dep), k)
        o_ref[...] = (acc[:, :MLA_V] / acc[:, MLA_V:MLA_V + 1]).astype(o_ref.dtype)

    @pl.when(g % 2 == 0)
    def _():
        stages(s0_ref, s1_ref, p0_ref, p1_ref)

    @pl.when(g % 2 == 1)
    def _():
        stages(s1_ref, s0_ref, p1_ref, p0_ref)


def _mla_ctx_kernel(q_ref, kc_ref, vc_ref, o_ref):
    s = _dot_nt(q_ref[...], kc_ref[...])
    p = jnp.exp((s - jnp.max(s, axis=-1, keepdims=True)).astype(BF16))
    acc = _dot(p, vc_ref[...])
    o_ref[...] = (acc[:, :MLA_V] / acc[:, MLA_V:MLA_V + 1]).astype(o_ref.dtype)


def mla_attention(q, k, v):
    nq = S // MLA_TQ
    n_steps = NB * MLA_H * nq

    def bhi(g):
        return g // (MLA_H * nq), (g // nq) % MLA_H, g % nq

    def cur(g):
        return bhi(jnp.minimum(g, n_steps - 1))

    def prev(g):
        return bhi(jnp.maximum(g - 2, 0))

    ob = pl.pallas_call(
        _mla_attn_kernel,
        grid=(n_steps + 2,),
        in_specs=[
            pl.BlockSpec((None, MLA_TQ, MLA_HW), lambda g: (cur(g)[1], cur(g)[0] * nq + cur(g)[2], 0)),
            pl.BlockSpec((None, S, MLA_HW), lambda g: (cur(g)[1], cur(g)[0], 0)),
            pl.BlockSpec((None, L, MLA_HW), lambda g: (cur(g)[1], CTX_BLK0 + cur(g)[0], 0)),
            pl.BlockSpec((None, S, MLA_HW), lambda g: (prev(g)[1], prev(g)[0], 0)),
            pl.BlockSpec((None, L, MLA_HW), lambda g: (prev(g)[1], CTX_BLK0 + prev(g)[0], 0)),
        ],
        out_specs=pl.BlockSpec((MLA_TQ, MLA_V), lambda g: (prev(g)[0] * nq + prev(g)[2], prev(g)[1])),
        out_shape=jax.ShapeDtypeStruct((NL, MLA_W), BF16),
        scratch_shapes=[pltpu.VMEM((MLA_TQ, L + S), F32), pltpu.VMEM((MLA_TQ, L + S), F32),
                        pltpu.VMEM((MLA_TQ, L + S), BF16), pltpu.VMEM((MLA_TQ, L + S), BF16),
                        pltpu.VMEM((MLA_TQ, MLA_HW), F32)],
        compiler_params=_cp(("arbitrary",)),
        name="mla_attention",
    )(q, k, k, v, v)
    ob_ctx = pl.pallas_call(
        _mla_ctx_kernel,
        grid=(NB, MLA_H),
        in_specs=[
            pl.BlockSpec((None, L, MLA_HW), lambda b, h: (h, CTX_BLK0 + b, 0)),
            pl.BlockSpec((None, L, MLA_HW), lambda b, h: (h, CTX_BLK0 + b, 0)),
            pl.BlockSpec((None, L, MLA_HW), lambda b, h: (h, CTX_BLK0 + b, 0)),
        ],
        out_specs=pl.BlockSpec((L, MLA_V), lambda b, h: (b, h)),
        out_shape=jax.ShapeDtypeStruct((NB * L, MLA_W), BF16),
        compiler_params=_cp(("arbitrary", "arbitrary")),
        name="mla_attention_ctx",
    )(q, k, v)
    return ob, ob_ctx


N_CH_C = L // CH
N_CH_L = S // CH
N_CH = N_CH_C + N_CH_L


def _gla_kernel(qf_ref, kf_ref, vf_ref, af_ref, qb_ref, kb_ref, vb_ref, ab_ref,
                wg_ref, bg_ref, of_ref, ob_ref, st_ref):
    t = pl.program_id(1)

    @pl.when(t == 0)
    def _():
        st_ref[...] = jnp.zeros_like(st_ref)

    row = lax.broadcasted_iota(jnp.int32, (CH, CH), 0)
    col = lax.broadcasted_iota(jnp.int32, (CH, CH), 1)
    dirs = (
        (qf_ref, kf_ref, vf_ref, af_ref, of_ref, col <= row, CH - 1),
        (qb_ref, kb_ref, vb_ref, ab_ref, ob_ref, col >= row, 0),
    )
    for d, (q_ref, k_ref, v_ref, a_ref, o_ref, keep, last) in enumerate(dirs):
        z = _dot(a_ref[...], wg_ref[d]) + bg_ref[d]
        g = (jnp.minimum(z, 0.0) - jnp.log(1.0 + jnp.exp(-jnp.abs(z)))) / GLA_TAU
        b = jnp.dot(keep.astype(F32), g, preferred_element_type=F32,
                    precision=lax.Precision.HIGHEST)
        b_end = b[last:last + 1, :]
        b_mid = b[CH // 2:CH // 2 + 1, :]
        q = q_ref[...].astype(F32) * (GLA_DK ** -0.5)
        k = k_ref[...].astype(F32)
        qa = (q * jnp.exp(b - b_mid)).astype(BF16)
        ka = (k * jnp.exp(b_mid - b)).astype(BF16)
        qd = (q * jnp.exp(b)).astype(BF16)
        ke = (k * jnp.exp(b_end - b)).astype(BF16)
        e_end = jnp.exp(b_end)
        v = v_ref[...]
        for h in range(GLA_H):
            ksl = slice(GLA_DK * h, GLA_DK * (h + 1))
            vsl = slice(GLA_DV * h, GLA_DV * (h + 1))
            att = jnp.where(keep, _dot_nt(qa[:, ksl], ka[:, ksl]), 0.0)
            st = st_ref[d, h]
            o = _dot_nt(qd[:, ksl], st.astype(BF16)) + _dot(att.astype(BF16), v[:, vsl])
            o_ref[:, vsl] = o
            st_ref[d, h] = st * e_end[:, ksl] + _dot_tn(v[:, vsl], ke[:, ksl])


def gla_scan(proj, wg_f, bg_f, wg_b, bg_b):
    wg = jnp.zeros((2, 128, GLA_KW), F32)
    wg = wg.at[0, 0:GLA_RANK].set(wg_f).at[1, GLA_RANK:2 * GLA_RANK].set(wg_b).astype(BF16)
    bg = jnp.stack([bg_f, bg_b]).reshape(2, 1, GLA_KW)

    ctx0 = NL // CH

    def fwd(b, t):
        return jnp.where(t < N_CH_C, ctx0 + N_CH_C * b + t, N_CH_L * b + t - N_CH_C)

    def bwd(b, t):
        return jnp.where(t < N_CH_C, ctx0 + N_CH_C * b + N_CH_C - 1 - t, N_CH_L * b + N_CH - 1 - t)

    def specs(rowfn):
        return [
            pl.BlockSpec((CH, GLA_KW), lambda b, t: (rowfn(b, t), _COL_GQ // GLA_KW)),
            pl.BlockSpec((CH, GLA_KW), lambda b, t: (rowfn(b, t), _COL_GK // GLA_KW)),
            pl.BlockSpec((CH, GLA_W), lambda b, t: (rowfn(b, t), _COL_GV // GLA_W)),
            pl.BlockSpec((CH, 128), lambda b, t: (rowfn(b, t), _COL_GA // 128)),
        ]

    return pl.pallas_call(
        _gla_kernel,
        grid=(NB, N_CH),
        in_specs=specs(fwd) + specs(bwd) + [
            pl.BlockSpec((2, 128, GLA_KW), lambda b, t: (0, 0, 0)),
            pl.BlockSpec((2, 1, GLA_KW), lambda b, t: (0, 0, 0)),
        ],
        out_specs=[pl.BlockSpec((CH, GLA_W), lambda b, t: (fwd(b, t), 0)),
                   pl.BlockSpec((CH, GLA_W), lambda b, t: (bwd(b, t), 0))],
        out_shape=[jax.ShapeDtypeStruct((NT, GLA_W), F32)] * 2,
        scratch_shapes=[pltpu.VMEM((2, GLA_H, GLA_DV, GLA_DK), F32)],
        compiler_params=_cp(("arbitrary", "arbitrary")),
        name="gla_scan",
    )(proj, proj, proj, proj, proj, proj, proj, proj, wg, bg)


def _gla_out_kernel(of_ref, ob_ref, g_ref, gn_ref, o_ref):
    for h in range(GLA_H):
        sl = slice(GLA_DV * h, GLA_DV * (h + 1))
        o = _rms_rows(of_ref[:, sl] + ob_ref[:, sl], gn_ref[...])
        gate = g_ref[:, sl].astype(F32)
        o_ref[:, sl] = (o * (gate * _sigmoid(gate))).astype(o_ref.dtype)


def gla_output(o_f, o_b, proj, on_g):
    tm = 512
    return pl.pallas_call(
        _gla_out_kernel,
        grid=(NT // tm,),
        in_specs=[
            pl.BlockSpec((tm, GLA_W), lambda i: (i, 0)),
            pl.BlockSpec((tm, GLA_W), lambda i: (i, 0)),
            pl.BlockSpec((tm, GLA_W), lambda i: (i, _COL_GG // GLA_W)),
            pl.BlockSpec((1, GLA_DV), lambda i: (0, 0)),
        ],
        out_specs=pl.BlockSpec((tm, GLA_W), lambda i: (i, 0)),
        out_shape=jax.ShapeDtypeStruct((NT, GLA_W), BF16),
        compiler_params=_cp(("arbitrary",)),
        name="gla_output",
    )(o_f, o_b, proj, on_g.reshape(1, GLA_DV))


OUT_TM = NB * L


def _out_proj_kernel(oa_ref, obl_ref, obc_ref, oc_ref, w_ref, x_ref, g_ref, o_ref):
    ob = jnp.where(pl.program_id(1) < NL // OUT_TM, obl_ref[...], obc_ref[...])
    a = jnp.concatenate([oa_ref[...], ob, oc_ref[...]], axis=1)
    o_ref[...] = x_ref[...] + g_ref[...] * _dot(a, w_ref[...])


def out_projection(oa, ob, ob_ctx, oc, w_out, layer, xs, mod, k_gate):
    tm, tn = OUT_TM, 1024
    nj = D // tn
    return pl.pallas_call(
        _out_proj_kernel,
        grid=(nj, NT // tm),
        in_specs=[
            pl.BlockSpec((tm, NA_W), lambda j, i: (i, 0)),
            pl.BlockSpec((tm, MLA_W), lambda j, i: (jnp.minimum(i, NL // tm - 1), 0)),
            pl.BlockSpec((tm, MLA_W), lambda j, i: (0, 0)),
            pl.BlockSpec((tm, GLA_W), lambda j, i: (i, 0)),
            pl.BlockSpec((None, D, tn), lambda j, i: (layer, 0, j)),
            pl.BlockSpec((tm, tn), lambda j, i: (i, j)),
            pl.BlockSpec((None, 1, tn), lambda j, i: (_row_group(i, tm), 0, k_gate * nj + j)),
        ],
        out_specs=pl.BlockSpec((tm, tn), lambda j, i: (i, j)),
        out_shape=jax.ShapeDtypeStruct((NT, D), F32),
        compiler_params=_cp(("arbitrary", "arbitrary")),
        name="out_projection",
    )(oa, ob, ob_ctx, oc, w_out, xs, mod)


def _route(logits, router_bias):
    scores = jax.nn.sigmoid(logits)
    grouped = (scores + router_bias.astype(F32)).reshape(NT, E_GROUPS, E_PER)

    def top2(a):
        idx = lax.broadcasted_iota(jnp.int32, a.shape, a.ndim - 1)
        i1 = jnp.argmax(a, axis=-1).astype(jnp.int32)
        rest = jnp.where(idx == i1[..., None], -jnp.inf, a)
        i2 = jnp.argmax(rest, axis=-1).astype(jnp.int32)
        return jnp.max(a, axis=-1), jnp.max(rest, axis=-1), i1, i2

    m1, m2, _, _ = top2(grouped)
    grp = jnp.argmax(m1 + m2, axis=-1).astype(jnp.int32)
    gsel = lax.broadcasted_iota(jnp.int32, (NT, E_GROUPS, E_PER), 1) == grp[:, None, None]
    in_group = jnp.sum(jnp.where(gsel, grouped, 0.0), axis=1)
    _, _, l1, l2 = top2(in_group)
    expert_idx = grp[:, None] * E_PER + jnp.stack([l1, l2], axis=-1)
    esel = lax.broadcasted_iota(jnp.int32, (NT, TOPK, E), 2) == expert_idx[:, :, None]
    w = jnp.sum(jnp.where(esel, scores[:, None, :], 0.0), axis=-1)
    return expert_idx, w / jnp.sum(w, axis=-1, keepdims=True)


def _dispatch_plan(expert_idx):
    flat_e = expert_idx.reshape(NK)
    onehot = (flat_e[:, None] == jnp.arange(E, dtype=jnp.int32)[None, :]).astype(jnp.int32)
    csum = jnp.cumsum(onehot, axis=0)
    counts = csum[-1]
    rank = jnp.take_along_axis(csum, flat_e[:, None], axis=1)[:, 0] - 1
    padded = (counts + MOE_R - 1) // MOE_R * MOE_R
    pad_end = jnp.cumsum(padded)
    pad_start = pad_end - padded
    dest = (pad_start[flat_e] + rank).astype(jnp.int32)
    row_token = jnp.zeros((MOE_P,), jnp.int32).at[dest].set(jnp.arange(NK, dtype=jnp.int32) // TOPK)
    blk0 = jnp.arange(MOE_NB, dtype=jnp.int32) * MOE_R
    block_expert = jnp.minimum(jnp.sum((pad_end[None, :] <= blk0[:, None]).astype(jnp.int32), axis=1), E - 1)
    n_active = (pad_end[-1] // MOE_R).astype(jnp.int32).reshape(1)
    n_valid = jnp.clip(counts[block_expert] - (blk0 - pad_start[block_expert]), 0, MOE_R).astype(jnp.int32)
    bidx = jnp.arange(MOE_NB, dtype=jnp.int32)
    prev_e = jnp.concatenate([jnp.full((1,), -1, jnp.int32), block_expert[:-1]])
    first = jnp.logical_and(bidx < n_active[0], block_expert != prev_e)
    later = jnp.where(first, bidx, MOE_NB)
    nxt_blk = jnp.concatenate([lax.cummin(later[::-1])[::-1][1:], jnp.full((1,), MOE_NB, jnp.int32)])
    nxt_e = jnp.where(nxt_blk < MOE_NB, block_expert[jnp.minimum(nxt_blk, MOE_NB - 1)], -1).astype(jnp.int32)
    plan = (block_expert, n_active, first.astype(jnp.int32), nxt_e)
    return dest, row_token, n_valid, plan


DH = D // 2


DMA_UNROLL = 8


def _for_rows(n, body):
    full = n // DMA_UNROLL

    def group(t, c):
        for u in range(DMA_UNROLL):
            body(t * DMA_UNROLL + u)
        return c

    def single(r, c):
        body(r)
        return c

    lax.fori_loop(0, full, group, 0)
    lax.fori_loop(full * DMA_UNROLL, n, single, 0)


def _gather_rows_kernel(tok_ref, nv_ref, h_ref, o_ref, sem):
    i = pl.program_id(0)
    base = i * MOE_R
    nv = nv_ref[i]

    @pl.when(nv < MOE_R)
    def _():
        o_ref[...] = jnp.zeros_like(o_ref)

    def copy(r):
        return pltpu.make_async_copy(h_ref.at[pl.ds(tok_ref[base + r], 1)], o_ref.at[pl.ds(r, 1)], sem)

    _for_rows(nv, lambda r: copy(r).start())
    _for_rows(nv, lambda r: copy(r).wait())


def gather_rows(row_token, n_valid, h):
    return pl.pallas_call(
        _gather_rows_kernel,
        grid_spec=pltpu.PrefetchScalarGridSpec(
            num_scalar_prefetch=2,
            grid=(MOE_NB,),
            in_specs=[pl.BlockSpec(memory_space=pl.ANY)],
            out_specs=pl.BlockSpec((MOE_R, DH), lambda i, tok, nv: (i, 0)),
            scratch_shapes=[pltpu.SemaphoreType.DMA(())],
        ),
        out_shape=jax.ShapeDtypeStruct((MOE_P, DH), jnp.uint32),
        compiler_params=_cp(("arbitrary",)),
        name="moe_gather_rows",
    )(row_token, n_valid, h)


def _stage_weights(layer, n_tiles, width, be_ref, first_ref, nxt_ref, active, mats):
    t = pl.program_id(0)
    i = pl.program_id(1)

    def copies(e, tt):
        cols = pl.ds(pl.multiple_of(tt * width, width), width)
        return [pltpu.make_async_copy(w.at[layer, e, :, cols], st, sm) for w, st, _, sm in mats]

    @pl.when(jnp.logical_and(active, first_ref[i] == 1))
    def _():
        @pl.when(jnp.logical_and(t == 0, i == 0))
        def _():
            for c in copies(be_ref[0], 0):
                c.start()

        for c in copies(be_ref[i], t):
            c.wait()
        for _, st, wb, _ in mats:
            wb[...] = st[...].astype(BF16)

        @pl.when(nxt_ref[i] >= 0)
        def _():
            for c in copies(nxt_ref[i], t):
                c.start()

        @pl.when(jnp.logical_and(nxt_ref[i] < 0, t + 1 < n_tiles))
        def _():
            for c in copies(be_ref[0], t + 1):
                c.start()


def _ffn_up_kernel(layer, be_ref, na_ref, first_ref, nxt_ref, x_ref, w1_ref, w3_ref, o_ref,
                   st1_ref, st3_ref, w1b_ref, w3b_ref, sem):
    i = pl.program_id(1)
    active = i < na_ref[0]
    _stage_weights(layer, FF // MOE_TF, MOE_TF, be_ref, first_ref, nxt_ref, active,
                   [(w1_ref, st1_ref, w1b_ref, sem.at[0]), (w3_ref, st3_ref, w3b_ref, sem.at[1])])

    @pl.when(active)
    def _():
        lo, hi = _unpack_pairs(x_ref[...])
        lo, hi = lo.astype(BF16), hi.astype(BF16)
        a = _dot(lo, w1b_ref[0:DH, :]) + _dot(hi, w1b_ref[DH:, :])
        b = _dot(lo, w3b_ref[0:DH, :]) + _dot(hi, w3b_ref[DH:, :])
        o_ref[...] = (a * _sigmoid(a) * b).astype(o_ref.dtype)

    @pl.when(jnp.logical_not(active))
    def _():
        o_ref[...] = jnp.zeros_like(o_ref)


def ffn_up(plan, xb, w1, w3, layer):
    last = lambda i, na: jnp.minimum(i, na[0] - 1)
    return pl.pallas_call(
        functools.partial(_ffn_up_kernel, layer),
        grid_spec=pltpu.PrefetchScalarGridSpec(
            num_scalar_prefetch=4,
            grid=(FF // MOE_TF, MOE_NB),
            in_specs=[
                pl.BlockSpec((MOE_R, DH), lambda j, i, be, na, fi, nx: (last(i, na), 0)),
                pl.BlockSpec(memory_space=pl.ANY),
                pl.BlockSpec(memory_space=pl.ANY),
            ],
            out_specs=pl.BlockSpec((MOE_R, MOE_TF), lambda j, i, be, na, fi, nx: (i, j)),
            scratch_shapes=[pltpu.VMEM((D, MOE_TF), F32), pltpu.VMEM((D, MOE_TF), F32),
                            pltpu.VMEM((D, MOE_TF), BF16), pltpu.VMEM((D, MOE_TF), BF16),
                            pltpu.SemaphoreType.DMA((2,))],
        ),
        out_shape=jax.ShapeDtypeStruct((MOE_P, FF), BF16),
        compiler_params=_cp(("arbitrary", "arbitrary")),
        name="moe_ffn_up",
    )(*plan, xb, w1, w3)


MOE_TN = 2048


def _ffn_down_kernel(layer, be_ref, na_ref, first_ref, nxt_ref, h_ref, w2_ref, o_ref, st2_ref, w2b_ref, sem):
    i = pl.program_id(1)
    active = i < na_ref[0]
    _stage_weights(layer, D // MOE_TN, MOE_TN, be_ref, first_ref, nxt_ref, active,
                   [(w2_ref, st2_ref, w2b_ref, sem.at[0])])

    @pl.when(active)
    def _():
        y = _dot(h_ref[...], w2b_ref[...])
        o_ref[...] = _pack_pairs(y[:, :MOE_TN // 2], y[:, MOE_TN // 2:])

    @pl.when(jnp.logical_not(active))
    def _():
        o_ref[...] = jnp.zeros_like(o_ref)


def ffn_down(plan, hmid, w2, layer):
    last = lambda i, na: jnp.minimum(i, na[0] - 1)
    return pl.pallas_call(
        functools.partial(_ffn_down_kernel, layer),
        grid_spec=pltpu.PrefetchScalarGridSpec(
            num_scalar_prefetch=4,
            grid=(D // MOE_TN, MOE_NB),
            in_specs=[
                pl.BlockSpec((MOE_R, FF), lambda n, i, be, na, fi, nx: (last(i, na), 0)),
                pl.BlockSpec(memory_space=pl.ANY),
            ],
            out_specs=pl.BlockSpec((MOE_R, MOE_TN // 2), lambda n, i, be, na, fi, nx: (i, n)),
            scratch_shapes=[pltpu.VMEM((FF, MOE_TN), F32), pltpu.VMEM((FF, MOE_TN), BF16),
                            pltpu.SemaphoreType.DMA((1,))],
        ),
        out_shape=jax.ShapeDtypeStruct((MOE_P, DH), jnp.uint32),
        compiler_params=_cp(("arbitrary", "arbitrary")),
        name="moe_ffn_down",
    )(*plan, hmid, w2)


CMB_T = 256


def _combine_kernel(pos_ref, yb_ref, gate_ref, x_ref, g_ref, o_ref, buf_ref, sem):
    base = pl.program_id(0) * CMB_T

    def copy(r, k):
        p = pos_ref[(base + r) * TOPK + k]
        return pltpu.make_async_copy(yb_ref.at[pl.ds(p, 1)], buf_ref.at[k, pl.ds(r, 1)], sem.at[k])

    def start(r, c):
        copy(r, 0).start()
        copy(r, 1).start()
        return c

    def wait(r, c):
        copy(r, 0).wait()
        copy(r, 1).wait()
        return c

    lax.fori_loop(0, CMB_T, start, 0, unroll=DMA_UNROLL)
    lax.fori_loop(0, CMB_T, wait, 0, unroll=DMA_UNROLL)
    gate = gate_ref[...]
    g0, g1 = gate[:, 0:1], gate[:, 1:2]
    half = MOE_TN // 2
    for n in range(D // MOE_TN):
        lo0, hi0 = _unpack_pairs(buf_ref[0, :, half * n: half * (n + 1)])
        lo1, hi1 = _unpack_pairs(buf_ref[1, :, half * n: half * (n + 1)])
        for part, y in enumerate((lo0 * g0 + lo1 * g1, hi0 * g0 + hi1 * g1)):
            sl = slice(MOE_TN * n + half * part, MOE_TN * n + half * (part + 1))
            o_ref[:, sl] = x_ref[:, sl] + g_ref[:, sl] * y


def moe_combine(dest, yb, gates, xs, mod, k_gate, out_rows):
    return pl.pallas_call(
        _combine_kernel,
        grid_spec=pltpu.PrefetchScalarGridSpec(
            num_scalar_prefetch=1,
            grid=(out_rows // CMB_T,),
            in_specs=[
                pl.BlockSpec(memory_space=pl.ANY),
                pl.BlockSpec((CMB_T, TOPK), lambda i, pos: (i, 0)),
                pl.BlockSpec((CMB_T, D), lambda i, pos: (i, 0)),
                pl.BlockSpec((None, 1, D), lambda i, pos: (_row_group(i, CMB_T), 0, k_gate)),
            ],
            out_specs=pl.BlockSpec((CMB_T, D), lambda i, pos: (i, 0)),
            scratch_shapes=[pltpu.VMEM((TOPK, CMB_T, DH), jnp.uint32), pltpu.SemaphoreType.DMA((TOPK,))],
        ),
        out_shape=jax.ShapeDtypeStruct((out_rows, D), F32),
        compiler_params=_cp(("arbitrary",)),
        name="moe_combine",
    )(dest, yb, gates, xs, mod)


IN_W = 9312


W_RB = 64
_RB_KR_SRC = 4608 // W_RB
_RB_GA_SRC = 9280 // W_RB
_RB_KR = _COL_KR // W_RB
_RB_GA = _COL_GA // W_RB


def _w_in_layout_kernel(w_ref, o_ref):
    j = pl.program_id(1)

    @pl.when(j < _RB_KR + 2)
    def _():
        o_ref[...] = w_ref[...].astype(BF16)

    @pl.when(jnp.logical_and(j >= _RB_KR + 2, j < _RB_GA))
    def _():
        x = w_ref[...]
        o_ref[...] = jnp.concatenate([x[16:32], x[0:16], x[48:64], x[32:48]], axis=0).astype(BF16)

    @pl.when(j == _RB_GA)
    def _():
        row = lax.broadcasted_iota(jnp.int32, (W_RB, D), 0)
        o_ref[...] = jnp.where(row < 2 * GLA_RANK, w_ref[...], 0.0).astype(BF16)

    @pl.when(j > _RB_GA)
    def _():
        o_ref[...] = jnp.zeros_like(o_ref)


def _w_in_layout(w):
    w_t = jnp.swapaxes(w, 1, 2)

    def src(j):
        seg2 = j + MLA_ROPE // W_RB
        return jnp.where(j < _RB_KR_SRC, j, jnp.where(j < _RB_KR, seg2, jnp.where(j < _RB_GA, _RB_KR_SRC, _RB_GA_SRC)))

    return pl.pallas_call(
        _w_in_layout_kernel,
        grid=(DEPTH, PW // W_RB),
        in_specs=[pl.BlockSpec((None, W_RB, D), lambda l, j: (l, src(j), 0))],
        out_specs=pl.BlockSpec((None, W_RB, D), lambda l, j: (l, j, 0)),
        out_shape=jax.ShapeDtypeStruct((DEPTH, PW, D), BF16),
        compiler_params=_cp(("arbitrary", "arbitrary")),
        name="w_in_layout",
    )(w_t)


def _w_uq_cols():
    nope = [MLA_QK * h + j for h in range(MLA_H) for j in range(MLA_NOPE)]
    rope = [MLA_QK * h + MLA_NOPE + d for h in range(MLA_H) for d in range(MLA_ROPE)]
    rope_s = [MLA_QK * h + MLA_NOPE + int(d) for h in range(MLA_H) for d in _ROPE_SWAP]
    return np.array(nope + rope + rope_s, np.int32)


def _w_ukv_cols():
    kn = [(MLA_NOPE + MLA_V) * h + j for h in range(MLA_H) for j in range(MLA_NOPE)]
    vv = [(MLA_NOPE + MLA_V) * h + MLA_NOPE + j for h in range(MLA_H) for j in range(MLA_V)]
    return np.array(kn + vv, np.int32)


def token_mixing_layer(xs, mod, layer, norm1, w_in_b, w_out_b, na_qn, na_kn, na_rpb, mla_qa, mla_kva, w_uq, w_ukv,
                       mla_qn, mla_kn, gwf, gbf, gwb, gbb, gla_on, c2, s2):
    h = norm_modulate(xs, norm1.reshape(1, D), mod, 0, 1, BF16)
    proj = matmul(h, w_in_b, layer, NT // 8, 512, BF16)
    oa = na_attention(proj, na_rpb, na_qn, na_kn)
    q = mla_q_prep(proj, w_uq[:, _w_uq_cols()].astype(BF16), mla_qa, mla_qn, c2, s2)
    k, v = mla_kv_prep(proj, w_ukv[:, _w_ukv_cols()].astype(BF16), mla_kva, mla_kn, c2, s2)
    ob, ob_ctx = mla_attention(q, k, v)
    o_f, o_b = gla_scan(proj, gwf, gbf, gwb, gbb)
    oc = gla_output(o_f, o_b, proj, gla_on)
    return out_projection(oa, ob, ob_ctx, oc, w_out_b, layer, xs, mod, 2)


def moe_layer(xs, mod, norm2, w_router_p, router_bias, w1, w3, w2, layer, out_rows):
    h, logits = norm_modulate(xs, norm2.reshape(1, D), mod, 3, 4, None, w_router_p)
    expert_idx, gates = _route(logits[:, :E], router_bias)
    dest, row_token, n_valid, plan = _dispatch_plan(expert_idx)
    xb = gather_rows(row_token, n_valid, h)
    hmid = ffn_up(plan, xb, w1, w3, layer)
    yb = ffn_down(plan, hmid, w2, layer)
    return moe_combine(dest, yb, gates, xs, mod, 5, out_rows)


def kernel(x, c, ctx, c_ctx, w_ada, b_ada, norm1, norm2, w_in, w_out, na_q_norm, na_k_norm, na_rpb, mla_qa_norm, mla_kva_norm, mla_w_uq, mla_w_ukv, mla_q_norm, mla_k_norm, gla_w_gate_f, gla_b_gate_f, gla_w_gate_b, gla_b_gate_b, gla_out_norm, w_router, router_bias, moe_w1, moe_w3, moe_w2):
    cond = jnp.concatenate([c, c_ctx[None, :], jnp.zeros((8 - NB - 1, D), F32)], axis=0)
    mods = ada_modulation(cond, w_ada, b_ada)
    xs = jnp.concatenate([x.reshape(NL, D), ctx.reshape(NB * L, D)], axis=0)
    c2, s2 = _rope_tables()
    w_router_p = jnp.concatenate([w_router, jnp.zeros((D, 128 - E), F32)], axis=1)
    w_in_b = _w_in_layout(w_in)
    w_out_b = w_out.astype(BF16)
    for l in range(DEPTH):
        mod = mods[l].reshape(8, 1, 6 * D)
        xs = token_mixing_layer(xs, mod, l, norm1[l], w_in_b, w_out_b, na_q_norm[l], na_k_norm[l], na_rpb[l],
                                mla_qa_norm[l], mla_kva_norm[l], mla_w_uq[l], mla_w_ukv[l],
                                mla_q_norm[l], mla_k_norm[l], gla_w_gate_f[l], gla_b_gate_f[l],
                                gla_w_gate_b[l], gla_b_gate_b[l], gla_out_norm[l], c2, s2)
        xs = moe_layer(xs, mod, norm2[l], w_router_p, router_bias, moe_w1, moe_w3, moe_w2, l,
                       NT if l < DEPTH - 1 else NL)
    return xs.reshape(NB, S, D)
```

```python
import functools

import numpy as np
import jax
import jax.numpy as jnp
from jax import lax
from jax.experimental import pallas as pl
from jax.experimental.pallas import tpu as pltpu

F32 = jnp.float32
BF16 = jnp.bfloat16

D = 4096
NB = 2
S = 4096
L = 256
DEPTH = 2
GW = 64
EPS = 1e-6
NL = NB * S
NT = NL + NB * L
NA_H, NA_D = 8, 128
NA_W = NA_H * NA_D
NA_R, NA_C = 8, 16
MLA_H = 12
MLA_QL, MLA_KVL = 1024, 512
MLA_NOPE, MLA_ROPE, MLA_V = 128, 64, 128
MLA_QK = MLA_NOPE + MLA_ROPE
MLA_W = MLA_H * MLA_V
GLA_H, GLA_DK, GLA_DV = 6, 128, 256
GLA_KW = GLA_H * GLA_DK
GLA_W = GLA_H * GLA_DV
GLA_RANK = 16
GLA_TAU = 16.0
CH = 128
E = 16
E_GROUPS = 4
E_PER = E // E_GROUPS
TOPK = 2
FF = 1024
ROPE_BASE = 10000.0

_COL_NAQ, _COL_NAK, _COL_NAV = 0, 1024, 2048
_COL_CQ, _COL_CKV = 3072, 4096
_COL_GQ, _COL_GK, _COL_GV, _COL_GG = 4608, 5376, 6144, 7680
_COL_KR = 9216
_COL_GA = 9472
PW = 9728

BLK = 256
NBLK_L = S // BLK
CTX_BLK0 = NL // BLK
MOE_R = 256
MOE_TF = 512
NK = NT * TOPK
MOE_NB = (NK + E * (MOE_R - 1)) // MOE_R
MOE_P = MOE_NB * MOE_R

NEG = -1e30
VMEM_LIMIT = 56 * 1024 * 1024


def _cp(sem, vmem=VMEM_LIMIT):
    return pltpu.CompilerParams(dimension_semantics=sem, vmem_limit_bytes=vmem)


def _dot(a, b):
    return jnp.dot(a, b, preferred_element_type=F32)


def _dot_nt(a, b):
    return lax.dot_general(a, b, (((1,), (1,)), ((), ())), preferred_element_type=F32)


def _dot_tn(a, b):
    return lax.dot_general(a, b, (((0,), (0,)), ((), ())), preferred_element_type=F32)


def _sigmoid(x):
    return 1.0 / (1.0 + jnp.exp(-x))


def _pack_pairs(lo, hi):
    lo_w = lax.bitcast_convert_type(lo.astype(BF16).astype(F32), jnp.uint32)
    hi_w = lax.bitcast_convert_type(hi.astype(BF16).astype(F32), jnp.uint32)
    return lax.shift_right_logical(lo_w, jnp.uint32(16)) | (hi_w & jnp.uint32(0xFFFF0000))


def _unpack_pairs(w):
    lo = lax.bitcast_convert_type(lax.shift_left(w, jnp.uint32(16)), F32)
    hi = lax.bitcast_convert_type(w & jnp.uint32(0xFFFF0000), F32)
    return lo, hi


def _row_group(i, tm):
    r0 = i * tm
    return jnp.where(r0 >= NL, 2, r0 // S)


def _ada_kernel(s_ref, w_ref, b_ref, o_ref):
    s = s_ref[...]
    s = s * _sigmoid(s)
    o_ref[...] = _dot(s.astype(BF16), w_ref[...].astype(BF16)) + b_ref[...]


def ada_modulation(cond, w_ada, b_ada):
    tn = 512
    n = 6 * D
    return pl.pallas_call(
        _ada_kernel,
        grid=(DEPTH, n // tn),
        in_specs=[
            pl.BlockSpec((8, D), lambda l, j: (0, 0)),
            pl.BlockSpec((None, D, tn), lambda l, j: (l, 0, j)),
            pl.BlockSpec((None, 1, tn), lambda l, j: (l, 0, j)),
        ],
        out_specs=pl.BlockSpec((None, 8, tn), lambda l, j: (l, 0, j)),
        out_shape=jax.ShapeDtypeStruct((DEPTH, 8, n), F32),
        compiler_params=_cp(("arbitrary", "arbitrary")),
        name="ada_modulation",
    )(cond, w_ada, b_ada.reshape(DEPTH, 1, n))


def _normmod(x, g, sh, sc):
    y = x * lax.rsqrt(jnp.mean(x * x, axis=-1, keepdims=True) + EPS) * g
    return y * (1.0 + sc) + sh


def _normmod_kernel(x_ref, g_ref, sh_ref, sc_ref, h_ref):
    h_ref[...] = _normmod(x_ref[...], g_ref[...], sh_ref[...], sc_ref[...]).astype(h_ref.dtype)


def _normmod_router_kernel(x_ref, g_ref, sh_ref, sc_ref, wr_ref, h_ref, lg_ref):
    h = _normmod(x_ref[...], g_ref[...], sh_ref[...], sc_ref[...])
    h_ref[...] = _pack_pairs(h[:, :D // 2], h[:, D // 2:])
    lg_ref[...] = _dot(h.astype(BF16), wr_ref[...].astype(BF16))


def norm_modulate(xs, gain, mod, k_shift, k_scale, out_dtype, w_router=None):
    tm = 512
    in_specs = [
        pl.BlockSpec((tm, D), lambda i: (i, 0)),
        pl.BlockSpec((1, D), lambda i: (0, 0)),
        pl.BlockSpec((None, 1, D), lambda i: (_row_group(i, tm), 0, k_shift)),
        pl.BlockSpec((None, 1, D), lambda i: (_row_group(i, tm), 0, k_scale)),
    ]
    if w_router is None:
        return pl.pallas_call(
            _normmod_kernel, grid=(NT // tm,), in_specs=in_specs,
            out_specs=pl.BlockSpec((tm, D), lambda i: (i, 0)),
            out_shape=jax.ShapeDtypeStruct((NT, D), out_dtype),
            compiler_params=_cp(("arbitrary",)), name="norm_modulate",
        )(xs, gain, mod, mod)
    return pl.pallas_call(
        _normmod_router_kernel, grid=(NT // tm,),
        in_specs=in_specs + [pl.BlockSpec((D, 128), lambda i: (0, 0))],
        out_specs=[pl.BlockSpec((tm, D // 2), lambda i: (i, 0)), pl.BlockSpec((tm, 128), lambda i: (i, 0))],
        out_shape=[jax.ShapeDtypeStruct((NT, D // 2), jnp.uint32), jax.ShapeDtypeStruct((NT, 128), F32)],
        compiler_params=_cp(("arbitrary",)), name="norm_modulate_router",
    )(xs, gain, mod, mod, w_router)


def _mm_kernel(a_ref, b_ref, o_ref):
    o_ref[...] = _dot_nt(a_ref[...], b_ref[...]).astype(o_ref.dtype)


def matmul(a, b, layer, tm, tn, out_dtype):
    m, k = a.shape
    n = b.shape[1]
    return pl.pallas_call(
        _mm_kernel,
        grid=(m // tm, n // tn),
        in_specs=[pl.BlockSpec((tm, k), lambda i, j: (i, 0)),
                  pl.BlockSpec((None, tn, k), lambda i, j: (layer, j, 0))],
        out_specs=pl.BlockSpec((tm, tn), lambda i, j: (i, j)),
        out_shape=jax.ShapeDtypeStruct((m, n), out_dtype),
        compiler_params=_cp(("arbitrary", "arbitrary")),
        name="matmul",
    )(a, b)


def _qblk(b, i):
    return jnp.where(i == 0, CTX_BLK0 + b, b * NBLK_L + i - 1)


def _rms_rows(x, g):
    return x * lax.rsqrt(jnp.mean(x * x, axis=-1, keepdims=True) + EPS) * g


NA_KROWS = 12
NA_KW = NA_KROWS * GW


NA_HG = 4
NA_GW = NA_HG * NA_D


def _na_kernel(q_ref, kl_ref, kc_ref, vl_ref, vc_ref, bias_ref, qn_ref, kn_ref, o_ref, ks_ref):
    i = pl.program_id(2)
    heads = [slice(NA_D * h, NA_D * (h + 1)) for h in range(NA_HG)]

    @pl.when(i == 0)
    def _():
        for hs in heads:
            ks_ref[0:L, hs] = _rms_rows(kc_ref[:, hs].astype(F32), kn_ref[...]).astype(BF16)
            ks_ref[L:, hs] = _rms_rows(kl_ref[:, hs].astype(F32), kn_ref[...]).astype(BF16)

    def query(hs):
        return (_rms_rows(q_ref[:, hs].astype(F32), qn_ref[...]) * (NA_D ** -0.5)).astype(BF16)

    @pl.when(i == 0)
    def _():
        for hs in heads:
            s_c = _dot_nt(query(hs), ks_ref[0:L, hs])
            p = jnp.exp(s_c - jnp.max(s_c, axis=-1, keepdims=True))
            l = jnp.sum(p, axis=-1, keepdims=True)
            o_ref[:, hs] = (_dot(p.astype(BF16), vc_ref[:, hs]) / l).astype(o_ref.dtype)

    @pl.when(i > 0)
    def _():
        row0 = jnp.clip(4 * (i - 1) - 4, 0, GW - NA_KROWS)
        k0 = pl.multiple_of(row0 * GW, BLK)
        for h, hs in enumerate(heads):
            q = query(hs)
            s_c = _dot_nt(q, ks_ref[0:L, hs])
            s_l = _dot_nt(q, ks_ref[pl.ds(L + k0, NA_KW), hs]) + bias_ref[h]
            m = jnp.maximum(jnp.max(s_c, axis=-1, keepdims=True), jnp.max(s_l, axis=-1, keepdims=True))
            p_c = jnp.exp(s_c - m)
            p_l = jnp.exp(s_l - m)
            l = jnp.sum(p_c, axis=-1, keepdims=True) + jnp.sum(p_l, axis=-1, keepdims=True)
            o = _dot(p_l.astype(BF16), vl_ref[pl.ds(k0, NA_KW), hs]) + _dot(p_c.astype(BF16), vc_ref[:, hs])
            o_ref[:, hs] = (o / l).astype(o_ref.dtype)


def _na_bias_index():
    dr = np.zeros((3, BLK, NA_KW), np.int32)
    dc = np.zeros((3, BLK, NA_KW), np.int32)
    ok = np.zeros((3, BLK, NA_KW), bool)
    rows = S // GW
    for p, blk in enumerate((0, 5, NBLK_L - 1)):
        ks = int(np.clip(4 * blk - 4, 0, GW - NA_KROWS))
        r = 4 * blk + np.arange(BLK) // GW
        c = np.arange(BLK) % GW
        rk = ks + np.arange(NA_KW) // GW
        ck = np.arange(NA_KW) % GW
        r0 = np.clip(r - NA_R // 2, 0, rows - NA_R)
        ws = np.clip(c - NA_C // 2, 0, GW - NA_C)
        row_ok = (rk[None, :] >= r0[:, None]) & (rk[None, :] < r0[:, None] + NA_R)
        col_ok = (ck[None, :] >= ws[:, None]) & (ck[None, :] < ws[:, None] + NA_C)
        ok[p] = row_ok & col_ok
        dr[p] = np.clip(rk[None, :] - r[:, None] + NA_R - 1, 0, 2 * NA_R - 2)
        dc[p] = np.clip(ck[None, :] - c[:, None], -(NA_C - 1), NA_C - 1) + NA_C - 1
    return dr, dc, ok


def _na_bias_table(rpb):
    dr, dc, ok = _na_bias_index()
    qr, kr = BLK // GW, NA_KROWS
    dr_t = dr.reshape(3, qr, GW, kr, GW)[:, :, 0, :, 0].reshape(3 * qr * kr)
    dc_t = dc[0].reshape(qr, GW, kr, GW)[0, :, 0, :].reshape(GW * GW)
    oh_c = jnp.asarray(np.eye(2 * NA_C - 1, dtype=np.float32)[:, dc_t])
    oh_r = jnp.asarray(np.eye(2 * NA_R - 1, dtype=np.float32)[dr_t])
    hi = lax.Precision.HIGHEST
    t1 = jnp.einsum('hrd,dx->hrx', rpb.astype(F32), oh_c, precision=hi)
    t2 = jnp.einsum('yr,hrx->hyx', oh_r, t1, precision=hi)
    t2 = t2.reshape(NA_H, 3, qr, kr, GW, GW).transpose(0, 1, 2, 4, 3, 5).reshape(NA_H, 3, BLK, NA_KW)
    return jnp.where(ok[None], t2, NEG)


def na_attention(proj, rpb, qn, kn):
    bias = _na_bias_table(rpb)

    def pat(i):
        return jnp.where(i <= 1, 0, jnp.where(i == NBLK_L, 2, 1))

    cq, ck, cv = _COL_NAQ // NA_GW, _COL_NAK // NA_GW, _COL_NAV // NA_GW
    return pl.pallas_call(
        _na_kernel,
        grid=(NB, NA_H // NA_HG, NBLK_L + 1),
        in_specs=[
            pl.BlockSpec((BLK, NA_GW), lambda b, h, i: (_qblk(b, i), cq + h)),
            pl.BlockSpec((S, NA_GW), lambda b, h, i: (b, ck + h)),
            pl.BlockSpec((L, NA_GW), lambda b, h, i: (CTX_BLK0 + b, ck + h)),
            pl.BlockSpec((S, NA_GW), lambda b, h, i: (b, cv + h)),
            pl.BlockSpec((L, NA_GW), lambda b, h, i: (CTX_BLK0 + b, cv + h)),
            pl.BlockSpec((NA_HG, None, BLK, NA_KW), lambda b, h, i: (h, pat(i), 0, 0)),
            pl.BlockSpec((1, NA_D), lambda b, h, i: (0, 0)),
            pl.BlockSpec((1, NA_D), lambda b, h, i: (0, 0)),
        ],
        out_specs=pl.BlockSpec((BLK, NA_GW), lambda b, h, i: (_qblk(b, i), h)),
        out_shape=jax.ShapeDtypeStruct((NT, NA_W), BF16),
        scratch_shapes=[pltpu.VMEM((L + S, NA_GW), BF16)],
        compiler_params=_cp(("arbitrary", "arbitrary", "arbitrary")),
        name="na_attention",
    )(proj, proj, proj, proj, proj, bias, qn.reshape(1, NA_D), kn.reshape(1, NA_D))


MLA_HW = 256


def _half_mask(h, width=128):
    lane = lax.broadcasted_iota(jnp.int32, (1, width), 1)
    return (lane < 64) if h % 2 == 0 else (lane >= 64)


def _mla_q_kernel(cq_ref, w_ref, qa_ref, gn_ref, g2_ref, gs2_ref, c2_ref, s2_ref, q_ref):
    x = _rms_rows(cq_ref[...].astype(F32), qa_ref[...]).astype(BF16)
    y = _dot(x, w_ref[...])
    nw = MLA_H * MLA_NOPE
    gc = g2_ref[...] * c2_ref[...]
    gs = gs2_ref[...] * s2_ref[...]
    for p in range(MLA_H // 2):
        r1 = y[:, nw + 128 * p: nw + 128 * (p + 1)]
        r2 = y[:, nw + 768 + 128 * p: nw + 768 + 128 * (p + 1)]
        rot = r1 * gc + r2 * gs
        sq = r1 * r1
        for h in (2 * p, 2 * p + 1):
            msk = _half_mask(h)
            nope = y[:, 128 * h: 128 * (h + 1)]
            ss = jnp.sum(nope * nope, axis=-1, keepdims=True) + jnp.sum(
                jnp.where(msk, sq, 0.0), axis=-1, keepdims=True)
            inv = lax.rsqrt(ss / MLA_QK + EPS) * (MLA_QK ** -0.5)
            q_ref[h, :, 0:128] = (nope * gn_ref[...] * inv).astype(BF16)
            q_ref[h, :, 128:256] = (jnp.where(msk, rot, 0.0) * inv).astype(BF16)


def _mla_kv_kernel(ckv_ref, kr_ref, w_ref, kva_ref, gn_ref, g2_ref, gs2_ref, c2_ref, s2_ref,
                   k_ref, v_ref):
    x = _rms_rows(ckv_ref[...].astype(F32), kva_ref[...]).astype(BF16)
    y = _dot(x, w_ref[...])
    kr = kr_ref[...].astype(F32)
    r1 = kr[:, 0:128]
    r2 = kr[:, 128:256]
    rot = r1 * (g2_ref[...] * c2_ref[...]) + r2 * (gs2_ref[...] * s2_ref[...])
    ss_r = jnp.sum(jnp.where(_half_mask(0), r1 * r1, 0.0), axis=-1, keepdims=True)
    nw = MLA_H * MLA_NOPE
    lane0 = lax.broadcasted_iota(jnp.int32, (kr.shape[0], 128), 1) == 0
    for h in range(MLA_H):
        nope = y[:, 128 * h: 128 * (h + 1)]
        ss = jnp.sum(nope * nope, axis=-1, keepdims=True) + ss_r
        inv = lax.rsqrt(ss / MLA_QK + EPS)
        k_ref[h, :, 0:128] = (nope * gn_ref[...] * inv).astype(BF16)
        k_ref[h, :, 128:256] = (jnp.where(_half_mask(h), rot, 0.0) * inv).astype(BF16)
        v_ref[h, :, 0:128] = y[:, nw + 128 * h: nw + 128 * (h + 1)].astype(BF16)
        v_ref[h, :, 128:256] = jnp.where(lane0, 1.0, 0.0).astype(BF16)


def _rope_tables():
    t = jnp.arange(S, dtype=jnp.int32)
    pos = (t // GW, t % GW)
    nf = MLA_ROPE // 4
    inv = ROPE_BASE ** (-jnp.arange(nf, dtype=F32) / nf)
    cs, sn = [], []
    for ax in range(2):
        ang = pos[ax].astype(F32)[:, None] * inv[None, :]
        c, s = jnp.cos(ang), jnp.sin(ang)
        cs += [c, c]
        sn += [-s, s]
    c64 = jnp.tile(jnp.concatenate(cs, axis=1), (NB, 1))
    s64 = jnp.tile(jnp.concatenate(sn, axis=1), (NB, 1))
    c64 = jnp.concatenate([c64, jnp.ones((NB * L, MLA_ROPE), F32)], axis=0)
    s64 = jnp.concatenate([s64, jnp.zeros((NB * L, MLA_ROPE), F32)], axis=0)
    return jnp.tile(c64, (1, 2)), jnp.tile(s64, (1, 2))


_ROPE_SWAP = np.concatenate([np.arange(16, 32), np.arange(0, 16), np.arange(48, 64), np.arange(32, 48)])


def _rope_gains(g):
    gr = g[MLA_NOPE:]
    return (g[:MLA_NOPE].reshape(1, 128), jnp.tile(gr, 2).reshape(1, 128),
            jnp.tile(gr[_ROPE_SWAP], 2).reshape(1, 128))


def mla_q_prep(proj, w_uq_r, qa_g, qn_g, c2, s2):
    tm = 512
    gn, g2, gs2 = _rope_gains(qn_g)
    vec = lambda w: pl.BlockSpec((1, w), lambda i: (0, 0))
    return pl.pallas_call(
        _mla_q_kernel,
        grid=(NT // tm,),
        in_specs=[
            pl.BlockSpec((tm, MLA_QL), lambda i: (i, _COL_CQ // MLA_QL)),
            pl.BlockSpec((MLA_QL, 3072), lambda i: (0, 0)),
            vec(MLA_QL), vec(128), vec(128), vec(128),
            pl.BlockSpec((tm, 128), lambda i: (i, 0)),
            pl.BlockSpec((tm, 128), lambda i: (i, 0)),
        ],
        out_specs=pl.BlockSpec((MLA_H, tm, MLA_HW), lambda i: (0, i, 0)),
        out_shape=jax.ShapeDtypeStruct((MLA_H, NT, MLA_HW), BF16),
        compiler_params=_cp(("arbitrary",)),
        name="mla_q_prep",
    )(proj, w_uq_r, qa_g.reshape(1, MLA_QL), gn, g2, gs2, c2, s2)


def mla_kv_prep(proj, w_ukv_r, kva_g, kn_g, c2, s2):
    tm = 512
    gn, g2, gs2 = _rope_gains(kn_g)
    vec = lambda w: pl.BlockSpec((1, w), lambda i: (0, 0))
    return pl.pallas_call(
        _mla_kv_kernel,
        grid=(NT // tm,),
        in_specs=[
            pl.BlockSpec((tm, MLA_KVL), lambda i: (i, _COL_CKV // MLA_KVL)),
            pl.BlockSpec((tm, 256), lambda i: (i, _COL_KR // 256)),
            pl.BlockSpec((MLA_KVL, 3072), lambda i: (0, 0)),
            vec(MLA_KVL), vec(128), vec(128), vec(128),
            pl.BlockSpec((tm, 128), lambda i: (i, 0)),
            pl.BlockSpec((tm, 128), lambda i: (i, 0)),
        ],
        out_specs=[pl.BlockSpec((MLA_H, tm, MLA_HW), lambda i: (0, i, 0)),
                   pl.BlockSpec((MLA_H, tm, MLA_HW), lambda i: (0, i, 0))],
        out_shape=[jax.ShapeDtypeStruct((MLA_H, NT, MLA_HW), BF16),
                   jax.ShapeDtypeStruct((MLA_H, NT, MLA_HW), BF16)],
        compiler_params=_cp(("arbitrary",)),
        name="mla_kv_prep",
    )(proj, proj, w_ukv_r, kva_g.reshape(1, MLA_KVL), gn, g2, gs2, c2, s2)


MLA_TQ = 256


def _mla_attn_kernel(q_ref, kl_ref, kc_ref, vl_ref, vc_ref, o_ref, s0_ref, s1_ref, p0_ref, p1_ref):
    g = pl.program_id(0)

    @pl.when(g == 0)
    def _():
        s0_ref[...] = jnp.zeros_like(s0_ref)
        s1_ref[...] = jnp.zeros_like(s1_ref)
        p0_ref[...] = jnp.ones_like(p0_ref)
        p1_ref[...] = jnp.ones_like(p1_ref)

    n_kc = (L + S) // L
    rows = MLA_TQ // (n_kc - 1)

    def tie(x, dep):
        if dep is None:
            return x
        return jnp.concatenate([x[0:16, :] + dep, x[16:, :]], axis=0)

    def stages(s_new, s_old, p_new, p_old):
        acc, dep = None, None
        for c in range(n_kc):
            keys = slice(L * c, L * (c + 1))
            k = kc_ref[...] if c == 0 else kl_ref[L * (c - 1): L * c, :]
            v = vc_ref[...] if c == 0 else vl_ref[L * (c - 1): L * c, :]
            if c < n_kc - 1:
                r = slice(rows * c, rows * (c + 1))
                s = s_old[r, :]
                p = jnp.exp((s - jnp.max(s, axis=-1, keepdims=True)).astype(BF16))
                p_old[r, :] = p
                dep = (p[0:16, 0:MLA_HW].astype(F32) * 0.0).astype(BF16)
            part = _dot(tie(p_new[:, keys], dep), v)
            acc = part if acc is None else acc + part
            s_new[:, keys] = _dot_nt(tie(q_ref[...], dep), k)
        o_ref[...] = (acc[:, :MLA_V] / acc[:, MLA_V:MLA_V + 1]).astype(o_ref.dtype)

    @pl.when(g % 2 == 0)
    def _():
        stages(s0_ref, s1_ref, p0_ref, p1_ref)

    @pl.when(g % 2 == 1)
    def _():
        stages(s1_ref, s0_ref, p1_ref, p0_ref)


def _mla_ctx_kernel(q_ref, kc_ref, vc_ref, o_ref):
    s = _dot_nt(q_ref[...], kc_ref[...])
    p = jnp.exp((s - jnp.max(s, axis=-1, keepdims=True)).astype(BF16))
    acc = _dot(p, vc_ref[...])
    o_ref[...] = (acc[:, :MLA_V] / acc[:, MLA_V:MLA_V + 1]).astype(o_ref.dtype)


def mla_attention(q, k, v):
    nq = S // MLA_TQ
    n_steps = NB * MLA_H * nq

    def bhi(g):
        return g // (MLA_H * nq), (g // nq) % MLA_H, g % nq

    def cur(g):
        return bhi(jnp.minimum(g, n_steps - 1))

    def prev(g):
        return bhi(jnp.maximum(g - 2, 0))

    ob = pl.pallas_call(
        _mla_attn_kernel,
        grid=(n_steps + 2,),
        in_specs=[
            pl.BlockSpec((None, MLA_TQ, MLA_HW), lambda g: (cur(g)[1], cur(g)[0] * nq + cur(g)[2], 0)),
            pl.BlockSpec((None, S, MLA_HW), lambda g: (cur(g)[1], cur(g)[0], 0)),
            pl.BlockSpec((None, L, MLA_HW), lambda g: (cur(g)[1], CTX_BLK0 + cur(g)[0], 0)),
            pl.BlockSpec((None, S, MLA_HW), lambda g: (prev(g)[1], prev(g)[0], 0)),
            pl.BlockSpec((None, L, MLA_HW), lambda g: (prev(g)[1], CTX_BLK0 + prev(g)[0], 0)),
        ],
        out_specs=pl.BlockSpec((MLA_TQ, MLA_V), lambda g: (prev(g)[0] * nq + prev(g)[2], prev(g)[1])),
        out_shape=jax.ShapeDtypeStruct((NL, MLA_W), BF16),
        scratch_shapes=[pltpu.VMEM((MLA_TQ, L + S), F32), pltpu.VMEM((MLA_TQ, L + S), F32),
                        pltpu.VMEM((MLA_TQ, L + S), BF16), pltpu.VMEM((MLA_TQ, L + S), BF16)],
        compiler_params=_cp(("arbitrary",)),
        name="mla_attention",
    )(q, k, k, v, v)
    ob_ctx = pl.pallas_call(
        _mla_ctx_kernel,
        grid=(NB, MLA_H),
        in_specs=[
            pl.BlockSpec((None, L, MLA_HW), lambda b, h: (h, CTX_BLK0 + b, 0)),
            pl.BlockSpec((None, L, MLA_HW), lambda b, h: (h, CTX_BLK0 + b, 0)),
            pl.BlockSpec((None, L, MLA_HW), lambda b, h: (h, CTX_BLK0 + b, 0)),
        ],
        out_specs=pl.BlockSpec((L, MLA_V), lambda b, h: (b, h)),
        out_shape=jax.ShapeDtypeStruct((NB * L, MLA_W), BF16),
        compiler_params=_cp(("arbitrary", "arbitrary")),
        name="mla_attention_ctx",
    )(q, k, v)
    return ob, ob_ctx


N_CH_C = L // CH
N_CH_L = S // CH
N_CH = N_CH_C + N_CH_L


def _gla_kernel(qf_ref, kf_ref, vf_ref, af_ref, qb_ref, kb_ref, vb_ref, ab_ref,
                wg_ref, bg_ref, of_ref, ob_ref, st_ref):
    t = pl.program_id(1)

    @pl.when(t == 0)
    def _():
        st_ref[...] = jnp.zeros_like(st_ref)

    row = lax.broadcasted_iota(jnp.int32, (CH, CH), 0)
    col = lax.broadcasted_iota(jnp.int32, (CH, CH), 1)
    dirs = (
        (qf_ref, kf_ref, vf_ref, af_ref, of_ref, col <= row, CH - 1),
        (qb_ref, kb_ref, vb_ref, ab_ref, ob_ref, col >= row, 0),
    )
    for d, (q_ref, k_ref, v_ref, a_ref, o_ref, keep, last) in enumerate(dirs):
        z = _dot(a_ref[...], wg_ref[d]) + bg_ref[d]
        g = (jnp.minimum(z, 0.0) - jnp.log(1.0 + jnp.exp(-jnp.abs(z)))) / GLA_TAU
        b = jnp.dot(keep.astype(F32), g, preferred_element_type=F32,
                    precision=lax.Precision.HIGHEST)
        b_end = b[last:last + 1, :]
        b_mid = b[CH // 2:CH // 2 + 1, :]
        q = q_ref[...].astype(F32) * (GLA_DK ** -0.5)
        k = k_ref[...].astype(F32)
        qa = (q * jnp.exp(b - b_mid)).astype(BF16)
        ka = (k * jnp.exp(b_mid - b)).astype(BF16)
        qd = (q * jnp.exp(b)).astype(BF16)
        ke = (k * jnp.exp(b_end - b)).astype(BF16)
        e_end = jnp.exp(b_end)
        v = v_ref[...]
        for h in range(GLA_H):
            ksl = slice(GLA_DK * h, GLA_DK * (h + 1))
            vsl = slice(GLA_DV * h, GLA_DV * (h + 1))
            att = jnp.where(keep, _dot_nt(qa[:, ksl], ka[:, ksl]), 0.0)
            st = st_ref[d, h]
            o = _dot_nt(qd[:, ksl], st.astype(BF16)) + _dot(att.astype(BF16), v[:, vsl])
            o_ref[:, vsl] = o
            st_ref[d, h] = st * e_end[:, ksl] + _dot_tn(v[:, vsl], ke[:, ksl])


def gla_scan(proj, wg_f, bg_f, wg_b, bg_b):
    wg = jnp.zeros((2, 128, GLA_KW), F32)
    wg = wg.at[0, 0:GLA_RANK].set(wg_f).at[1, GLA_RANK:2 * GLA_RANK].set(wg_b).astype(BF16)
    bg = jnp.stack([bg_f, bg_b]).reshape(2, 1, GLA_KW)

    ctx0 = NL // CH

    def fwd(b, t):
        return jnp.where(t < N_CH_C, ctx0 + N_CH_C * b + t, N_CH_L * b + t - N_CH_C)

    def bwd(b, t):
        return jnp.where(t < N_CH_C, ctx0 + N_CH_C * b + N_CH_C - 1 - t, N_CH_L * b + N_CH - 1 - t)

    def specs(rowfn):
        return [
            pl.BlockSpec((CH, GLA_KW), lambda b, t: (rowfn(b, t), _COL_GQ // GLA_KW)),
            pl.BlockSpec((CH, GLA_KW), lambda b, t: (rowfn(b, t), _COL_GK // GLA_KW)),
            pl.BlockSpec((CH, GLA_W), lambda b, t: (rowfn(b, t), _COL_GV // GLA_W)),
            pl.BlockSpec((CH, 128), lambda b, t: (rowfn(b, t), _COL_GA // 128)),
        ]

    return pl.pallas_call(
        _gla_kernel,
        grid=(NB, N_CH),
        in_specs=specs(fwd) + specs(bwd) + [
            pl.BlockSpec((2, 128, GLA_KW), lambda b, t: (0, 0, 0)),
            pl.BlockSpec((2, 1, GLA_KW), lambda b, t: (0, 0, 0)),
        ],
        out_specs=[pl.BlockSpec((CH, GLA_W), lambda b, t: (fwd(b, t), 0)),
                   pl.BlockSpec((CH, GLA_W), lambda b, t: (bwd(b, t), 0))],
        out_shape=[jax.ShapeDtypeStruct((NT, GLA_W), F32)] * 2,
        scratch_shapes=[pltpu.VMEM((2, GLA_H, GLA_DV, GLA_DK), F32)],
        compiler_params=_cp(("arbitrary", "arbitrary")),
        name="gla_scan",
    )(proj, proj, proj, proj, proj, proj, proj, proj, wg, bg)


def _gla_out_kernel(of_ref, ob_ref, g_ref, gn_ref, o_ref):
    for h in range(GLA_H):
        sl = slice(GLA_DV * h, GLA_DV * (h + 1))
        o = _rms_rows(of_ref[:, sl] + ob_ref[:, sl], gn_ref[...])
        gate = g_ref[:, sl].astype(F32)
        o_ref[:, sl] = (o * (gate * _sigmoid(gate))).astype(o_ref.dtype)


def gla_output(o_f, o_b, proj, on_g):
    tm = 512
    return pl.pallas_call(
        _gla_out_kernel,
        grid=(NT // tm,),
        in_specs=[
            pl.BlockSpec((tm, GLA_W), lambda i: (i, 0)),
            pl.BlockSpec((tm, GLA_W), lambda i: (i, 0)),
            pl.BlockSpec((tm, GLA_W), lambda i: (i, _COL_GG // GLA_W)),
            pl.BlockSpec((1, GLA_DV), lambda i: (0, 0)),
        ],
        out_specs=pl.BlockSpec((tm, GLA_W), lambda i: (i, 0)),
        out_shape=jax.ShapeDtypeStruct((NT, GLA_W), BF16),
        compiler_params=_cp(("arbitrary",)),
        name="gla_output",
    )(o_f, o_b, proj, on_g.reshape(1, GLA_DV))


OUT_TM = NB * L


def _out_proj_kernel(oa_ref, obl_ref, obc_ref, oc_ref, w_ref, x_ref, g_ref, o_ref):
    ob = jnp.where(pl.program_id(1) < NL // OUT_TM, obl_ref[...], obc_ref[...])
    a = jnp.concatenate([oa_ref[...], ob, oc_ref[...]], axis=1)
    o_ref[...] = x_ref[...] + g_ref[...] * _dot(a, w_ref[...])


def out_projection(oa, ob, ob_ctx, oc, w_out, layer, xs, mod, k_gate):
    tm, tn = OUT_TM, 1024
    nj = D // tn
    return pl.pallas_call(
        _out_proj_kernel,
        grid=(nj, NT // tm),
        in_specs=[
            pl.BlockSpec((tm, NA_W), lambda j, i: (i, 0)),
            pl.BlockSpec((tm, MLA_W), lambda j, i: (jnp.minimum(i, NL // tm - 1), 0)),
            pl.BlockSpec((tm, MLA_W), lambda j, i: (0, 0)),
            pl.BlockSpec((tm, GLA_W), lambda j, i: (i, 0)),
            pl.BlockSpec((None, D, tn), lambda j, i: (layer, 0, j)),
            pl.BlockSpec((tm, tn), lambda j, i: (i, j)),
            pl.BlockSpec((None, 1, tn), lambda j, i: (_row_group(i, tm), 0, k_gate * nj + j)),
        ],
        out_specs=pl.BlockSpec((tm, tn), lambda j, i: (i, j)),
        out_shape=jax.ShapeDtypeStruct((NT, D), F32),
        compiler_params=_cp(("arbitrary", "arbitrary")),
        name="out_projection",
    )(oa, ob, ob_ctx, oc, w_out, xs, mod)


def _route(logits, router_bias):
    scores = jax.nn.sigmoid(logits)
    grouped = (scores + router_bias.astype(F32)).reshape(NT, E_GROUPS, E_PER)

    def top2(a):
        idx = lax.broadcasted_iota(jnp.int32, a.shape, a.ndim - 1)
        i1 = jnp.argmax(a, axis=-1).astype(jnp.int32)
        rest = jnp.where(idx == i1[..., None], -jnp.inf, a)
        i2 = jnp.argmax(rest, axis=-1).astype(jnp.int32)
        return jnp.max(a, axis=-1), jnp.max(rest, axis=-1), i1, i2

    m1, m2, _, _ = top2(grouped)
    grp = jnp.argmax(m1 + m2, axis=-1).astype(jnp.int32)
    gsel = lax.broadcasted_iota(jnp.int32, (NT, E_GROUPS, E_PER), 1) == grp[:, None, None]
    in_group = jnp.sum(jnp.where(gsel, grouped, 0.0), axis=1)
    _, _, l1, l2 = top2(in_group)
    expert_idx = grp[:, None] * E_PER + jnp.stack([l1, l2], axis=-1)
    esel = lax.broadcasted_iota(jnp.int32, (NT, TOPK, E), 2) == expert_idx[:, :, None]
    w = jnp.sum(jnp.where(esel, scores[:, None, :], 0.0), axis=-1)
    return expert_idx, w / jnp.sum(w, axis=-1, keepdims=True)


def _dispatch_plan(expert_idx):
    flat_e = expert_idx.reshape(NK)
    onehot = (flat_e[:, None] == jnp.arange(E, dtype=jnp.int32)[None, :]).astype(jnp.int32)
    csum = jnp.cumsum(onehot, axis=0)
    counts = csum[-1]
    rank = jnp.take_along_axis(csum, flat_e[:, None], axis=1)[:, 0] - 1
    padded = (counts + MOE_R - 1) // MOE_R * MOE_R
    pad_end = jnp.cumsum(padded)
    pad_start = pad_end - padded
    dest = (pad_start[flat_e] + rank).astype(jnp.int32)
    row_token = jnp.zeros((MOE_P,), jnp.int32).at[dest].set(jnp.arange(NK, dtype=jnp.int32) // TOPK)
    blk0 = jnp.arange(MOE_NB, dtype=jnp.int32) * MOE_R
    block_expert = jnp.minimum(jnp.sum((pad_end[None, :] <= blk0[:, None]).astype(jnp.int32), axis=1), E - 1)
    n_active = (pad_end[-1] // MOE_R).astype(jnp.int32).reshape(1)
    n_valid = jnp.clip(counts[block_expert] - (blk0 - pad_start[block_expert]), 0, MOE_R).astype(jnp.int32)
    bidx = jnp.arange(MOE_NB, dtype=jnp.int32)
    prev_e = jnp.concatenate([jnp.full((1,), -1, jnp.int32), block_expert[:-1]])
    first = jnp.logical_and(bidx < n_active[0], block_expert != prev_e)
    later = jnp.where(first, bidx, MOE_NB)
    nxt_blk = jnp.concatenate([lax.cummin(later[::-1])[::-1][1:], jnp.full((1,), MOE_NB, jnp.int32)])
    nxt_e = jnp.where(nxt_blk < MOE_NB, block_expert[jnp.minimum(nxt_blk, MOE_NB - 1)], -1).astype(jnp.int32)
    plan = (block_expert, n_active, first.astype(jnp.int32), nxt_e)
    return dest, row_token, n_valid, plan


DH = D // 2


DMA_UNROLL = 8


def _for_rows(n, body):
    full = n // DMA_UNROLL

    def group(t, c):
        for u in range(DMA_UNROLL):
            body(t * DMA_UNROLL + u)
        return c

    def single(r, c):
        body(r)
        return c

    lax.fori_loop(0, full, group, 0)
    lax.fori_loop(full * DMA_UNROLL, n, single, 0)


def _gather_rows_kernel(tok_ref, nv_ref, h_ref, o_ref, sem):
    i = pl.program_id(0)
    base = i * MOE_R
    nv = nv_ref[i]

    @pl.when(nv < MOE_R)
    def _():
        o_ref[...] = jnp.zeros_like(o_ref)

    def copy(r):
        return pltpu.make_async_copy(h_ref.at[pl.ds(tok_ref[base + r], 1)], o_ref.at[pl.ds(r, 1)], sem)

    _for_rows(nv, lambda r: copy(r).start())
    _for_rows(nv, lambda r: copy(r).wait())


def gather_rows(row_token, n_valid, h):
    return pl.pallas_call(
        _gather_rows_kernel,
        grid_spec=pltpu.PrefetchScalarGridSpec(
            num_scalar_prefetch=2,
            grid=(MOE_NB,),
            in_specs=[pl.BlockSpec(memory_space=pl.ANY)],
            out_specs=pl.BlockSpec((MOE_R, DH), lambda i, tok, nv: (i, 0)),
            scratch_shapes=[pltpu.SemaphoreType.DMA(())],
        ),
        out_shape=jax.ShapeDtypeStruct((MOE_P, DH), jnp.uint32),
        compiler_params=_cp(("arbitrary",)),
        name="moe_gather_rows",
    )(row_token, n_valid, h)


def _stage_weights(layer, n_tiles, width, be_ref, first_ref, nxt_ref, active, mats):
    t = pl.program_id(0)
    i = pl.program_id(1)

    def copies(e, tt):
        cols = pl.ds(pl.multiple_of(tt * width, width), width)
        return [pltpu.make_async_copy(w.at[layer, e, :, cols], st, sm) for w, st, _, sm in mats]

    @pl.when(jnp.logical_and(active, first_ref[i] == 1))
    def _():
        @pl.when(jnp.logical_and(t == 0, i == 0))
        def _():
            for c in copies(be_ref[0], 0):
                c.start()

        for c in copies(be_ref[i], t):
            c.wait()
        for _, st, wb, _ in mats:
            wb[...] = st[...].astype(BF16)

        @pl.when(nxt_ref[i] >= 0)
        def _():
            for c in copies(nxt_ref[i], t):
                c.start()

        @pl.when(jnp.logical_and(nxt_ref[i] < 0, t + 1 < n_tiles))
        def _():
            for c in copies(be_ref[0], t + 1):
                c.start()


def _ffn_up_kernel(layer, be_ref, na_ref, first_ref, nxt_ref, x_ref, w1_ref, w3_ref, o_ref,
                   st1_ref, st3_ref, w1b_ref, w3b_ref, sem):
    i = pl.program_id(1)
    active = i < na_ref[0]
    _stage_weights(layer, FF // MOE_TF, MOE_TF, be_ref, first_ref, nxt_ref, active,
                   [(w1_ref, st1_ref, w1b_ref, sem.at[0]), (w3_ref, st3_ref, w3b_ref, sem.at[1])])

    @pl.when(active)
    def _():
        lo, hi = _unpack_pairs(x_ref[...])
        lo, hi = lo.astype(BF16), hi.astype(BF16)
        a = _dot(lo, w1b_ref[0:DH, :]) + _dot(hi, w1b_ref[DH:, :])
        b = _dot(lo, w3b_ref[0:DH, :]) + _dot(hi, w3b_ref[DH:, :])
        o_ref[...] = (a * _sigmoid(a) * b).astype(o_ref.dtype)

    @pl.when(jnp.logical_not(active))
    def _():
        o_ref[...] = jnp.zeros_like(o_ref)


def ffn_up(plan, xb, w1, w3, layer):
    last = lambda i, na: jnp.minimum(i, na[0] - 1)
    return pl.pallas_call(
        functools.partial(_ffn_up_kernel, layer),
        grid_spec=pltpu.PrefetchScalarGridSpec(
            num_scalar_prefetch=4,
            grid=(FF // MOE_TF, MOE_NB),
            in_specs=[
                pl.BlockSpec((MOE_R, DH), lambda j, i, be, na, fi, nx: (last(i, na), 0)),
                pl.BlockSpec(memory_space=pl.ANY),
                pl.BlockSpec(memory_space=pl.ANY),
            ],
            out_specs=pl.BlockSpec((MOE_R, MOE_TF), lambda j, i, be, na, fi, nx: (i, j)),
            scratch_shapes=[pltpu.VMEM((D, MOE_TF), F32), pltpu.VMEM((D, MOE_TF), F32),
                            pltpu.VMEM((D, MOE_TF), BF16), pltpu.VMEM((D, MOE_TF), BF16),
                            pltpu.SemaphoreType.DMA((2,))],
        ),
        out_shape=jax.ShapeDtypeStruct((MOE_P, FF), BF16),
        compiler_params=_cp(("arbitrary", "arbitrary")),
        name="moe_ffn_up",
    )(*plan, xb, w1, w3)


MOE_TN = 2048


def _ffn_down_kernel(layer, be_ref, na_ref, first_ref, nxt_ref, h_ref, w2_ref, o_ref, st2_ref, w2b_ref, sem):
    i = pl.program_id(1)
    active = i < na_ref[0]
    _stage_weights(layer, D // MOE_TN, MOE_TN, be_ref, first_ref, nxt_ref, active,
                   [(w2_ref, st2_ref, w2b_ref, sem.at[0])])

    @pl.when(active)
    def _():
        y = _dot(h_ref[...], w2b_ref[...])
        o_ref[...] = _pack_pairs(y[:, :MOE_TN // 2], y[:, MOE_TN // 2:])

    @pl.when(jnp.logical_not(active))
    def _():
        o_ref[...] = jnp.zeros_like(o_ref)


def ffn_down(plan, hmid, w2, layer):
    last = lambda i, na: jnp.minimum(i, na[0] - 1)
    return pl.pallas_call(
        functools.partial(_ffn_down_kernel, layer),
        grid_spec=pltpu.PrefetchScalarGridSpec(
            num_scalar_prefetch=4,
            grid=(D // MOE_TN, MOE_NB),
            in_specs=[
                pl.BlockSpec((MOE_R, FF), lambda n, i, be, na, fi, nx: (last(i, na), 0)),
                pl.BlockSpec(memory_space=pl.ANY),
            ],
            out_specs=pl.BlockSpec((MOE_R, MOE_TN // 2), lambda n, i, be, na, fi, nx: (i, n)),
            scratch_shapes=[pltpu.VMEM((FF, MOE_TN), F32), pltpu.VMEM((FF, MOE_TN), BF16),
                            pltpu.SemaphoreType.DMA((1,))],
        ),
        out_shape=jax.ShapeDtypeStruct((MOE_P, DH), jnp.uint32),
        compiler_params=_cp(("arbitrary", "arbitrary")),
        name="moe_ffn_down",
    )(*plan, hmid, w2)


CMB_T = 256


def _combine_kernel(pos_ref, yb_ref, gate_ref, x_ref, g_ref, o_ref, buf_ref, sem):
    base = pl.program_id(0) * CMB_T

    def copy(r, k):
        p = pos_ref[(base + r) * TOPK + k]
        return pltpu.make_async_copy(yb_ref.at[pl.ds(p, 1)], buf_ref.at[k, pl.ds(r, 1)], sem.at[k])

    def start(r, c):
        copy(r, 0).start()
        copy(r, 1).start()
        return c

    def wait(r, c):
        copy(r, 0).wait()
        copy(r, 1).wait()
        return c

    lax.fori_loop(0, CMB_T, start, 0, unroll=DMA_UNROLL)
    lax.fori_loop(0, CMB_T, wait, 0, unroll=DMA_UNROLL)
    gate = gate_ref[...]
    g0, g1 = gate[:, 0:1], gate[:, 1:2]
    half = MOE_TN // 2
    for n in range(D // MOE_TN):
        lo0, hi0 = _unpack_pairs(buf_ref[0, :, half * n: half * (n + 1)])
        lo1, hi1 = _unpack_pairs(buf_ref[1, :, half * n: half * (n + 1)])
        for part, y in enumerate((lo0 * g0 + lo1 * g1, hi0 * g0 + hi1 * g1)):
            sl = slice(MOE_TN * n + half * part, MOE_TN * n + half * (part + 1))
            o_ref[:, sl] = x_ref[:, sl] + g_ref[:, sl] * y


def moe_combine(dest, yb, gates, xs, mod, k_gate, out_rows):
    return pl.pallas_call(
        _combine_kernel,
        grid_spec=pltpu.PrefetchScalarGridSpec(
            num_scalar_prefetch=1,
            grid=(out_rows // CMB_T,),
            in_specs=[
                pl.BlockSpec(memory_space=pl.ANY),
                pl.BlockSpec((CMB_T, TOPK), lambda i, pos: (i, 0)),
                pl.BlockSpec((CMB_T, D), lambda i, pos: (i, 0)),
                pl.BlockSpec((None, 1, D), lambda i, pos: (_row_group(i, CMB_T), 0, k_gate)),
            ],
            out_specs=pl.BlockSpec((CMB_T, D), lambda i, pos: (i, 0)),
            scratch_shapes=[pltpu.VMEM((TOPK, CMB_T, DH), jnp.uint32), pltpu.SemaphoreType.DMA((TOPK,))],
        ),
        out_shape=jax.ShapeDtypeStruct((out_rows, D), F32),
        compiler_params=_cp(("arbitrary",)),
        name="moe_combine",
    )(dest, yb, gates, xs, mod)


IN_W = 9312


W_RB = 512
W_NB = PW // W_RB
_W_SRC_KR = 4608
_W_SRC_GA = 9280


def _w_in_layout_kernel(w_ref, o_ref, buf_ref, sem):
    g = pl.program_id(0)

    def run(step, slot, op):
        l, k = step // W_NB, step % W_NB

        @pl.when(k < W_NB - 1)
        def _():
            r0 = pl.multiple_of(jnp.where(k < _W_SRC_KR // W_RB, k * W_RB, k * W_RB + MLA_ROPE), MLA_ROPE)
            op(pltpu.make_async_copy(w_ref.at[l, pl.ds(r0, W_RB)], buf_ref.at[slot], sem.at[slot]))

        @pl.when(k == W_NB - 1)
        def _():
            op(pltpu.make_async_copy(w_ref.at[l, pl.ds(_W_SRC_KR, MLA_ROPE)],
                                     buf_ref.at[slot, pl.ds(0, MLA_ROPE)], sem.at[slot]))
            op(pltpu.make_async_copy(w_ref.at[l, pl.ds(_W_SRC_GA, 2 * GLA_RANK)],
                                     buf_ref.at[slot, pl.ds(MLA_ROPE, 2 * GLA_RANK)], sem.at[slot]))

    start = lambda c: c.start()
    wait = lambda c: c.wait()

    @pl.when(g == 0)
    def _():
        run(0, 0, start)

    @pl.when(g + 1 < DEPTH * W_NB)
    def _():
        run(g + 1, (g + 1) % 2, start)

    run(g, g % 2, wait)
    k = g % W_NB

    @pl.when(k < W_NB - 1)
    def _():
        o_ref[...] = buf_ref[g % 2].astype(BF16)

    @pl.when(k == W_NB - 1)
    def _():
        x = buf_ref[g % 2, 0:MLA_ROPE, :]
        xs = jnp.concatenate([x[16:32], x[0:16], x[48:64], x[32:48]], axis=0)
        ga = buf_ref[g % 2, MLA_ROPE:MLA_ROPE + 2 * GLA_RANK, :]
        o_ref[...] = jnp.concatenate(
            [x, x, xs, xs, ga, jnp.zeros((W_RB - 4 * MLA_ROPE - 2 * GLA_RANK, D), F32)], axis=0).astype(BF16)


def _w_in_layout(w):
    w_t = jnp.swapaxes(w, 1, 2)
    return pl.pallas_call(
        _w_in_layout_kernel,
        grid=(DEPTH * W_NB,),
        in_specs=[pl.BlockSpec(memory_space=pl.ANY)],
        out_specs=pl.BlockSpec((None, W_RB, D), lambda g: (g // W_NB, g % W_NB, 0)),
        out_shape=jax.ShapeDtypeStruct((DEPTH, PW, D), BF16),
        scratch_shapes=[pltpu.VMEM((2, W_RB, D), F32), pltpu.SemaphoreType.DMA((2,))],
        compiler_params=_cp(("arbitrary",)),
        name="w_in_layout",
    )(w_t)


def _w_uq_cols():
    nope = [MLA_QK * h + j for h in range(MLA_H) for j in range(MLA_NOPE)]
    rope = [MLA_QK * h + MLA_NOPE + d for h in range(MLA_H) for d in range(MLA_ROPE)]
    rope_s = [MLA_QK * h + MLA_NOPE + int(d) for h in range(MLA_H) for d in _ROPE_SWAP]
    return np.array(nope + rope + rope_s, np.int32)


def _w_ukv_cols():
    kn = [(MLA_NOPE + MLA_V) * h + j for h in range(MLA_H) for j in range(MLA_NOPE)]
    vv = [(MLA_NOPE + MLA_V) * h + MLA_NOPE + j for h in range(MLA_H) for j in range(MLA_V)]
    return np.array(kn + vv, np.int32)


def token_mixing_layer(xs, mod, layer, norm1, w_in_b, w_out_b, na_qn, na_kn, na_rpb, mla_qa, mla_kva, w_uq, w_ukv,
                       mla_qn, mla_kn, gwf, gbf, gwb, gbb, gla_on, c2, s2):
    h = norm_modulate(xs, norm1.reshape(1, D), mod, 0, 1, BF16)
    proj = matmul(h, w_in_b, layer, NT // 8, 512, BF16)
    oa = na_attention(proj, na_rpb, na_qn, na_kn)
    q = mla_q_prep(proj, w_uq[:, _w_uq_cols()].astype(BF16), mla_qa, mla_qn, c2, s2)
    k, v = mla_kv_prep(proj, w_ukv[:, _w_ukv_cols()].astype(BF16), mla_kva, mla_kn, c2, s2)
    ob, ob_ctx = mla_attention(q, k, v)
    o_f, o_b = gla_scan(proj, gwf, gbf, gwb, gbb)
    oc = gla_output(o_f, o_b, proj, gla_on)
    return out_projection(oa, ob, ob_ctx, oc, w_out_b, layer, xs, mod, 2)


def moe_layer(xs, mod, norm2, w_router_p, router_bias, w1, w3, w2, layer, out_rows):
    h, logits = norm_modulate(xs, norm2.reshape(1, D), mod, 3, 4, None, w_router_p)
    expert_idx, gates = _route(logits[:, :E], router_bias)
    dest, row_token, n_valid, plan = _dispatch_plan(expert_idx)
    xb = gather_rows(row_token, n_valid, h)
    hmid = ffn_up(plan, xb, w1, w3, layer)
    yb = ffn_down(plan, hmid, w2, layer)
    return moe_combine(dest, yb, gates, xs, mod, 5, out_rows)


def kernel(x, c, ctx, c_ctx, w_ada, b_ada, norm1, norm2, w_in, w_out, na_q_norm, na_k_norm, na_rpb, mla_qa_norm, mla_kva_norm, mla_w_uq, mla_w_ukv, mla_q_norm, mla_k_norm, gla_w_gate_f, gla_b_gate_f, gla_w_gate_b, gla_b_gate_b, gla_out_norm, w_router, router_bias, moe_w1, moe_w3, moe_w2):
    cond = jnp.concatenate([c, c_ctx[None, :], jnp.zeros((8 - NB - 1, D), F32)], axis=0)
    mods = ada_modulation(cond, w_ada, b_ada)
    xs = jnp.concatenate([x.reshape(NL, D), ctx.reshape(NB * L, D)], axis=0)
    c2, s2 = _rope_tables()
    w_router_p = jnp.concatenate([w_router, jnp.zeros((D, 128 - E), F32)], axis=1)
    w_in_b = _w_in_layout(w_in)
    w_out_b = w_out.astype(BF16)
    for l in range(DEPTH):
        mod = mods[l].reshape(8, 1, 6 * D)
        xs = token_mixing_layer(xs, mod, l, norm1[l], w_in_b, w_out_b, na_q_norm[l], na_k_norm[l], na_rpb[l],
                                mla_qa_norm[l], mla_kva_norm[l], mla_w_uq[l], mla_w_ukv[l],
                                mla_q_norm[l], mla_k_norm[l], gla_w_gate_f[l], gla_b_gate_f[l],
                                gla_w_gate_b[l], gla_b_gate_b[l], gla_out_norm[l], c2, s2)
        xs = moe_layer(xs, mod, norm2[l], w_router_p, router_bias, moe_w1, moe_w3, moe_w2, l,
                       NT if l < DEPTH - 1 else NL)
    return xs.reshape(NB, S, D)
```

```python
import functools

import numpy as np
import jax
import jax.numpy as jnp
from jax import lax
from jax.experimental import pallas as pl
from jax.experimental.pallas import tpu as pltpu

F32 = jnp.float32
BF16 = jnp.bfloat16

D = 4096
NB = 2
S = 4096
L = 256
DEPTH = 2
GW = 64
EPS = 1e-6
NL = NB * S
NT = NL + NB * L
NA_H, NA_D = 8, 128
NA_W = NA_H * NA_D
NA_R, NA_C = 8, 16
MLA_H = 12
MLA_QL, MLA_KVL = 1024, 512
MLA_NOPE, MLA_ROPE, MLA_V = 128, 64, 128
MLA_QK = MLA_NOPE + MLA_ROPE
MLA_W = MLA_H * MLA_V
GLA_H, GLA_DK, GLA_DV = 6, 128, 256
GLA_KW = GLA_H * GLA_DK
GLA_W = GLA_H * GLA_DV
GLA_RANK = 16
GLA_TAU = 16.0
CH = 128
E = 16
E_GROUPS = 4
E_PER = E // E_GROUPS
TOPK = 2
FF = 1024
ROPE_BASE = 10000.0

_COL_NAQ, _COL_NAK, _COL_NAV = 0, 1024, 2048
_COL_CQ, _COL_CKV = 3072, 4096
_COL_GQ, _COL_GK, _COL_GV, _COL_GG = 4608, 5376, 6144, 7680
_COL_KR = 9216
_COL_GA = 9472
PW = 9728

BLK = 256
NBLK_L = S // BLK
CTX_BLK0 = NL // BLK
MOE_R = 256
MOE_TF = 512


def _moe_blocks(n_tokens):
    return (n_tokens * TOPK + E * (MOE_R - 1)) // MOE_R


NEG = -1e30
VMEM_LIMIT = 56 * 1024 * 1024


def _cp(sem, vmem=VMEM_LIMIT):
    return pltpu.CompilerParams(dimension_semantics=sem, vmem_limit_bytes=vmem)


def _dot(a, b):
    return jnp.dot(a, b, preferred_element_type=F32)


def _dot_nt(a, b):
    return lax.dot_general(a, b, (((1,), (1,)), ((), ())), preferred_element_type=F32)


def _dot_tn(a, b):
    return lax.dot_general(a, b, (((0,), (0,)), ((), ())), preferred_element_type=F32)


def _sigmoid(x):
    return 1.0 / (1.0 + jnp.exp(-x))


def _pack_pairs(lo, hi):
    lo_w = lax.bitcast_convert_type(lo.astype(BF16).astype(F32), jnp.uint32)
    hi_w = lax.bitcast_convert_type(hi.astype(BF16).astype(F32), jnp.uint32)
    return lax.shift_right_logical(lo_w, jnp.uint32(16)) | (hi_w & jnp.uint32(0xFFFF0000))


def _unpack_pairs(w):
    lo = lax.bitcast_convert_type(lax.shift_left(w, jnp.uint32(16)), F32)
    hi = lax.bitcast_convert_type(w & jnp.uint32(0xFFFF0000), F32)
    return lo, hi


def _row_group(i, tm):
    r0 = i * tm
    return jnp.where(r0 >= NL, 2, r0 // S)


def _ada_kernel(s_ref, w_ref, b_ref, o_ref):
    s = s_ref[...]
    s = s * _sigmoid(s)
    o_ref[...] = _dot(s.astype(BF16), w_ref[...].astype(BF16)) + b_ref[...]


def ada_modulation(cond, w_ada, b_ada):
    tn = 512
    n = 6 * D
    return pl.pallas_call(
        _ada_kernel,
        grid=(DEPTH, n // tn),
        in_specs=[
            pl.BlockSpec((8, D), lambda l, j: (0, 0)),
            pl.BlockSpec((None, D, tn), lambda l, j: (l, 0, j)),
            pl.BlockSpec((None, 1, tn), lambda l, j: (l, 0, j)),
        ],
        out_specs=pl.BlockSpec((None, 8, tn), lambda l, j: (l, 0, j)),
        out_shape=jax.ShapeDtypeStruct((DEPTH, 8, n), F32),
        compiler_params=_cp(("arbitrary", "arbitrary")),
        name="ada_modulation",
    )(cond, w_ada, b_ada.reshape(DEPTH, 1, n))


def _normmod(x, g, sh, sc):
    y = x * lax.rsqrt(jnp.mean(x * x, axis=-1, keepdims=True) + EPS) * g
    return y * (1.0 + sc) + sh


CTX_ROWS = NB * L
NLB = NL // CTX_ROWS


def _stream_block(i, xl_ref, xc_ref):
    return jnp.where(i < NLB, xl_ref[...], xc_ref[...])


def _normmod_kernel(xl_ref, xc_ref, g_ref, sh_ref, sc_ref, h_ref):
    x = _stream_block(pl.program_id(0), xl_ref, xc_ref)
    h_ref[...] = _normmod(x, g_ref[...], sh_ref[...], sc_ref[...]).astype(h_ref.dtype)


def _normmod_router_kernel(xl_ref, xc_ref, g_ref, sh_ref, sc_ref, wr_ref, h_ref, lg_ref):
    x = _stream_block(pl.program_id(0), xl_ref, xc_ref)
    h = _normmod(x, g_ref[...], sh_ref[...], sc_ref[...])
    h_ref[...] = _pack_pairs(h[:, :D // 2], h[:, D // 2:])
    lg_ref[...] = _dot(h.astype(BF16), wr_ref[...].astype(BF16))


def norm_modulate(xl, xc, gain, mod, k_shift, k_scale, out_dtype, w_router=None, n_rows=NT):
    tm = CTX_ROWS
    in_specs = [
        pl.BlockSpec((tm, D), lambda i: (jnp.minimum(i, NLB - 1), 0)),
        pl.BlockSpec((tm, D), lambda i: (0, 0)),
        pl.BlockSpec((1, D), lambda i: (0, 0)),
        pl.BlockSpec((None, 1, D), lambda i: (_row_group(i, tm), 0, k_shift)),
        pl.BlockSpec((None, 1, D), lambda i: (_row_group(i, tm), 0, k_scale)),
    ]
    if w_router is None:
        return pl.pallas_call(
            _normmod_kernel, grid=(n_rows // tm,), in_specs=in_specs,
            out_specs=pl.BlockSpec((tm, D), lambda i: (i, 0)),
            out_shape=jax.ShapeDtypeStruct((n_rows, D), out_dtype),
            compiler_params=_cp(("arbitrary",)), name="norm_modulate",
        )(xl, xc, gain, mod, mod)
    return pl.pallas_call(
        _normmod_router_kernel, grid=(n_rows // tm,),
        in_specs=in_specs + [pl.BlockSpec((D, 128), lambda i: (0, 0))],
        out_specs=[pl.BlockSpec((tm, D // 2), lambda i: (i, 0)), pl.BlockSpec((tm, 128), lambda i: (i, 0))],
        out_shape=[jax.ShapeDtypeStruct((n_rows, D // 2), jnp.uint32), jax.ShapeDtypeStruct((n_rows, 128), F32)],
        compiler_params=_cp(("arbitrary",)), name="norm_modulate_router",
    )(xl, xc, gain, mod, mod, w_router)


def _mm_kernel(a_ref, b_ref, o_ref):
    o_ref[...] = _dot_nt(a_ref[...], b_ref[...]).astype(o_ref.dtype)


def matmul(a, b, layer, tm, tn, out_dtype):
    m, k = a.shape
    n = b.shape[1]
    return pl.pallas_call(
        _mm_kernel,
        grid=(m // tm, n // tn),
        in_specs=[pl.BlockSpec((tm, k), lambda i, j: (i, 0)),
                  pl.BlockSpec((None, tn, k), lambda i, j: (layer, j, 0))],
        out_specs=pl.BlockSpec((tm, tn), lambda i, j: (i, j)),
        out_shape=jax.ShapeDtypeStruct((m, n), out_dtype),
        compiler_params=_cp(("arbitrary", "arbitrary")),
        name="matmul",
    )(a, b)


def _qblk(b, i):
    return jnp.where(i == 0, CTX_BLK0 + b, b * NBLK_L + i - 1)


def _rms_rows(x, g):
    return x * lax.rsqrt(jnp.mean(x * x, axis=-1, keepdims=True) + EPS) * g


NA_KROWS = 12
NA_KW = NA_KROWS * GW


NA_HG = 4
NA_GW = NA_HG * NA_D


def _na_kernel(tiles, q_ref, kl_ref, kc_ref, vl_ref, vc_ref, pair_ref, qn_ref, kn_ref, o_ref, ks_ref, bias_ref):
    i = pl.program_id(2)
    heads = [slice(NA_D * h, NA_D * (h + 1)) for h in range(NA_HG)]

    @pl.when(i == 0)
    def _():
        for hs in heads:
            ks_ref[0:L, hs] = _rms_rows(kc_ref[:, hs].astype(F32), kn_ref[...]).astype(BF16)
            ks_ref[L:, hs] = _rms_rows(kl_ref[:, hs].astype(F32), kn_ref[...]).astype(BF16)

    left, ok_l, ok_r = tiles
    lane = lax.broadcasted_iota(jnp.int32, (GW, 2 * GW), 1)
    for p, first_step in enumerate((1, 2, NBLK_L)):
        @pl.when(i == first_step)
        def _(p=p):
            for h in range(NA_HG):
                for a in range(BLK // GW):
                    for m in range(NA_KROWS // 2):
                        dst = (h, slice(GW * a, GW * (a + 1)), slice(2 * GW * m, 2 * GW * (m + 1)))
                        if not (ok_l[p, a, m] or ok_r[p, a, m]):
                            bias_ref[dst] = jnp.full((GW, 2 * GW), NEG, F32)
                            continue
                        t = pair_ref[h, int(left[p, a, m]) + 1]
                        if not ok_l[p, a, m]:
                            t = jnp.where(lane >= GW, t, NEG)
                        if not ok_r[p, a, m]:
                            t = jnp.where(lane < GW, t, NEG)
                        bias_ref[dst] = t

    def query(hs):
        return (_rms_rows(q_ref[:, hs].astype(F32), qn_ref[...]) * (NA_D ** -0.5)).astype(BF16)

    @pl.when(i == 0)
    def _():
        for hs in heads:
            s_c = _dot_nt(query(hs), ks_ref[0:L, hs])
            p = jnp.exp(s_c - jnp.max(s_c, axis=-1, keepdims=True))
            l = jnp.sum(p, axis=-1, keepdims=True)
            o_ref[:, hs] = (_dot(p.astype(BF16), vc_ref[:, hs]) / l).astype(o_ref.dtype)

    @pl.when(i > 0)
    def _():
        row0 = jnp.clip(4 * (i - 1) - 4, 0, GW - NA_KROWS)
        k0 = pl.multiple_of(row0 * GW, BLK)
        for h, hs in enumerate(heads):
            q = query(hs)
            s_c = _dot_nt(q, ks_ref[0:L, hs])
            s_l = _dot_nt(q, ks_ref[pl.ds(L + k0, NA_KW), hs]) + bias_ref[h]
            m = jnp.maximum(jnp.max(s_c, axis=-1, keepdims=True), jnp.max(s_l, axis=-1, keepdims=True))
            p_c = jnp.exp(s_c - m)
            p_l = jnp.exp(s_l - m)
            l = jnp.sum(p_c, axis=-1, keepdims=True) + jnp.sum(p_l, axis=-1, keepdims=True)
            o = _dot(p_l.astype(BF16), vl_ref[pl.ds(k0, NA_KW), hs]) + _dot(p_c.astype(BF16), vc_ref[:, hs])
            o_ref[:, hs] = (o / l).astype(o_ref.dtype)


def _na_bias_index():
    dr = np.zeros((3, BLK, NA_KW), np.int32)
    dc = np.zeros((3, BLK, NA_KW), np.int32)
    ok = np.zeros((3, BLK, NA_KW), bool)
    rows = S // GW
    for p, blk in enumerate((0, 5, NBLK_L - 1)):
        ks = int(np.clip(4 * blk - 4, 0, GW - NA_KROWS))
        r = 4 * blk + np.arange(BLK) // GW
        c = np.arange(BLK) % GW
        rk = ks + np.arange(NA_KW) // GW
        ck = np.arange(NA_KW) % GW
        r0 = np.clip(r - NA_R // 2, 0, rows - NA_R)
        ws = np.clip(c - NA_C // 2, 0, GW - NA_C)
        row_ok = (rk[None, :] >= r0[:, None]) & (rk[None, :] < r0[:, None] + NA_R)
        col_ok = (ck[None, :] >= ws[:, None]) & (ck[None, :] < ws[:, None] + NA_C)
        ok[p] = row_ok & col_ok
        dr[p] = np.clip(rk[None, :] - r[:, None] + NA_R - 1, 0, 2 * NA_R - 2)
        dc[p] = np.clip(ck[None, :] - c[:, None], -(NA_C - 1), NA_C - 1) + NA_C - 1
    return dr, dc, ok


NA_NDR = 2 * NA_R


def _na_bias_tiles():
    dr, _, ok = _na_bias_index()
    qr, kr = BLK // GW, NA_KROWS
    dr_t = dr.reshape(3, qr, GW, kr, GW)[:, :, 0, :, 0]
    ok_t = ok.reshape(3, qr, GW, kr, GW).any(axis=(2, 4))
    rk0 = np.array([int(np.clip(4 * blk - 4, 0, GW - NA_KROWS)) for blk in (0, 5, NBLK_L - 1)])
    r = np.array([4 * blk for blk in (0, 5, NBLK_L - 1)])[:, None, None] + np.arange(qr)[None, :, None]
    left = (rk0[:, None, None] + 2 * np.arange(kr // 2)[None, None, :]) - r + NA_R - 1
    assert np.all(dr_t[:, :, 0::2][ok_t[:, :, 0::2]] == left[ok_t[:, :, 0::2]])
    return left, ok_t[:, :, 0::2], ok_t[:, :, 1::2]


def _na_bias_pairs(rpb):
    _, dc, ok = _na_bias_index()
    qr, kr = BLK // GW, NA_KROWS
    dc_t = dc[0].reshape(qr, GW, kr, GW)[0, :, 0, :].reshape(GW * GW)
    col_ok = ok.reshape(3, qr, GW, kr, GW).any(axis=(0, 1, 3))
    oh_c = jnp.asarray(np.eye(2 * NA_C - 1, dtype=np.float32)[:, dc_t])
    t1 = jnp.einsum('hrd,dx->hrx', rpb.astype(F32), oh_c, precision=lax.Precision.HIGHEST)
    t1 = jnp.where(col_ok.reshape(1, 1, GW * GW), t1, NEG).reshape(NA_H, 2 * NA_R - 1, GW, GW)
    t1 = jnp.pad(t1, ((0, 0), (1, 1), (0, 0), (0, 0)))
    return jnp.concatenate([t1[:, :-1], t1[:, 1:]], axis=-1)


def na_attention(proj, rpb, qn, kn):
    pairs = _na_bias_pairs(rpb)

    cq, ck, cv = _COL_NAQ // NA_GW, _COL_NAK // NA_GW, _COL_NAV // NA_GW
    return pl.pallas_call(
        functools.partial(_na_kernel, _na_bias_tiles()),
        grid=(NB, NA_H // NA_HG, NBLK_L + 1),
        in_specs=[
            pl.BlockSpec((BLK, NA_GW), lambda b, h, i: (_qblk(b, i), cq + h)),
            pl.BlockSpec((S, NA_GW), lambda b, h, i: (b, ck + h)),
            pl.BlockSpec((L, NA_GW), lambda b, h, i: (CTX_BLK0 + b, ck + h)),
            pl.BlockSpec((S, NA_GW), lambda b, h, i: (b, cv + h)),
            pl.BlockSpec((L, NA_GW), lambda b, h, i: (CTX_BLK0 + b, cv + h)),
            pl.BlockSpec((NA_HG, NA_NDR, GW, 2 * GW), lambda b, h, i: (h, 0, 0, 0)),
            pl.BlockSpec((1, NA_D), lambda b, h, i: (0, 0)),
            pl.BlockSpec((1, NA_D), lambda b, h, i: (0, 0)),
        ],
        out_specs=pl.BlockSpec((BLK, NA_GW), lambda b, h, i: (_qblk(b, i), h)),
        out_shape=jax.ShapeDtypeStruct((NT, NA_W), BF16),
        scratch_shapes=[pltpu.VMEM((L + S, NA_GW), BF16), pltpu.VMEM((NA_HG, BLK, NA_KW), F32)],
        compiler_params=_cp(("arbitrary", "arbitrary", "arbitrary")),
        name="na_attention",
    )(proj, proj, proj, proj, proj, pairs, qn.reshape(1, NA_D), kn.reshape(1, NA_D))


MLA_HW = 256


def _half_mask(h, width=128):
    lane = lax.broadcasted_iota(jnp.int32, (1, width), 1)
    return (lane < 64) if h % 2 == 0 else (lane >= 64)


def _mla_q_kernel(cq_ref, w_ref, qa_ref, gn_ref, g2_ref, gs2_ref, c2_ref, s2_ref, q_ref):
    x = _rms_rows(cq_ref[...].astype(F32), qa_ref[...]).astype(BF16)
    y = _dot(x, w_ref[...])
    nw = MLA_H * MLA_NOPE
    gc = g2_ref[...] * c2_ref[...]
    gs = gs2_ref[...] * s2_ref[...]
    for p in range(MLA_H // 2):
        r1 = y[:, nw + 128 * p: nw + 128 * (p + 1)]
        r2 = y[:, nw + 768 + 128 * p: nw + 768 + 128 * (p + 1)]
        rot = r1 * gc + r2 * gs
        sq = r1 * r1
        for h in (2 * p, 2 * p + 1):
            msk = _half_mask(h)
            nope = y[:, 128 * h: 128 * (h + 1)]
            ss = jnp.sum(nope * nope, axis=-1, keepdims=True) + jnp.sum(
                jnp.where(msk, sq, 0.0), axis=-1, keepdims=True)
            inv = lax.rsqrt(ss / MLA_QK + EPS) * (MLA_QK ** -0.5)
            q_ref[h, :, 0:128] = (nope * gn_ref[...] * inv).astype(BF16)
            q_ref[h, :, 128:256] = (jnp.where(msk, rot, 0.0) * inv).astype(BF16)


def _mla_kv_kernel(ckv_ref, kr_ref, w_ref, kva_ref, gn_ref, g2_ref, gs2_ref, c2_ref, s2_ref,
                   k_ref, v_ref):
    x = _rms_rows(ckv_ref[...].astype(F32), kva_ref[...]).astype(BF16)
    y = _dot(x, w_ref[...])
    kr = kr_ref[...].astype(F32)
    r1 = kr[:, 0:128]
    r2 = kr[:, 128:256]
    rot = r1 * (g2_ref[...] * c2_ref[...]) + r2 * (gs2_ref[...] * s2_ref[...])
    ss_r = jnp.sum(jnp.where(_half_mask(0), r1 * r1, 0.0), axis=-1, keepdims=True)
    nw = MLA_H * MLA_NOPE
    lane0 = lax.broadcasted_iota(jnp.int32, (kr.shape[0], 128), 1) == 0
    for h in range(MLA_H):
        nope = y[:, 128 * h: 128 * (h + 1)]
        ss = jnp.sum(nope * nope, axis=-1, keepdims=True) + ss_r
        inv = lax.rsqrt(ss / MLA_QK + EPS)
        k_ref[h, :, 0:128] = (nope * gn_ref[...] * inv).astype(BF16)
        k_ref[h, :, 128:256] = (jnp.where(_half_mask(h), rot, 0.0) * inv).astype(BF16)
        v_ref[h, :, 0:128] = y[:, nw + 128 * h: nw + 128 * (h + 1)].astype(BF16)
        v_ref[h, :, 128:256] = jnp.where(lane0, 1.0, 0.0).astype(BF16)


def _rope_tables():
    t = jnp.arange(S, dtype=jnp.int32)
    pos = (t // GW, t % GW)
    nf = MLA_ROPE // 4
    inv = ROPE_BASE ** (-jnp.arange(nf, dtype=F32) / nf)
    cs, sn = [], []
    for ax in range(2):
        ang = pos[ax].astype(F32)[:, None] * inv[None, :]
        c, s = jnp.cos(ang), jnp.sin(ang)
        cs += [c, c]
        sn += [-s, s]
    c64 = jnp.tile(jnp.concatenate(cs, axis=1), (NB, 1))
    s64 = jnp.tile(jnp.concatenate(sn, axis=1), (NB, 1))
    c64 = jnp.concatenate([c64, jnp.ones((NB * L, MLA_ROPE), F32)], axis=0)
    s64 = jnp.concatenate([s64, jnp.zeros((NB * L, MLA_ROPE), F32)], axis=0)
    return jnp.tile(c64, (1, 2)), jnp.tile(s64, (1, 2))


_ROPE_SWAP = np.concatenate([np.arange(16, 32), np.arange(0, 16), np.arange(48, 64), np.arange(32, 48)])


def _rope_gains(g):
    gr = g[MLA_NOPE:]
    return (g[:MLA_NOPE].reshape(1, 128), jnp.tile(gr, 2).reshape(1, 128),
            jnp.tile(gr[_ROPE_SWAP], 2).reshape(1, 128))


def mla_q_prep(proj, w_uq_r, qa_g, qn_g, c2, s2):
    tm = 512
    gn, g2, gs2 = _rope_gains(qn_g)
    vec = lambda w: pl.BlockSpec((1, w), lambda i: (0, 0))
    return pl.pallas_call(
        _mla_q_kernel,
        grid=(NT // tm,),
        in_specs=[
            pl.BlockSpec((tm, MLA_QL), lambda i: (i, _COL_CQ // MLA_QL)),
            pl.BlockSpec((MLA_QL, 3072), lambda i: (0, 0)),
            vec(MLA_QL), vec(128), vec(128), vec(128),
            pl.BlockSpec((tm, 128), lambda i: (i, 0)),
            pl.BlockSpec((tm, 128), lambda i: (i, 0)),
        ],
        out_specs=pl.BlockSpec((MLA_H, tm, MLA_HW), lambda i: (0, i, 0)),
        out_shape=jax.ShapeDtypeStruct((MLA_H, NT, MLA_HW), BF16),
        compiler_params=_cp(("arbitrary",)),
        name="mla_q_prep",
    )(proj, w_uq_r, qa_g.reshape(1, MLA_QL), gn, g2, gs2, c2, s2)


def mla_kv_prep(proj, w_ukv_r, kva_g, kn_g, c2, s2):
    tm = 512
    gn, g2, gs2 = _rope_gains(kn_g)
    vec = lambda w: pl.BlockSpec((1, w), lambda i: (0, 0))
    return pl.pallas_call(
        _mla_kv_kernel,
        grid=(NT // tm,),
        in_specs=[
            pl.BlockSpec((tm, MLA_KVL), lambda i: (i, _COL_CKV // MLA_KVL)),
            pl.BlockSpec((tm, 256), lambda i: (i, _COL_KR // 256)),
            pl.BlockSpec((MLA_KVL, 3072), lambda i: (0, 0)),
            vec(MLA_KVL), vec(128), vec(128), vec(128),
            pl.BlockSpec((tm, 128), lambda i: (i, 0)),
            pl.BlockSpec((tm, 128), lambda i: (i, 0)),
        ],
        out_specs=[pl.BlockSpec((MLA_H, tm, MLA_HW), lambda i: (0, i, 0)),
                   pl.BlockSpec((MLA_H, tm, MLA_HW), lambda i: (0, i, 0))],
        out_shape=[jax.ShapeDtypeStruct((MLA_H, NT, MLA_HW), BF16),
                   jax.ShapeDtypeStruct((MLA_H, NT, MLA_HW), BF16)],
        compiler_params=_cp(("arbitrary",)),
        name="mla_kv_prep",
    )(proj, proj, w_ukv_r, kva_g.reshape(1, MLA_KVL), gn, g2, gs2, c2, s2)


MLA_TQ = 256


def _mla_attn_kernel(q_ref, kl_ref, kc_ref, vl_ref, vc_ref, o_ref, s0_ref, s1_ref, p0_ref, p1_ref):
    g = pl.program_id(0)

    @pl.when(g == 0)
    def _():
        s0_ref[...] = jnp.zeros_like(s0_ref)
        s1_ref[...] = jnp.zeros_like(s1_ref)
        p0_ref[...] = jnp.ones_like(p0_ref)
        p1_ref[...] = jnp.ones_like(p1_ref)

    n_kc = (L + S) // L
    rows = MLA_TQ // (n_kc - 1)

    def tie(x, dep):
        if dep is None:
            return x
        return jnp.concatenate([x[0:16, :] + dep, x[16:, :]], axis=0)

    def stages(s_new, s_old, p_new, p_old):
        acc, dep = None, None
        for c in range(n_kc):
            keys = slice(L * c, L * (c + 1))
            k = kc_ref[...] if c == 0 else kl_ref[L * (c - 1): L * c, :]
            v = vc_ref[...] if c == 0 else vl_ref[L * (c - 1): L * c, :]
            if c < n_kc - 1:
                r = slice(rows * c, rows * (c + 1))
                s = s_old[r, :]
                p = jnp.exp((s - jnp.max(s, axis=-1, keepdims=True)).astype(BF16))
                p_old[r, :] = p
                dep = (p[0:16, 0:MLA_HW].astype(F32) * 0.0).astype(BF16)
            part = _dot(tie(p_new[:, keys], dep), v)
            acc = part if acc is None else acc + part
            s_new[:, keys] = _dot_nt(tie(q_ref[...], dep), k)
        o_ref[...] = (acc[:, :MLA_V] / acc[:, MLA_V:MLA_V + 1]).astype(o_ref.dtype)

    @pl.when(g % 2 == 0)
    def _():
        stages(s0_ref, s1_ref, p0_ref, p1_ref)

    @pl.when(g % 2 == 1)
    def _():
        stages(s1_ref, s0_ref, p1_ref, p0_ref)


def _mla_ctx_kernel(q_ref, kc_ref, vc_ref, o_ref):
    s = _dot_nt(q_ref[...], kc_ref[...])
    p = jnp.exp((s - jnp.max(s, axis=-1, keepdims=True)).astype(BF16))
    acc = _dot(p, vc_ref[...])
    o_ref[...] = (acc[:, :MLA_V] / acc[:, MLA_V:MLA_V + 1]).astype(o_ref.dtype)


def mla_attention(q, k, v, need_ctx):
    nq = S // MLA_TQ
    n_steps = NB * MLA_H * nq

    def bhi(g):
        return g // (MLA_H * nq), (g // nq) % MLA_H, g % nq

    def cur(g):
        return bhi(jnp.minimum(g, n_steps - 1))

    def prev(g):
        return bhi(jnp.maximum(g - 2, 0))

    ob = pl.pallas_call(
        _mla_attn_kernel,
        grid=(n_steps + 2,),
        in_specs=[
            pl.BlockSpec((None, MLA_TQ, MLA_HW), lambda g: (cur(g)[1], cur(g)[0] * nq + cur(g)[2], 0)),
            pl.BlockSpec((None, S, MLA_HW), lambda g: (cur(g)[1], cur(g)[0], 0)),
            pl.BlockSpec((None, L, MLA_HW), lambda g: (cur(g)[1], CTX_BLK0 + cur(g)[0], 0)),
            pl.BlockSpec((None, S, MLA_HW), lambda g: (prev(g)[1], prev(g)[0], 0)),
            pl.BlockSpec((None, L, MLA_HW), lambda g: (prev(g)[1], CTX_BLK0 + prev(g)[0], 0)),
        ],
        out_specs=pl.BlockSpec((MLA_TQ, MLA_V), lambda g: (prev(g)[0] * nq + prev(g)[2], prev(g)[1])),
        out_shape=jax.ShapeDtypeStruct((NL, MLA_W), BF16),
        scratch_shapes=[pltpu.VMEM((MLA_TQ, L + S), F32), pltpu.VMEM((MLA_TQ, L + S), F32),
                        pltpu.VMEM((MLA_TQ, L + S), BF16), pltpu.VMEM((MLA_TQ, L + S), BF16)],
        compiler_params=_cp(("arbitrary",)),
        name="mla_attention",
    )(q, k, k, v, v)
    if not need_ctx:
        return ob, None
    ob_ctx = pl.pallas_call(
        _mla_ctx_kernel,
        grid=(NB, MLA_H),
        in_specs=[
            pl.BlockSpec((None, L, MLA_HW), lambda b, h: (h, CTX_BLK0 + b, 0)),
            pl.BlockSpec((None, L, MLA_HW), lambda b, h: (h, CTX_BLK0 + b, 0)),
            pl.BlockSpec((None, L, MLA_HW), lambda b, h: (h, CTX_BLK0 + b, 0)),
        ],
        out_specs=pl.BlockSpec((L, MLA_V), lambda b, h: (b, h)),
        out_shape=jax.ShapeDtypeStruct((NB * L, MLA_W), BF16),
        compiler_params=_cp(("arbitrary", "arbitrary")),
        name="mla_attention_ctx",
    )(q, k, v)
    return ob, ob_ctx


N_CH_C = L // CH
N_CH_L = S // CH
N_CH = N_CH_C + N_CH_L


def _gla_kernel(qf_ref, kf_ref, vf_ref, af_ref, qb_ref, kb_ref, vb_ref, ab_ref,
                wg_ref, bg_ref, of_ref, ob_ref, st_ref):
    t = pl.program_id(1)

    @pl.when(t == 0)
    def _():
        st_ref[...] = jnp.zeros_like(st_ref)

    row = lax.broadcasted_iota(jnp.int32, (CH, CH), 0)
    col = lax.broadcasted_iota(jnp.int32, (CH, CH), 1)
    dirs = (
        (qf_ref, kf_ref, vf_ref, af_ref, of_ref, col <= row, CH - 1),
        (qb_ref, kb_ref, vb_ref, ab_ref, ob_ref, col >= row, 0),
    )
    for d, (q_ref, k_ref, v_ref, a_ref, o_ref, keep, last) in enumerate(dirs):
        z = _dot(a_ref[...], wg_ref[d]) + bg_ref[d]
        g = (jnp.minimum(z, 0.0) - jnp.log(1.0 + jnp.exp(-jnp.abs(z)))) / GLA_TAU
        b = jnp.dot(keep.astype(F32), g, preferred_element_type=F32,
                    precision=lax.Precision.HIGHEST)
        b_end = b[last:last + 1, :]
        b_mid = b[CH // 2:CH // 2 + 1, :]
        q = q_ref[...].astype(F32) * (GLA_DK ** -0.5)
        k = k_ref[...].astype(F32)
        qa = (q * jnp.exp(b - b_mid)).astype(BF16)
        ka = (k * jnp.exp(b_mid - b)).astype(BF16)
        qd = (q * jnp.exp(b)).astype(BF16)
        ke = (k * jnp.exp(b_end - b)).astype(BF16)
        e_end = jnp.exp(b_end)
        v = v_ref[...]
        for h in range(GLA_H):
            ksl = slice(GLA_DK * h, GLA_DK * (h + 1))
            vsl = slice(GLA_DV * h, GLA_DV * (h + 1))
            att = jnp.where(keep, _dot_nt(qa[:, ksl], ka[:, ksl]), 0.0)
            st = st_ref[d, h]
            o = _dot_nt(qd[:, ksl], st.astype(BF16)) + _dot(att.astype(BF16), v[:, vsl])
            o_ref[:, vsl] = o.astype(o_ref.dtype)
            st_ref[d, h] = st * e_end[:, ksl] + _dot_tn(v[:, vsl], ke[:, ksl])


def gla_scan(proj, wg_f, bg_f, wg_b, bg_b):
    wg = jnp.zeros((2, 128, GLA_KW), F32)
    wg = wg.at[0, 0:GLA_RANK].set(wg_f).at[1, GLA_RANK:2 * GLA_RANK].set(wg_b).astype(BF16)
    bg = jnp.stack([bg_f, bg_b]).reshape(2, 1, GLA_KW)

    ctx0 = NL // CH

    def fwd(b, t):
        return jnp.where(t < N_CH_C, ctx0 + N_CH_C * b + t, N_CH_L * b + t - N_CH_C)

    def bwd(b, t):
        return jnp.where(t < N_CH_C, ctx0 + N_CH_C * b + N_CH_C - 1 - t, N_CH_L * b + N_CH - 1 - t)

    def specs(rowfn):
        return [
            pl.BlockSpec((CH, GLA_KW), lambda b, t: (rowfn(b, t), _COL_GQ // GLA_KW)),
            pl.BlockSpec((CH, GLA_KW), lambda b, t: (rowfn(b, t), _COL_GK // GLA_KW)),
            pl.BlockSpec((CH, GLA_W), lambda b, t: (rowfn(b, t), _COL_GV // GLA_W)),
            pl.BlockSpec((CH, 128), lambda b, t: (rowfn(b, t), _COL_GA // 128)),
        ]

    return pl.pallas_call(
        _gla_kernel,
        grid=(NB, N_CH),
        in_specs=specs(fwd) + specs(bwd) + [
            pl.BlockSpec((2, 128, GLA_KW), lambda b, t: (0, 0, 0)),
            pl.BlockSpec((2, 1, GLA_KW), lambda b, t: (0, 0, 0)),
        ],
        out_specs=[pl.BlockSpec((CH, GLA_W), lambda b, t: (fwd(b, t), 0)),
                   pl.BlockSpec((CH, GLA_W), lambda b, t: (bwd(b, t), 0))],
        out_shape=[jax.ShapeDtypeStruct((NT, GLA_W), BF16)] * 2,
        scratch_shapes=[pltpu.VMEM((2, GLA_H, GLA_DV, GLA_DK), F32)],
        compiler_params=_cp(("arbitrary", "arbitrary")),
        name="gla_scan",
    )(proj, proj, proj, proj, proj, proj, proj, proj, wg, bg)


def _gla_out_kernel(of_ref, ob_ref, g_ref, gn_ref, o_ref):
    for h in range(GLA_H):
        sl = slice(GLA_DV * h, GLA_DV * (h + 1))
        o = _rms_rows(of_ref[:, sl].astype(F32) + ob_ref[:, sl].astype(F32), gn_ref[...])
        gate = g_ref[:, sl].astype(F32)
        o_ref[:, sl] = (o * (gate * _sigmoid(gate))).astype(o_ref.dtype)


def gla_output(o_f, o_b, proj, on_g):
    tm = 512
    return pl.pallas_call(
        _gla_out_kernel,
        grid=(NT // tm,),
        in_specs=[
            pl.BlockSpec((tm, GLA_W), lambda i: (i, 0)),
            pl.BlockSpec((tm, GLA_W), lambda i: (i, 0)),
            pl.BlockSpec((tm, GLA_W), lambda i: (i, _COL_GG // GLA_W)),
            pl.BlockSpec((1, GLA_DV), lambda i: (0, 0)),
        ],
        out_specs=pl.BlockSpec((tm, GLA_W), lambda i: (i, 0)),
        out_shape=jax.ShapeDtypeStruct((NT, GLA_W), BF16),
        compiler_params=_cp(("arbitrary",)),
        name="gla_output",
    )(o_f, o_b, proj, on_g.reshape(1, GLA_DV))


def _out_proj_kernel(with_ctx, oa_ref, obl_ref, obc_ref, oc_ref, w_ref, xl_ref, xc_ref, g_ref, *o_refs):
    i = pl.program_id(1)
    a = jnp.concatenate([oa_ref[...], _stream_block(i, obl_ref, obc_ref), oc_ref[...]], axis=1)
    y = _stream_block(i, xl_ref, xc_ref) + g_ref[...] * _dot(a, w_ref[...])
    if not with_ctx:
        o_refs[0][...] = y
        return

    @pl.when(i < NLB)
    def _():
        o_refs[0][...] = y

    @pl.when(i == NLB)
    def _():
        o_refs[1][...] = y


def out_projection(oa, ob, ob_ctx, oc, w_out, layer, xl, xc, mod, k_gate, with_ctx):
    tm, tn = CTX_ROWS, 1024
    nj = D // tn
    lat = lambda j, i: (jnp.minimum(i, NLB - 1), j)
    out_specs = [pl.BlockSpec((tm, tn), lat)]
    out_shape = [jax.ShapeDtypeStruct((NL, D), F32)]
    if with_ctx:
        out_specs.append(pl.BlockSpec((tm, tn), lambda j, i: (0, j)))
        out_shape.append(jax.ShapeDtypeStruct((CTX_ROWS, D), F32))
    res = pl.pallas_call(
        functools.partial(_out_proj_kernel, with_ctx),
        grid=(nj, NLB + (1 if with_ctx else 0)),
        in_specs=[
            pl.BlockSpec((tm, NA_W), lambda j, i: (i, 0)),
            pl.BlockSpec((tm, MLA_W), lambda j, i: (jnp.minimum(i, NLB - 1), 0)),
            pl.BlockSpec((tm, MLA_W), lambda j, i: (0, 0)),
            pl.BlockSpec((tm, GLA_W), lambda j, i: (i, 0)),
            pl.BlockSpec((None, D, tn), lambda j, i: (layer, 0, j)),
            pl.BlockSpec((tm, tn), lat),
            pl.BlockSpec((tm, tn), lambda j, i: (0, j)),
            pl.BlockSpec((None, 1, tn), lambda j, i: (_row_group(i, tm), 0, k_gate * nj + j)),
        ],
        out_specs=out_specs,
        out_shape=out_shape,
        compiler_params=_cp(("arbitrary", "arbitrary")),
        name="out_projection",
    )(oa, ob, ob_ctx if with_ctx else ob, oc, w_out, xl, xc, mod)
    return (res[0], res[1]) if with_ctx else (res[0], None)


def _route(logits, router_bias):
    n = logits.shape[0]
    scores = jax.nn.sigmoid(logits)
    grouped = (scores + router_bias.astype(F32)).reshape(n, E_GROUPS, E_PER)

    def top2(a):
        idx = lax.broadcasted_iota(jnp.int32, a.shape, a.ndim - 1)
        i1 = jnp.argmax(a, axis=-1).astype(jnp.int32)
        rest = jnp.where(idx == i1[..., None], -jnp.inf, a)
        i2 = jnp.argmax(rest, axis=-1).astype(jnp.int32)
        return jnp.max(a, axis=-1), jnp.max(rest, axis=-1), i1, i2

    m1, m2, _, _ = top2(grouped)
    grp = jnp.argmax(m1 + m2, axis=-1).astype(jnp.int32)
    gsel = lax.broadcasted_iota(jnp.int32, (n, E_GROUPS, E_PER), 1) == grp[:, None, None]
    in_group = jnp.sum(jnp.where(gsel, grouped, 0.0), axis=1)
    _, _, l1, l2 = top2(in_group)
    expert_idx = grp[:, None] * E_PER + jnp.stack([l1, l2], axis=-1)
    esel = lax.broadcasted_iota(jnp.int32, (n, TOPK, E), 2) == expert_idx[:, :, None]
    w = jnp.sum(jnp.where(esel, scores[:, None, :], 0.0), axis=-1)
    return expert_idx, w / jnp.sum(w, axis=-1, keepdims=True)


def _dispatch_plan(expert_idx):
    nk = expert_idx.shape[0] * TOPK
    nb = _moe_blocks(expert_idx.shape[0])
    flat_e = expert_idx.reshape(nk)
    onehot = (flat_e[:, None] == jnp.arange(E, dtype=jnp.int32)[None, :]).astype(jnp.int32)
    csum = jnp.cumsum(onehot, axis=0)
    counts = csum[-1]
    rank = jnp.take_along_axis(csum, flat_e[:, None], axis=1)[:, 0] - 1
    padded = (counts + MOE_R - 1) // MOE_R * MOE_R
    pad_end = jnp.cumsum(padded)
    pad_start = pad_end - padded
    dest = (pad_start[flat_e] + rank).astype(jnp.int32)
    row_token = jnp.zeros((nb * MOE_R,), jnp.int32).at[dest].set(jnp.arange(nk, dtype=jnp.int32) // TOPK)
    blk0 = jnp.arange(nb, dtype=jnp.int32) * MOE_R
    block_expert = jnp.minimum(jnp.sum((pad_end[None, :] <= blk0[:, None]).astype(jnp.int32), axis=1), E - 1)
    n_active = (pad_end[-1] // MOE_R).astype(jnp.int32).reshape(1)
    n_valid = jnp.clip(counts[block_expert] - (blk0 - pad_start[block_expert]), 0, MOE_R).astype(jnp.int32)
    bidx = jnp.arange(nb, dtype=jnp.int32)
    prev_e = jnp.concatenate([jnp.full((1,), -1, jnp.int32), block_expert[:-1]])
    first = jnp.logical_and(bidx < n_active[0], block_expert != prev_e)
    later = jnp.where(first, bidx, nb)
    nxt_blk = jnp.concatenate([lax.cummin(later[::-1])[::-1][1:], jnp.full((1,), nb, jnp.int32)])
    nxt_e = jnp.where(nxt_blk < nb, block_expert[jnp.minimum(nxt_blk, nb - 1)], -1).astype(jnp.int32)
    plan = (block_expert, n_active, first.astype(jnp.int32), nxt_e)
    return dest, row_token, n_valid, plan


DH = D // 2


DMA_UNROLL = 8


def _for_rows(n, body):
    full = n // DMA_UNROLL

    def group(t, c):
        for u in range(DMA_UNROLL):
            body(t * DMA_UNROLL + u)
        return c

    def single(r, c):
        body(r)
        return c

    lax.fori_loop(0, full, group, 0)
    lax.fori_loop(full * DMA_UNROLL, n, single, 0)


def _gather_rows_kernel(tok_ref, nv_ref, h_ref, o_ref, sem):
    i = pl.program_id(0)
    base = i * MOE_R
    nv = nv_ref[i]

    @pl.when(nv < MOE_R)
    def _():
        o_ref[...] = jnp.zeros_like(o_ref)

    def copy(r):
        return pltpu.make_async_copy(h_ref.at[pl.ds(tok_ref[base + r], 1)], o_ref.at[pl.ds(r, 1)], sem)

    _for_rows(nv, lambda r: copy(r).start())
    _for_rows(nv, lambda r: copy(r).wait())


def gather_rows(row_token, n_valid, h):
    nb = n_valid.shape[0]
    return pl.pallas_call(
        _gather_rows_kernel,
        grid_spec=pltpu.PrefetchScalarGridSpec(
            num_scalar_prefetch=2,
            grid=(nb,),
            in_specs=[pl.BlockSpec(memory_space=pl.ANY)],
            out_specs=pl.BlockSpec((MOE_R, DH), lambda i, tok, nv: (i, 0)),
            scratch_shapes=[pltpu.SemaphoreType.DMA(())],
        ),
        out_shape=jax.ShapeDtypeStruct((nb * MOE_R, DH), jnp.uint32),
        compiler_params=_cp(("arbitrary",)),
        name="moe_gather_rows",
    )(row_token, n_valid, h)


def _stage_weights(layer, n_tiles, width, be_ref, first_ref, nxt_ref, active, mats):
    t = pl.program_id(0)
    i = pl.program_id(1)

    def copies(e, tt):
        cols = pl.ds(pl.multiple_of(tt * width, width), width)
        return [pltpu.make_async_copy(w.at[layer, e, :, cols], st, sm) for w, st, _, sm in mats]

    @pl.when(jnp.logical_and(active, first_ref[i] == 1))
    def _():
        @pl.when(jnp.logical_and(t == 0, i == 0))
        def _():
            for c in copies(be_ref[0], 0):
                c.start()

        for c in copies(be_ref[i], t):
            c.wait()
        for _, st, wb, _ in mats:
            wb[...] = st[...].astype(BF16)

        @pl.when(nxt_ref[i] >= 0)
        def _():
            for c in copies(nxt_ref[i], t):
                c.start()

        @pl.when(jnp.logical_and(nxt_ref[i] < 0, t + 1 < n_tiles))
        def _():
            for c in copies(be_ref[0], t + 1):
                c.start()


def _ffn_up_kernel(layer, be_ref, na_ref, first_ref, nxt_ref, x_ref, w1_ref, w3_ref, o_ref,
                   st1_ref, st3_ref, w1b_ref, w3b_ref, sem):
    i = pl.program_id(1)
    active = i < na_ref[0]
    _stage_weights(layer, FF // MOE_TF, MOE_TF, be_ref, first_ref, nxt_ref, active,
                   [(w1_ref, st1_ref, w1b_ref, sem.at[0]), (w3_ref, st3_ref, w3b_ref, sem.at[1])])

    @pl.when(active)
    def _():
        lo, hi = _unpack_pairs(x_ref[...])
        lo, hi = lo.astype(BF16), hi.astype(BF16)
        a = _dot(lo, w1b_ref[0:DH, :]) + _dot(hi, w1b_ref[DH:, :])
        b = _dot(lo, w3b_ref[0:DH, :]) + _dot(hi, w3b_ref[DH:, :])
        o_ref[...] = (a * _sigmoid(a) * b).astype(o_ref.dtype)

    @pl.when(jnp.logical_not(active))
    def _():
        o_ref[...] = jnp.zeros_like(o_ref)


def ffn_up(plan, xb, w1, w3, layer):
    nb = plan[0].shape[0]
    last = lambda i, na: jnp.minimum(i, na[0] - 1)
    return pl.pallas_call(
        functools.partial(_ffn_up_kernel, layer),
        grid_spec=pltpu.PrefetchScalarGridSpec(
            num_scalar_prefetch=4,
            grid=(FF // MOE_TF, nb),
            in_specs=[
                pl.BlockSpec((MOE_R, DH), lambda j, i, be, na, fi, nx: (last(i, na), 0)),
                pl.BlockSpec(memory_space=pl.ANY),
                pl.BlockSpec(memory_space=pl.ANY),
            ],
            out_specs=pl.BlockSpec((MOE_R, MOE_TF), lambda j, i, be, na, fi, nx: (i, j)),
            scratch_shapes=[pltpu.VMEM((D, MOE_TF), F32), pltpu.VMEM((D, MOE_TF), F32),
                            pltpu.VMEM((D, MOE_TF), BF16), pltpu.VMEM((D, MOE_TF), BF16),
                            pltpu.SemaphoreType.DMA((2,))],
        ),
        out_shape=jax.ShapeDtypeStruct((nb * MOE_R, FF), BF16),
        compiler_params=_cp(("arbitrary", "arbitrary")),
        name="moe_ffn_up",
    )(*plan, xb, w1, w3)


MOE_TN = 2048


def _ffn_down_kernel(layer, be_ref, na_ref, first_ref, nxt_ref, h_ref, w2_ref, o_ref, st2_ref, w2b_ref, sem):
    i = pl.program_id(1)
    active = i < na_ref[0]
    _stage_weights(layer, D // MOE_TN, MOE_TN, be_ref, first_ref, nxt_ref, active,
                   [(w2_ref, st2_ref, w2b_ref, sem.at[0])])

    @pl.when(active)
    def _():
        y = _dot(h_ref[...], w2b_ref[...])
        o_ref[...] = _pack_pairs(y[:, :MOE_TN // 2], y[:, MOE_TN // 2:])

    @pl.when(jnp.logical_not(active))
    def _():
        o_ref[...] = jnp.zeros_like(o_ref)


def ffn_down(plan, hmid, w2, layer):
    nb = plan[0].shape[0]
    last = lambda i, na: jnp.minimum(i, na[0] - 1)
    return pl.pallas_call(
        functools.partial(_ffn_down_kernel, layer),
        grid_spec=pltpu.PrefetchScalarGridSpec(
            num_scalar_prefetch=4,
            grid=(D // MOE_TN, nb),
            in_specs=[
                pl.BlockSpec((MOE_R, FF), lambda n, i, be, na, fi, nx: (last(i, na), 0)),
                pl.BlockSpec(memory_space=pl.ANY),
            ],
            out_specs=pl.BlockSpec((MOE_R, MOE_TN // 2), lambda n, i, be, na, fi, nx: (i, n)),
            scratch_shapes=[pltpu.VMEM((FF, MOE_TN), F32), pltpu.VMEM((FF, MOE_TN), BF16),
                            pltpu.SemaphoreType.DMA((1,))],
        ),
        out_shape=jax.ShapeDtypeStruct((nb * MOE_R, DH), jnp.uint32),
        compiler_params=_cp(("arbitrary", "arbitrary")),
        name="moe_ffn_down",
    )(*plan, hmid, w2)


CMB_T = 256


def _combine_kernel(with_ctx, pos_ref, yb_ref, gate_ref, xl_ref, xc_ref, g_ref, *rest):
    o_refs, (buf_ref, sem) = rest[:-2], rest[-2:]
    i = pl.program_id(0)
    n_lat = NL // CMB_T
    base = i * CMB_T

    def copy(r, k):
        p = pos_ref[(base + r) * TOPK + k]
        return pltpu.make_async_copy(yb_ref.at[pl.ds(p, 1)], buf_ref.at[k, pl.ds(r, 1)], sem.at[k])

    def start(r, c):
        copy(r, 0).start()
        copy(r, 1).start()
        return c

    def wait(r, c):
        copy(r, 0).wait()
        copy(r, 1).wait()
        return c

    lax.fori_loop(0, CMB_T, start, 0, unroll=DMA_UNROLL)
    lax.fori_loop(0, CMB_T, wait, 0, unroll=DMA_UNROLL)
    gate = gate_ref[...]
    g0, g1 = gate[:, 0:1], gate[:, 1:2]
    half = MOE_TN // 2
    def emit(x_ref, o_ref):
        for n in range(D // MOE_TN):
            lo0, hi0 = _unpack_pairs(buf_ref[0, :, half * n: half * (n + 1)])
            lo1, hi1 = _unpack_pairs(buf_ref[1, :, half * n: half * (n + 1)])
            for part, y in enumerate((lo0 * g0 + lo1 * g1, hi0 * g0 + hi1 * g1)):
                sl = slice(MOE_TN * n + half * part, MOE_TN * n + half * (part + 1))
                o_ref[:, sl] = x_ref[:, sl] + g_ref[:, sl] * y

    if not with_ctx:
        emit(xl_ref, o_refs[0])
        return

    @pl.when(i < n_lat)
    def _():
        emit(xl_ref, o_refs[0])

    @pl.when(i >= n_lat)
    def _():
        emit(xc_ref, o_refs[1])


def moe_combine(dest, yb, gates, xl, xc, mod, k_gate, with_ctx):
    n_lat = NL // CMB_T
    lat = lambda i, pos: (jnp.minimum(i, n_lat - 1), 0)
    ctx = lambda i, pos: (jnp.maximum(i - n_lat, 0), 0)
    out_specs = [pl.BlockSpec((CMB_T, D), lat)]
    out_shape = [jax.ShapeDtypeStruct((NL, D), F32)]
    if with_ctx:
        out_specs.append(pl.BlockSpec((CMB_T, D), ctx))
        out_shape.append(jax.ShapeDtypeStruct((CTX_ROWS, D), F32))
    res = pl.pallas_call(
        functools.partial(_combine_kernel, with_ctx),
        grid_spec=pltpu.PrefetchScalarGridSpec(
            num_scalar_prefetch=1,
            grid=((NT if with_ctx else NL) // CMB_T,),
            in_specs=[
                pl.BlockSpec(memory_space=pl.ANY),
                pl.BlockSpec((CMB_T, TOPK), lambda i, pos: (i, 0)),
                pl.BlockSpec((CMB_T, D), lat),
                pl.BlockSpec((CMB_T, D), ctx),
                pl.BlockSpec((None, 1, D), lambda i, pos: (_row_group(i, CMB_T), 0, k_gate)),
            ],
            out_specs=out_specs,
            scratch_shapes=[pltpu.VMEM((TOPK, CMB_T, DH), jnp.uint32), pltpu.SemaphoreType.DMA((TOPK,))],
        ),
        out_shape=out_shape,
        compiler_params=_cp(("arbitrary",)),
        name="moe_combine",
    )(dest, yb, gates, xl, xc, mod)
    return (res[0], res[1]) if with_ctx else (res[0], None)


IN_W = 9312


W_RB = 512
W_NB = PW // W_RB
_W_SRC_KR = 4608
_W_SRC_GA = 9280


def _w_in_layout_kernel(w_ref, o_ref, buf_ref, sem):
    g = pl.program_id(0)

    def run(step, slot, op):
        l, k = step // W_NB, step % W_NB

        @pl.when(k < W_NB - 1)
        def _():
            r0 = pl.multiple_of(jnp.where(k < _W_SRC_KR // W_RB, k * W_RB, k * W_RB + MLA_ROPE), MLA_ROPE)
            op(pltpu.make_async_copy(w_ref.at[l, pl.ds(r0, W_RB)], buf_ref.at[slot], sem.at[slot]))

        @pl.when(k == W_NB - 1)
        def _():
            op(pltpu.make_async_copy(w_ref.at[l, pl.ds(_W_SRC_KR, MLA_ROPE)],
                                     buf_ref.at[slot, pl.ds(0, MLA_ROPE)], sem.at[slot]))
            op(pltpu.make_async_copy(w_ref.at[l, pl.ds(_W_SRC_GA, 2 * GLA_RANK)],
                                     buf_ref.at[slot, pl.ds(MLA_ROPE, 2 * GLA_RANK)], sem.at[slot]))

    start = lambda c: c.start()
    wait = lambda c: c.wait()

    @pl.when(g == 0)
    def _():
        run(0, 0, start)

    @pl.when(g + 1 < DEPTH * W_NB)
    def _():
        run(g + 1, (g + 1) % 2, start)

    run(g, g % 2, wait)
    k = g % W_NB

    @pl.when(k < W_NB - 1)
    def _():
        o_ref[...] = buf_ref[g % 2].astype(BF16)

    @pl.when(k == W_NB - 1)
    def _():
        x = buf_ref[g % 2, 0:MLA_ROPE, :]
        xs = jnp.concatenate([x[16:32], x[0:16], x[48:64], x[32:48]], axis=0)
        ga = buf_ref[g % 2, MLA_ROPE:MLA_ROPE + 2 * GLA_RANK, :]
        o_ref[...] = jnp.concatenate(
            [x, x, xs, xs, ga, jnp.zeros((W_RB - 4 * MLA_ROPE - 2 * GLA_RANK, D), F32)], axis=0).astype(BF16)


def _w_in_layout(w):
    w_t = jnp.swapaxes(w, 1, 2)
    return pl.pallas_call(
        _w_in_layout_kernel,
        grid=(DEPTH * W_NB,),
        in_specs=[pl.BlockSpec(memory_space=pl.ANY)],
        out_specs=pl.BlockSpec((None, W_RB, D), lambda g: (g // W_NB, g % W_NB, 0)),
        out_shape=jax.ShapeDtypeStruct((DEPTH, PW, D), BF16),
        scratch_shapes=[pltpu.VMEM((2, W_RB, D), F32), pltpu.SemaphoreType.DMA((2,))],
        compiler_params=_cp(("arbitrary",)),
        name="w_in_layout",
    )(w_t)


def _w_uq_cols():
    nope = [MLA_QK * h + j for h in range(MLA_H) for j in range(MLA_NOPE)]
    rope = [MLA_QK * h + MLA_NOPE + d for h in range(MLA_H) for d in range(MLA_ROPE)]
    rope_s = [MLA_QK * h + MLA_NOPE + int(d) for h in range(MLA_H) for d in _ROPE_SWAP]
    return np.array(nope + rope + rope_s, np.int32)


def _w_ukv_cols():
    kn = [(MLA_NOPE + MLA_V) * h + j for h in range(MLA_H) for j in range(MLA_NOPE)]
    vv = [(MLA_NOPE + MLA_V) * h + MLA_NOPE + j for h in range(MLA_H) for j in range(MLA_V)]
    return np.array(kn + vv, np.int32)


def token_mixing_layer(xl, xc, mod, layer, need_ctx, norm1, w_in_b, w_out_b, na_qn, na_kn, na_rpb, mla_qa, mla_kva,
                       w_uq, w_ukv, mla_qn, mla_kn, gwf, gbf, gwb, gbb, gla_on, c2, s2):
    h = norm_modulate(xl, xc, norm1.reshape(1, D), mod, 0, 1, BF16)
    proj = matmul(h, w_in_b, layer, NT // 8, 512, BF16)
    oa = na_attention(proj, na_rpb, na_qn, na_kn)
    q = mla_q_prep(proj, w_uq[:, _w_uq_cols()].astype(BF16), mla_qa, mla_qn, c2, s2)
    k, v = mla_kv_prep(proj, w_ukv[:, _w_ukv_cols()].astype(BF16), mla_kva, mla_kn, c2, s2)
    ob, ob_ctx = mla_attention(q, k, v, need_ctx)
    o_f, o_b = gla_scan(proj, gwf, gbf, gwb, gbb)
    oc = gla_output(o_f, o_b, proj, gla_on)
    return out_projection(oa, ob, ob_ctx, oc, w_out_b, layer, xl, xc, mod, 2, need_ctx)


def moe_layer(xl, xc, mod, norm2, w_router_p, router_bias, w1, w3, w2, layer, need_ctx):
    h, logits = norm_modulate(xl, xc, norm2.reshape(1, D), mod, 3, 4, None, w_router_p,
                              NT if need_ctx else NL)
    expert_idx, gates = _route(logits[:, :E], router_bias)
    dest, row_token, n_valid, plan = _dispatch_plan(expert_idx)
    xb = gather_rows(row_token, n_valid, h)
    hmid = ffn_up(plan, xb, w1, w3, layer)
    yb = ffn_down(plan, hmid, w2, layer)
    return moe_combine(dest, yb, gates, xl, xc, mod, 5, need_ctx)


def kernel(x, c, ctx, c_ctx, w_ada, b_ada, norm1, norm2, w_in, w_out, na_q_norm, na_k_norm, na_rpb, mla_qa_norm, mla_kva_norm, mla_w_uq, mla_w_ukv, mla_q_norm, mla_k_norm, gla_w_gate_f, gla_b_gate_f, gla_w_gate_b, gla_b_gate_b, gla_out_norm, w_router, router_bias, moe_w1, moe_w3, moe_w2):
    cond = jnp.concatenate([c, c_ctx[None, :], jnp.zeros((8 - NB - 1, D), F32)], axis=0)
    mods = ada_modulation(cond, w_ada, b_ada)
    xl, xc = x.reshape(NL, D), ctx.reshape(CTX_ROWS, D)
    c2, s2 = _rope_tables()
    w_router_p = jnp.concatenate([w_router, jnp.zeros((D, 128 - E), F32)], axis=1)
    w_in_b = _w_in_layout(w_in)
    w_out_b = w_out.astype(BF16)
    for l in range(DEPTH):
        mod = mods[l].reshape(8, 1, 6 * D)
        need_ctx = l < DEPTH - 1
        xl, xc_new = token_mixing_layer(xl, xc, mod, l, need_ctx, norm1[l], w_in_b, w_out_b, na_q_norm[l],
                                        na_k_norm[l], na_rpb[l], mla_qa_norm[l], mla_kva_norm[l], mla_w_uq[l],
                                        mla_w_ukv[l], mla_q_norm[l], mla_k_norm[l], gla_w_gate_f[l],
                                        gla_b_gate_f[l], gla_w_gate_b[l], gla_b_gate_b[l], gla_out_norm[l], c2, s2)
        xc = xc_new if need_ctx else xc
        xl, xc_new = moe_layer(xl, xc, mod, norm2[l], w_router_p, router_bias, moe_w1, moe_w3, moe_w2, l, need_ctx)
        xc = xc_new if need_ctx else xc
    return xl.reshape(NB, S, D)
```

```python
import functools

import numpy as np
import jax
import jax.numpy as jnp
from jax import lax
from jax.experimental import pallas as pl
from jax.experimental.pallas import tpu as pltpu

F32 = jnp.float32
BF16 = jnp.bfloat16

D = 4096
NB = 2
S = 4096
L = 256
DEPTH = 2
GW = 64
EPS = 1e-6
NL = NB * S
NT = NL + NB * L
NA_H, NA_D = 8, 128
NA_W = NA_H * NA_D
NA_R, NA_C = 8, 16
MLA_H = 12
MLA_QL, MLA_KVL = 1024, 512
MLA_NOPE, MLA_ROPE, MLA_V = 128, 64, 128
MLA_QK = MLA_NOPE + MLA_ROPE
MLA_W = MLA_H * MLA_V
GLA_H, GLA_DK, GLA_DV = 6, 128, 256
GLA_KW = GLA_H * GLA_DK
GLA_W = GLA_H * GLA_DV
GLA_RANK = 16
GLA_TAU = 16.0
CH = 128
E = 16
E_GROUPS = 4
E_PER = E // E_GROUPS
TOPK = 2
FF = 1024
ROPE_BASE = 10000.0

_COL_NAQ, _COL_NAK, _COL_NAV = 0, 1024, 2048
_COL_CQ, _COL_CKV = 3072, 4096
_COL_GQ, _COL_GK, _COL_GV, _COL_GG = 4608, 5376, 6144, 7680
_COL_KR = 9216
_COL_GA = 9472
PW = 9728

BLK = 256
NBLK_L = S // BLK
CTX_BLK0 = NL // BLK
MOE_R = 256
MOE_TF = 512


def _moe_blocks(n_tokens):
    return (n_tokens * TOPK + E * (MOE_R - 1)) // MOE_R


NEG = -1e30
V7X_VMEM_BYTES = 64 * 1024 * 1024
VMEM_LIMIT = V7X_VMEM_BYTES - 8 * 1024 * 1024


def _cp(sem, vmem=VMEM_LIMIT):
    return pltpu.CompilerParams(dimension_semantics=sem, vmem_limit_bytes=vmem)


def _dot(a, b):
    return jnp.dot(a, b, preferred_element_type=F32)


def _dot_nt(a, b):
    return lax.dot_general(a, b, (((1,), (1,)), ((), ())), preferred_element_type=F32)


def _dot_tn(a, b):
    return lax.dot_general(a, b, (((0,), (0,)), ((), ())), preferred_element_type=F32)


def _sigmoid(x):
    return 1.0 / (1.0 + jnp.exp(-x))


def _pack_pairs(lo, hi):
    lo_w = lax.bitcast_convert_type(lo.astype(BF16).astype(F32), jnp.uint32)
    hi_w = lax.bitcast_convert_type(hi.astype(BF16).astype(F32), jnp.uint32)
    return lax.shift_right_logical(lo_w, jnp.uint32(16)) | (hi_w & jnp.uint32(0xFFFF0000))


def _unpack_pairs(w):
    lo = lax.bitcast_convert_type(lax.shift_left(w, jnp.uint32(16)), F32)
    hi = lax.bitcast_convert_type(w & jnp.uint32(0xFFFF0000), F32)
    return lo, hi


def _row_group(i, tm):
    r0 = i * tm
    return jnp.where(r0 >= NL, 2, r0 // S)


def _ada_kernel(s_ref, w_ref, b_ref, o_ref):
    s = s_ref[...]
    s = s * _sigmoid(s)
    o_ref[...] = _dot(s.astype(BF16), w_ref[...].astype(BF16)) + b_ref[...]


def ada_modulation(cond, w_ada, b_ada):
    tn = 512
    n = 6 * D
    return pl.pallas_call(
        _ada_kernel,
        grid=(DEPTH, n // tn),
        in_specs=[
            pl.BlockSpec((8, D), lambda l, j: (0, 0)),
            pl.BlockSpec((None, D, tn), lambda l, j: (l, 0, j)),
            pl.BlockSpec((None, 1, tn), lambda l, j: (l, 0, j)),
        ],
        out_specs=pl.BlockSpec((None, 8, tn), lambda l, j: (l, 0, j)),
        out_shape=jax.ShapeDtypeStruct((DEPTH, 8, n), F32),
        compiler_params=_cp(("arbitrary", "arbitrary")),
        name="ada_modulation",
    )(cond, w_ada, b_ada.reshape(DEPTH, 1, n))


def _normmod(x, g, sh, sc):
    y = x * lax.rsqrt(jnp.mean(x * x, axis=-1, keepdims=True) + EPS) * g
    return y * (1.0 + sc) + sh


CTX_ROWS = NB * L
NLB = NL // CTX_ROWS


def _stream_block(i, xl_ref, xc_ref):
    return jnp.where(i < NLB, xl_ref[...], xc_ref[...])


def _normmod_kernel(xl_ref, xc_ref, g_ref, sh_ref, sc_ref, h_ref):
    x = _stream_block(pl.program_id(0), xl_ref, xc_ref)
    h_ref[...] = _normmod(x, g_ref[...], sh_ref[...], sc_ref[...]).astype(h_ref.dtype)


def _normmod_router_kernel(xl_ref, xc_ref, g_ref, sh_ref, sc_ref, wr_ref, h_ref, lg_ref):
    x = _stream_block(pl.program_id(0), xl_ref, xc_ref)
    h = _normmod(x, g_ref[...], sh_ref[...], sc_ref[...])
    h_ref[...] = _pack_pairs(h[:, :D // 2], h[:, D // 2:])
    lg_ref[...] = _dot(h.astype(BF16), wr_ref[...].astype(BF16))


def norm_modulate(xl, xc, gain, mod, k_shift, k_scale, out_dtype, w_router=None, n_rows=NT):
    tm = CTX_ROWS
    in_specs = [
        pl.BlockSpec((tm, D), lambda i: (jnp.minimum(i, NLB - 1), 0)),
        pl.BlockSpec((tm, D), lambda i: (0, 0)),
        pl.BlockSpec((1, D), lambda i: (0, 0)),
        pl.BlockSpec((None, 1, D), lambda i: (_row_group(i, tm), 0, k_shift)),
        pl.BlockSpec((None, 1, D), lambda i: (_row_group(i, tm), 0, k_scale)),
    ]
    if w_router is None:
        return pl.pallas_call(
            _normmod_kernel, grid=(n_rows // tm,), in_specs=in_specs,
            out_specs=pl.BlockSpec((tm, D), lambda i: (i, 0)),
            out_shape=jax.ShapeDtypeStruct((n_rows, D), out_dtype),
            compiler_params=_cp(("arbitrary",)), name="norm_modulate",
        )(xl, xc, gain, mod, mod)
    return pl.pallas_call(
        _normmod_router_kernel, grid=(n_rows // tm,),
        in_specs=in_specs + [pl.BlockSpec((D, 128), lambda i: (0, 0))],
        out_specs=[pl.BlockSpec((tm, D // 2), lambda i: (i, 0)), pl.BlockSpec((tm, 128), lambda i: (i, 0))],
        out_shape=[jax.ShapeDtypeStruct((n_rows, D // 2), jnp.uint32), jax.ShapeDtypeStruct((n_rows, 128), F32)],
        compiler_params=_cp(("arbitrary",)), name="norm_modulate_router",
    )(xl, xc, gain, mod, mod, w_router)


def _mm_kernel(a_ref, b_ref, o_ref):
    o_ref[...] = _dot_nt(a_ref[...], b_ref[...]).astype(o_ref.dtype)


def matmul(a, b, layer, tm, tn, out_dtype):
    m, k = a.shape
    n = b.shape[1]
    return pl.pallas_call(
        _mm_kernel,
        grid=(m // tm, n // tn),
        in_specs=[pl.BlockSpec((tm, k), lambda i, j: (i, 0)),
                  pl.BlockSpec((None, tn, k), lambda i, j: (layer, j, 0))],
        out_specs=pl.BlockSpec((tm, tn), lambda i, j: (i, j)),
        out_shape=jax.ShapeDtypeStruct((m, n), out_dtype),
        compiler_params=_cp(("arbitrary", "arbitrary")),
        name="matmul",
    )(a, b)


def _qblk(b, i):
    return jnp.where(i == 0, CTX_BLK0 + b, b * NBLK_L + i - 1)


def _rms_rows(x, g):
    return x * lax.rsqrt(jnp.mean(x * x, axis=-1, keepdims=True) + EPS) * g


NA_KROWS = 12
NA_KW = NA_KROWS * GW


NA_HG = 4
NA_GW = NA_HG * NA_D


def _na_kernel(tiles, q_ref, kl_ref, kc_ref, vl_ref, vc_ref, pair_ref, qn_ref, kn_ref, o_ref, ks_ref, bias_ref):
    i = pl.program_id(2)
    heads = [slice(NA_D * h, NA_D * (h + 1)) for h in range(NA_HG)]

    @pl.when(i == 0)
    def _():
        for hs in heads:
            ks_ref[0:L, hs] = _rms_rows(kc_ref[:, hs].astype(F32), kn_ref[...]).astype(BF16)
            ks_ref[L:, hs] = _rms_rows(kl_ref[:, hs].astype(F32), kn_ref[...]).astype(BF16)

    left, ok_l, ok_r = tiles
    lane = lax.broadcasted_iota(jnp.int32, (GW, 2 * GW), 1)
    for p, first_step in enumerate((1, 2, NBLK_L)):
        @pl.when(i == first_step)
        def _(p=p):
            for h in range(NA_HG):
                for a in range(BLK // GW):
                    for m in range(NA_KROWS // 2):
                        dst = (h, slice(GW * a, GW * (a + 1)), slice(2 * GW * m, 2 * GW * (m + 1)))
                        if not (ok_l[p, a, m] or ok_r[p, a, m]):
                            bias_ref[dst] = jnp.full((GW, 2 * GW), NEG, F32)
                            continue
                        t = pair_ref[h, int(left[p, a, m]) + 1]
                        if not ok_l[p, a, m]:
                            t = jnp.where(lane >= GW, t, NEG)
                        if not ok_r[p, a, m]:
                            t = jnp.where(lane < GW, t, NEG)
                        bias_ref[dst] = t

    def query(hs):
        return (_rms_rows(q_ref[:, hs].astype(F32), qn_ref[...]) * (NA_D ** -0.5)).astype(BF16)

    @pl.when(i == 0)
    def _():
        for hs in heads:
            s_c = _dot_nt(query(hs), ks_ref[0:L, hs])
            p = jnp.exp(s_c - jnp.max(s_c, axis=-1, keepdims=True))
            l = jnp.sum(p, axis=-1, keepdims=True)
            o_ref[:, hs] = (_dot(p.astype(BF16), vc_ref[:, hs]) / l).astype(o_ref.dtype)

    @pl.when(i > 0)
    def _():
        row0 = jnp.clip(4 * (i - 1) - 4, 0, GW - NA_KROWS)
        k0 = pl.multiple_of(row0 * GW, BLK)
        for h, hs in enumerate(heads):
            q = query(hs)
            s_c = _dot_nt(q, ks_ref[0:L, hs])
            s_l = _dot_nt(q, ks_ref[pl.ds(L + k0, NA_KW), hs]) + bias_ref[h]
            m = jnp.maximum(jnp.max(s_c, axis=-1, keepdims=True), jnp.max(s_l, axis=-1, keepdims=True))
            p_c = jnp.exp(s_c - m)
            p_l = jnp.exp(s_l - m)
            l = jnp.sum(p_c, axis=-1, keepdims=True) + jnp.sum(p_l, axis=-1, keepdims=True)
            o = _dot(p_l.astype(BF16), vl_ref[pl.ds(k0, NA_KW), hs]) + _dot(p_c.astype(BF16), vc_ref[:, hs])
            o_ref[:, hs] = (o / l).astype(o_ref.dtype)


def _na_bias_index():
    dr = np.zeros((3, BLK, NA_KW), np.int32)
    dc = np.zeros((3, BLK, NA_KW), np.int32)
    ok = np.zeros((3, BLK, NA_KW), bool)
    rows = S // GW
    for p, blk in enumerate((0, 5, NBLK_L - 1)):
        ks = int(np.clip(4 * blk - 4, 0, GW - NA_KROWS))
        r = 4 * blk + np.arange(BLK) // GW
        c = np.arange(BLK) % GW
        rk = ks + np.arange(NA_KW) // GW
        ck = np.arange(NA_KW) % GW
        r0 = np.clip(r - NA_R // 2, 0, rows - NA_R)
        ws = np.clip(c - NA_C // 2, 0, GW - NA_C)
        row_ok = (rk[None, :] >= r0[:, None]) & (rk[None, :] < r0[:, None] + NA_R)
        col_ok = (ck[None, :] >= ws[:, None]) & (ck[None, :] < ws[:, None] + NA_C)
        ok[p] = row_ok & col_ok
        dr[p] = np.clip(rk[None, :] - r[:, None] + NA_R - 1, 0, 2 * NA_R - 2)
        dc[p] = np.clip(ck[None, :] - c[:, None], -(NA_C - 1), NA_C - 1) + NA_C - 1
    return dr, dc, ok


NA_NDR = 2 * NA_R


def _na_bias_tiles():
    dr, _, ok = _na_bias_index()
    qr, kr = BLK // GW, NA_KROWS
    dr_t = dr.reshape(3, qr, GW, kr, GW)[:, :, 0, :, 0]
    ok_t = ok.reshape(3, qr, GW, kr, GW).any(axis=(2, 4))
    rk0 = np.array([int(np.clip(4 * blk - 4, 0, GW - NA_KROWS)) for blk in (0, 5, NBLK_L - 1)])
    r = np.array([4 * blk for blk in (0, 5, NBLK_L - 1)])[:, None, None] + np.arange(qr)[None, :, None]
    left = (rk0[:, None, None] + 2 * np.arange(kr // 2)[None, None, :]) - r + NA_R - 1
    assert np.all(dr_t[:, :, 0::2][ok_t[:, :, 0::2]] == left[ok_t[:, :, 0::2]])
    return left, ok_t[:, :, 0::2], ok_t[:, :, 1::2]


def _na_bias_pairs(rpb):
    _, dc, ok = _na_bias_index()
    qr, kr = BLK // GW, NA_KROWS
    dc_t = dc[0].reshape(qr, GW, kr, GW)[0, :, 0, :].reshape(GW * GW)
    col_ok = ok.reshape(3, qr, GW, kr, GW).any(axis=(0, 1, 3))
    oh_c = jnp.asarray(np.eye(2 * NA_C - 1, dtype=np.float32)[:, dc_t])
    t1 = jnp.einsum('hrd,dx->hrx', rpb.astype(F32), oh_c, precision=lax.Precision.HIGHEST)
    t1 = jnp.where(col_ok.reshape(1, 1, GW * GW), t1, NEG).reshape(NA_H, 2 * NA_R - 1, GW, GW)
    t1 = jnp.pad(t1, ((0, 0), (1, 1), (0, 0), (0, 0)))
    return jnp.concatenate([t1[:, :-1], t1[:, 1:]], axis=-1)


def na_attention(proj, rpb, qn, kn):
    pairs = _na_bias_pairs(rpb)

    cq, ck, cv = _COL_NAQ // NA_GW, _COL_NAK // NA_GW, _COL_NAV // NA_GW
    return pl.pallas_call(
        functools.partial(_na_kernel, _na_bias_tiles()),
        grid=(NB, NA_H // NA_HG, NBLK_L + 1),
        in_specs=[
            pl.BlockSpec((BLK, NA_GW), lambda b, h, i: (_qblk(b, i), cq + h)),
            pl.BlockSpec((S, NA_GW), lambda b, h, i: (b, ck + h)),
            pl.BlockSpec((L, NA_GW), lambda b, h, i: (CTX_BLK0 + b, ck + h)),
            pl.BlockSpec((S, NA_GW), lambda b, h, i: (b, cv + h)),
            pl.BlockSpec((L, NA_GW), lambda b, h, i: (CTX_BLK0 + b, cv + h)),
            pl.BlockSpec((NA_HG, NA_NDR, GW, 2 * GW), lambda b, h, i: (h, 0, 0, 0)),
            pl.BlockSpec((1, NA_D), lambda b, h, i: (0, 0)),
            pl.BlockSpec((1, NA_D), lambda b, h, i: (0, 0)),
        ],
        out_specs=pl.BlockSpec((BLK, NA_GW), lambda b, h, i: (_qblk(b, i), h)),
        out_shape=jax.ShapeDtypeStruct((NT, NA_W), BF16),
        scratch_shapes=[pltpu.VMEM((L + S, NA_GW), BF16), pltpu.VMEM((NA_HG, BLK, NA_KW), F32)],
        compiler_params=_cp(("arbitrary", "arbitrary", "arbitrary")),
        name="na_attention",
    )(proj, proj, proj, proj, proj, pairs, qn.reshape(1, NA_D), kn.reshape(1, NA_D))


MLA_HW = 256


def _half_mask(h, width=128):
    lane = lax.broadcasted_iota(jnp.int32, (1, width), 1)
    return (lane < 64) if h % 2 == 0 else (lane >= 64)


def _mla_q_kernel(cq_ref, w_ref, qa_ref, gn_ref, g2_ref, gs2_ref, c2_ref, s2_ref, q_ref):
    x = _rms_rows(cq_ref[...].astype(F32), qa_ref[...]).astype(BF16)
    y = _dot(x, w_ref[...])
    nw = MLA_H * MLA_NOPE
    gc = g2_ref[...] * c2_ref[...]
    gs = gs2_ref[...] * s2_ref[...]
    for p in range(MLA_H // 2):
        r1 = y[:, nw + 128 * p: nw + 128 * (p + 1)]
        r2 = y[:, nw + 768 + 128 * p: nw + 768 + 128 * (p + 1)]
        rot = r1 * gc + r2 * gs
        sq = r1 * r1
        for h in (2 * p, 2 * p + 1):
            msk = _half_mask(h)
            nope = y[:, 128 * h: 128 * (h + 1)]
            ss = jnp.sum(nope * nope, axis=-1, keepdims=True) + jnp.sum(
                jnp.where(msk, sq, 0.0), axis=-1, keepdims=True)
            inv = lax.rsqrt(ss / MLA_QK + EPS) * (MLA_QK ** -0.5)
            q_ref[h, :, 0:128] = (nope * gn_ref[...] * inv).astype(BF16)
            q_ref[h, :, 128:256] = (jnp.where(msk, rot, 0.0) * inv).astype(BF16)


def _mla_kv_kernel(ckv_ref, kr_ref, w_ref, kva_ref, gn_ref, g2_ref, gs2_ref, c2_ref, s2_ref,
                   k_ref, v_ref):
    x = _rms_rows(ckv_ref[...].astype(F32), kva_ref[...]).astype(BF16)
    y = _dot(x, w_ref[...])
    kr = kr_ref[...].astype(F32)
    r1 = kr[:, 0:128]
    r2 = kr[:, 128:256]
    rot = r1 * (g2_ref[...] * c2_ref[...]) + r2 * (gs2_ref[...] * s2_ref[...])
    ss_r = jnp.sum(jnp.where(_half_mask(0), r1 * r1, 0.0), axis=-1, keepdims=True)
    nw = MLA_H * MLA_NOPE
    lane0 = lax.broadcasted_iota(jnp.int32, (kr.shape[0], 128), 1) == 0
    for h in range(MLA_H):
        nope = y[:, 128 * h: 128 * (h + 1)]
        ss = jnp.sum(nope * nope, axis=-1, keepdims=True) + ss_r
        inv = lax.rsqrt(ss / MLA_QK + EPS)
        k_ref[h, :, 0:128] = (nope * gn_ref[...] * inv).astype(BF16)
        k_ref[h, :, 128:256] = (jnp.where(_half_mask(h), rot, 0.0) * inv).astype(BF16)
        v_ref[h, :, 0:128] = y[:, nw + 128 * h: nw + 128 * (h + 1)].astype(BF16)
        v_ref[h, :, 128:256] = jnp.where(lane0, 1.0, 0.0).astype(BF16)


def _rope_tables():
    t = jnp.arange(S, dtype=jnp.int32)
    pos = (t // GW, t % GW)
    nf = MLA_ROPE // 4
    inv = ROPE_BASE ** (-jnp.arange(nf, dtype=F32) / nf)
    cs, sn = [], []
    for ax in range(2):
        ang = pos[ax].astype(F32)[:, None] * inv[None, :]
        c, s = jnp.cos(ang), jnp.sin(ang)
        cs += [c, c]
        sn += [-s, s]
    c64 = jnp.tile(jnp.concatenate(cs, axis=1), (NB, 1))
    s64 = jnp.tile(jnp.concatenate(sn, axis=1), (NB, 1))
    c64 = jnp.concatenate([c64, jnp.ones((NB * L, MLA_ROPE), F32)], axis=0)
    s64 = jnp.concatenate([s64, jnp.zeros((NB * L, MLA_ROPE), F32)], axis=0)
    return jnp.tile(c64, (1, 2)), jnp.tile(s64, (1, 2))


_ROPE_SWAP = np.concatenate([np.arange(16, 32), np.arange(0, 16), np.arange(48, 64), np.arange(32, 48)])


def _rope_gains(g):
    gr = g[MLA_NOPE:]
    return (g[:MLA_NOPE].reshape(1, 128), jnp.tile(gr, 2).reshape(1, 128),
            jnp.tile(gr[_ROPE_SWAP], 2).reshape(1, 128))


def mla_q_prep(proj, w_uq_r, qa_g, qn_g, c2, s2):
    tm = 512
    gn, g2, gs2 = _rope_gains(qn_g)
    vec = lambda w: pl.BlockSpec((1, w), lambda i: (0, 0))
    return pl.pallas_call(
        _mla_q_kernel,
        grid=(NT // tm,),
        in_specs=[
            pl.BlockSpec((tm, MLA_QL), lambda i: (i, _COL_CQ // MLA_QL)),
            pl.BlockSpec((MLA_QL, 3072), lambda i: (0, 0)),
            vec(MLA_QL), vec(128), vec(128), vec(128),
            pl.BlockSpec((tm, 128), lambda i: (i, 0)),
            pl.BlockSpec((tm, 128), lambda i: (i, 0)),
        ],
        out_specs=pl.BlockSpec((MLA_H, tm, MLA_HW), lambda i: (0, i, 0)),
        out_shape=jax.ShapeDtypeStruct((MLA_H, NT, MLA_HW), BF16),
        compiler_params=_cp(("arbitrary",)),
        name="mla_q_prep",
    )(proj, w_uq_r, qa_g.reshape(1, MLA_QL), gn, g2, gs2, c2, s2)


def mla_kv_prep(proj, w_ukv_r, kva_g, kn_g, c2, s2):
    tm = 512
    gn, g2, gs2 = _rope_gains(kn_g)
    vec = lambda w: pl.BlockSpec((1, w), lambda i: (0, 0))
    return pl.pallas_call(
        _mla_kv_kernel,
        grid=(NT // tm,),
        in_specs=[
            pl.BlockSpec((tm, MLA_KVL), lambda i: (i, _COL_CKV // MLA_KVL)),
            pl.BlockSpec((tm, 256), lambda i: (i, _COL_KR // 256)),
            pl.BlockSpec((MLA_KVL, 3072), lambda i: (0, 0)),
            vec(MLA_KVL), vec(128), vec(128), vec(128),
            pl.BlockSpec((tm, 128), lambda i: (i, 0)),
            pl.BlockSpec((tm, 128), lambda i: (i, 0)),
        ],
        out_specs=[pl.BlockSpec((MLA_H, tm, MLA_HW), lambda i: (0, i, 0)),
                   pl.BlockSpec((MLA_H, tm, MLA_HW), lambda i: (0, i, 0))],
        out_shape=[jax.ShapeDtypeStruct((MLA_H, NT, MLA_HW), BF16),
                   jax.ShapeDtypeStruct((MLA_H, NT, MLA_HW), BF16)],
        compiler_params=_cp(("arbitrary",)),
        name="mla_kv_prep",
    )(proj, proj, w_ukv_r, kva_g.reshape(1, MLA_KVL), gn, g2, gs2, c2, s2)


MLA_TQ = 256


def _mla_attn_kernel(q_ref, kl_ref, kc_ref, vl_ref, vc_ref, o_ref, s0_ref, s1_ref, p0_ref, p1_ref):
    g = pl.program_id(0)

    @pl.when(g == 0)
    def _():
        s0_ref[...] = jnp.zeros_like(s0_ref)
        s1_ref[...] = jnp.zeros_like(s1_ref)
        p0_ref[...] = jnp.ones_like(p0_ref)
        p1_ref[...] = jnp.ones_like(p1_ref)

    n_kc = (L + S) // L
    rows = MLA_TQ // (n_kc - 1)

    def tie(x, dep):
        if dep is None:
            return x
        return jnp.concatenate([x[0:16, :] + dep, x[16:, :]], axis=0)

    def stages(s_new, s_old, p_new, p_old):
        acc, dep = None, None
        for c in range(n_kc):
            keys = slice(L * c, L * (c + 1))
            k = kc_ref[...] if c == 0 else kl_ref[L * (c - 1): L * c, :]
            v = vc_ref[...] if c == 0 else vl_ref[L * (c - 1): L * c, :]
            if c < n_kc - 1:
                r = slice(rows * c, rows * (c + 1))
                s = s_old[r, :]
                p = jnp.exp((s - jnp.max(s, axis=-1, keepdims=True)).astype(BF16))
                p_old[r, :] = p
                dep = (p[0:16, 0:MLA_HW].astype(F32) * 0.0).astype(BF16)
            part = _dot(tie(p_new[:, keys], dep), v)
            acc = part if acc is None else acc + part
            s_new[:, keys] = _dot_nt(tie(q_ref[...], dep), k)
        o_ref[...] = (acc[:, :MLA_V] / acc[:, MLA_V:MLA_V + 1]).astype(o_ref.dtype)

    @pl.when(g % 2 == 0)
    def _():
        stages(s0_ref, s1_ref, p0_ref, p1_ref)

    @pl.when(g % 2 == 1)
    def _():
        stages(s1_ref, s0_ref, p1_ref, p0_ref)


def _mla_ctx_kernel(q_ref, kc_ref, vc_ref, o_ref):
    s = _dot_nt(q_ref[...], kc_ref[...])
    p = jnp.exp((s - jnp.max(s, axis=-1, keepdims=True)).astype(BF16))
    acc = _dot(p, vc_ref[...])
    o_ref[...] = (acc[:, :MLA_V] / acc[:, MLA_V:MLA_V + 1]).astype(o_ref.dtype)


def mla_attention(q, k, v, need_ctx):
    nq = S // MLA_TQ
    n_steps = NB * MLA_H * nq

    def bhi(g):
        return g // (MLA_H * nq), (g // nq) % MLA_H, g % nq

    def cur(g):
        return bhi(jnp.minimum(g, n_steps - 1))

    def prev(g):
        return bhi(jnp.maximum(g - 2, 0))

    ob = pl.pallas_call(
        _mla_attn_kernel,
        grid=(n_steps + 2,),
        in_specs=[
            pl.BlockSpec((None, MLA_TQ, MLA_HW), lambda g: (cur(g)[1], cur(g)[0] * nq + cur(g)[2], 0)),
            pl.BlockSpec((None, S, MLA_HW), lambda g: (cur(g)[1], cur(g)[0], 0)),
            pl.BlockSpec((None, L, MLA_HW), lambda g: (cur(g)[1], CTX_BLK0 + cur(g)[0], 0)),
            pl.BlockSpec((None, S, MLA_HW), lambda g: (prev(g)[1], prev(g)[0], 0)),
            pl.BlockSpec((None, L, MLA_HW), lambda g: (prev(g)[1], CTX_BLK0 + prev(g)[0], 0)),
        ],
        out_specs=pl.BlockSpec((MLA_TQ, MLA_V), lambda g: (prev(g)[0] * nq + prev(g)[2], prev(g)[1])),
        out_shape=jax.ShapeDtypeStruct((NL, MLA_W), BF16),
        scratch_shapes=[pltpu.VMEM((MLA_TQ, L + S), F32), pltpu.VMEM((MLA_TQ, L + S), F32),
                        pltpu.VMEM((MLA_TQ, L + S), BF16), pltpu.VMEM((MLA_TQ, L + S), BF16)],
        compiler_params=_cp(("arbitrary",)),
        name="mla_attention",
    )(q, k, k, v, v)
    if not need_ctx:
        return ob, None
    ob_ctx = pl.pallas_call(
        _mla_ctx_kernel,
        grid=(NB, MLA_H),
        in_specs=[
            pl.BlockSpec((None, L, MLA_HW), lambda b, h: (h, CTX_BLK0 + b, 0)),
            pl.BlockSpec((None, L, MLA_HW), lambda b, h: (h, CTX_BLK0 + b, 0)),
            pl.BlockSpec((None, L, MLA_HW), lambda b, h: (h, CTX_BLK0 + b, 0)),
        ],
        out_specs=pl.BlockSpec((L, MLA_V), lambda b, h: (b, h)),
        out_shape=jax.ShapeDtypeStruct((NB * L, MLA_W), BF16),
        compiler_params=_cp(("arbitrary", "arbitrary")),
        name="mla_attention_ctx",
    )(q, k, v)
    return ob, ob_ctx


N_CH_C = L // CH
N_CH_L = S // CH
N_CH = N_CH_C + N_CH_L


def _gla_kernel(qf_ref, kf_ref, vf_ref, af_ref, qb_ref, kb_ref, vb_ref, ab_ref,
                wg_ref, bg_ref, of_ref, ob_ref, st_ref):
    t = pl.program_id(1)

    @pl.when(t == 0)
    def _():
        st_ref[...] = jnp.zeros_like(st_ref)

    row = lax.broadcasted_iota(jnp.int32, (CH, CH), 0)
    col = lax.broadcasted_iota(jnp.int32, (CH, CH), 1)
    dirs = (
        (qf_ref, kf_ref, vf_ref, af_ref, of_ref, col <= row, CH - 1),
        (qb_ref, kb_ref, vb_ref, ab_ref, ob_ref, col >= row, 0),
    )
    for d, (q_ref, k_ref, v_ref, a_ref, o_ref, keep, last) in enumerate(dirs):
        z = _dot(a_ref[...], wg_ref[d]) + bg_ref[d]
        g = (jnp.minimum(z, 0.0) - jnp.log(1.0 + jnp.exp(-jnp.abs(z)))) / GLA_TAU
        b = jnp.dot(keep.astype(F32), g, preferred_element_type=F32,
                    precision=lax.Precision.HIGHEST)
        b_end = b[last:last + 1, :]
        b_mid = b[CH // 2:CH // 2 + 1, :]
        q = q_ref[...].astype(F32) * (GLA_DK ** -0.5)
        k = k_ref[...].astype(F32)
        qa = (q * jnp.exp(b - b_mid)).astype(BF16)
        ka = (k * jnp.exp(b_mid - b)).astype(BF16)
        qd = (q * jnp.exp(b)).astype(BF16)
        ke = (k * jnp.exp(b_end - b)).astype(BF16)
        e_end = jnp.exp(b_end)
        v = v_ref[...]
        for h in range(GLA_H):
            ksl = slice(GLA_DK * h, GLA_DK * (h + 1))
            vsl = slice(GLA_DV * h, GLA_DV * (h + 1))
            att = jnp.where(keep, _dot_nt(qa[:, ksl], ka[:, ksl]), 0.0)
            st = st_ref[d, h]
            o = _dot_nt(qd[:, ksl], st.astype(BF16)) + _dot(att.astype(BF16), v[:, vsl])
            o_ref[:, vsl] = o.astype(o_ref.dtype)
            st_ref[d, h] = st * e_end[:, ksl] + _dot_tn(v[:, vsl], ke[:, ksl])


def gla_scan(proj, wg_f, bg_f, wg_b, bg_b):
    wg = jnp.zeros((2, 128, GLA_KW), F32)
    wg = wg.at[0, 0:GLA_RANK].set(wg_f).at[1, GLA_RANK:2 * GLA_RANK].set(wg_b).astype(BF16)
    bg = jnp.stack([bg_f, bg_b]).reshape(2, 1, GLA_KW)

    ctx0 = NL // CH

    def fwd(b, t):
        return jnp.where(t < N_CH_C, ctx0 + N_CH_C * b + t, N_CH_L * b + t - N_CH_C)

    def bwd(b, t):
        return jnp.where(t < N_CH_C, ctx0 + N_CH_C * b + N_CH_C - 1 - t, N_CH_L * b + N_CH - 1 - t)

    def specs(rowfn):
        return [
            pl.BlockSpec((CH, GLA_KW), lambda b, t: (rowfn(b, t), _COL_GQ // GLA_KW)),
            pl.BlockSpec((CH, GLA_KW), lambda b, t: (rowfn(b, t), _COL_GK // GLA_KW)),
            pl.BlockSpec((CH, GLA_W), lambda b, t: (rowfn(b, t), _COL_GV // GLA_W)),
            pl.BlockSpec((CH, 128), lambda b, t: (rowfn(b, t), _COL_GA // 128)),
        ]

    return pl.pallas_call(
        _gla_kernel,
        grid=(NB, N_CH),
        in_specs=specs(fwd) + specs(bwd) + [
            pl.BlockSpec((2, 128, GLA_KW), lambda b, t: (0, 0, 0)),
            pl.BlockSpec((2, 1, GLA_KW), lambda b, t: (0, 0, 0)),
        ],
        out_specs=[pl.BlockSpec((CH, GLA_W), lambda b, t: (fwd(b, t), 0)),
                   pl.BlockSpec((CH, GLA_W), lambda b, t: (bwd(b, t), 0))],
        out_shape=[jax.ShapeDtypeStruct((NT, GLA_W), BF16)] * 2,
        scratch_shapes=[pltpu.VMEM((2, GLA_H, GLA_DV, GLA_DK), F32)],
        compiler_params=_cp(("arbitrary", "arbitrary")),
        name="gla_scan",
    )(proj, proj, proj, proj, proj, proj, proj, proj, wg, bg)


def _gla_out_kernel(of_ref, ob_ref, g_ref, gn_ref, o_ref):
    for h in range(GLA_H):
        sl = slice(GLA_DV * h, GLA_DV * (h + 1))
        o = _rms_rows(of_ref[:, sl].astype(F32) + ob_ref[:, sl].astype(F32), gn_ref[...])
        gate = g_ref[:, sl].astype(F32)
        o_ref[:, sl] = (o * (gate * _sigmoid(gate))).astype(o_ref.dtype)


def gla_output(o_f, o_b, proj, on_g):
    tm = 512
    return pl.pallas_call(
        _gla_out_kernel,
        grid=(NT // tm,),
        in_specs=[
            pl.BlockSpec((tm, GLA_W), lambda i: (i, 0)),
            pl.BlockSpec((tm, GLA_W), lambda i: (i, 0)),
            pl.BlockSpec((tm, GLA_W), lambda i: (i, _COL_GG // GLA_W)),
            pl.BlockSpec((1, GLA_DV), lambda i: (0, 0)),
        ],
        out_specs=pl.BlockSpec((tm, GLA_W), lambda i: (i, 0)),
        out_shape=jax.ShapeDtypeStruct((NT, GLA_W), BF16),
        compiler_params=_cp(("arbitrary",)),
        name="gla_output",
    )(o_f, o_b, proj, on_g.reshape(1, GLA_DV))


def _out_proj_kernel(with_ctx, oa_ref, obl_ref, obc_ref, oc_ref, w_ref, xl_ref, xc_ref, g_ref, *o_refs):
    i = pl.program_id(1)
    a = jnp.concatenate([oa_ref[...], _stream_block(i, obl_ref, obc_ref), oc_ref[...]], axis=1)
    y = _stream_block(i, xl_ref, xc_ref) + g_ref[...] * _dot(a, w_ref[...])
    if not with_ctx:
        o_refs[0][...] = y
        return

    @pl.when(i < NLB)
    def _():
        o_refs[0][...] = y

    @pl.when(i == NLB)
    def _():
        o_refs[1][...] = y


def out_projection(oa, ob, ob_ctx, oc, w_out, layer, xl, xc, mod, k_gate, with_ctx):
    tm, tn = CTX_ROWS, 1024
    nj = D // tn
    lat = lambda j, i: (jnp.minimum(i, NLB - 1), j)
    out_specs = [pl.BlockSpec((tm, tn), lat)]
    out_shape = [jax.ShapeDtypeStruct((NL, D), F32)]
    if with_ctx:
        out_specs.append(pl.BlockSpec((tm, tn), lambda j, i: (0, j)))
        out_shape.append(jax.ShapeDtypeStruct((CTX_ROWS, D), F32))
    res = pl.pallas_call(
        functools.partial(_out_proj_kernel, with_ctx),
        grid=(nj, NLB + (1 if with_ctx else 0)),
        in_specs=[
            pl.BlockSpec((tm, NA_W), lambda j, i: (i, 0)),
            pl.BlockSpec((tm, MLA_W), lambda j, i: (jnp.minimum(i, NLB - 1), 0)),
            pl.BlockSpec((tm, MLA_W), lambda j, i: (0, 0)),
            pl.BlockSpec((tm, GLA_W), lambda j, i: (i, 0)),
            pl.BlockSpec((None, D, tn), lambda j, i: (layer, 0, j)),
            pl.BlockSpec((tm, tn), lat),
            pl.BlockSpec((tm, tn), lambda j, i: (0, j)),
            pl.BlockSpec((None, 1, tn), lambda j, i: (_row_group(i, tm), 0, k_gate * nj + j)),
        ],
        out_specs=out_specs,
        out_shape=out_shape,
        compiler_params=_cp(("arbitrary", "arbitrary")),
        name="out_projection",
    )(oa, ob, ob_ctx if with_ctx else ob, oc, w_out, xl, xc, mod)
    return (res[0], res[1]) if with_ctx else (res[0], None)


def _route(logits, router_bias):
    n = logits.shape[0]
    scores = jax.nn.sigmoid(logits)
    grouped = (scores + router_bias.astype(F32)).reshape(n, E_GROUPS, E_PER)

    def top2(a):
        idx = lax.broadcasted_iota(jnp.int32, a.shape, a.ndim - 1)
        i1 = jnp.argmax(a, axis=-1).astype(jnp.int32)
        rest = jnp.where(idx == i1[..., None], -jnp.inf, a)
        i2 = jnp.argmax(rest, axis=-1).astype(jnp.int32)
        return jnp.max(a, axis=-1), jnp.max(rest, axis=-1), i1, i2

    m1, m2, _, _ = top2(grouped)
    grp = jnp.argmax(m1 + m2, axis=-1).astype(jnp.int32)
    gsel = lax.broadcasted_iota(jnp.int32, (n, E_GROUPS, E_PER), 1) == grp[:, None, None]
    in_group = jnp.sum(jnp.where(gsel, grouped, 0.0), axis=1)
    _, _, l1, l2 = top2(in_group)
    expert_idx = grp[:, None] * E_PER + jnp.stack([l1, l2], axis=-1)
    esel = lax.broadcasted_iota(jnp.int32, (n, TOPK, E), 2) == expert_idx[:, :, None]
    w = jnp.sum(jnp.where(esel, scores[:, None, :], 0.0), axis=-1)
    return expert_idx, w / jnp.sum(w, axis=-1, keepdims=True)


def _dispatch_plan(expert_idx):
    nk = expert_idx.shape[0] * TOPK
    nb = _moe_blocks(expert_idx.shape[0])
    flat_e = expert_idx.reshape(nk)
    onehot = (flat_e[:, None] == jnp.arange(E, dtype=jnp.int32)[None, :]).astype(jnp.int32)
    csum = jnp.cumsum(onehot, axis=0)
    counts = csum[-1]
    rank = jnp.take_along_axis(csum, flat_e[:, None], axis=1)[:, 0] - 1
    padded = (counts + MOE_R - 1) // MOE_R * MOE_R
    pad_end = jnp.cumsum(padded)
    pad_start = pad_end - padded
    dest = (pad_start[flat_e] + rank).astype(jnp.int32)
    row_token = jnp.zeros((nb * MOE_R,), jnp.int32).at[dest].set(jnp.arange(nk, dtype=jnp.int32) // TOPK)
    blk0 = jnp.arange(nb, dtype=jnp.int32) * MOE_R
    block_expert = jnp.minimum(jnp.sum((pad_end[None, :] <= blk0[:, None]).astype(jnp.int32), axis=1), E - 1)
    n_active = (pad_end[-1] // MOE_R).astype(jnp.int32).reshape(1)
    n_valid = jnp.clip(counts[block_expert] - (blk0 - pad_start[block_expert]), 0, MOE_R).astype(jnp.int32)
    bidx = jnp.arange(nb, dtype=jnp.int32)
    prev_e = jnp.concatenate([jnp.full((1,), -1, jnp.int32), block_expert[:-1]])
    first = jnp.logical_and(bidx < n_active[0], block_expert != prev_e)
    later = jnp.where(first, bidx, nb)
    nxt_blk = jnp.concatenate([lax.cummin(later[::-1])[::-1][1:], jnp.full((1,), nb, jnp.int32)])
    nxt_e = jnp.where(nxt_blk < nb, block_expert[jnp.minimum(nxt_blk, nb - 1)], -1).astype(jnp.int32)
    plan = (block_expert, n_active, first.astype(jnp.int32), nxt_e)
    return dest, row_token, n_valid, plan


DH = D // 2


DMA_UNROLL = 8


def _for_rows(n, body):
    full = n // DMA_UNROLL

    def group(t, c):
        for u in range(DMA_UNROLL):
            body(t * DMA_UNROLL + u)
        return c

    def single(r, c):
        body(r)
        return c

    lax.fori_loop(0, full, group, 0)
    lax.fori_loop(full * DMA_UNROLL, n, single, 0)


def _gather_rows_kernel(tok_ref, nv_ref, h_ref, o_ref, sem):
    i = pl.program_id(0)
    base = i * MOE_R
    nv = nv_ref[i]

    @pl.when(nv < MOE_R)
    def _():
        o_ref[...] = jnp.zeros_like(o_ref)

    def copy(r):
        return pltpu.make_async_copy(h_ref.at[pl.ds(tok_ref[base + r], 1)], o_ref.at[pl.ds(r, 1)], sem)

    _for_rows(nv, lambda r: copy(r).start())
    _for_rows(nv, lambda r: copy(r).wait())


def gather_rows(row_token, n_valid, h):
    nb = n_valid.shape[0]
    return pl.pallas_call(
        _gather_rows_kernel,
        grid_spec=pltpu.PrefetchScalarGridSpec(
            num_scalar_prefetch=2,
            grid=(nb,),
            in_specs=[pl.BlockSpec(memory_space=pl.ANY)],
            out_specs=pl.BlockSpec((MOE_R, DH), lambda i, tok, nv: (i, 0)),
            scratch_shapes=[pltpu.SemaphoreType.DMA(())],
        ),
        out_shape=jax.ShapeDtypeStruct((nb * MOE_R, DH), jnp.uint32),
        compiler_params=_cp(("arbitrary",)),
        name="moe_gather_rows",
    )(row_token, n_valid, h)


def _stage_weights(layer, n_tiles, width, be_ref, first_ref, nxt_ref, active, mats):
    t = pl.program_id(0)
    i = pl.program_id(1)

    def copies(e, tt):
        cols = pl.ds(pl.multiple_of(tt * width, width), width)
        return [pltpu.make_async_copy(w.at[layer, e, :, cols], st, sm) for w, st, _, sm in mats]

    @pl.when(jnp.logical_and(active, first_ref[i] == 1))
    def _():
        @pl.when(jnp.logical_and(t == 0, i == 0))
        def _():
            for c in copies(be_ref[0], 0):
                c.start()

        for c in copies(be_ref[i], t):
            c.wait()
        for _, st, wb, _ in mats:
            wb[...] = st[...].astype(BF16)

        @pl.when(nxt_ref[i] >= 0)
        def _():
            for c in copies(nxt_ref[i], t):
                c.start()

        @pl.when(jnp.logical_and(nxt_ref[i] < 0, t + 1 < n_tiles))
        def _():
            for c in copies(be_ref[0], t + 1):
                c.start()


def _ffn_up_kernel(layer, be_ref, na_ref, first_ref, nxt_ref, x_ref, w1_ref, w3_ref, o_ref,
                   st1_ref, st3_ref, w1b_ref, w3b_ref, sem):
    i = pl.program_id(1)
    active = i < na_ref[0]
    _stage_weights(layer, FF // MOE_TF, MOE_TF, be_ref, first_ref, nxt_ref, active,
                   [(w1_ref, st1_ref, w1b_ref, sem.at[0]), (w3_ref, st3_ref, w3b_ref, sem.at[1])])

    @pl.when(active)
    def _():
        lo, hi = _unpack_pairs(x_ref[...])
        lo, hi = lo.astype(BF16), hi.astype(BF16)
        a = _dot(lo, w1b_ref[0:DH, :]) + _dot(hi, w1b_ref[DH:, :])
        b = _dot(lo, w3b_ref[0:DH, :]) + _dot(hi, w3b_ref[DH:, :])
        o_ref[...] = (a * _sigmoid(a) * b).astype(o_ref.dtype)

    @pl.when(jnp.logical_not(active))
    def _():
        o_ref[...] = jnp.zeros_like(o_ref)


def ffn_up(plan, xb, w1, w3, layer):
    nb = plan[0].shape[0]
    last = lambda i, na: jnp.minimum(i, na[0] - 1)
    return pl.pallas_call(
        functools.partial(_ffn_up_kernel, layer),
        grid_spec=pltpu.PrefetchScalarGridSpec(
            num_scalar_prefetch=4,
            grid=(FF // MOE_TF, nb),
            in_specs=[
                pl.BlockSpec((MOE_R, DH), lambda j, i, be, na, fi, nx: (last(i, na), 0)),
                pl.BlockSpec(memory_space=pl.ANY),
                pl.BlockSpec(memory_space=pl.ANY),
            ],
            out_specs=pl.BlockSpec((MOE_R, MOE_TF), lambda j, i, be, na, fi, nx: (i, j)),
            scratch_shapes=[pltpu.VMEM((D, MOE_TF), F32), pltpu.VMEM((D, MOE_TF), F32),
                            pltpu.VMEM((D, MOE_TF), BF16), pltpu.VMEM((D, MOE_TF), BF16),
                            pltpu.SemaphoreType.DMA((2,))],
        ),
        out_shape=jax.ShapeDtypeStruct((nb * MOE_R, FF), BF16),
        compiler_params=_cp(("arbitrary", "arbitrary")),
        name="moe_ffn_up",
    )(*plan, xb, w1, w3)


MOE_TN = 2048


def _ffn_down_kernel(layer, be_ref, na_ref, first_ref, nxt_ref, h_ref, w2_ref, o_ref, st2_ref, w2b_ref, sem):
    i = pl.program_id(1)
    active = i < na_ref[0]
    _stage_weights(layer, D // MOE_TN, MOE_TN, be_ref, first_ref, nxt_ref, active,
                   [(w2_ref, st2_ref, w2b_ref, sem.at[0])])

    @pl.when(active)
    def _():
        y = _dot(h_ref[...], w2b_ref[...])
        o_ref[...] = _pack_pairs(y[:, :MOE_TN // 2], y[:, MOE_TN // 2:])

    @pl.when(jnp.logical_not(active))
    def _():
        o_ref[...] = jnp.zeros_like(o_ref)


def ffn_down(plan, hmid, w2, layer):
    nb = plan[0].shape[0]
    last = lambda i, na: jnp.minimum(i, na[0] - 1)
    return pl.pallas_call(
        functools.partial(_ffn_down_kernel, layer),
        grid_spec=pltpu.PrefetchScalarGridSpec(
            num_scalar_prefetch=4,
            grid=(D // MOE_TN, nb),
            in_specs=[
                pl.BlockSpec((MOE_R, FF), lambda n, i, be, na, fi, nx: (last(i, na), 0)),
                pl.BlockSpec(memory_space=pl.ANY),
            ],
            out_specs=pl.BlockSpec((MOE_R, MOE_TN // 2), lambda n, i, be, na, fi, nx: (i, n)),
            scratch_shapes=[pltpu.VMEM((FF, MOE_TN), F32), pltpu.VMEM((FF, MOE_TN), BF16),
                            pltpu.SemaphoreType.DMA((1,))],
        ),
        out_shape=jax.ShapeDtypeStruct((nb * MOE_R, DH), jnp.uint32),
        compiler_params=_cp(("arbitrary", "arbitrary")),
        name="moe_ffn_down",
    )(*plan, hmid, w2)


CMB_T = 256


def _combine_kernel(with_ctx, pos_ref, yb_ref, gate_ref, xl_ref, xc_ref, g_ref, *rest):
    o_refs, (buf_ref, sem) = rest[:-2], rest[-2:]
    i = pl.program_id(0)
    n_lat = NL // CMB_T
    base = i * CMB_T

    def copy(r, k):
        p = pos_ref[(base + r) * TOPK + k]
        return pltpu.make_async_copy(yb_ref.at[pl.ds(p, 1)], buf_ref.at[k, pl.ds(r, 1)], sem.at[k])

    def start(r, c):
        copy(r, 0).start()
        copy(r, 1).start()
        return c

    def wait(r, c):
        copy(r, 0).wait()
        copy(r, 1).wait()
        return c

    lax.fori_loop(0, CMB_T, start, 0, unroll=DMA_UNROLL)
    lax.fori_loop(0, CMB_T, wait, 0, unroll=DMA_UNROLL)
    gate = gate_ref[...]
    g0, g1 = gate[:, 0:1], gate[:, 1:2]
    half = MOE_TN // 2
    def emit(x_ref, o_ref):
        for n in range(D // MOE_TN):
            lo0, hi0 = _unpack_pairs(buf_ref[0, :, half * n: half * (n + 1)])
            lo1, hi1 = _unpack_pairs(buf_ref[1, :, half * n: half * (n + 1)])
            for part, y in enumerate((lo0 * g0 + lo1 * g1, hi0 * g0 + hi1 * g1)):
                sl = slice(MOE_TN * n + half * part, MOE_TN * n + half * (part + 1))
                o_ref[:, sl] = x_ref[:, sl] + g_ref[:, sl] * y

    if not with_ctx:
        emit(xl_ref, o_refs[0])
        return

    @pl.when(i < n_lat)
    def _():
        emit(xl_ref, o_refs[0])

    @pl.when(i >= n_lat)
    def _():
        emit(xc_ref, o_refs[1])


def moe_combine(dest, yb, gates, xl, xc, mod, k_gate, with_ctx):
    n_lat = NL // CMB_T
    lat = lambda i, pos: (jnp.minimum(i, n_lat - 1), 0)
    ctx = lambda i, pos: (jnp.maximum(i - n_lat, 0), 0)
    out_specs = [pl.BlockSpec((CMB_T, D), lat)]
    out_shape = [jax.ShapeDtypeStruct((NL, D), F32)]
    if with_ctx:
        out_specs.append(pl.BlockSpec((CMB_T, D), ctx))
        out_shape.append(jax.ShapeDtypeStruct((CTX_ROWS, D), F32))
    res = pl.pallas_call(
        functools.partial(_combine_kernel, with_ctx),
        grid_spec=pltpu.PrefetchScalarGridSpec(
            num_scalar_prefetch=1,
            grid=((NT if with_ctx else NL) // CMB_T,),
            in_specs=[
                pl.BlockSpec(memory_space=pl.ANY),
                pl.BlockSpec((CMB_T, TOPK), lambda i, pos: (i, 0)),
                pl.BlockSpec((CMB_T, D), lat),
                pl.BlockSpec((CMB_T, D), ctx),
                pl.BlockSpec((None, 1, D), lambda i, pos: (_row_group(i, CMB_T), 0, k_gate)),
            ],
            out_specs=out_specs,
            scratch_shapes=[pltpu.VMEM((TOPK, CMB_T, DH), jnp.uint32), pltpu.SemaphoreType.DMA((TOPK,))],
        ),
        out_shape=out_shape,
        compiler_params=_cp(("arbitrary",)),
        name="moe_combine",
    )(dest, yb, gates, xl, xc, mod)
    return (res[0], res[1]) if with_ctx else (res[0], None)


IN_W = 9312


W_RB = 512
W_NB = PW // W_RB
_W_SRC_KR = 4608
_W_SRC_GA = 9280


def _w_in_layout_kernel(w_ref, o_ref, buf_ref, sem):
    g = pl.program_id(0)

    def run(step, slot, op):
        l, k = step // W_NB, step % W_NB

        @pl.when(k < W_NB - 1)
        def _():
            r0 = pl.multiple_of(jnp.where(k < _W_SRC_KR // W_RB, k * W_RB, k * W_RB + MLA_ROPE), MLA_ROPE)
            op(pltpu.make_async_copy(w_ref.at[l, pl.ds(r0, W_RB)], buf_ref.at[slot], sem.at[slot]))

        @pl.when(k == W_NB - 1)
        def _():
            op(pltpu.make_async_copy(w_ref.at[l, pl.ds(_W_SRC_KR, MLA_ROPE)],
                                     buf_ref.at[slot, pl.ds(0, MLA_ROPE)], sem.at[slot]))
            op(pltpu.make_async_copy(w_ref.at[l, pl.ds(_W_SRC_GA, 2 * GLA_RANK)],
                                     buf_ref.at[slot, pl.ds(MLA_ROPE, 2 * GLA_RANK)], sem.at[slot]))

    start = lambda c: c.start()
    wait = lambda c: c.wait()

    @pl.when(g == 0)
    def _():
        run(0, 0, start)

    @pl.when(g + 1 < DEPTH * W_NB)
    def _():
        run(g + 1, (g + 1) % 2, start)

    run(g, g % 2, wait)
    k = g % W_NB

    @pl.when(k < W_NB - 1)
    def _():
        o_ref[...] = buf_ref[g % 2].astype(BF16)

    @pl.when(k == W_NB - 1)
    def _():
        x = buf_ref[g % 2, 0:MLA_ROPE, :]
        xs = jnp.concatenate([x[16:32], x[0:16], x[48:64], x[32:48]], axis=0)
        ga = buf_ref[g % 2, MLA_ROPE:MLA_ROPE + 2 * GLA_RANK, :]
        o_ref[...] = jnp.concatenate(
            [x, x, xs, xs, ga, jnp.zeros((W_RB - 4 * MLA_ROPE - 2 * GLA_RANK, D), F32)], axis=0).astype(BF16)


def _w_in_layout(w):
    w_t = jnp.swapaxes(w, 1, 2)
    return pl.pallas_call(
        _w_in_layout_kernel,
        grid=(DEPTH * W_NB,),
        in_specs=[pl.BlockSpec(memory_space=pl.ANY)],
        out_specs=pl.BlockSpec((None, W_RB, D), lambda g: (g // W_NB, g % W_NB, 0)),
        out_shape=jax.ShapeDtypeStruct((DEPTH, PW, D), BF16),
        scratch_shapes=[pltpu.VMEM((2, W_RB, D), F32), pltpu.SemaphoreType.DMA((2,))],
        compiler_params=_cp(("arbitrary",)),
        name="w_in_layout",
    )(w_t)


def _w_uq_cols():
    nope = [MLA_QK * h + j for h in range(MLA_H) for j in range(MLA_NOPE)]
    rope = [MLA_QK * h + MLA_NOPE + d for h in range(MLA_H) for d in range(MLA_ROPE)]
    rope_s = [MLA_QK * h + MLA_NOPE + int(d) for h in range(MLA_H) for d in _ROPE_SWAP]
    return np.array(nope + rope + rope_s, np.int32)


def _w_ukv_cols():
    kn = [(MLA_NOPE + MLA_V) * h + j for h in range(MLA_H) for j in range(MLA_NOPE)]
    vv = [(MLA_NOPE + MLA_V) * h + MLA_NOPE + j for h in range(MLA_H) for j in range(MLA_V)]
    return np.array(kn + vv, np.int32)


def token_mixing_layer(xl, xc, mod, layer, need_ctx, norm1, w_in_b, w_out_b, na_qn, na_kn, na_rpb, mla_qa, mla_kva,
                       w_uq, w_ukv, mla_qn, mla_kn, gwf, gbf, gwb, gbb, gla_on, c2, s2):
    h = norm_modulate(xl, xc, norm1.reshape(1, D), mod, 0, 1, BF16)
    proj = matmul(h, w_in_b, layer, NT // 4, 512, BF16)
    oa = na_attention(proj, na_rpb, na_qn, na_kn)
    q = mla_q_prep(proj, w_uq[:, _w_uq_cols()].astype(BF16), mla_qa, mla_qn, c2, s2)
    k, v = mla_kv_prep(proj, w_ukv[:, _w_ukv_cols()].astype(BF16), mla_kva, mla_kn, c2, s2)
    ob, ob_ctx = mla_attention(q, k, v, need_ctx)
    o_f, o_b = gla_scan(proj, gwf, gbf, gwb, gbb)
    oc = gla_output(o_f, o_b, proj, gla_on)
    return out_projection(oa, ob, ob_ctx, oc, w_out_b, layer, xl, xc, mod, 2, need_ctx)


def moe_layer(xl, xc, mod, norm2, w_router_p, router_bias, w1, w3, w2, layer, need_ctx):
    h, logits = norm_modulate(xl, xc, norm2.reshape(1, D), mod, 3, 4, None, w_router_p,
                              NT if need_ctx else NL)
    expert_idx, gates = _route(logits[:, :E], router_bias)
    dest, row_token, n_valid, plan = _dispatch_plan(expert_idx)
    xb = gather_rows(row_token, n_valid, h)
    hmid = ffn_up(plan, xb, w1, w3, layer)
    yb = ffn_down(plan, hmid, w2, layer)
    return moe_combine(dest, yb, gates, xl, xc, mod, 5, need_ctx)


def kernel(x, c, ctx, c_ctx, w_ada, b_ada, norm1, norm2, w_in, w_out, na_q_norm, na_k_norm, na_rpb, mla_qa_norm, mla_kva_norm, mla_w_uq, mla_w_ukv, mla_q_norm, mla_k_norm, gla_w_gate_f, gla_b_gate_f, gla_w_gate_b, gla_b_gate_b, gla_out_norm, w_router, router_bias, moe_w1, moe_w3, moe_w2):
    cond = jnp.concatenate([c, c_ctx[None, :], jnp.zeros((8 - NB - 1, D), F32)], axis=0)
    mods = ada_modulation(cond, w_ada, b_ada)
    xl, xc = x.reshape(NL, D), ctx.reshape(CTX_ROWS, D)
    c2, s2 = _rope_tables()
    w_router_p = jnp.concatenate([w_router, jnp.zeros((D, 128 - E), F32)], axis=1)
    w_in_b = _w_in_layout(w_in)
    w_out_b = w_out.astype(BF16)
    for l in range(DEPTH):
        mod = mods[l].reshape(8, 1, 6 * D)
        need_ctx = l < DEPTH - 1
        xl, xc_new = token_mixing_layer(xl, xc, mod, l, need_ctx, norm1[l], w_in_b, w_out_b, na_q_norm[l],
                                        na_k_norm[l], na_rpb[l], mla_qa_norm[l], mla_kva_norm[l], mla_w_uq[l],
                                        mla_w_ukv[l], mla_q_norm[l], mla_k_norm[l], gla_w_gate_f[l],
                                        gla_b_gate_f[l], gla_w_gate_b[l], gla_b_gate_b[l], gla_out_norm[l], c2, s2)
        xc = xc_new if need_ctx else xc
        xl, xc_new = moe_layer(xl, xc, mod, norm2[l], w_router_p, router_bias, moe_w1, moe_w3, moe_w2, l, need_ctx)
        xc = xc_new if need_ctx else xc
    return xl.reshape(NB, S, D)
```

```python
import functools

import numpy as np
import jax
import jax.numpy as jnp
from jax import lax
from jax.experimental import pallas as pl
from jax.experimental.pallas import tpu as pltpu

F32 = jnp.float32
BF16 = jnp.bfloat16

D = 4096
NB = 2
S = 4096
L = 256
DEPTH = 2
GW = 64
EPS = 1e-6
NL = NB * S
NT = NL + NB * L
NA_H, NA_D = 8, 128
NA_W = NA_H * NA_D
NA_R, NA_C = 8, 16
MLA_H = 12
MLA_QL, MLA_KVL = 1024, 512
MLA_NOPE, MLA_ROPE, MLA_V = 128, 64, 128
MLA_QK = MLA_NOPE + MLA_ROPE
MLA_W = MLA_H * MLA_V
GLA_H, GLA_DK, GLA_DV = 6, 128, 256
GLA_KW = GLA_H * GLA_DK
GLA_W = GLA_H * GLA_DV
GLA_RANK = 16
GLA_TAU = 16.0
CH = 128
E = 16
E_GROUPS = 4
E_PER = E // E_GROUPS
TOPK = 2
FF = 1024
ROPE_BASE = 10000.0

_COL_NAQ, _COL_NAK, _COL_NAV = 0, 1024, 2048
_COL_CQ, _COL_CKV = 3072, 4096
_COL_GQ, _COL_GK, _COL_GV, _COL_GG = 4608, 5376, 6144, 7680
_COL_KR = 9216
_COL_GA = 9472
PW = 9728

BLK = 256
NBLK_L = S // BLK
CTX_BLK0 = NL // BLK
MOE_R = 256
MOE_TF = 512


def _moe_blocks(n_tokens):
    return (n_tokens * TOPK + E * (MOE_R - 1)) // MOE_R


NEG = -1e30
V7X_VMEM_BYTES = 64 * 1024 * 1024
VMEM_LIMIT = V7X_VMEM_BYTES - 8 * 1024 * 1024


def _cp(sem, vmem=VMEM_LIMIT):
    return pltpu.CompilerParams(dimension_semantics=sem, vmem_limit_bytes=vmem)


def _dot(a, b):
    return jnp.dot(a, b, preferred_element_type=F32)


def _dot_nt(a, b):
    return lax.dot_general(a, b, (((1,), (1,)), ((), ())), preferred_element_type=F32)


def _dot_tn(a, b):
    return lax.dot_general(a, b, (((0,), (0,)), ((), ())), preferred_element_type=F32)


def _sigmoid(x):
    return 1.0 / (1.0 + jnp.exp(-x))


def _pack_pairs(lo, hi):
    lo_w = lax.bitcast_convert_type(lo.astype(BF16).astype(F32), jnp.uint32)
    hi_w = lax.bitcast_convert_type(hi.astype(BF16).astype(F32), jnp.uint32)
    return lax.shift_right_logical(lo_w, jnp.uint32(16)) | (hi_w & jnp.uint32(0xFFFF0000))


def _unpack_pairs(w):
    lo = lax.bitcast_convert_type(lax.shift_left(w, jnp.uint32(16)), F32)
    hi = lax.bitcast_convert_type(w & jnp.uint32(0xFFFF0000), F32)
    return lo, hi


def _row_group(i, tm):
    r0 = i * tm
    return jnp.where(r0 >= NL, 2, r0 // S)


def _ada_kernel(s_ref, w_ref, b_ref, o_ref):
    s = s_ref[...]
    s = s * _sigmoid(s)
    o_ref[...] = _dot(s.astype(BF16), w_ref[...].astype(BF16)) + b_ref[...]


def ada_modulation(cond, w_ada, b_ada):
    tn = 512
    n = 6 * D
    return pl.pallas_call(
        _ada_kernel,
        grid=(DEPTH, n // tn),
        in_specs=[
            pl.BlockSpec((8, D), lambda l, j: (0, 0)),
            pl.BlockSpec((None, D, tn), lambda l, j: (l, 0, j)),
            pl.BlockSpec((None, 1, tn), lambda l, j: (l, 0, j)),
        ],
        out_specs=pl.BlockSpec((None, 8, tn), lambda l, j: (l, 0, j)),
        out_shape=jax.ShapeDtypeStruct((DEPTH, 8, n), F32),
        compiler_params=_cp(("arbitrary", "arbitrary")),
        name="ada_modulation",
    )(cond, w_ada, b_ada.reshape(DEPTH, 1, n))


def _normmod(x, g, sh, sc):
    y = x * lax.rsqrt(jnp.mean(x * x, axis=-1, keepdims=True) + EPS) * g
    return y * (1.0 + sc) + sh


CTX_ROWS = NB * L
NLB = NL // CTX_ROWS


def _stream_block(i, xl_ref, xc_ref):
    return jnp.where(i < NLB, xl_ref[...], xc_ref[...])


def _normmod_kernel(xl_ref, xc_ref, g_ref, sh_ref, sc_ref, h_ref):
    x = _stream_block(pl.program_id(0), xl_ref, xc_ref)
    h_ref[...] = _normmod(x, g_ref[...], sh_ref[...], sc_ref[...]).astype(h_ref.dtype)


def _normmod_router_kernel(xl_ref, xc_ref, g_ref, sh_ref, sc_ref, wr_ref, h_ref, lg_ref):
    x = _stream_block(pl.program_id(0), xl_ref, xc_ref)
    h = _normmod(x, g_ref[...], sh_ref[...], sc_ref[...])
    h_ref[...] = _pack_pairs(h[:, :D // 2], h[:, D // 2:])
    lg_ref[...] = _dot(h.astype(BF16), wr_ref[...].astype(BF16))


def norm_modulate(xl, xc, gain, mod, k_shift, k_scale, out_dtype, w_router=None, n_rows=NT):
    tm = CTX_ROWS
    in_specs = [
        pl.BlockSpec((tm, D), lambda i: (jnp.minimum(i, NLB - 1), 0)),
        pl.BlockSpec((tm, D), lambda i: (0, 0)),
        pl.BlockSpec((1, D), lambda i: (0, 0)),
        pl.BlockSpec((None, 1, D), lambda i: (_row_group(i, tm), 0, k_shift)),
        pl.BlockSpec((None, 1, D), lambda i: (_row_group(i, tm), 0, k_scale)),
    ]
    if w_router is None:
        return pl.pallas_call(
            _normmod_kernel, grid=(n_rows // tm,), in_specs=in_specs,
            out_specs=pl.BlockSpec((tm, D), lambda i: (i, 0)),
            out_shape=jax.ShapeDtypeStruct((n_rows, D), out_dtype),
            compiler_params=_cp(("arbitrary",)), name="norm_modulate",
        )(xl, xc, gain, mod, mod)
    return pl.pallas_call(
        _normmod_router_kernel, grid=(n_rows // tm,),
        in_specs=in_specs + [pl.BlockSpec((D, 128), lambda i: (0, 0))],
        out_specs=[pl.BlockSpec((tm, D // 2), lambda i: (i, 0)), pl.BlockSpec((tm, 128), lambda i: (i, 0))],
        out_shape=[jax.ShapeDtypeStruct((n_rows, D // 2), jnp.uint32), jax.ShapeDtypeStruct((n_rows, 128), F32)],
        compiler_params=_cp(("arbitrary",)), name="norm_modulate_router",
    )(xl, xc, gain, mod, mod, w_router)


def _mm_kernel(a_ref, b_ref, o_ref):
    o_ref[...] = _dot_nt(a_ref[...], b_ref[...]).astype(o_ref.dtype)


def matmul(a, b, layer, tm, tn, out_dtype):
    m, k = a.shape
    n = b.shape[1]
    return pl.pallas_call(
        _mm_kernel,
        grid=(m // tm, n // tn),
        in_specs=[pl.BlockSpec((tm, k), lambda i, j: (i, 0)),
                  pl.BlockSpec((None, tn, k), lambda i, j: (layer, j, 0))],
        out_specs=pl.BlockSpec((tm, tn), lambda i, j: (i, j)),
        out_shape=jax.ShapeDtypeStruct((m, n), out_dtype),
        compiler_params=_cp(("arbitrary", "arbitrary")),
        name="matmul",
    )(a, b)


def _qblk(b, i):
    return jnp.where(i == 0, CTX_BLK0 + b, b * NBLK_L + i - 1)


def _rms_rows(x, g):
    return x * lax.rsqrt(jnp.mean(x * x, axis=-1, keepdims=True) + EPS) * g


NA_KROWS = 12
NA_KW = NA_KROWS * GW


NA_HG = 4
NA_GW = NA_HG * NA_D


def _na_kernel(tiles, q_ref, kl_ref, kc_ref, vl_ref, vc_ref, pair_ref, qn_ref, kn_ref, o_ref, ks_ref, bias_ref):
    i = pl.program_id(2)
    heads = [slice(NA_D * h, NA_D * (h + 1)) for h in range(NA_HG)]

    @pl.when(i == 0)
    def _():
        for hs in heads:
            ks_ref[0:L, hs] = _rms_rows(kc_ref[:, hs].astype(F32), kn_ref[...]).astype(BF16)
            ks_ref[L:, hs] = _rms_rows(kl_ref[:, hs].astype(F32), kn_ref[...]).astype(BF16)

    left, ok_l, ok_r = tiles
    lane = lax.broadcasted_iota(jnp.int32, (GW, 2 * GW), 1)
    for p, first_step in enumerate((1, 2, NBLK_L)):
        @pl.when(i == first_step)
        def _(p=p):
            for h in range(NA_HG):
                for a in range(BLK // GW):
                    for m in range(NA_KROWS // 2):
                        dst = (h, slice(GW * a, GW * (a + 1)), slice(2 * GW * m, 2 * GW * (m + 1)))
                        if not (ok_l[p, a, m] or ok_r[p, a, m]):
                            bias_ref[dst] = jnp.full((GW, 2 * GW), NEG, F32)
                            continue
                        t = pair_ref[h, int(left[p, a, m]) + 1]
                        if not ok_l[p, a, m]:
                            t = jnp.where(lane >= GW, t, NEG)
                        if not ok_r[p, a, m]:
                            t = jnp.where(lane < GW, t, NEG)
                        bias_ref[dst] = t

    def query(hs):
        return (_rms_rows(q_ref[:, hs].astype(F32), qn_ref[...]) * (NA_D ** -0.5)).astype(BF16)

    @pl.when(i == 0)
    def _():
        for hs in heads:
            s_c = _dot_nt(query(hs), ks_ref[0:L, hs])
            p = jnp.exp(s_c - jnp.max(s_c, axis=-1, keepdims=True))
            l = jnp.sum(p, axis=-1, keepdims=True)
            o_ref[:, hs] = (_dot(p.astype(BF16), vc_ref[:, hs]) / l).astype(o_ref.dtype)

    @pl.when(i > 0)
    def _():
        row0 = jnp.clip(4 * (i - 1) - 4, 0, GW - NA_KROWS)
        k0 = pl.multiple_of(row0 * GW, BLK)
        for h, hs in enumerate(heads):
            q = query(hs)
            s_c = _dot_nt(q, ks_ref[0:L, hs])
            s_l = _dot_nt(q, ks_ref[pl.ds(L + k0, NA_KW), hs]) + bias_ref[h]
            m = jnp.maximum(jnp.max(s_c, axis=-1, keepdims=True), jnp.max(s_l, axis=-1, keepdims=True))
            p_c = jnp.exp(s_c - m)
            p_l = jnp.exp(s_l - m)
            l = jnp.sum(p_c, axis=-1, keepdims=True) + jnp.sum(p_l, axis=-1, keepdims=True)
            o = _dot(p_l.astype(BF16), vl_ref[pl.ds(k0, NA_KW), hs]) + _dot(p_c.astype(BF16), vc_ref[:, hs])
            o_ref[:, hs] = (o / l).astype(o_ref.dtype)


def _na_bias_index():
    dr = np.zeros((3, BLK, NA_KW), np.int32)
    dc = np.zeros((3, BLK, NA_KW), np.int32)
    ok = np.zeros((3, BLK, NA_KW), bool)
    rows = S // GW
    for p, blk in enumerate((0, 5, NBLK_L - 1)):
        ks = int(np.clip(4 * blk - 4, 0, GW - NA_KROWS))
        r = 4 * blk + np.arange(BLK) // GW
        c = np.arange(BLK) % GW
        rk = ks + np.arange(NA_KW) // GW
        ck = np.arange(NA_KW) % GW
        r0 = np.clip(r - NA_R // 2, 0, rows - NA_R)
        ws = np.clip(c - NA_C // 2, 0, GW - NA_C)
        row_ok = (rk[None, :] >= r0[:, None]) & (rk[None, :] < r0[:, None] + NA_R)
        col_ok = (ck[None, :] >= ws[:, None]) & (ck[None, :] < ws[:, None] + NA_C)
        ok[p] = row_ok & col_ok
        dr[p] = np.clip(rk[None, :] - r[:, None] + NA_R - 1, 0, 2 * NA_R - 2)
        dc[p] = np.clip(ck[None, :] - c[:, None], -(NA_C - 1), NA_C - 1) + NA_C - 1
    return dr, dc, ok


NA_NDR = 2 * NA_R


def _na_bias_tiles():
    dr, _, ok = _na_bias_index()
    qr, kr = BLK // GW, NA_KROWS
    dr_t = dr.reshape(3, qr, GW, kr, GW)[:, :, 0, :, 0]
    ok_t = ok.reshape(3, qr, GW, kr, GW).any(axis=(2, 4))
    rk0 = np.array([int(np.clip(4 * blk - 4, 0, GW - NA_KROWS)) for blk in (0, 5, NBLK_L - 1)])
    r = np.array([4 * blk for blk in (0, 5, NBLK_L - 1)])[:, None, None] + np.arange(qr)[None, :, None]
    left = (rk0[:, None, None] + 2 * np.arange(kr // 2)[None, None, :]) - r + NA_R - 1
    assert np.all(dr_t[:, :, 0::2][ok_t[:, :, 0::2]] == left[ok_t[:, :, 0::2]])
    return left, ok_t[:, :, 0::2], ok_t[:, :, 1::2]


def _na_bias_pairs(rpb):
    _, dc, ok = _na_bias_index()
    qr, kr = BLK // GW, NA_KROWS
    dc_t = dc[0].reshape(qr, GW, kr, GW)[0, :, 0, :].reshape(GW * GW)
    col_ok = ok.reshape(3, qr, GW, kr, GW).any(axis=(0, 1, 3))
    oh_c = jnp.asarray(np.eye(2 * NA_C - 1, dtype=np.float32)[:, dc_t])
    t1 = jnp.einsum('hrd,dx->hrx', rpb.astype(F32), oh_c, precision=lax.Precision.HIGHEST)
    t1 = jnp.where(col_ok.reshape(1, 1, GW * GW), t1, NEG).reshape(NA_H, 2 * NA_R - 1, GW, GW)
    t1 = jnp.pad(t1, ((0, 0), (1, 1), (0, 0), (0, 0)))
    return jnp.concatenate([t1[:, :-1], t1[:, 1:]], axis=-1)


def na_attention(proj, rpb, qn, kn):
    pairs = _na_bias_pairs(rpb)

    cq, ck, cv = _COL_NAQ // NA_GW, _COL_NAK // NA_GW, _COL_NAV // NA_GW
    return pl.pallas_call(
        functools.partial(_na_kernel, _na_bias_tiles()),
        grid=(NB, NA_H // NA_HG, NBLK_L + 1),
        in_specs=[
            pl.BlockSpec((BLK, NA_GW), lambda b, h, i: (_qblk(b, i), cq + h)),
            pl.BlockSpec((S, NA_GW), lambda b, h, i: (b, ck + h)),
            pl.BlockSpec((L, NA_GW), lambda b, h, i: (CTX_BLK0 + b, ck + h)),
            pl.BlockSpec((S, NA_GW), lambda b, h, i: (b, cv + h)),
            pl.BlockSpec((L, NA_GW), lambda b, h, i: (CTX_BLK0 + b, cv + h)),
            pl.BlockSpec((NA_HG, NA_NDR, GW, 2 * GW), lambda b, h, i: (h, 0, 0, 0)),
            pl.BlockSpec((1, NA_D), lambda b, h, i: (0, 0)),
            pl.BlockSpec((1, NA_D), lambda b, h, i: (0, 0)),
        ],
        out_specs=pl.BlockSpec((BLK, NA_GW), lambda b, h, i: (_qblk(b, i), h)),
        out_shape=jax.ShapeDtypeStruct((NT, NA_W), BF16),
        scratch_shapes=[pltpu.VMEM((L + S, NA_GW), BF16), pltpu.VMEM((NA_HG, BLK, NA_KW), F32)],
        compiler_params=_cp(("arbitrary", "arbitrary", "arbitrary")),
        name="na_attention",
    )(proj, proj, proj, proj, proj, pairs, qn.reshape(1, NA_D), kn.reshape(1, NA_D))


MLA_HW = 256


def _half_mask(h, width=128):
    lane = lax.broadcasted_iota(jnp.int32, (1, width), 1)
    return (lane < 64) if h % 2 == 0 else (lane >= 64)


def _mla_q_kernel(cq_ref, w_ref, qa_ref, gn_ref, g2_ref, gs2_ref, c2_ref, s2_ref, q_ref):
    x = _rms_rows(cq_ref[...].astype(F32), qa_ref[...]).astype(BF16)
    y = _dot(x, w_ref[...])
    nw = MLA_H * MLA_NOPE
    gc = g2_ref[...] * c2_ref[...]
    gs = gs2_ref[...] * s2_ref[...]
    for p in range(MLA_H // 2):
        r1 = y[:, nw + 128 * p: nw + 128 * (p + 1)]
        r2 = y[:, nw + 768 + 128 * p: nw + 768 + 128 * (p + 1)]
        rot = r1 * gc + r2 * gs
        sq = r1 * r1
        for h in (2 * p, 2 * p + 1):
            msk = _half_mask(h)
            nope = y[:, 128 * h: 128 * (h + 1)]
            ss = jnp.sum(nope * nope, axis=-1, keepdims=True) + jnp.sum(
                jnp.where(msk, sq, 0.0), axis=-1, keepdims=True)
            inv = lax.rsqrt(ss / MLA_QK + EPS) * (MLA_QK ** -0.5)
            q_ref[h, :, 0:128] = (nope * gn_ref[...] * inv).astype(BF16)
            q_ref[h, :, 128:256] = (jnp.where(msk, rot, 0.0) * inv).astype(BF16)


def _mla_kv_kernel(ckv_ref, kr_ref, w_ref, kva_ref, gn_ref, g2_ref, gs2_ref, c2_ref, s2_ref,
                   k_ref, v_ref):
    x = _rms_rows(ckv_ref[...].astype(F32), kva_ref[...]).astype(BF16)
    y = _dot(x, w_ref[...])
    kr = kr_ref[...].astype(F32)
    r1 = kr[:, 0:128]
    r2 = kr[:, 128:256]
    rot = r1 * (g2_ref[...] * c2_ref[...]) + r2 * (gs2_ref[...] * s2_ref[...])
    ss_r = jnp.sum(jnp.where(_half_mask(0), r1 * r1, 0.0), axis=-1, keepdims=True)
    nw = MLA_H * MLA_NOPE
    lane0 = lax.broadcasted_iota(jnp.int32, (kr.shape[0], 128), 1) == 0
    for h in range(MLA_H):
        nope = y[:, 128 * h: 128 * (h + 1)]
        ss = jnp.sum(nope * nope, axis=-1, keepdims=True) + ss_r
        inv = lax.rsqrt(ss / MLA_QK + EPS)
        k_ref[h, :, 0:128] = (nope * gn_ref[...] * inv).astype(BF16)
        k_ref[h, :, 128:256] = (jnp.where(_half_mask(h), rot, 0.0) * inv).astype(BF16)
        v_ref[h, :, 0:128] = y[:, nw + 128 * h: nw + 128 * (h + 1)].astype(BF16)
        v_ref[h, :, 128:256] = jnp.where(lane0, 1.0, 0.0).astype(BF16)


def _rope_tables():
    t = jnp.arange(S, dtype=jnp.int32)
    pos = (t // GW, t % GW)
    nf = MLA_ROPE // 4
    inv = ROPE_BASE ** (-jnp.arange(nf, dtype=F32) / nf)
    cs, sn = [], []
    for ax in range(2):
        ang = pos[ax].astype(F32)[:, None] * inv[None, :]
        c, s = jnp.cos(ang), jnp.sin(ang)
        cs += [c, c]
        sn += [-s, s]
    c64 = jnp.tile(jnp.concatenate(cs, axis=1), (NB, 1))
    s64 = jnp.tile(jnp.concatenate(sn, axis=1), (NB, 1))
    c64 = jnp.concatenate([c64, jnp.ones((NB * L, MLA_ROPE), F32)], axis=0)
    s64 = jnp.concatenate([s64, jnp.zeros((NB * L, MLA_ROPE), F32)], axis=0)
    return jnp.tile(c64, (1, 2)), jnp.tile(s64, (1, 2))


_ROPE_SWAP = np.concatenate([np.arange(16, 32), np.arange(0, 16), np.arange(48, 64), np.arange(32, 48)])


def _rope_gains(g):
    gr = g[MLA_NOPE:]
    return (g[:MLA_NOPE].reshape(1, 128), jnp.tile(gr, 2).reshape(1, 128),
            jnp.tile(gr[_ROPE_SWAP], 2).reshape(1, 128))


def mla_q_prep(proj, w_uq_r, qa_g, qn_g, c2, s2):
    tm = 512
    gn, g2, gs2 = _rope_gains(qn_g)
    vec = lambda w: pl.BlockSpec((1, w), lambda i: (0, 0))
    return pl.pallas_call(
        _mla_q_kernel,
        grid=(NT // tm,),
        in_specs=[
            pl.BlockSpec((tm, MLA_QL), lambda i: (i, _COL_CQ // MLA_QL)),
            pl.BlockSpec((MLA_QL, 3072), lambda i: (0, 0)),
            vec(MLA_QL), vec(128), vec(128), vec(128),
            pl.BlockSpec((tm, 128), lambda i: (i, 0)),
            pl.BlockSpec((tm, 128), lambda i: (i, 0)),
        ],
        out_specs=pl.BlockSpec((MLA_H, tm, MLA_HW), lambda i: (0, i, 0)),
        out_shape=jax.ShapeDtypeStruct((MLA_H, NT, MLA_HW), BF16),
        compiler_params=_cp(("arbitrary",)),
        name="mla_q_prep",
    )(proj, w_uq_r, qa_g.reshape(1, MLA_QL), gn, g2, gs2, c2, s2)


def mla_kv_prep(proj, w_ukv_r, kva_g, kn_g, c2, s2):
    tm = 512
    gn, g2, gs2 = _rope_gains(kn_g)
    vec = lambda w: pl.BlockSpec((1, w), lambda i: (0, 0))
    return pl.pallas_call(
        _mla_kv_kernel,
        grid=(NT // tm,),
        in_specs=[
            pl.BlockSpec((tm, MLA_KVL), lambda i: (i, _COL_CKV // MLA_KVL)),
            pl.BlockSpec((tm, 256), lambda i: (i, _COL_KR // 256)),
            pl.BlockSpec((MLA_KVL, 3072), lambda i: (0, 0)),
            vec(MLA_KVL), vec(128), vec(128), vec(128),
            pl.BlockSpec((tm, 128), lambda i: (i, 0)),
            pl.BlockSpec((tm, 128), lambda i: (i, 0)),
        ],
        out_specs=[pl.BlockSpec((MLA_H, tm, MLA_HW), lambda i: (0, i, 0)),
                   pl.BlockSpec((MLA_H, tm, MLA_HW), lambda i: (0, i, 0))],
        out_shape=[jax.ShapeDtypeStruct((MLA_H, NT, MLA_HW), BF16),
                   jax.ShapeDtypeStruct((MLA_H, NT, MLA_HW), BF16)],
        compiler_params=_cp(("arbitrary",)),
        name="mla_kv_prep",
    )(proj, proj, w_ukv_r, kva_g.reshape(1, MLA_KVL), gn, g2, gs2, c2, s2)


MLA_TQ = 256


def _mla_attn_kernel(q_ref, kl_ref, kc_ref, vl_ref, vc_ref, o_ref, s0_ref, s1_ref, p0_ref, p1_ref):
    g = pl.program_id(0)

    @pl.when(g == 0)
    def _():
        s0_ref[...] = jnp.zeros_like(s0_ref)
        s1_ref[...] = jnp.zeros_like(s1_ref)
        p0_ref[...] = jnp.ones_like(p0_ref)
        p1_ref[...] = jnp.ones_like(p1_ref)

    n_kc = (L + S) // L
    rows = MLA_TQ // (n_kc - 1)

    def tie(x, dep):
        if dep is None:
            return x
        return jnp.concatenate([x[0:16, :] + dep, x[16:, :]], axis=0)

    def stages(s_new, s_old, p_new, p_old):
        acc, dep = None, None
        for c in range(n_kc):
            keys = slice(L * c, L * (c + 1))
            k = kc_ref[...] if c == 0 else kl_ref[L * (c - 1): L * c, :]
            v = vc_ref[...] if c == 0 else vl_ref[L * (c - 1): L * c, :]
            if c < n_kc - 1:
                r = slice(rows * c, rows * (c + 1))
                s = s_old[r, :]
                p = jnp.exp(s - jnp.max(s, axis=-1, keepdims=True)).astype(BF16)
                p_old[r, :] = p
                dep = (p[0:16, 0:MLA_HW].astype(F32) * 0.0).astype(BF16)
            part = _dot(tie(p_new[:, keys], dep), v)
            acc = part if acc is None else acc + part
            s_new[:, keys] = _dot_nt(tie(q_ref[...], dep), k)
        o_ref[...] = (acc[:, :MLA_V] / acc[:, MLA_V:MLA_V + 1]).astype(o_ref.dtype)

    @pl.when(g % 2 == 0)
    def _():
        stages(s0_ref, s1_ref, p0_ref, p1_ref)

    @pl.when(g % 2 == 1)
    def _():
        stages(s1_ref, s0_ref, p1_ref, p0_ref)


def _mla_ctx_kernel(q_ref, kc_ref, vc_ref, o_ref):
    s = _dot_nt(q_ref[...], kc_ref[...])
    p = jnp.exp(s - jnp.max(s, axis=-1, keepdims=True)).astype(BF16)
    acc = _dot(p, vc_ref[...])
    o_ref[...] = (acc[:, :MLA_V] / acc[:, MLA_V:MLA_V + 1]).astype(o_ref.dtype)


def mla_attention(q, k, v, need_ctx):
    nq = S // MLA_TQ
    n_steps = NB * MLA_H * nq

    def bhi(g):
        return g // (MLA_H * nq), (g // nq) % MLA_H, g % nq

    def cur(g):
        return bhi(jnp.minimum(g, n_steps - 1))

    def prev(g):
        return bhi(jnp.maximum(g - 2, 0))

    ob = pl.pallas_call(
        _mla_attn_kernel,
        grid=(n_steps + 2,),
        in_specs=[
            pl.BlockSpec((None, MLA_TQ, MLA_HW), lambda g: (cur(g)[1], cur(g)[0] * nq + cur(g)[2], 0)),
            pl.BlockSpec((None, S, MLA_HW), lambda g: (cur(g)[1], cur(g)[0], 0)),
            pl.BlockSpec((None, L, MLA_HW), lambda g: (cur(g)[1], CTX_BLK0 + cur(g)[0], 0)),
            pl.BlockSpec((None, S, MLA_HW), lambda g: (prev(g)[1], prev(g)[0], 0)),
            pl.BlockSpec((None, L, MLA_HW), lambda g: (prev(g)[1], CTX_BLK0 + prev(g)[0], 0)),
        ],
        out_specs=pl.BlockSpec((MLA_TQ, MLA_V), lambda g: (prev(g)[0] * nq + prev(g)[2], prev(g)[1])),
        out_shape=jax.ShapeDtypeStruct((NL, MLA_W), BF16),
        scratch_shapes=[pltpu.VMEM((MLA_TQ, L + S), F32), pltpu.VMEM((MLA_TQ, L + S), F32),
                        pltpu.VMEM((MLA_TQ, L + S), BF16), pltpu.VMEM((MLA_TQ, L + S), BF16)],
        compiler_params=_cp(("arbitrary",)),
        name="mla_attention",
    )(q, k, k, v, v)
    if not need_ctx:
        return ob, None
    ob_ctx = pl.pallas_call(
        _mla_ctx_kernel,
        grid=(NB, MLA_H),
        in_specs=[
            pl.BlockSpec((None, L, MLA_HW), lambda b, h: (h, CTX_BLK0 + b, 0)),
            pl.BlockSpec((None, L, MLA_HW), lambda b, h: (h, CTX_BLK0 + b, 0)),
            pl.BlockSpec((None, L, MLA_HW), lambda b, h: (h, CTX_BLK0 + b, 0)),
        ],
        out_specs=pl.BlockSpec((L, MLA_V), lambda b, h: (b, h)),
        out_shape=jax.ShapeDtypeStruct((NB * L, MLA_W), BF16),
        compiler_params=_cp(("arbitrary", "arbitrary")),
        name="mla_attention_ctx",
    )(q, k, v)
    return ob, ob_ctx


N_CH_C = L // CH
N_CH_L = S // CH
N_CH = N_CH_C + N_CH_L


def _gla_kernel(qf_ref, kf_ref, vf_ref, af_ref, qb_ref, kb_ref, vb_ref, ab_ref,
                wg_ref, bg_ref, of_ref, ob_ref, st_ref):
    t = pl.program_id(1)

    @pl.when(t == 0)
    def _():
        st_ref[...] = jnp.zeros_like(st_ref)

    row = lax.broadcasted_iota(jnp.int32, (CH, CH), 0)
    col = lax.broadcasted_iota(jnp.int32, (CH, CH), 1)
    dirs = (
        (qf_ref, kf_ref, vf_ref, af_ref, of_ref, col <= row, CH - 1),
        (qb_ref, kb_ref, vb_ref, ab_ref, ob_ref, col >= row, 0),
    )
    for d, (q_ref, k_ref, v_ref, a_ref, o_ref, keep, last) in enumerate(dirs):
        z = _dot(a_ref[...], wg_ref[d]) + bg_ref[d]
        g = (jnp.minimum(z, 0.0) - jnp.log(1.0 + jnp.exp(-jnp.abs(z)))) / GLA_TAU
        b = jnp.dot(keep.astype(F32), g, preferred_element_type=F32,
                    precision=lax.Precision.HIGHEST)
        b_end = b[last:last + 1, :]
        b_mid = b[CH // 2:CH // 2 + 1, :]
        q = q_ref[...].astype(F32) * (GLA_DK ** -0.5)
        k = k_ref[...].astype(F32)
        qa = (q * jnp.exp(b - b_mid)).astype(BF16)
        ka = (k * jnp.exp(b_mid - b)).astype(BF16)
        qd = (q * jnp.exp(b)).astype(BF16)
        ke = (k * jnp.exp(b_end - b)).astype(BF16)
        e_end = jnp.exp(b_end)
        v = v_ref[...]
        for h in range(GLA_H):
            ksl = slice(GLA_DK * h, GLA_DK * (h + 1))
            vsl = slice(GLA_DV * h, GLA_DV * (h + 1))
            att = jnp.where(keep, _dot_nt(qa[:, ksl], ka[:, ksl]), 0.0)
            st = st_ref[d, h]
            o = _dot_nt(qd[:, ksl], st.astype(BF16)) + _dot(att.astype(BF16), v[:, vsl])
            o_ref[:, vsl] = o.astype(o_ref.dtype)
            st_ref[d, h] = st * e_end[:, ksl] + _dot_tn(v[:, vsl], ke[:, ksl])


def gla_scan(proj, wg_f, bg_f, wg_b, bg_b):
    wg = jnp.zeros((2, 128, GLA_KW), F32)
    wg = wg.at[0, 0:GLA_RANK].set(wg_f).at[1, GLA_RANK:2 * GLA_RANK].set(wg_b).astype(BF16)
    bg = jnp.stack([bg_f, bg_b]).reshape(2, 1, GLA_KW)

    ctx0 = NL // CH

    def fwd(b, t):
        return jnp.where(t < N_CH_C, ctx0 + N_CH_C * b + t, N_CH_L * b + t - N_CH_C)

    def bwd(b, t):
        return jnp.where(t < N_CH_C, ctx0 + N_CH_C * b + N_CH_C - 1 - t, N_CH_L * b + N_CH - 1 - t)

    def specs(rowfn):
        return [
            pl.BlockSpec((CH, GLA_KW), lambda b, t: (rowfn(b, t), _COL_GQ // GLA_KW)),
            pl.BlockSpec((CH, GLA_KW), lambda b, t: (rowfn(b, t), _COL_GK // GLA_KW)),
            pl.BlockSpec((CH, GLA_W), lambda b, t: (rowfn(b, t), _COL_GV // GLA_W)),
            pl.BlockSpec((CH, 128), lambda b, t: (rowfn(b, t), _COL_GA // 128)),
        ]

    return pl.pallas_call(
        _gla_kernel,
        grid=(NB, N_CH),
        in_specs=specs(fwd) + specs(bwd) + [
            pl.BlockSpec((2, 128, GLA_KW), lambda b, t: (0, 0, 0)),
            pl.BlockSpec((2, 1, GLA_KW), lambda b, t: (0, 0, 0)),
        ],
        out_specs=[pl.BlockSpec((CH, GLA_W), lambda b, t: (fwd(b, t), 0)),
                   pl.BlockSpec((CH, GLA_W), lambda b, t: (bwd(b, t), 0))],
        out_shape=[jax.ShapeDtypeStruct((NT, GLA_W), BF16)] * 2,
        scratch_shapes=[pltpu.VMEM((2, GLA_H, GLA_DV, GLA_DK), F32)],
        compiler_params=_cp(("arbitrary", "arbitrary")),
        name="gla_scan",
    )(proj, proj, proj, proj, proj, proj, proj, proj, wg, bg)


def _gla_out_kernel(of_ref, ob_ref, g_ref, gn_ref, o_ref):
    for h in range(GLA_H):
        sl = slice(GLA_DV * h, GLA_DV * (h + 1))
        o = _rms_rows(of_ref[:, sl].astype(F32) + ob_ref[:, sl].astype(F32), gn_ref[...])
        gate = g_ref[:, sl].astype(F32)
        o_ref[:, sl] = (o * (gate * _sigmoid(gate))).astype(o_ref.dtype)


def gla_output(o_f, o_b, proj, on_g):
    tm = 512
    return pl.pallas_call(
        _gla_out_kernel,
        grid=(NT // tm,),
        in_specs=[
            pl.BlockSpec((tm, GLA_W), lambda i: (i, 0)),
            pl.BlockSpec((tm, GLA_W), lambda i: (i, 0)),
            pl.BlockSpec((tm, GLA_W), lambda i: (i, _COL_GG // GLA_W)),
            pl.BlockSpec((1, GLA_DV), lambda i: (0, 0)),
        ],
        out_specs=pl.BlockSpec((tm, GLA_W), lambda i: (i, 0)),
        out_shape=jax.ShapeDtypeStruct((NT, GLA_W), BF16),
        compiler_params=_cp(("arbitrary",)),
        name="gla_output",
    )(o_f, o_b, proj, on_g.reshape(1, GLA_DV))


def _out_proj_kernel(with_ctx, oa_ref, obl_ref, obc_ref, oc_ref, w_ref, xl_ref, xc_ref, g_ref, *o_refs):
    i = pl.program_id(1)
    a = jnp.concatenate([oa_ref[...], _stream_block(i, obl_ref, obc_ref), oc_ref[...]], axis=1)
    y = _stream_block(i, xl_ref, xc_ref) + g_ref[...] * _dot(a, w_ref[...])
    if not with_ctx:
        o_refs[0][...] = y
        return

    @pl.when(i < NLB)
    def _():
        o_refs[0][...] = y

    @pl.when(i == NLB)
    def _():
        o_refs[1][...] = y


def out_projection(oa, ob, ob_ctx, oc, w_out, layer, xl, xc, mod, k_gate, with_ctx):
    tm, tn = CTX_ROWS, 1024
    nj = D // tn
    lat = lambda j, i: (jnp.minimum(i, NLB - 1), j)
    out_specs = [pl.BlockSpec((tm, tn), lat)]
    out_shape = [jax.ShapeDtypeStruct((NL, D), F32)]
    if with_ctx:
        out_specs.append(pl.BlockSpec((tm, tn), lambda j, i: (0, j)))
        out_shape.append(jax.ShapeDtypeStruct((CTX_ROWS, D), F32))
    res = pl.pallas_call(
        functools.partial(_out_proj_kernel, with_ctx),
        grid=(nj, NLB + (1 if with_ctx else 0)),
        in_specs=[
            pl.BlockSpec((tm, NA_W), lambda j, i: (i, 0)),
            pl.BlockSpec((tm, MLA_W), lambda j, i: (jnp.minimum(i, NLB - 1), 0)),
            pl.BlockSpec((tm, MLA_W), lambda j, i: (0, 0)),
            pl.BlockSpec((tm, GLA_W), lambda j, i: (i, 0)),
            pl.BlockSpec((None, D, tn), lambda j, i: (layer, 0, j)),
            pl.BlockSpec((tm, tn), lat),
            pl.BlockSpec((tm, tn), lambda j, i: (0, j)),
            pl.BlockSpec((None, 1, tn), lambda j, i: (_row_group(i, tm), 0, k_gate * nj + j)),
        ],
        out_specs=out_specs,
        out_shape=out_shape,
        compiler_params=_cp(("arbitrary", "arbitrary")),
        name="out_projection",
    )(oa, ob, ob_ctx if with_ctx else ob, oc, w_out, xl, xc, mod)
    return (res[0], res[1]) if with_ctx else (res[0], None)


def _route(logits, router_bias):
    n = logits.shape[0]
    scores = jax.nn.sigmoid(logits)
    grouped = (scores + router_bias.astype(F32)).reshape(n, E_GROUPS, E_PER)

    def top2(a):
        idx = lax.broadcasted_iota(jnp.int32, a.shape, a.ndim - 1)
        i1 = jnp.argmax(a, axis=-1).astype(jnp.int32)
        rest = jnp.where(idx == i1[..., None], -jnp.inf, a)
        i2 = jnp.argmax(rest, axis=-1).astype(jnp.int32)
        return jnp.max(a, axis=-1), jnp.max(rest, axis=-1), i1, i2

    m1, m2, _, _ = top2(grouped)
    grp = jnp.argmax(m1 + m2, axis=-1).astype(jnp.int32)
    gsel = lax.broadcasted_iota(jnp.int32, (n, E_GROUPS, E_PER), 1) == grp[:, None, None]
    in_group = jnp.sum(jnp.where(gsel, grouped, 0.0), axis=1)
    _, _, l1, l2 = top2(in_group)
    expert_idx = grp[:, None] * E_PER + jnp.stack([l1, l2], axis=-1)
    esel = lax.broadcasted_iota(jnp.int32, (n, TOPK, E), 2) == expert_idx[:, :, None]
    w = jnp.sum(jnp.where(esel, scores[:, None, :], 0.0), axis=-1)
    return expert_idx, w / jnp.sum(w, axis=-1, keepdims=True)


def _dispatch_plan(expert_idx):
    nk = expert_idx.shape[0] * TOPK
    nb = _moe_blocks(expert_idx.shape[0])
    flat_e = expert_idx.reshape(nk)
    onehot = (flat_e[:, None] == jnp.arange(E, dtype=jnp.int32)[None, :]).astype(jnp.int32)
    csum = jnp.cumsum(onehot, axis=0)
    counts = csum[-1]
    rank = jnp.take_along_axis(csum, flat_e[:, None], axis=1)[:, 0] - 1
    padded = (counts + MOE_R - 1) // MOE_R * MOE_R
    pad_end = jnp.cumsum(padded)
    pad_start = pad_end - padded
    dest = (pad_start[flat_e] + rank).astype(jnp.int32)
    row_token = jnp.zeros((nb * MOE_R,), jnp.int32).at[dest].set(jnp.arange(nk, dtype=jnp.int32) // TOPK)
    blk0 = jnp.arange(nb, dtype=jnp.int32) * MOE_R
    block_expert = jnp.minimum(jnp.sum((pad_end[None, :] <= blk0[:, None]).astype(jnp.int32), axis=1), E - 1)
    n_active = (pad_end[-1] // MOE_R).astype(jnp.int32).reshape(1)
    n_valid = jnp.clip(counts[block_expert] - (blk0 - pad_start[block_expert]), 0, MOE_R).astype(jnp.int32)
    bidx = jnp.arange(nb, dtype=jnp.int32)
    prev_e = jnp.concatenate([jnp.full((1,), -1, jnp.int32), block_expert[:-1]])
    first = jnp.logical_and(bidx < n_active[0], block_expert != prev_e)
    later = jnp.where(first, bidx, nb)
    nxt_blk = jnp.concatenate([lax.cummin(later[::-1])[::-1][1:], jnp.full((1,), nb, jnp.int32)])
    nxt_e = jnp.where(nxt_blk < nb, block_expert[jnp.minimum(nxt_blk, nb - 1)], -1).astype(jnp.int32)
    plan = (block_expert, n_active, first.astype(jnp.int32), nxt_e)
    return dest, row_token, n_valid, plan


DH = D // 2


DMA_UNROLL = 8


def _for_rows(n, body):
    full = n // DMA_UNROLL

    def group(t, c):
        for u in range(DMA_UNROLL):
            body(t * DMA_UNROLL + u)
        return c

    def single(r, c):
        body(r)
        return c

    lax.fori_loop(0, full, group, 0)
    lax.fori_loop(full * DMA_UNROLL, n, single, 0)


def _gather_rows_kernel(tok_ref, nv_ref, h_ref, o_ref, sem):
    i = pl.program_id(0)
    base = i * MOE_R
    nv = nv_ref[i]

    @pl.when(nv < MOE_R)
    def _():
        o_ref[...] = jnp.zeros_like(o_ref)

    def copy(r):
        return pltpu.make_async_copy(h_ref.at[pl.ds(tok_ref[base + r], 1)], o_ref.at[pl.ds(r, 1)], sem)

    _for_rows(nv, lambda r: copy(r).start())
    _for_rows(nv, lambda r: copy(r).wait())


def gather_rows(row_token, n_valid, h):
    nb = n_valid.shape[0]
    return pl.pallas_call(
        _gather_rows_kernel,
        grid_spec=pltpu.PrefetchScalarGridSpec(
            num_scalar_prefetch=2,
            grid=(nb,),
            in_specs=[pl.BlockSpec(memory_space=pl.ANY)],
            out_specs=pl.BlockSpec((MOE_R, DH), lambda i, tok, nv: (i, 0)),
            scratch_shapes=[pltpu.SemaphoreType.DMA(())],
        ),
        out_shape=jax.ShapeDtypeStruct((nb * MOE_R, DH), jnp.uint32),
        compiler_params=_cp(("arbitrary",)),
        name="moe_gather_rows",
    )(row_token, n_valid, h)


def _stage_weights(layer, n_tiles, width, be_ref, first_ref, nxt_ref, active, mats):
    t = pl.program_id(0)
    i = pl.program_id(1)

    def copies(e, tt):
        cols = pl.ds(pl.multiple_of(tt * width, width), width)
        return [pltpu.make_async_copy(w.at[layer, e, :, cols], st, sm) for w, st, _, sm in mats]

    @pl.when(jnp.logical_and(active, first_ref[i] == 1))
    def _():
        @pl.when(jnp.logical_and(t == 0, i == 0))
        def _():
            for c in copies(be_ref[0], 0):
                c.start()

        for c in copies(be_ref[i], t):
            c.wait()
        for _, st, wb, _ in mats:
            wb[...] = st[...].astype(BF16)

        @pl.when(nxt_ref[i] >= 0)
        def _():
            for c in copies(nxt_ref[i], t):
                c.start()

        @pl.when(jnp.logical_and(nxt_ref[i] < 0, t + 1 < n_tiles))
        def _():
            for c in copies(be_ref[0], t + 1):
                c.start()


def _ffn_up_kernel(layer, be_ref, na_ref, first_ref, nxt_ref, x_ref, w1_ref, w3_ref, o_ref,
                   st1_ref, st3_ref, w1b_ref, w3b_ref, sem):
    i = pl.program_id(1)
    active = i < na_ref[0]
    _stage_weights(layer, FF // MOE_TF, MOE_TF, be_ref, first_ref, nxt_ref, active,
                   [(w1_ref, st1_ref, w1b_ref, sem.at[0]), (w3_ref, st3_ref, w3b_ref, sem.at[1])])

    @pl.when(active)
    def _():
        lo, hi = _unpack_pairs(x_ref[...])
        lo, hi = lo.astype(BF16), hi.astype(BF16)
        a = _dot(lo, w1b_ref[0:DH, :]) + _dot(hi, w1b_ref[DH:, :])
        b = _dot(lo, w3b_ref[0:DH, :]) + _dot(hi, w3b_ref[DH:, :])
        o_ref[...] = (a * _sigmoid(a) * b).astype(o_ref.dtype)

    @pl.when(jnp.logical_not(active))
    def _():
        o_ref[...] = jnp.zeros_like(o_ref)


def ffn_up(plan, xb, w1, w3, layer):
    nb = plan[0].shape[0]
    last = lambda i, na: jnp.minimum(i, na[0] - 1)
    return pl.pallas_call(
        functools.partial(_ffn_up_kernel, layer),
        grid_spec=pltpu.PrefetchScalarGridSpec(
            num_scalar_prefetch=4,
            grid=(FF // MOE_TF, nb),
            in_specs=[
                pl.BlockSpec((MOE_R, DH), lambda j, i, be, na, fi, nx: (last(i, na), 0)),
                pl.BlockSpec(memory_space=pl.ANY),
                pl.BlockSpec(memory_space=pl.ANY),
            ],
            out_specs=pl.BlockSpec((MOE_R, MOE_TF), lambda j, i, be, na, fi, nx: (i, j)),
            scratch_shapes=[pltpu.VMEM((D, MOE_TF), F32), pltpu.VMEM((D, MOE_TF), F32),
                            pltpu.VMEM((D, MOE_TF), BF16), pltpu.VMEM((D, MOE_TF), BF16),
                            pltpu.SemaphoreType.DMA((2,))],
        ),
        out_shape=jax.ShapeDtypeStruct((nb * MOE_R, FF), BF16),
        compiler_params=_cp(("arbitrary", "arbitrary")),
        name="moe_ffn_up",
    )(*plan, xb, w1, w3)


MOE_TN = 2048


def _ffn_down_kernel(layer, be_ref, na_ref, first_ref, nxt_ref, h_ref, w2_ref, o_ref, st2_ref, w2b_ref, sem):
    i = pl.program_id(1)
    active = i < na_ref[0]
    _stage_weights(layer, D // MOE_TN, MOE_TN, be_ref, first_ref, nxt_ref, active,
                   [(w2_ref, st2_ref, w2b_ref, sem.at[0])])

    @pl.when(active)
    def _():
        y = _dot(h_ref[...], w2b_ref[...])
        o_ref[...] = _pack_pairs(y[:, :MOE_TN // 2], y[:, MOE_TN // 2:])

    @pl.when(jnp.logical_not(active))
    def _():
        o_ref[...] = jnp.zeros_like(o_ref)


def ffn_down(plan, hmid, w2, layer):
    nb = plan[0].shape[0]
    last = lambda i, na: jnp.minimum(i, na[0] - 1)
    return pl.pallas_call(
        functools.partial(_ffn_down_kernel, layer),
        grid_spec=pltpu.PrefetchScalarGridSpec(
            num_scalar_prefetch=4,
            grid=(D // MOE_TN, nb),
            in_specs=[
                pl.BlockSpec((MOE_R, FF), lambda n, i, be, na, fi, nx: (last(i, na), 0)),
                pl.BlockSpec(memory_space=pl.ANY),
            ],
            out_specs=pl.BlockSpec((MOE_R, MOE_TN // 2), lambda n, i, be, na, fi, nx: (i, n)),
            scratch_shapes=[pltpu.VMEM((FF, MOE_TN), F32), pltpu.VMEM((FF, MOE_TN), BF16),
                            pltpu.SemaphoreType.DMA((1,))],
        ),
        out_shape=jax.ShapeDtypeStruct((nb * MOE_R, DH), jnp.uint32),
        compiler_params=_cp(("arbitrary", "arbitrary")),
        name="moe_ffn_down",
    )(*plan, hmid, w2)


CMB_T = 256
CMB_RC = 16


def _combine_kernel(with_ctx, pos_ref, yb_ref, gate_ref, xl_ref, xc_ref, g_ref, *rest):
    o_refs, (buf_ref, sem) = rest[:-2], rest[-2:]
    i = pl.program_id(0)
    n_lat = NL // CMB_T
    base = i * CMB_T

    def copy(r, k):
        p = pos_ref[(base + r) * TOPK + k]
        return pltpu.make_async_copy(yb_ref.at[pl.ds(p, 1)], buf_ref.at[k, pl.ds(r, 1)], sem.at[k])

    def start(r, c):
        copy(r, 0).start()
        copy(r, 1).start()
        return c

    def wait(r, c):
        copy(r, 0).wait()
        copy(r, 1).wait()
        return c

    lax.fori_loop(0, CMB_T, start, 0, unroll=DMA_UNROLL)
    lax.fori_loop(0, CMB_T, wait, 0, unroll=DMA_UNROLL)
    half = MOE_TN // 2

    def emit(x_ref, o_ref):
        def rows(t, c):
            r = pl.ds(pl.multiple_of(t * CMB_RC, CMB_RC), CMB_RC)
            g0, g1 = gate_ref[r, 0:1], gate_ref[r, 1:2]
            for n in range(D // MOE_TN):
                lo0, hi0 = _unpack_pairs(buf_ref[0, r, half * n: half * (n + 1)])
                lo1, hi1 = _unpack_pairs(buf_ref[1, r, half * n: half * (n + 1)])
                for part, y in enumerate((lo0 * g0 + lo1 * g1, hi0 * g0 + hi1 * g1)):
                    sl = slice(MOE_TN * n + half * part, MOE_TN * n + half * (part + 1))
                    o_ref[r, sl] = x_ref[r, sl] + g_ref[:, sl] * y
            return c

        lax.fori_loop(0, CMB_T // CMB_RC, rows, 0)

    if not with_ctx:
        emit(xl_ref, o_refs[0])
        return

    @pl.when(i < n_lat)
    def _():
        emit(xl_ref, o_refs[0])

    @pl.when(i >= n_lat)
    def _():
        emit(xc_ref, o_refs[1])


def moe_combine(dest, yb, gates, xl, xc, mod, k_gate, with_ctx):
    n_lat = NL // CMB_T
    lat = lambda i, pos: (jnp.minimum(i, n_lat - 1), 0)
    ctx = lambda i, pos: (jnp.maximum(i - n_lat, 0), 0)
    out_specs = [pl.BlockSpec((CMB_T, D), lat)]
    out_shape = [jax.ShapeDtypeStruct((NL, D), F32)]
    if with_ctx:
        out_specs.append(pl.BlockSpec((CMB_T, D), ctx))
        out_shape.append(jax.ShapeDtypeStruct((CTX_ROWS, D), F32))
    res = pl.pallas_call(
        functools.partial(_combine_kernel, with_ctx),
        grid_spec=pltpu.PrefetchScalarGridSpec(
            num_scalar_prefetch=1,
            grid=((NT if with_ctx else NL) // CMB_T,),
            in_specs=[
                pl.BlockSpec(memory_space=pl.ANY),
                pl.BlockSpec((CMB_T, TOPK), lambda i, pos: (i, 0)),
                pl.BlockSpec((CMB_T, D), lat),
                pl.BlockSpec((CMB_T, D), ctx),
                pl.BlockSpec((None, 1, D), lambda i, pos: (_row_group(i, CMB_T), 0, k_gate)),
            ],
            out_specs=out_specs,
            scratch_shapes=[pltpu.VMEM((TOPK, CMB_T, DH), jnp.uint32), pltpu.SemaphoreType.DMA((TOPK,))],
        ),
        out_shape=out_shape,
        compiler_params=_cp(("arbitrary",)),
        name="moe_combine",
    )(dest, yb, gates, xl, xc, mod)
    return (res[0], res[1]) if with_ctx else (res[0], None)


IN_W = 9312


W_RB = 512
W_NB = PW // W_RB
_W_SRC_KR = 4608
_W_SRC_GA = 9280


def _w_in_layout_kernel(w_ref, o_ref, buf_ref, sem):
    g = pl.program_id(0)

    def run(step, slot, op):
        l, k = step // W_NB, step % W_NB

        @pl.when(k < W_NB - 1)
        def _():
            r0 = pl.multiple_of(jnp.where(k < _W_SRC_KR // W_RB, k * W_RB, k * W_RB + MLA_ROPE), MLA_ROPE)
            op(pltpu.make_async_copy(w_ref.at[l, pl.ds(r0, W_RB)], buf_ref.at[slot], sem.at[slot]))

        @pl.when(k == W_NB - 1)
        def _():
            op(pltpu.make_async_copy(w_ref.at[l, pl.ds(_W_SRC_KR, MLA_ROPE)],
                                     buf_ref.at[slot, pl.ds(0, MLA_ROPE)], sem.at[slot]))
            op(pltpu.make_async_copy(w_ref.at[l, pl.ds(_W_SRC_GA, 2 * GLA_RANK)],
                                     buf_ref.at[slot, pl.ds(MLA_ROPE, 2 * GLA_RANK)], sem.at[slot]))

    start = lambda c: c.start()
    wait = lambda c: c.wait()

    @pl.when(g == 0)
    def _():
        run(0, 0, start)

    @pl.when(g + 1 < DEPTH * W_NB)
    def _():
        run(g + 1, (g + 1) % 2, start)

    run(g, g % 2, wait)
    k = g % W_NB

    @pl.when(k < W_NB - 1)
    def _():
        o_ref[...] = buf_ref[g % 2].astype(BF16)

    @pl.when(k == W_NB - 1)
    def _():
        x = buf_ref[g % 2, 0:MLA_ROPE, :]
        xs = jnp.concatenate([x[16:32], x[0:16], x[48:64], x[32:48]], axis=0)
        ga = buf_ref[g % 2, MLA_ROPE:MLA_ROPE + 2 * GLA_RANK, :]
        o_ref[...] = jnp.concatenate(
            [x, x, xs, xs, ga, jnp.zeros((W_RB - 4 * MLA_ROPE - 2 * GLA_RANK, D), F32)], axis=0).astype(BF16)


def _w_in_layout(w):
    w_t = jnp.swapaxes(w, 1, 2)
    return pl.pallas_call(
        _w_in_layout_kernel,
        grid=(DEPTH * W_NB,),
        in_specs=[pl.BlockSpec(memory_space=pl.ANY)],
        out_specs=pl.BlockSpec((None, W_RB, D), lambda g: (g // W_NB, g % W_NB, 0)),
        out_shape=jax.ShapeDtypeStruct((DEPTH, PW, D), BF16),
        scratch_shapes=[pltpu.VMEM((2, W_RB, D), F32), pltpu.SemaphoreType.DMA((2,))],
        compiler_params=_cp(("arbitrary",)),
        name="w_in_layout",
    )(w_t)


def _w_uq_cols():
    nope = [MLA_QK * h + j for h in range(MLA_H) for j in range(MLA_NOPE)]
    rope = [MLA_QK * h + MLA_NOPE + d for h in range(MLA_H) for d in range(MLA_ROPE)]
    rope_s = [MLA_QK * h + MLA_NOPE + int(d) for h in range(MLA_H) for d in _ROPE_SWAP]
    return np.array(nope + rope + rope_s, np.int32)


def _w_ukv_cols():
    kn = [(MLA_NOPE + MLA_V) * h + j for h in range(MLA_H) for j in range(MLA_NOPE)]
    vv = [(MLA_NOPE + MLA_V) * h + MLA_NOPE + j for h in range(MLA_H) for j in range(MLA_V)]
    return np.array(kn + vv, np.int32)


def token_mixing_layer(xl, xc, mod, layer, need_ctx, norm1, w_in_b, w_out_b, na_qn, na_kn, na_rpb, mla_qa, mla_kva,
                       w_uq, w_ukv, mla_qn, mla_kn, gwf, gbf, gwb, gbb, gla_on, c2, s2):
    h = norm_modulate(xl, xc, norm1.reshape(1, D), mod, 0, 1, BF16)
    proj = matmul(h, w_in_b, layer, NT // 4, 512, BF16)
    oa = na_attention(proj, na_rpb, na_qn, na_kn)
    q = mla_q_prep(proj, w_uq[:, _w_uq_cols()].astype(BF16), mla_qa, mla_qn, c2, s2)
    k, v = mla_kv_prep(proj, w_ukv[:, _w_ukv_cols()].astype(BF16), mla_kva, mla_kn, c2, s2)
    ob, ob_ctx = mla_attention(q, k, v, need_ctx)
    o_f, o_b = gla_scan(proj, gwf, gbf, gwb, gbb)
    oc = gla_output(o_f, o_b, proj, gla_on)
    return out_projection(oa, ob, ob_ctx, oc, w_out_b, layer, xl, xc, mod, 2, need_ctx)


def moe_layer(xl, xc, mod, norm2, w_router_p, router_bias, w1, w3, w2, layer, need_ctx):
    h, logits = norm_modulate(xl, xc, norm2.reshape(1, D), mod, 3, 4, None, w_router_p,
                              NT if need_ctx else NL)
    expert_idx, gates = _route(logits[:, :E], router_bias)
    dest, row_token, n_valid, plan = _dispatch_plan(expert_idx)
    xb = gather_rows(row_token, n_valid, h)
    hmid = ffn_up(plan, xb, w1, w3, layer)
    yb = ffn_down(plan, hmid, w2, layer)
    return moe_combine(dest, yb, gates, xl, xc, mod, 5, need_ctx)


def kernel(x, c, ctx, c_ctx, w_ada, b_ada, norm1, norm2, w_in, w_out, na_q_norm, na_k_norm, na_rpb, mla_qa_norm, mla_kva_norm, mla_w_uq, mla_w_ukv, mla_q_norm, mla_k_norm, gla_w_gate_f, gla_b_gate_f, gla_w_gate_b, gla_b_gate_b, gla_out_norm, w_router, router_bias, moe_w1, moe_w3, moe_w2):
    cond = jnp.concatenate([c, c_ctx[None, :], jnp.zeros((8 - NB - 1, D), F32)], axis=0)
    mods = ada_modulation(cond, w_ada, b_ada)
    xl, xc = x.reshape(NL, D), ctx.reshape(CTX_ROWS, D)
    c2, s2 = _rope_tables()
    w_router_p = jnp.concatenate([w_router, jnp.zeros((D, 128 - E), F32)], axis=1)
    w_in_b = _w_in_layout(w_in)
    w_out_b = w_out.astype(BF16)
    for l in range(DEPTH):
        mod = mods[l].reshape(8, 1, 6 * D)
        need_ctx = l < DEPTH - 1
        xl, xc_new = token_mixing_layer(xl, xc, mod, l, need_ctx, norm1[l], w_in_b, w_out_b, na_q_norm[l],
                                        na_k_norm[l], na_rpb[l], mla_qa_norm[l], mla_kva_norm[l], mla_w_uq[l],
                                        mla_w_ukv[l], mla_q_norm[l], mla_k_norm[l], gla_w_gate_f[l],
                                        gla_b_gate_f[l], gla_w_gate_b[l], gla_b_gate_b[l], gla_out_norm[l], c2, s2)
        xc = xc_new if need_ctx else xc
        xl, xc_new = moe_layer(xl, xc, mod, norm2[l], w_router_p, router_bias, moe_w1, moe_w3, moe_w2, l, need_ctx)
        xc = xc_new if need_ctx else xc
    return xl.reshape(NB, S, D)
```

```python
import functools

import numpy as np
import jax
import jax.numpy as jnp
from jax import lax
from jax.experimental import pallas as pl
from jax.experimental.pallas import tpu as pltpu

F32 = jnp.float32
BF16 = jnp.bfloat16

D = 4096
NB = 2
S = 4096
L = 256
DEPTH = 2
GW = 64
EPS = 1e-6
NL = NB * S
NT = NL + NB * L
NA_H, NA_D = 8, 128
NA_W = NA_H * NA_D
NA_R, NA_C = 8, 16
MLA_H = 12
MLA_QL, MLA_KVL = 1024, 512
MLA_NOPE, MLA_ROPE, MLA_V = 128, 64, 128
MLA_QK = MLA_NOPE + MLA_ROPE
MLA_W = MLA_H * MLA_V
GLA_H, GLA_DK, GLA_DV = 6, 128, 256
GLA_KW = GLA_H * GLA_DK
GLA_W = GLA_H * GLA_DV
GLA_RANK = 16
GLA_TAU = 16.0
CH = 128
E = 16
E_GROUPS = 4
E_PER = E // E_GROUPS
TOPK = 2
FF = 1024
ROPE_BASE = 10000.0

_COL_NAQ, _COL_NAK, _COL_NAV = 0, 1024, 2048
_COL_CQ, _COL_CKV = 3072, 4096
_COL_GQ, _COL_GK, _COL_GV, _COL_GG = 4608, 5376, 6144, 7680
_COL_KR = 9216
_COL_GA = 9472
PW = 9728

BLK = 256
NBLK_L = S // BLK
CTX_BLK0 = NL // BLK
MOE_R = 256
MOE_TF = 512


def _moe_blocks(n_tokens):
    return (n_tokens * TOPK + E * (MOE_R - 1)) // MOE_R


NEG = -1e30
V7X_VMEM_BYTES = 64 * 1024 * 1024
VMEM_LIMIT = V7X_VMEM_BYTES - 8 * 1024 * 1024


def _cp(sem, vmem=VMEM_LIMIT):
    return pltpu.CompilerParams(dimension_semantics=sem, vmem_limit_bytes=vmem)


def _dot(a, b):
    return jnp.dot(a, b, preferred_element_type=F32)


def _dot_nt(a, b):
    return lax.dot_general(a, b, (((1,), (1,)), ((), ())), preferred_element_type=F32)


def _dot_tn(a, b):
    return lax.dot_general(a, b, (((0,), (0,)), ((), ())), preferred_element_type=F32)


def _sigmoid(x):
    return 1.0 / (1.0 + jnp.exp(-x))


def _pack_pairs(lo, hi):
    lo_w = lax.bitcast_convert_type(lo.astype(BF16).astype(F32), jnp.uint32)
    hi_w = lax.bitcast_convert_type(hi.astype(BF16).astype(F32), jnp.uint32)
    return lax.shift_right_logical(lo_w, jnp.uint32(16)) | (hi_w & jnp.uint32(0xFFFF0000))


def _unpack_pairs(w):
    lo = lax.bitcast_convert_type(lax.shift_left(w, jnp.uint32(16)), F32)
    hi = lax.bitcast_convert_type(w & jnp.uint32(0xFFFF0000), F32)
    return lo, hi


def _row_group(i, tm):
    r0 = i * tm
    return jnp.where(r0 >= NL, 2, r0 // S)


def _ada_kernel(s_ref, w_ref, b_ref, o_ref):
    s = s_ref[...]
    s = s * _sigmoid(s)
    o_ref[...] = _dot(s.astype(BF16), w_ref[...].astype(BF16)) + b_ref[...]


def ada_modulation(cond, w_ada, b_ada):
    tn = 512
    n = 6 * D
    return pl.pallas_call(
        _ada_kernel,
        grid=(DEPTH, n // tn),
        in_specs=[
            pl.BlockSpec((8, D), lambda l, j: (0, 0)),
            pl.BlockSpec((None, D, tn), lambda l, j: (l, 0, j)),
            pl.BlockSpec((None, 1, tn), lambda l, j: (l, 0, j)),
        ],
        out_specs=pl.BlockSpec((None, 8, tn), lambda l, j: (l, 0, j)),
        out_shape=jax.ShapeDtypeStruct((DEPTH, 8, n), F32),
        compiler_params=_cp(("arbitrary", "arbitrary")),
        name="ada_modulation",
    )(cond, w_ada, b_ada.reshape(DEPTH, 1, n))


def _normmod(x, g, sh, sc):
    y = x * lax.rsqrt(jnp.mean(x * x, axis=-1, keepdims=True) + EPS) * g
    return y * (1.0 + sc) + sh


CTX_ROWS = NB * L
NLB = NL // CTX_ROWS


def _stream_block(i, xl_ref, xc_ref):
    return jnp.where(i < NLB, xl_ref[...], xc_ref[...])


def _normmod_kernel(xl_ref, xc_ref, g_ref, sh_ref, sc_ref, h_ref):
    x = _stream_block(pl.program_id(0), xl_ref, xc_ref)
    h_ref[...] = _normmod(x, g_ref[...], sh_ref[...], sc_ref[...]).astype(h_ref.dtype)


def _normmod_router_kernel(xl_ref, xc_ref, g_ref, sh_ref, sc_ref, wr_ref, h_ref, lg_ref):
    x = _stream_block(pl.program_id(0), xl_ref, xc_ref)
    h = _normmod(x, g_ref[...], sh_ref[...], sc_ref[...])
    h_ref[...] = _pack_pairs(h[:, :D // 2], h[:, D // 2:])
    lg_ref[...] = _dot(h.astype(BF16), wr_ref[...].astype(BF16))


def norm_modulate(xl, xc, gain, mod, k_shift, k_scale, out_dtype, w_router=None, n_rows=NT):
    tm = CTX_ROWS
    in_specs = [
        pl.BlockSpec((tm, D), lambda i: (jnp.minimum(i, NLB - 1), 0)),
        pl.BlockSpec((tm, D), lambda i: (0, 0)),
        pl.BlockSpec((1, D), lambda i: (0, 0)),
        pl.BlockSpec((None, 1, D), lambda i: (_row_group(i, tm), 0, k_shift)),
        pl.BlockSpec((None, 1, D), lambda i: (_row_group(i, tm), 0, k_scale)),
    ]
    if w_router is None:
        return pl.pallas_call(
            _normmod_kernel, grid=(n_rows // tm,), in_specs=in_specs,
            out_specs=pl.BlockSpec((tm, D), lambda i: (i, 0)),
            out_shape=jax.ShapeDtypeStruct((n_rows, D), out_dtype),
            compiler_params=_cp(("arbitrary",)), name="norm_modulate",
        )(xl, xc, gain, mod, mod)
    return pl.pallas_call(
        _normmod_router_kernel, grid=(n_rows // tm,),
        in_specs=in_specs + [pl.BlockSpec((D, 128), lambda i: (0, 0))],
        out_specs=[pl.BlockSpec((tm, D // 2), lambda i: (i, 0)), pl.BlockSpec((tm, 128), lambda i: (i, 0))],
        out_shape=[jax.ShapeDtypeStruct((n_rows, D // 2), jnp.uint32), jax.ShapeDtypeStruct((n_rows, 128), F32)],
        compiler_params=_cp(("arbitrary",)), name="norm_modulate_router",
    )(xl, xc, gain, mod, mod, w_router)


def _mm_kernel(a_ref, b_ref, o_ref):
    o_ref[...] = _dot_nt(a_ref[...], b_ref[...]).astype(o_ref.dtype)


def matmul(a, b, layer, tm, tn, out_dtype):
    m, k = a.shape
    n = b.shape[1]
    return pl.pallas_call(
        _mm_kernel,
        grid=(m // tm, n // tn),
        in_specs=[pl.BlockSpec((tm, k), lambda i, j: (i, 0)),
                  pl.BlockSpec((None, tn, k), lambda i, j: (layer, j, 0))],
        out_specs=pl.BlockSpec((tm, tn), lambda i, j: (i, j)),
        out_shape=jax.ShapeDtypeStruct((m, n), out_dtype),
        compiler_params=_cp(("arbitrary", "arbitrary")),
        name="matmul",
    )(a, b)


def _qblk(b, i):
    return jnp.where(i == 0, CTX_BLK0 + b, b * NBLK_L + i - 1)


def _rms_rows(x, g):
    return x * lax.rsqrt(jnp.mean(x * x, axis=-1, keepdims=True) + EPS) * g


NA_KROWS = 12
NA_KW = NA_KROWS * GW


NA_HG = 4
NA_GW = NA_HG * NA_D


def _na_kernel(tiles, q_ref, kl_ref, kc_ref, vl_ref, vc_ref, pair_ref, qn_ref, kn_ref, o_ref, ks_ref, bias_ref):
    i = pl.program_id(2)
    heads = [slice(NA_D * h, NA_D * (h + 1)) for h in range(NA_HG)]

    @pl.when(i == 0)
    def _():
        for hs in heads:
            ks_ref[0:L, hs] = _rms_rows(kc_ref[:, hs].astype(F32), kn_ref[...]).astype(BF16)
            ks_ref[L:, hs] = _rms_rows(kl_ref[:, hs].astype(F32), kn_ref[...]).astype(BF16)

    left, ok_l, ok_r = tiles
    lane = lax.broadcasted_iota(jnp.int32, (GW, 2 * GW), 1)
    for p, first_step in enumerate((1, 2, NBLK_L)):
        @pl.when(i == first_step)
        def _(p=p):
            for h in range(NA_HG):
                for a in range(BLK // GW):
                    for m in range(NA_KROWS // 2):
                        dst = (h, slice(GW * a, GW * (a + 1)), slice(2 * GW * m, 2 * GW * (m + 1)))
                        if not (ok_l[p, a, m] or ok_r[p, a, m]):
                            bias_ref[dst] = jnp.full((GW, 2 * GW), NEG, F32)
                            continue
                        t = pair_ref[h, int(left[p, a, m]) + 1]
                        if not ok_l[p, a, m]:
                            t = jnp.where(lane >= GW, t, NEG)
                        if not ok_r[p, a, m]:
                            t = jnp.where(lane < GW, t, NEG)
                        bias_ref[dst] = t

    def query(hs):
        return (_rms_rows(q_ref[:, hs].astype(F32), qn_ref[...]) * (NA_D ** -0.5)).astype(BF16)

    @pl.when(i == 0)
    def _():
        for hs in heads:
            s_c = _dot_nt(query(hs), ks_ref[0:L, hs])
            p = jnp.exp(s_c - jnp.max(s_c, axis=-1, keepdims=True))
            l = jnp.sum(p, axis=-1, keepdims=True)
            o_ref[:, hs] = (_dot(p.astype(BF16), vc_ref[:, hs]) / l).astype(o_ref.dtype)

    @pl.when(i > 0)
    def _():
        row0 = jnp.clip(4 * (i - 1) - 4, 0, GW - NA_KROWS)
        k0 = pl.multiple_of(row0 * GW, BLK)
        for h, hs in enumerate(heads):
            q = query(hs)
            s_c = _dot_nt(q, ks_ref[0:L, hs])
            s_l = _dot_nt(q, ks_ref[pl.ds(L + k0, NA_KW), hs]) + bias_ref[h]
            m = jnp.maximum(jnp.max(s_c, axis=-1, keepdims=True), jnp.max(s_l, axis=-1, keepdims=True))
            p_c = jnp.exp(s_c - m)
            p_l = jnp.exp(s_l - m)
            l = jnp.sum(p_c, axis=-1, keepdims=True) + jnp.sum(p_l, axis=-1, keepdims=True)
            o = _dot(p_l.astype(BF16), vl_ref[pl.ds(k0, NA_KW), hs]) + _dot(p_c.astype(BF16), vc_ref[:, hs])
            o_ref[:, hs] = (o / l).astype(o_ref.dtype)


def _na_bias_index():
    dr = np.zeros((3, BLK, NA_KW), np.int32)
    dc = np.zeros((3, BLK, NA_KW), np.int32)
    ok = np.zeros((3, BLK, NA_KW), bool)
    rows = S // GW
    for p, blk in enumerate((0, 5, NBLK_L - 1)):
        ks = int(np.clip(4 * blk - 4, 0, GW - NA_KROWS))
        r = 4 * blk + np.arange(BLK) // GW
        c = np.arange(BLK) % GW
        rk = ks + np.arange(NA_KW) // GW
        ck = np.arange(NA_KW) % GW
        r0 = np.clip(r - NA_R // 2, 0, rows - NA_R)
        ws = np.clip(c - NA_C // 2, 0, GW - NA_C)
        row_ok = (rk[None, :] >= r0[:, None]) & (rk[None, :] < r0[:, None] + NA_R)
        col_ok = (ck[None, :] >= ws[:, None]) & (ck[None, :] < ws[:, None] + NA_C)
        ok[p] = row_ok & col_ok
        dr[p] = np.clip(rk[None, :] - r[:, None] + NA_R - 1, 0, 2 * NA_R - 2)
        dc[p] = np.clip(ck[None, :] - c[:, None], -(NA_C - 1), NA_C - 1) + NA_C - 1
    return dr, dc, ok


NA_NDR = 2 * NA_R


def _na_bias_tiles():
    dr, _, ok = _na_bias_index()
    qr, kr = BLK // GW, NA_KROWS
    dr_t = dr.reshape(3, qr, GW, kr, GW)[:, :, 0, :, 0]
    ok_t = ok.reshape(3, qr, GW, kr, GW).any(axis=(2, 4))
    rk0 = np.array([int(np.clip(4 * blk - 4, 0, GW - NA_KROWS)) for blk in (0, 5, NBLK_L - 1)])
    r = np.array([4 * blk for blk in (0, 5, NBLK_L - 1)])[:, None, None] + np.arange(qr)[None, :, None]
    left = (rk0[:, None, None] + 2 * np.arange(kr // 2)[None, None, :]) - r + NA_R - 1
    assert np.all(dr_t[:, :, 0::2][ok_t[:, :, 0::2]] == left[ok_t[:, :, 0::2]])
    return left, ok_t[:, :, 0::2], ok_t[:, :, 1::2]


def _na_bias_pairs(rpb):
    _, dc, ok = _na_bias_index()
    qr, kr = BLK // GW, NA_KROWS
    dc_t = dc[0].reshape(qr, GW, kr, GW)[0, :, 0, :].reshape(GW * GW)
    col_ok = ok.reshape(3, qr, GW, kr, GW).any(axis=(0, 1, 3))
    oh_c = jnp.asarray(np.eye(2 * NA_C - 1, dtype=np.float32)[:, dc_t])
    t1 = jnp.einsum('hrd,dx->hrx', rpb.astype(F32), oh_c, precision=lax.Precision.HIGHEST)
    t1 = jnp.where(col_ok.reshape(1, 1, GW * GW), t1, NEG).reshape(NA_H, 2 * NA_R - 1, GW, GW)
    t1 = jnp.pad(t1, ((0, 0), (1, 1), (0, 0), (0, 0)))
    return jnp.concatenate([t1[:, :-1], t1[:, 1:]], axis=-1)


def na_attention(proj, rpb, qn, kn):
    pairs = _na_bias_pairs(rpb)

    cq, ck, cv = _COL_NAQ // NA_GW, _COL_NAK // NA_GW, _COL_NAV // NA_GW
    return pl.pallas_call(
        functools.partial(_na_kernel, _na_bias_tiles()),
        grid=(NB, NA_H // NA_HG, NBLK_L + 1),
        in_specs=[
            pl.BlockSpec((BLK, NA_GW), lambda b, h, i: (_qblk(b, i), cq + h)),
            pl.BlockSpec((S, NA_GW), lambda b, h, i: (b, ck + h)),
            pl.BlockSpec((L, NA_GW), lambda b, h, i: (CTX_BLK0 + b, ck + h)),
            pl.BlockSpec((S, NA_GW), lambda b, h, i: (b, cv + h)),
            pl.BlockSpec((L, NA_GW), lambda b, h, i: (CTX_BLK0 + b, cv + h)),
            pl.BlockSpec((NA_HG, NA_NDR, GW, 2 * GW), lambda b, h, i: (h, 0, 0, 0)),
            pl.BlockSpec((1, NA_D), lambda b, h, i: (0, 0)),
            pl.BlockSpec((1, NA_D), lambda b, h, i: (0, 0)),
        ],
        out_specs=pl.BlockSpec((BLK, NA_GW), lambda b, h, i: (_qblk(b, i), h)),
        out_shape=jax.ShapeDtypeStruct((NT, NA_W), BF16),
        scratch_shapes=[pltpu.VMEM((L + S, NA_GW), BF16), pltpu.VMEM((NA_HG, BLK, NA_KW), F32)],
        compiler_params=_cp(("arbitrary", "arbitrary", "arbitrary")),
        name="na_attention",
    )(proj, proj, proj, proj, proj, pairs, qn.reshape(1, NA_D), kn.reshape(1, NA_D))


MLA_HW = 256


def _half_mask(h, width=128):
    lane = lax.broadcasted_iota(jnp.int32, (1, width), 1)
    return (lane < 64) if h % 2 == 0 else (lane >= 64)


def _mla_q_kernel(cq_ref, w_ref, qa_ref, gn_ref, g2_ref, gs2_ref, c2_ref, s2_ref, q_ref):
    x = _rms_rows(cq_ref[...].astype(F32), qa_ref[...]).astype(BF16)
    y = _dot(x, w_ref[...])
    nw = MLA_H * MLA_NOPE
    gc = g2_ref[...] * c2_ref[...]
    gs = gs2_ref[...] * s2_ref[...]
    for p in range(MLA_H // 2):
        r1 = y[:, nw + 128 * p: nw + 128 * (p + 1)]
        r2 = y[:, nw + 768 + 128 * p: nw + 768 + 128 * (p + 1)]
        rot = r1 * gc + r2 * gs
        sq = r1 * r1
        for h in (2 * p, 2 * p + 1):
            msk = _half_mask(h)
            nope = y[:, 128 * h: 128 * (h + 1)]
            ss = jnp.sum(nope * nope, axis=-1, keepdims=True) + jnp.sum(
                jnp.where(msk, sq, 0.0), axis=-1, keepdims=True)
            inv = lax.rsqrt(ss / MLA_QK + EPS) * (MLA_QK ** -0.5)
            q_ref[h, :, 0:128] = (nope * gn_ref[...] * inv).astype(BF16)
            q_ref[h, :, 128:256] = (jnp.where(msk, rot, 0.0) * inv).astype(BF16)


def _mla_kv_kernel(ckv_ref, kr_ref, w_ref, kva_ref, gn_ref, g2_ref, gs2_ref, c2_ref, s2_ref,
                   k_ref, v_ref):
    x = _rms_rows(ckv_ref[...].astype(F32), kva_ref[...]).astype(BF16)
    y = _dot(x, w_ref[...])
    kr = kr_ref[...].astype(F32)
    r1 = kr[:, 0:128]
    r2 = kr[:, 128:256]
    rot = r1 * (g2_ref[...] * c2_ref[...]) + r2 * (gs2_ref[...] * s2_ref[...])
    ss_r = jnp.sum(jnp.where(_half_mask(0), r1 * r1, 0.0), axis=-1, keepdims=True)
    nw = MLA_H * MLA_NOPE
    lane0 = lax.broadcasted_iota(jnp.int32, (kr.shape[0], 128), 1) == 0
    for h in range(MLA_H):
        nope = y[:, 128 * h: 128 * (h + 1)]
        ss = jnp.sum(nope * nope, axis=-1, keepdims=True) + ss_r
        inv = lax.rsqrt(ss / MLA_QK + EPS)
        k_ref[h, :, 0:128] = (nope * gn_ref[...] * inv).astype(BF16)
        k_ref[h, :, 128:256] = (jnp.where(_half_mask(h), rot, 0.0) * inv).astype(BF16)
        v_ref[h, :, 0:128] = y[:, nw + 128 * h: nw + 128 * (h + 1)].astype(BF16)
        v_ref[h, :, 128:256] = jnp.where(lane0, 1.0, 0.0).astype(BF16)


def _rope_tables():
    t = jnp.arange(S, dtype=jnp.int32)
    pos = (t // GW, t % GW)
    nf = MLA_ROPE // 4
    inv = ROPE_BASE ** (-jnp.arange(nf, dtype=F32) / nf)
    cs, sn = [], []
    for ax in range(2):
        ang = pos[ax].astype(F32)[:, None] * inv[None, :]
        c, s = jnp.cos(ang), jnp.sin(ang)
        cs += [c, c]
        sn += [-s, s]
    c64 = jnp.tile(jnp.concatenate(cs, axis=1), (NB, 1))
    s64 = jnp.tile(jnp.concatenate(sn, axis=1), (NB, 1))
    c64 = jnp.concatenate([c64, jnp.ones((NB * L, MLA_ROPE), F32)], axis=0)
    s64 = jnp.concatenate([s64, jnp.zeros((NB * L, MLA_ROPE), F32)], axis=0)
    return jnp.tile(c64, (1, 2)), jnp.tile(s64, (1, 2))


_ROPE_SWAP = np.concatenate([np.arange(16, 32), np.arange(0, 16), np.arange(48, 64), np.arange(32, 48)])


def _rope_gains(g):
    gr = g[MLA_NOPE:]
    return (g[:MLA_NOPE].reshape(1, 128), jnp.tile(gr, 2).reshape(1, 128),
            jnp.tile(gr[_ROPE_SWAP], 2).reshape(1, 128))


def mla_q_prep(proj, w_uq_r, qa_g, qn_g, c2, s2):
    tm = 512
    gn, g2, gs2 = _rope_gains(qn_g)
    vec = lambda w: pl.BlockSpec((1, w), lambda i: (0, 0))
    return pl.pallas_call(
        _mla_q_kernel,
        grid=(NT // tm,),
        in_specs=[
            pl.BlockSpec((tm, MLA_QL), lambda i: (i, _COL_CQ // MLA_QL)),
            pl.BlockSpec((MLA_QL, 3072), lambda i: (0, 0)),
            vec(MLA_QL), vec(128), vec(128), vec(128),
            pl.BlockSpec((tm, 128), lambda i: (i, 0)),
            pl.BlockSpec((tm, 128), lambda i: (i, 0)),
        ],
        out_specs=pl.BlockSpec((MLA_H, tm, MLA_HW), lambda i: (0, i, 0)),
        out_shape=jax.ShapeDtypeStruct((MLA_H, NT, MLA_HW), BF16),
        compiler_params=_cp(("arbitrary",)),
        name="mla_q_prep",
    )(proj, w_uq_r, qa_g.reshape(1, MLA_QL), gn, g2, gs2, c2, s2)


def mla_kv_prep(proj, w_ukv_r, kva_g, kn_g, c2, s2):
    tm = 512
    gn, g2, gs2 = _rope_gains(kn_g)
    vec = lambda w: pl.BlockSpec((1, w), lambda i: (0, 0))
    return pl.pallas_call(
        _mla_kv_kernel,
        grid=(NT // tm,),
        in_specs=[
            pl.BlockSpec((tm, MLA_KVL), lambda i: (i, _COL_CKV // MLA_KVL)),
            pl.BlockSpec((tm, 256), lambda i: (i, _COL_KR // 256)),
            pl.BlockSpec((MLA_KVL, 3072), lambda i: (0, 0)),
            vec(MLA_KVL), vec(128), vec(128), vec(128),
            pl.BlockSpec((tm, 128), lambda i: (i, 0)),
            pl.BlockSpec((tm, 128), lambda i: (i, 0)),
        ],
        out_specs=[pl.BlockSpec((MLA_H, tm, MLA_HW), lambda i: (0, i, 0)),
                   pl.BlockSpec((MLA_H, tm, MLA_HW), lambda i: (0, i, 0))],
        out_shape=[jax.ShapeDtypeStruct((MLA_H, NT, MLA_HW), BF16),
                   jax.ShapeDtypeStruct((MLA_H, NT, MLA_HW), BF16)],
        compiler_params=_cp(("arbitrary",)),
        name="mla_kv_prep",
    )(proj, proj, w_ukv_r, kva_g.reshape(1, MLA_KVL), gn, g2, gs2, c2, s2)


MLA_TQ = 256


def _mla_attn_kernel(q_ref, kl_ref, kc_ref, vl_ref, vc_ref, o_ref, s0_ref, s1_ref, p0_ref, p1_ref):
    g = pl.program_id(0)

    @pl.when(g == 0)
    def _():
        s0_ref[...] = jnp.zeros_like(s0_ref)
        s1_ref[...] = jnp.zeros_like(s1_ref)
        p0_ref[...] = jnp.ones_like(p0_ref)
        p1_ref[...] = jnp.ones_like(p1_ref)

    n_kc = (L + S) // L
    rows = MLA_TQ // (n_kc - 1)

    def tie(x, dep):
        if dep is None:
            return x
        return jnp.concatenate([x[0:16, :] + dep, x[16:, :]], axis=0)

    def stages(s_new, s_old, p_new, p_old):
        acc, dep = None, None
        for c in range(n_kc):
            keys = slice(L * c, L * (c + 1))
            k = kc_ref[...] if c == 0 else kl_ref[L * (c - 1): L * c, :]
            v = vc_ref[...] if c == 0 else vl_ref[L * (c - 1): L * c, :]
            part = _dot(p_new[:, keys], v)
            acc = part if acc is None else acc + part
            s_new[:, keys] = _dot_nt(tie(q_ref[...], dep), k)
            if c < n_kc - 1:
                r = slice(rows * c, rows * (c + 1))
                s = s_old[r, :]
                p = jnp.exp(s - jnp.max(s, axis=-1, keepdims=True)).astype(BF16)
                p_old[r, :] = p
                dep = (p[0:16, 0:MLA_HW].astype(F32) * 0.0).astype(BF16)
        o_ref[...] = (acc[:, :MLA_V] / acc[:, MLA_V:MLA_V + 1]).astype(o_ref.dtype)

    @pl.when(g % 2 == 0)
    def _():
        stages(s0_ref, s1_ref, p0_ref, p1_ref)

    @pl.when(g % 2 == 1)
    def _():
        stages(s1_ref, s0_ref, p1_ref, p0_ref)


def _mla_ctx_kernel(q_ref, kc_ref, vc_ref, o_ref):
    s = _dot_nt(q_ref[...], kc_ref[...])
    p = jnp.exp(s - jnp.max(s, axis=-1, keepdims=True)).astype(BF16)
    acc = _dot(p, vc_ref[...])
    o_ref[...] = (acc[:, :MLA_V] / acc[:, MLA_V:MLA_V + 1]).astype(o_ref.dtype)


def mla_attention(q, k, v, need_ctx):
    nq = S // MLA_TQ
    n_steps = NB * MLA_H * nq

    def bhi(g):
        return g // (MLA_H * nq), (g // nq) % MLA_H, g % nq

    def cur(g):
        return bhi(jnp.minimum(g, n_steps - 1))

    def prev(g):
        return bhi(jnp.maximum(g - 2, 0))

    ob = pl.pallas_call(
        _mla_attn_kernel,
        grid=(n_steps + 2,),
        in_specs=[
            pl.BlockSpec((None, MLA_TQ, MLA_HW), lambda g: (cur(g)[1], cur(g)[0] * nq + cur(g)[2], 0)),
            pl.BlockSpec((None, S, MLA_HW), lambda g: (cur(g)[1], cur(g)[0], 0)),
            pl.BlockSpec((None, L, MLA_HW), lambda g: (cur(g)[1], CTX_BLK0 + cur(g)[0], 0)),
            pl.BlockSpec((None, S, MLA_HW), lambda g: (prev(g)[1], prev(g)[0], 0)),
            pl.BlockSpec((None, L, MLA_HW), lambda g: (prev(g)[1], CTX_BLK0 + prev(g)[0], 0)),
        ],
        out_specs=pl.BlockSpec((MLA_TQ, MLA_V), lambda g: (prev(g)[0] * nq + prev(g)[2], prev(g)[1])),
        out_shape=jax.ShapeDtypeStruct((NL, MLA_W), BF16),
        scratch_shapes=[pltpu.VMEM((MLA_TQ, L + S), F32), pltpu.VMEM((MLA_TQ, L + S), F32),
                        pltpu.VMEM((MLA_TQ, L + S), BF16), pltpu.VMEM((MLA_TQ, L + S), BF16)],
        compiler_params=_cp(("arbitrary",)),
        name="mla_attention",
    )(q, k, k, v, v)
    if not need_ctx:
        return ob, None
    ob_ctx = pl.pallas_call(
        _mla_ctx_kernel,
        grid=(NB, MLA_H),
        in_specs=[
            pl.BlockSpec((None, L, MLA_HW), lambda b, h: (h, CTX_BLK0 + b, 0)),
            pl.BlockSpec((None, L, MLA_HW), lambda b, h: (h, CTX_BLK0 + b, 0)),
            pl.BlockSpec((None, L, MLA_HW), lambda b, h: (h, CTX_BLK0 + b, 0)),
        ],
        out_specs=pl.BlockSpec((L, MLA_V), lambda b, h: (b, h)),
        out_shape=jax.ShapeDtypeStruct((NB * L, MLA_W), BF16),
        compiler_params=_cp(("arbitrary", "arbitrary")),
        name="mla_attention_ctx",
    )(q, k, v)
    return ob, ob_ctx


N_CH_C = L // CH
N_CH_L = S // CH
N_CH = N_CH_C + N_CH_L


def _gla_kernel(qf_ref, kf_ref, vf_ref, af_ref, qb_ref, kb_ref, vb_ref, ab_ref,
                wg_ref, bg_ref, of_ref, ob_ref, st_ref):
    t = pl.program_id(1)

    @pl.when(t == 0)
    def _():
        st_ref[...] = jnp.zeros_like(st_ref)

    row = lax.broadcasted_iota(jnp.int32, (CH, CH), 0)
    col = lax.broadcasted_iota(jnp.int32, (CH, CH), 1)
    dirs = (
        (qf_ref, kf_ref, vf_ref, af_ref, of_ref, col <= row, CH - 1),
        (qb_ref, kb_ref, vb_ref, ab_ref, ob_ref, col >= row, 0),
    )
    for d, (q_ref, k_ref, v_ref, a_ref, o_ref, keep, last) in enumerate(dirs):
        z = _dot(a_ref[...], wg_ref[d]) + bg_ref[d]
        g = (jnp.minimum(z, 0.0) - jnp.log(1.0 + jnp.exp(-jnp.abs(z)))) / GLA_TAU
        b = jnp.dot(keep.astype(F32), g, preferred_element_type=F32,
                    precision=lax.Precision.HIGHEST)
        b_end = b[last:last + 1, :]
        b_mid = b[CH // 2:CH // 2 + 1, :]
        q = q_ref[...].astype(F32) * (GLA_DK ** -0.5)
        k = k_ref[...].astype(F32)
        qa = (q * jnp.exp(b - b_mid)).astype(BF16)
        ka = (k * jnp.exp(b_mid - b)).astype(BF16)
        qd = (q * jnp.exp(b)).astype(BF16)
        ke = (k * jnp.exp(b_end - b)).astype(BF16)
        e_end = jnp.exp(b_end)
        v = v_ref[...]
        for h in range(GLA_H):
            ksl = slice(GLA_DK * h, GLA_DK * (h + 1))
            vsl = slice(GLA_DV * h, GLA_DV * (h + 1))
            att = jnp.where(keep, _dot_nt(qa[:, ksl], ka[:, ksl]), 0.0)
            st = st_ref[d, h]
            o = _dot_nt(qd[:, ksl], st.astype(BF16)) + _dot(att.astype(BF16), v[:, vsl])
            o_ref[:, vsl] = o.astype(o_ref.dtype)
            st_ref[d, h] = st * e_end[:, ksl] + _dot_tn(v[:, vsl], ke[:, ksl])


def gla_scan(proj, wg_f, bg_f, wg_b, bg_b):
    wg = jnp.zeros((2, 128, GLA_KW), F32)
    wg = wg.at[0, 0:GLA_RANK].set(wg_f).at[1, GLA_RANK:2 * GLA_RANK].set(wg_b).astype(BF16)
    bg = jnp.stack([bg_f, bg_b]).reshape(2, 1, GLA_KW)

    ctx0 = NL // CH

    def fwd(b, t):
        return jnp.where(t < N_CH_C, ctx0 + N_CH_C * b + t, N_CH_L * b + t - N_CH_C)

    def bwd(b, t):
        return jnp.where(t < N_CH_C, ctx0 + N_CH_C * b + N_CH_C - 1 - t, N_CH_L * b + N_CH - 1 - t)

    def specs(rowfn):
        return [
            pl.BlockSpec((CH, GLA_KW), lambda b, t: (rowfn(b, t), _COL_GQ // GLA_KW)),
            pl.BlockSpec((CH, GLA_KW), lambda b, t: (rowfn(b, t), _COL_GK // GLA_KW)),
            pl.BlockSpec((CH, GLA_W), lambda b, t: (rowfn(b, t), _COL_GV // GLA_W)),
            pl.BlockSpec((CH, 128), lambda b, t: (rowfn(b, t), _COL_GA // 128)),
        ]

    return pl.pallas_call(
        _gla_kernel,
        grid=(NB, N_CH),
        in_specs=specs(fwd) + specs(bwd) + [
            pl.BlockSpec((2, 128, GLA_KW), lambda b, t: (0, 0, 0)),
            pl.BlockSpec((2, 1, GLA_KW), lambda b, t: (0, 0, 0)),
        ],
        out_specs=[pl.BlockSpec((CH, GLA_W), lambda b, t: (fwd(b, t), 0)),
                   pl.BlockSpec((CH, GLA_W), lambda b, t: (bwd(b, t), 0))],
        out_shape=[jax.ShapeDtypeStruct((NT, GLA_W), BF16)] * 2,
        scratch_shapes=[pltpu.VMEM((2, GLA_H, GLA_DV, GLA_DK), F32)],
        compiler_params=_cp(("arbitrary", "arbitrary")),
        name="gla_scan",
    )(proj, proj, proj, proj, proj, proj, proj, proj, wg, bg)


def _gla_out_kernel(of_ref, ob_ref, g_ref, gn_ref, o_ref):
    for h in range(GLA_H):
        sl = slice(GLA_DV * h, GLA_DV * (h + 1))
        o = _rms_rows(of_ref[:, sl].astype(F32) + ob_ref[:, sl].astype(F32), gn_ref[...])
        gate = g_ref[:, sl].astype(F32)
        o_ref[:, sl] = (o * (gate * _sigmoid(gate))).astype(o_ref.dtype)


def gla_output(o_f, o_b, proj, on_g):
    tm = 512
    return pl.pallas_call(
        _gla_out_kernel,
        grid=(NT // tm,),
        in_specs=[
            pl.BlockSpec((tm, GLA_W), lambda i: (i, 0)),
            pl.BlockSpec((tm, GLA_W), lambda i: (i, 0)),
            pl.BlockSpec((tm, GLA_W), lambda i: (i, _COL_GG // GLA_W)),
            pl.BlockSpec((1, GLA_DV), lambda i: (0, 0)),
        ],
        out_specs=pl.BlockSpec((tm, GLA_W), lambda i: (i, 0)),
        out_shape=jax.ShapeDtypeStruct((NT, GLA_W), BF16),
        compiler_params=_cp(("arbitrary",)),
        name="gla_output",
    )(o_f, o_b, proj, on_g.reshape(1, GLA_DV))


def _out_proj_kernel(with_ctx, oa_ref, obl_ref, obc_ref, oc_ref, w_ref, xl_ref, xc_ref, g_ref, *o_refs):
    i = pl.program_id(1)
    a = jnp.concatenate([oa_ref[...], _stream_block(i, obl_ref, obc_ref), oc_ref[...]], axis=1)
    y = _stream_block(i, xl_ref, xc_ref) + g_ref[...] * _dot(a, w_ref[...])
    if not with_ctx:
        o_refs[0][...] = y
        return

    @pl.when(i < NLB)
    def _():
        o_refs[0][...] = y

    @pl.when(i == NLB)
    def _():
        o_refs[1][...] = y


def out_projection(oa, ob, ob_ctx, oc, w_out, layer, xl, xc, mod, k_gate, with_ctx):
    tm, tn = CTX_ROWS, 1024
    nj = D // tn
    lat = lambda j, i: (jnp.minimum(i, NLB - 1), j)
    out_specs = [pl.BlockSpec((tm, tn), lat)]
    out_shape = [jax.ShapeDtypeStruct((NL, D), F32)]
    if with_ctx:
        out_specs.append(pl.BlockSpec((tm, tn), lambda j, i: (0, j)))
        out_shape.append(jax.ShapeDtypeStruct((CTX_ROWS, D), F32))
    res = pl.pallas_call(
        functools.partial(_out_proj_kernel, with_ctx),
        grid=(nj, NLB + (1 if with_ctx else 0)),
        in_specs=[
            pl.BlockSpec((tm, NA_W), lambda j, i: (i, 0)),
            pl.BlockSpec((tm, MLA_W), lambda j, i: (jnp.minimum(i, NLB - 1), 0)),
            pl.BlockSpec((tm, MLA_W), lambda j, i: (0, 0)),
            pl.BlockSpec((tm, GLA_W), lambda j, i: (i, 0)),
            pl.BlockSpec((None, D, tn), lambda j, i: (layer, 0, j)),
            pl.BlockSpec((tm, tn), lat),
            pl.BlockSpec((tm, tn), lambda j, i: (0, j)),
            pl.BlockSpec((None, 1, tn), lambda j, i: (_row_group(i, tm), 0, k_gate * nj + j)),
        ],
        out_specs=out_specs,
        out_shape=out_shape,
        compiler_params=_cp(("arbitrary", "arbitrary")),
        name="out_projection",
    )(oa, ob, ob_ctx if with_ctx else ob, oc, w_out, xl, xc, mod)
    return (res[0], res[1]) if with_ctx else (res[0], None)


def _route(logits, router_bias):
    n = logits.shape[0]
    scores = jax.nn.sigmoid(logits)
    grouped = (scores + router_bias.astype(F32)).reshape(n, E_GROUPS, E_PER)

    def top2(a):
        idx = lax.broadcasted_iota(jnp.int32, a.shape, a.ndim - 1)
        i1 = jnp.argmax(a, axis=-1).astype(jnp.int32)
        rest = jnp.where(idx == i1[..., None], -jnp.inf, a)
        i2 = jnp.argmax(rest, axis=-1).astype(jnp.int32)
        return jnp.max(a, axis=-1), jnp.max(rest, axis=-1), i1, i2

    m1, m2, _, _ = top2(grouped)
    grp = jnp.argmax(m1 + m2, axis=-1).astype(jnp.int32)
    gsel = lax.broadcasted_iota(jnp.int32, (n, E_GROUPS, E_PER), 1) == grp[:, None, None]
    in_group = jnp.sum(jnp.where(gsel, grouped, 0.0), axis=1)
    _, _, l1, l2 = top2(in_group)
    expert_idx = grp[:, None] * E_PER + jnp.stack([l1, l2], axis=-1)
    esel = lax.broadcasted_iota(jnp.int32, (n, TOPK, E), 2) == expert_idx[:, :, None]
    w = jnp.sum(jnp.where(esel, scores[:, None, :], 0.0), axis=-1)
    return expert_idx, w / jnp.sum(w, axis=-1, keepdims=True)


def _dispatch_plan(expert_idx):
    nk = expert_idx.shape[0] * TOPK
    nb = _moe_blocks(expert_idx.shape[0])
    flat_e = expert_idx.reshape(nk)
    onehot = (flat_e[:, None] == jnp.arange(E, dtype=jnp.int32)[None, :]).astype(jnp.int32)
    csum = jnp.cumsum(onehot, axis=0)
    counts = csum[-1]
    rank = jnp.take_along_axis(csum, flat_e[:, None], axis=1)[:, 0] - 1
    padded = (counts + MOE_R - 1) // MOE_R * MOE_R
    pad_end = jnp.cumsum(padded)
    pad_start = pad_end - padded
    dest = (pad_start[flat_e] + rank).astype(jnp.int32)
    row_token = jnp.zeros((nb * MOE_R,), jnp.int32).at[dest].set(jnp.arange(nk, dtype=jnp.int32) // TOPK)
    blk0 = jnp.arange(nb, dtype=jnp.int32) * MOE_R
    block_expert = jnp.minimum(jnp.sum((pad_end[None, :] <= blk0[:, None]).astype(jnp.int32), axis=1), E - 1)
    n_active = (pad_end[-1] // MOE_R).astype(jnp.int32).reshape(1)
    n_valid = jnp.clip(counts[block_expert] - (blk0 - pad_start[block_expert]), 0, MOE_R).astype(jnp.int32)
    bidx = jnp.arange(nb, dtype=jnp.int32)
    prev_e = jnp.concatenate([jnp.full((1,), -1, jnp.int32), block_expert[:-1]])
    first = jnp.logical_and(bidx < n_active[0], block_expert != prev_e)
    later = jnp.where(first, bidx, nb)
    nxt_blk = jnp.concatenate([lax.cummin(later[::-1])[::-1][1:], jnp.full((1,), nb, jnp.int32)])
    nxt_e = jnp.where(nxt_blk < nb, block_expert[jnp.minimum(nxt_blk, nb - 1)], -1).astype(jnp.int32)
    plan = (block_expert, n_active, first.astype(jnp.int32), nxt_e)
    return dest, row_token, n_valid, plan


DH = D // 2


DMA_UNROLL = 8


def _for_rows(n, body):
    full = n // DMA_UNROLL

    def group(t, c):
        for u in range(DMA_UNROLL):
            body(t * DMA_UNROLL + u)
        return c

    def single(r, c):
        body(r)
        return c

    lax.fori_loop(0, full, group, 0)
    lax.fori_loop(full * DMA_UNROLL, n, single, 0)


def _gather_rows_kernel(tok_ref, nv_ref, h_ref, o_ref, sem):
    i = pl.program_id(0)
    base = i * MOE_R
    nv = nv_ref[i]

    @pl.when(nv < MOE_R)
    def _():
        o_ref[...] = jnp.zeros_like(o_ref)

    def copy(r):
        return pltpu.make_async_copy(h_ref.at[pl.ds(tok_ref[base + r], 1)], o_ref.at[pl.ds(r, 1)], sem)

    _for_rows(nv, lambda r: copy(r).start())
    _for_rows(nv, lambda r: copy(r).wait())


def gather_rows(row_token, n_valid, h):
    nb = n_valid.shape[0]
    return pl.pallas_call(
        _gather_rows_kernel,
        grid_spec=pltpu.PrefetchScalarGridSpec(
            num_scalar_prefetch=2,
            grid=(nb,),
            in_specs=[pl.BlockSpec(memory_space=pl.ANY)],
            out_specs=pl.BlockSpec((MOE_R, DH), lambda i, tok, nv: (i, 0)),
            scratch_shapes=[pltpu.SemaphoreType.DMA(())],
        ),
        out_shape=jax.ShapeDtypeStruct((nb * MOE_R, DH), jnp.uint32),
        compiler_params=_cp(("arbitrary",)),
        name="moe_gather_rows",
    )(row_token, n_valid, h)


def _stage_weights(layer, n_tiles, width, be_ref, first_ref, nxt_ref, active, mats):
    t = pl.program_id(0)
    i = pl.program_id(1)

    def copies(e, tt):
        cols = pl.ds(pl.multiple_of(tt * width, width), width)
        return [pltpu.make_async_copy(w.at[layer, e, :, cols], st, sm) for w, st, _, sm in mats]

    @pl.when(jnp.logical_and(active, first_ref[i] == 1))
    def _():
        @pl.when(jnp.logical_and(t == 0, i == 0))
        def _():
            for c in copies(be_ref[0], 0):
                c.start()

        for c in copies(be_ref[i], t):
            c.wait()
        for _, st, wb, _ in mats:
            wb[...] = st[...].astype(BF16)

        @pl.when(nxt_ref[i] >= 0)
        def _():
            for c in copies(nxt_ref[i], t):
                c.start()

        @pl.when(jnp.logical_and(nxt_ref[i] < 0, t + 1 < n_tiles))
        def _():
            for c in copies(be_ref[0], t + 1):
                c.start()


def _ffn_up_kernel(layer, be_ref, na_ref, first_ref, nxt_ref, x_ref, w1_ref, w3_ref, o_ref,
                   st1_ref, st3_ref, w1b_ref, w3b_ref, sem):
    i = pl.program_id(1)
    active = i < na_ref[0]
    _stage_weights(layer, FF // MOE_TF, MOE_TF, be_ref, first_ref, nxt_ref, active,
                   [(w1_ref, st1_ref, w1b_ref, sem.at[0]), (w3_ref, st3_ref, w3b_ref, sem.at[1])])

    @pl.when(active)
    def _():
        lo, hi = _unpack_pairs(x_ref[...])
        lo, hi = lo.astype(BF16), hi.astype(BF16)
        a = _dot(lo, w1b_ref[0:DH, :]) + _dot(hi, w1b_ref[DH:, :])
        b = _dot(lo, w3b_ref[0:DH, :]) + _dot(hi, w3b_ref[DH:, :])
        o_ref[...] = (a * _sigmoid(a) * b).astype(o_ref.dtype)

    @pl.when(jnp.logical_not(active))
    def _():
        o_ref[...] = jnp.zeros_like(o_ref)


def ffn_up(plan, xb, w1, w3, layer):
    nb = plan[0].shape[0]
    last = lambda i, na: jnp.minimum(i, na[0] - 1)
    return pl.pallas_call(
        functools.partial(_ffn_up_kernel, layer),
        grid_spec=pltpu.PrefetchScalarGridSpec(
            num_scalar_prefetch=4,
            grid=(FF // MOE_TF, nb),
            in_specs=[
                pl.BlockSpec((MOE_R, DH), lambda j, i, be, na, fi, nx: (last(i, na), 0)),
                pl.BlockSpec(memory_space=pl.ANY),
                pl.BlockSpec(memory_space=pl.ANY),
            ],
            out_specs=pl.BlockSpec((MOE_R, MOE_TF), lambda j, i, be, na, fi, nx: (i, j)),
            scratch_shapes=[pltpu.VMEM((D, MOE_TF), F32), pltpu.VMEM((D, MOE_TF), F32),
                            pltpu.VMEM((D, MOE_TF), BF16), pltpu.VMEM((D, MOE_TF), BF16),
                            pltpu.SemaphoreType.DMA((2,))],
        ),
        out_shape=jax.ShapeDtypeStruct((nb * MOE_R, FF), BF16),
        compiler_params=_cp(("arbitrary", "arbitrary")),
        name="moe_ffn_up",
    )(*plan, xb, w1, w3)


MOE_TN = 2048


def _ffn_down_kernel(layer, be_ref, na_ref, first_ref, nxt_ref, h_ref, w2_ref, o_ref, st2_ref, w2b_ref, sem):
    i = pl.program_id(1)
    active = i < na_ref[0]
    _stage_weights(layer, D // MOE_TN, MOE_TN, be_ref, first_ref, nxt_ref, active,
                   [(w2_ref, st2_ref, w2b_ref, sem.at[0])])

    @pl.when(active)
    def _():
        y = _dot(h_ref[...], w2b_ref[...])
        o_ref[...] = _pack_pairs(y[:, :MOE_TN // 2], y[:, MOE_TN // 2:])

    @pl.when(jnp.logical_not(active))
    def _():
        o_ref[...] = jnp.zeros_like(o_ref)


def ffn_down(plan, hmid, w2, layer):
    nb = plan[0].shape[0]
    last = lambda i, na: jnp.minimum(i, na[0] - 1)
    return pl.pallas_call(
        functools.partial(_ffn_down_kernel, layer),
        grid_spec=pltpu.PrefetchScalarGridSpec(
            num_scalar_prefetch=4,
            grid=(D // MOE_TN, nb),
            in_specs=[
                pl.BlockSpec((MOE_R, FF), lambda n, i, be, na, fi, nx: (last(i, na), 0)),
                pl.BlockSpec(memory_space=pl.ANY),
            ],
            out_specs=pl.BlockSpec((MOE_R, MOE_TN // 2), lambda n, i, be, na, fi, nx: (i, n)),
            scratch_shapes=[pltpu.VMEM((FF, MOE_TN), F32), pltpu.VMEM((FF, MOE_TN), BF16),
                            pltpu.SemaphoreType.DMA((1,))],
        ),
        out_shape=jax.ShapeDtypeStruct((nb * MOE_R, DH), jnp.uint32),
        compiler_params=_cp(("arbitrary", "arbitrary")),
        name="moe_ffn_down",
    )(*plan, hmid, w2)


CMB_T = 256
CMB_RC = 16


def _combine_kernel(with_ctx, pos_ref, yb_ref, gate_ref, xl_ref, xc_ref, g_ref, *rest):
    o_refs, (buf_ref, sem) = rest[:-2], rest[-2:]
    i = pl.program_id(0)
    n_lat = NL // CMB_T
    base = i * CMB_T

    def copy(r, k):
        p = pos_ref[(base + r) * TOPK + k]
        return pltpu.make_async_copy(yb_ref.at[pl.ds(p, 1)], buf_ref.at[k, pl.ds(r, 1)], sem.at[k])

    def start(r, c):
        copy(r, 0).start()
        copy(r, 1).start()
        return c

    def wait(r, c):
        copy(r, 0).wait()
        copy(r, 1).wait()
        return c

    lax.fori_loop(0, CMB_T, start, 0, unroll=DMA_UNROLL)
    lax.fori_loop(0, CMB_T, wait, 0, unroll=DMA_UNROLL)
    half = MOE_TN // 2

    def emit(x_ref, o_ref):
        def rows(t, c):
            r = pl.ds(pl.multiple_of(t * CMB_RC, CMB_RC), CMB_RC)
            g0, g1 = gate_ref[r, 0:1], gate_ref[r, 1:2]
            for n in range(D // MOE_TN):
                lo0, hi0 = _unpack_pairs(buf_ref[0, r, half * n: half * (n + 1)])
                lo1, hi1 = _unpack_pairs(buf_ref[1, r, half * n: half * (n + 1)])
                for part, y in enumerate((lo0 * g0 + lo1 * g1, hi0 * g0 + hi1 * g1)):
                    sl = slice(MOE_TN * n + half * part, MOE_TN * n + half * (part + 1))
                    o_ref[r, sl] = x_ref[r, sl] + g_ref[:, sl] * y
            return c

        lax.fori_loop(0, CMB_T // CMB_RC, rows, 0)

    if not with_ctx:
        emit(xl_ref, o_refs[0])
        return

    @pl.when(i < n_lat)
    def _():
        emit(xl_ref, o_refs[0])

    @pl.when(i >= n_lat)
    def _():
        emit(xc_ref, o_refs[1])


def moe_combine(dest, yb, gates, xl, xc, mod, k_gate, with_ctx):
    n_lat = NL // CMB_T
    lat = lambda i, pos: (jnp.minimum(i, n_lat - 1), 0)
    ctx = lambda i, pos: (jnp.maximum(i - n_lat, 0), 0)
    out_specs = [pl.BlockSpec((CMB_T, D), lat)]
    out_shape = [jax.ShapeDtypeStruct((NL, D), F32)]
    if with_ctx:
        out_specs.append(pl.BlockSpec((CMB_T, D), ctx))
        out_shape.append(jax.ShapeDtypeStruct((CTX_ROWS, D), F32))
    res = pl.pallas_call(
        functools.partial(_combine_kernel, with_ctx),
        grid_spec=pltpu.PrefetchScalarGridSpec(
            num_scalar_prefetch=1,
            grid=((NT if with_ctx else NL) // CMB_T,),
            in_specs=[
                pl.BlockSpec(memory_space=pl.ANY),
                pl.BlockSpec((CMB_T, TOPK), lambda i, pos: (i, 0)),
                pl.BlockSpec((CMB_T, D), lat),
                pl.BlockSpec((CMB_T, D), ctx),
                pl.BlockSpec((None, 1, D), lambda i, pos: (_row_group(i, CMB_T), 0, k_gate)),
            ],
            out_specs=out_specs,
            scratch_shapes=[pltpu.VMEM((TOPK, CMB_T, DH), jnp.uint32), pltpu.SemaphoreType.DMA((TOPK,))],
        ),
        out_shape=out_shape,
        compiler_params=_cp(("arbitrary",)),
        name="moe_combine",
    )(dest, yb, gates, xl, xc, mod)
    return (res[0], res[1]) if with_ctx else (res[0], None)


IN_W = 9312


W_RB = 512
W_NB = PW // W_RB
_W_SRC_KR = 4608
_W_SRC_GA = 9280


def _w_in_layout_kernel(w_ref, o_ref, buf_ref, sem):
    g = pl.program_id(0)

    def run(step, slot, op):
        l, k = step // W_NB, step % W_NB

        @pl.when(k < W_NB - 1)
        def _():
            r0 = pl.multiple_of(jnp.where(k < _W_SRC_KR // W_RB, k * W_RB, k * W_RB + MLA_ROPE), MLA_ROPE)
            op(pltpu.make_async_copy(w_ref.at[l, pl.ds(r0, W_RB)], buf_ref.at[slot], sem.at[slot]))

        @pl.when(k == W_NB - 1)
        def _():
            op(pltpu.make_async_copy(w_ref.at[l, pl.ds(_W_SRC_KR, MLA_ROPE)],
                                     buf_ref.at[slot, pl.ds(0, MLA_ROPE)], sem.at[slot]))
            op(pltpu.make_async_copy(w_ref.at[l, pl.ds(_W_SRC_GA, 2 * GLA_RANK)],
                                     buf_ref.at[slot, pl.ds(MLA_ROPE, 2 * GLA_RANK)], sem.at[slot]))

    start = lambda c: c.start()
    wait = lambda c: c.wait()

    @pl.when(g == 0)
    def _():
        run(0, 0, start)

    @pl.when(g + 1 < DEPTH * W_NB)
    def _():
        run(g + 1, (g + 1) % 2, start)

    run(g, g % 2, wait)
    k = g % W_NB

    @pl.when(k < W_NB - 1)
    def _():
        o_ref[...] = buf_ref[g % 2].astype(BF16)

    @pl.when(k == W_NB - 1)
    def _():
        x = buf_ref[g % 2, 0:MLA_ROPE, :]
        xs = jnp.concatenate([x[16:32], x[0:16], x[48:64], x[32:48]], axis=0)
        ga = buf_ref[g % 2, MLA_ROPE:MLA_ROPE + 2 * GLA_RANK, :]
        o_ref[...] = jnp.concatenate(
            [x, x, xs, xs, ga, jnp.zeros((W_RB - 4 * MLA_ROPE - 2 * GLA_RANK, D), F32)], axis=0).astype(BF16)


def _w_in_layout(w):
    w_t = jnp.swapaxes(w, 1, 2)
    return pl.pallas_call(
        _w_in_layout_kernel,
        grid=(DEPTH * W_NB,),
        in_specs=[pl.BlockSpec(memory_space=pl.ANY)],
        out_specs=pl.BlockSpec((None, W_RB, D), lambda g: (g // W_NB, g % W_NB, 0)),
        out_shape=jax.ShapeDtypeStruct((DEPTH, PW, D), BF16),
        scratch_shapes=[pltpu.VMEM((2, W_RB, D), F32), pltpu.SemaphoreType.DMA((2,))],
        compiler_params=_cp(("arbitrary",)),
        name="w_in_layout",
    )(w_t)


def _w_uq_cols():
    nope = [MLA_QK * h + j for h in range(MLA_H) for j in range(MLA_NOPE)]
    rope = [MLA_QK * h + MLA_NOPE + d for h in range(MLA_H) for d in range(MLA_ROPE)]
    rope_s = [MLA_QK * h + MLA_NOPE + int(d) for h in range(MLA_H) for d in _ROPE_SWAP]
    return np.array(nope + rope + rope_s, np.int32)


def _w_ukv_cols():
    kn = [(MLA_NOPE + MLA_V) * h + j for h in range(MLA_H) for j in range(MLA_NOPE)]
    vv = [(MLA_NOPE + MLA_V) * h + MLA_NOPE + j for h in range(MLA_H) for j in range(MLA_V)]
    return np.array(kn + vv, np.int32)


def token_mixing_layer(xl, xc, mod, layer, need_ctx, norm1, w_in_b, w_out_b, na_qn, na_kn, na_rpb, mla_qa, mla_kva,
                       w_uq, w_ukv, mla_qn, mla_kn, gwf, gbf, gwb, gbb, gla_on, c2, s2):
    h = norm_modulate(xl, xc, norm1.reshape(1, D), mod, 0, 1, BF16)
    proj = matmul(h, w_in_b, layer, NT // 4, 512, BF16)
    oa = na_attention(proj, na_rpb, na_qn, na_kn)
    q = mla_q_prep(proj, w_uq[:, _w_uq_cols()].astype(BF16), mla_qa, mla_qn, c2, s2)
    k, v = mla_kv_prep(proj, w_ukv[:, _w_ukv_cols()].astype(BF16), mla_kva, mla_kn, c2, s2)
    ob, ob_ctx = mla_attention(q, k, v, need_ctx)
    o_f, o_b = gla_scan(proj, gwf, gbf, gwb, gbb)
    oc = gla_output(o_f, o_b, proj, gla_on)
    return out_projection(oa, ob, ob_ctx, oc, w_out_b, layer, xl, xc, mod, 2, need_ctx)


def moe_layer(xl, xc, mod, norm2, w_router_p, router_bias, w1, w3, w2, layer, need_ctx):
    h, logits = norm_modulate(xl, xc, norm2.reshape(1, D), mod, 3, 4, None, w_router_p,
                              NT if need_ctx else NL)
    expert_idx, gates = _route(logits[:, :E], router_bias)
    dest, row_token, n_valid, plan = _dispatch_plan(expert_idx)
    xb = gather_rows(row_token, n_valid, h)
    hmid = ffn_up(plan, xb, w1, w3, layer)
    yb = ffn_down(plan, hmid, w2, layer)
    return moe_combine(dest, yb, gates, xl, xc, mod, 5, need_ctx)


def kernel(x, c, ctx, c_ctx, w_ada, b_ada, norm1, norm2, w_in, w_out, na_q_norm, na_k_norm, na_rpb, mla_qa_norm, mla_kva_norm, mla_w_uq, mla_w_ukv, mla_q_norm, mla_k_norm, gla_w_gate_f, gla_b_gate_f, gla_w_gate_b, gla_b_gate_b, gla_out_norm, w_router, router_bias, moe_w1, moe_w3, moe_w2):
    cond = jnp.concatenate([c, c_ctx[None, :], jnp.zeros((8 - NB - 1, D), F32)], axis=0)
    mods = ada_modulation(cond, w_ada, b_ada)
    xl, xc = x.reshape(NL, D), ctx.reshape(CTX_ROWS, D)
    c2, s2 = _rope_tables()
    w_router_p = jnp.concatenate([w_router, jnp.zeros((D, 128 - E), F32)], axis=1)
    w_in_b = _w_in_layout(w_in)
    w_out_b = w_out.astype(BF16)
    for l in range(DEPTH):
        mod = mods[l].reshape(8, 1, 6 * D)
        need_ctx = l < DEPTH - 1
        xl, xc_new = token_mixing_layer(xl, xc, mod, l, need_ctx, norm1[l], w_in_b, w_out_b, na_q_norm[l],
                                        na_k_norm[l], na_rpb[l], mla_qa_norm[l], mla_kva_norm[l], mla_w_uq[l],
                                        mla_w_ukv[l], mla_q_norm[l], mla_k_norm[l], gla_w_gate_f[l],
                                        gla_b_gate_f[l], gla_w_gate_b[l], gla_b_gate_b[l], gla_out_norm[l], c2, s2)
        xc = xc_new if need_ctx else xc
        xl, xc_new = moe_layer(xl, xc, mod, norm2[l], w_router_p, router_bias, moe_w1, moe_w3, moe_w2, l, need_ctx)
        xc = xc_new if need_ctx else xc
    return xl.reshape(NB, S, D)
```

```python
import functools

import numpy as np
import jax
import jax.numpy as jnp
from jax import lax
from jax.experimental import pallas as pl
from jax.experimental.pallas import tpu as pltpu

F32 = jnp.float32
BF16 = jnp.bfloat16

D = 4096
NB = 2
S = 4096
L = 256
DEPTH = 2
GW = 64
EPS = 1e-6
NL = NB * S
NT = NL + NB * L
NA_H, NA_D = 8, 128
NA_W = NA_H * NA_D
NA_R, NA_C = 8, 16
MLA_H = 12
MLA_QL, MLA_KVL = 1024, 512
MLA_NOPE, MLA_ROPE, MLA_V = 128, 64, 128
MLA_QK = MLA_NOPE + MLA_ROPE
MLA_W = MLA_H * MLA_V
GLA_H, GLA_DK, GLA_DV = 6, 128, 256
GLA_KW = GLA_H * GLA_DK
GLA_W = GLA_H * GLA_DV
GLA_RANK = 16
GLA_TAU = 16.0
CH = 128
E = 16
E_GROUPS = 4
E_PER = E // E_GROUPS
TOPK = 2
FF = 1024
ROPE_BASE = 10000.0

_COL_NAQ, _COL_NAK, _COL_NAV = 0, 1024, 2048
_COL_CQ, _COL_CKV = 3072, 4096
_COL_GQ, _COL_GK, _COL_GV, _COL_GG = 4608, 5376, 6144, 7680
_COL_KR = 9216
_COL_GA = 9472
PW = 9728

BLK = 256
NBLK_L = S // BLK
CTX_BLK0 = NL // BLK
MOE_R = 256
MOE_TF = 512


def _moe_blocks(n_tokens):
    return (n_tokens * TOPK + E * (MOE_R - 1)) // MOE_R


NEG = -1e30
V7X_VMEM_BYTES = 64 * 1024 * 1024
VMEM_LIMIT = V7X_VMEM_BYTES - 8 * 1024 * 1024


def _cp(sem, vmem=VMEM_LIMIT):
    return pltpu.CompilerParams(dimension_semantics=sem, vmem_limit_bytes=vmem)


def _dot(a, b):
    return jnp.dot(a, b, preferred_element_type=F32)


def _dot_nt(a, b):
    return lax.dot_general(a, b, (((1,), (1,)), ((), ())), preferred_element_type=F32)


def _dot_tn(a, b):
    return lax.dot_general(a, b, (((0,), (0,)), ((), ())), preferred_element_type=F32)


def _sigmoid(x):
    return 1.0 / (1.0 + jnp.exp(-x))


def _pack_pairs(lo, hi):
    lo_w = lax.bitcast_convert_type(lo.astype(BF16).astype(F32), jnp.uint32)
    hi_w = lax.bitcast_convert_type(hi.astype(BF16).astype(F32), jnp.uint32)
    return lax.shift_right_logical(lo_w, jnp.uint32(16)) | (hi_w & jnp.uint32(0xFFFF0000))


def _unpack_pairs(w):
    lo = lax.bitcast_convert_type(lax.shift_left(w, jnp.uint32(16)), F32)
    hi = lax.bitcast_convert_type(w & jnp.uint32(0xFFFF0000), F32)
    return lo, hi


def _row_group(i, tm):
    r0 = i * tm
    return jnp.where(r0 >= NL, 2, r0 // S)


def _ada_kernel(s_ref, w_ref, b_ref, o_ref):
    s = s_ref[...]
    s = s * _sigmoid(s)
    o_ref[...] = _dot(s.astype(BF16), w_ref[...].astype(BF16)) + b_ref[...]


def ada_modulation(cond, w_ada, b_ada):
    tn = 512
    n = 6 * D
    return pl.pallas_call(
        _ada_kernel,
        grid=(DEPTH, n // tn),
        in_specs=[
            pl.BlockSpec((8, D), lambda l, j: (0, 0)),
            pl.BlockSpec((None, D, tn), lambda l, j: (l, 0, j)),
            pl.BlockSpec((None, 1, tn), lambda l, j: (l, 0, j)),
        ],
        out_specs=pl.BlockSpec((None, 8, tn), lambda l, j: (l, 0, j)),
        out_shape=jax.ShapeDtypeStruct((DEPTH, 8, n), F32),
        compiler_params=_cp(("arbitrary", "arbitrary")),
        name="ada_modulation",
    )(cond, w_ada, b_ada.reshape(DEPTH, 1, n))


def _normmod(x, g, sh, sc):
    y = x * lax.rsqrt(jnp.mean(x * x, axis=-1, keepdims=True) + EPS) * g
    return y * (1.0 + sc) + sh


CTX_ROWS = NB * L
NLB = NL // CTX_ROWS


def _stream_block(i, xl_ref, xc_ref):
    return jnp.where(i < NLB, xl_ref[...], xc_ref[...])


def _normmod_kernel(xl_ref, xc_ref, g_ref, sh_ref, sc_ref, h_ref):
    x = _stream_block(pl.program_id(0), xl_ref, xc_ref)
    h_ref[...] = _normmod(x, g_ref[...], sh_ref[...], sc_ref[...]).astype(h_ref.dtype)


def _normmod_router_kernel(xl_ref, xc_ref, g_ref, sh_ref, sc_ref, wr_ref, h_ref, lg_ref):
    x = _stream_block(pl.program_id(0), xl_ref, xc_ref)
    h = _normmod(x, g_ref[...], sh_ref[...], sc_ref[...])
    h_ref[...] = _pack_pairs(h[:, :D // 2], h[:, D // 2:])
    lg_ref[...] = _dot(h.astype(BF16), wr_ref[...].astype(BF16))


def norm_modulate(xl, xc, gain, mod, k_shift, k_scale, out_dtype, w_router=None, n_rows=NT):
    tm = CTX_ROWS
    in_specs = [
        pl.BlockSpec((tm, D), lambda i: (jnp.minimum(i, NLB - 1), 0)),
        pl.BlockSpec((tm, D), lambda i: (0, 0)),
        pl.BlockSpec((1, D), lambda i: (0, 0)),
        pl.BlockSpec((None, 1, D), lambda i: (_row_group(i, tm), 0, k_shift)),
        pl.BlockSpec((None, 1, D), lambda i: (_row_group(i, tm), 0, k_scale)),
    ]
    if w_router is None:
        return pl.pallas_call(
            _normmod_kernel, grid=(n_rows // tm,), in_specs=in_specs,
            out_specs=pl.BlockSpec((tm, D), lambda i: (i, 0)),
            out_shape=jax.ShapeDtypeStruct((n_rows, D), out_dtype),
            compiler_params=_cp(("arbitrary",)), name="norm_modulate",
        )(xl, xc, gain, mod, mod)
    return pl.pallas_call(
        _normmod_router_kernel, grid=(n_rows // tm,),
        in_specs=in_specs + [pl.BlockSpec((D, 128), lambda i: (0, 0))],
        out_specs=[pl.BlockSpec((tm, D // 2), lambda i: (i, 0)), pl.BlockSpec((tm, 128), lambda i: (i, 0))],
        out_shape=[jax.ShapeDtypeStruct((n_rows, D // 2), jnp.uint32), jax.ShapeDtypeStruct((n_rows, 128), F32)],
        compiler_params=_cp(("arbitrary",)), name="norm_modulate_router",
    )(xl, xc, gain, mod, mod, w_router)


def _mm_kernel(a_ref, b_ref, o_ref):
    o_ref[...] = _dot_nt(a_ref[...], b_ref[...]).astype(o_ref.dtype)


def matmul(a, b, layer, tm, tn, out_dtype):
    m, k = a.shape
    n = b.shape[1]
    return pl.pallas_call(
        _mm_kernel,
        grid=(m // tm, n // tn),
        in_specs=[pl.BlockSpec((tm, k), lambda i, j: (i, 0)),
                  pl.BlockSpec((None, tn, k), lambda i, j: (layer, j, 0))],
        out_specs=pl.BlockSpec((tm, tn), lambda i, j: (i, j)),
        out_shape=jax.ShapeDtypeStruct((m, n), out_dtype),
        compiler_params=_cp(("arbitrary", "arbitrary")),
        name="matmul",
    )(a, b)


def _qblk(b, i):
    return jnp.where(i == 0, CTX_BLK0 + b, b * NBLK_L + i - 1)


def _rms_rows(x, g):
    return x * lax.rsqrt(jnp.mean(x * x, axis=-1, keepdims=True) + EPS) * g


NA_KROWS = 12
NA_KW = NA_KROWS * GW


NA_HG = 4
NA_GW = NA_HG * NA_D


def _na_kernel(tiles, q_ref, kl_ref, kc_ref, vl_ref, vc_ref, pair_ref, qn_ref, kn_ref, o_ref, ks_ref, bias_ref):
    i = pl.program_id(2)
    heads = [slice(NA_D * h, NA_D * (h + 1)) for h in range(NA_HG)]

    @pl.when(i == 0)
    def _():
        for hs in heads:
            ks_ref[0:L, hs] = _rms_rows(kc_ref[:, hs].astype(F32), kn_ref[...]).astype(BF16)
            ks_ref[L:, hs] = _rms_rows(kl_ref[:, hs].astype(F32), kn_ref[...]).astype(BF16)

    left, ok_l, ok_r = tiles
    lane = lax.broadcasted_iota(jnp.int32, (GW, 2 * GW), 1)
    for p, first_step in enumerate((1, 2, NBLK_L)):
        @pl.when(i == first_step)
        def _(p=p):
            for h in range(NA_HG):
                for a in range(BLK // GW):
                    for m in range(NA_KROWS // 2):
                        dst = (h, slice(GW * a, GW * (a + 1)), slice(2 * GW * m, 2 * GW * (m + 1)))
                        if not (ok_l[p, a, m] or ok_r[p, a, m]):
                            bias_ref[dst] = jnp.full((GW, 2 * GW), NEG, F32)
                            continue
                        t = pair_ref[h, int(left[p, a, m]) + 1]
                        if not ok_l[p, a, m]:
                            t = jnp.where(lane >= GW, t, NEG)
                        if not ok_r[p, a, m]:
                            t = jnp.where(lane < GW, t, NEG)
                        bias_ref[dst] = t

    def query(hs):
        return (_rms_rows(q_ref[:, hs].astype(F32), qn_ref[...]) * (NA_D ** -0.5)).astype(BF16)

    @pl.when(i == 0)
    def _():
        for hs in heads:
            s_c = _dot_nt(query(hs), ks_ref[0:L, hs])
            p = jnp.exp(s_c - jnp.max(s_c, axis=-1, keepdims=True))
            l = jnp.sum(p, axis=-1, keepdims=True)
            o_ref[:, hs] = (_dot(p.astype(BF16), vc_ref[:, hs]) / l).astype(o_ref.dtype)

    @pl.when(i > 0)
    def _():
        row0 = jnp.clip(4 * (i - 1) - 4, 0, GW - NA_KROWS)
        k0 = pl.multiple_of(row0 * GW, BLK)
        for h, hs in enumerate(heads):
            q = query(hs)
            s_c = _dot_nt(q, ks_ref[0:L, hs])
            s_l = _dot_nt(q, ks_ref[pl.ds(L + k0, NA_KW), hs]) + bias_ref[h]
            m = jnp.maximum(jnp.max(s_c, axis=-1, keepdims=True), jnp.max(s_l, axis=-1, keepdims=True))
            p_c = jnp.exp(s_c - m)
            p_l = jnp.exp(s_l - m)
            l = jnp.sum(p_c, axis=-1, keepdims=True) + jnp.sum(p_l, axis=-1, keepdims=True)
            o = _dot(p_l.astype(BF16), vl_ref[pl.ds(k0, NA_KW), hs]) + _dot(p_c.astype(BF16), vc_ref[:, hs])
            o_ref[:, hs] = (o / l).astype(o_ref.dtype)


def _na_bias_index():
    dr = np.zeros((3, BLK, NA_KW), np.int32)
    dc = np.zeros((3, BLK, NA_KW), np.int32)
    ok = np.zeros((3, BLK, NA_KW), bool)
    rows = S // GW
    for p, blk in enumerate((0, 5, NBLK_L - 1)):
        ks = int(np.clip(4 * blk - 4, 0, GW - NA_KROWS))
        r = 4 * blk + np.arange(BLK) // GW
        c = np.arange(BLK) % GW
        rk = ks + np.arange(NA_KW) // GW
        ck = np.arange(NA_KW) % GW
        r0 = np.clip(r - NA_R // 2, 0, rows - NA_R)
        ws = np.clip(c - NA_C // 2, 0, GW - NA_C)
        row_ok = (rk[None, :] >= r0[:, None]) & (rk[None, :] < r0[:, None] + NA_R)
        col_ok = (ck[None, :] >= ws[:, None]) & (ck[None, :] < ws[:, None] + NA_C)
        ok[p] = row_ok & col_ok
        dr[p] = np.clip(rk[None, :] - r[:, None] + NA_R - 1, 0, 2 * NA_R - 2)
        dc[p] = np.clip(ck[None, :] - c[:, None], -(NA_C - 1), NA_C - 1) + NA_C - 1
    return dr, dc, ok


NA_NDR = 2 * NA_R


def _na_bias_tiles():
    dr, _, ok = _na_bias_index()
    qr, kr = BLK // GW, NA_KROWS
    dr_t = dr.reshape(3, qr, GW, kr, GW)[:, :, 0, :, 0]
    ok_t = ok.reshape(3, qr, GW, kr, GW).any(axis=(2, 4))
    rk0 = np.array([int(np.clip(4 * blk - 4, 0, GW - NA_KROWS)) for blk in (0, 5, NBLK_L - 1)])
    r = np.array([4 * blk for blk in (0, 5, NBLK_L - 1)])[:, None, None] + np.arange(qr)[None, :, None]
    left = (rk0[:, None, None] + 2 * np.arange(kr // 2)[None, None, :]) - r + NA_R - 1
    assert np.all(dr_t[:, :, 0::2][ok_t[:, :, 0::2]] == left[ok_t[:, :, 0::2]])
    return left, ok_t[:, :, 0::2], ok_t[:, :, 1::2]


def _na_bias_pairs(rpb):
    _, dc, ok = _na_bias_index()
    qr, kr = BLK // GW, NA_KROWS
    dc_t = dc[0].reshape(qr, GW, kr, GW)[0, :, 0, :].reshape(GW * GW)
    col_ok = ok.reshape(3, qr, GW, kr, GW).any(axis=(0, 1, 3))
    oh_c = jnp.asarray(np.eye(2 * NA_C - 1, dtype=np.float32)[:, dc_t])
    t1 = jnp.einsum('hrd,dx->hrx', rpb.astype(F32), oh_c, precision=lax.Precision.HIGHEST)
    t1 = jnp.where(col_ok.reshape(1, 1, GW * GW), t1, NEG).reshape(NA_H, 2 * NA_R - 1, GW, GW)
    t1 = jnp.pad(t1, ((0, 0), (1, 1), (0, 0), (0, 0)))
    return jnp.concatenate([t1[:, :-1], t1[:, 1:]], axis=-1)


def na_attention(proj, rpb, qn, kn):
    pairs = _na_bias_pairs(rpb)

    cq, ck, cv = _COL_NAQ // NA_GW, _COL_NAK // NA_GW, _COL_NAV // NA_GW
    return pl.pallas_call(
        functools.partial(_na_kernel, _na_bias_tiles()),
        grid=(NB, NA_H // NA_HG, NBLK_L + 1),
        in_specs=[
            pl.BlockSpec((BLK, NA_GW), lambda b, h, i: (_qblk(b, i), cq + h)),
            pl.BlockSpec((S, NA_GW), lambda b, h, i: (b, ck + h)),
            pl.BlockSpec((L, NA_GW), lambda b, h, i: (CTX_BLK0 + b, ck + h)),
            pl.BlockSpec((S, NA_GW), lambda b, h, i: (b, cv + h)),
            pl.BlockSpec((L, NA_GW), lambda b, h, i: (CTX_BLK0 + b, cv + h)),
            pl.BlockSpec((NA_HG, NA_NDR, GW, 2 * GW), lambda b, h, i: (h, 0, 0, 0)),
            pl.BlockSpec((1, NA_D), lambda b, h, i: (0, 0)),
            pl.BlockSpec((1, NA_D), lambda b, h, i: (0, 0)),
        ],
        out_specs=pl.BlockSpec((BLK, NA_GW), lambda b, h, i: (_qblk(b, i), h)),
        out_shape=jax.ShapeDtypeStruct((NT, NA_W), BF16),
        scratch_shapes=[pltpu.VMEM((L + S, NA_GW), BF16), pltpu.VMEM((NA_HG, BLK, NA_KW), F32)],
        compiler_params=_cp(("arbitrary", "arbitrary", "arbitrary")),
        name="na_attention",
    )(proj, proj, proj, proj, proj, pairs, qn.reshape(1, NA_D), kn.reshape(1, NA_D))


MLA_HW = 256


def _half_mask(h, width=128):
    lane = lax.broadcasted_iota(jnp.int32, (1, width), 1)
    return (lane < 64) if h % 2 == 0 else (lane >= 64)


def _mla_q_kernel(cq_ref, w_ref, qa_ref, gn_ref, g2_ref, gs2_ref, c2_ref, s2_ref, q_ref):
    x = _rms_rows(cq_ref[...].astype(F32), qa_ref[...]).astype(BF16)
    y = _dot(x, w_ref[...])
    nw = MLA_H * MLA_NOPE
    gc = g2_ref[...] * c2_ref[...]
    gs = gs2_ref[...] * s2_ref[...]
    for p in range(MLA_H // 2):
        r1 = y[:, nw + 128 * p: nw + 128 * (p + 1)]
        r2 = y[:, nw + 768 + 128 * p: nw + 768 + 128 * (p + 1)]
        rot = r1 * gc + r2 * gs
        sq = r1 * r1
        for h in (2 * p, 2 * p + 1):
            msk = _half_mask(h)
            nope = y[:, 128 * h: 128 * (h + 1)]
            ss = jnp.sum(nope * nope, axis=-1, keepdims=True) + jnp.sum(
                jnp.where(msk, sq, 0.0), axis=-1, keepdims=True)
            inv = lax.rsqrt(ss / MLA_QK + EPS) * (MLA_QK ** -0.5)
            q_ref[h, :, 0:128] = (nope * gn_ref[...] * inv).astype(BF16)
            q_ref[h, :, 128:256] = (jnp.where(msk, rot, 0.0) * inv).astype(BF16)


def _mla_kv_kernel(ckv_ref, kr_ref, w_ref, kva_ref, gn_ref, g2_ref, gs2_ref, c2_ref, s2_ref,
                   k_ref, v_ref):
    x = _rms_rows(ckv_ref[...].astype(F32), kva_ref[...]).astype(BF16)
    y = _dot(x, w_ref[...])
    kr = kr_ref[...].astype(F32)
    r1 = kr[:, 0:128]
    r2 = kr[:, 128:256]
    rot = r1 * (g2_ref[...] * c2_ref[...]) + r2 * (gs2_ref[...] * s2_ref[...])
    ss_r = jnp.sum(jnp.where(_half_mask(0), r1 * r1, 0.0), axis=-1, keepdims=True)
    nw = MLA_H * MLA_NOPE
    lane0 = lax.broadcasted_iota(jnp.int32, (kr.shape[0], 128), 1) == 0
    for h in range(MLA_H):
        nope = y[:, 128 * h: 128 * (h + 1)]
        ss = jnp.sum(nope * nope, axis=-1, keepdims=True) + ss_r
        inv = lax.rsqrt(ss / MLA_QK + EPS)
        k_ref[h, :, 0:128] = (nope * gn_ref[...] * inv).astype(BF16)
        k_ref[h, :, 128:256] = (jnp.where(_half_mask(h), rot, 0.0) * inv).astype(BF16)
        v_ref[h, :, 0:128] = y[:, nw + 128 * h: nw + 128 * (h + 1)].astype(BF16)
        v_ref[h, :, 128:256] = jnp.where(lane0, 1.0, 0.0).astype(BF16)


def _rope_tables():
    t = jnp.arange(S, dtype=jnp.int32)
    pos = (t // GW, t % GW)
    nf = MLA_ROPE // 4
    inv = ROPE_BASE ** (-jnp.arange(nf, dtype=F32) / nf)
    cs, sn = [], []
    for ax in range(2):
        ang = pos[ax].astype(F32)[:, None] * inv[None, :]
        c, s = jnp.cos(ang), jnp.sin(ang)
        cs += [c, c]
        sn += [-s, s]
    c64 = jnp.tile(jnp.concatenate(cs, axis=1), (NB, 1))
    s64 = jnp.tile(jnp.concatenate(sn, axis=1), (NB, 1))
    c64 = jnp.concatenate([c64, jnp.ones((NB * L, MLA_ROPE), F32)], axis=0)
    s64 = jnp.concatenate([s64, jnp.zeros((NB * L, MLA_ROPE), F32)], axis=0)
    return jnp.tile(c64, (1, 2)), jnp.tile(s64, (1, 2))


_ROPE_SWAP = np.concatenate([np.arange(16, 32), np.arange(0, 16), np.arange(48, 64), np.arange(32, 48)])


def _rope_gains(g):
    gr = g[MLA_NOPE:]
    return (g[:MLA_NOPE].reshape(1, 128), jnp.tile(gr, 2).reshape(1, 128),
            jnp.tile(gr[_ROPE_SWAP], 2).reshape(1, 128))


def mla_q_prep(proj, w_uq_r, qa_g, qn_g, c2, s2):
    tm = 512
    gn, g2, gs2 = _rope_gains(qn_g)
    vec = lambda w: pl.BlockSpec((1, w), lambda i: (0, 0))
    return pl.pallas_call(
        _mla_q_kernel,
        grid=(NT // tm,),
        in_specs=[
            pl.BlockSpec((tm, MLA_QL), lambda i: (i, _COL_CQ // MLA_QL)),
            pl.BlockSpec((MLA_QL, 3072), lambda i: (0, 0)),
            vec(MLA_QL), vec(128), vec(128), vec(128),
            pl.BlockSpec((tm, 128), lambda i: (i, 0)),
            pl.BlockSpec((tm, 128), lambda i: (i, 0)),
        ],
        out_specs=pl.BlockSpec((MLA_H, tm, MLA_HW), lambda i: (0, i, 0)),
        out_shape=jax.ShapeDtypeStruct((MLA_H, NT, MLA_HW), BF16),
        compiler_params=_cp(("arbitrary",)),
        name="mla_q_prep",
    )(proj, w_uq_r, qa_g.reshape(1, MLA_QL), gn, g2, gs2, c2, s2)


def mla_kv_prep(proj, w_ukv_r, kva_g, kn_g, c2, s2):
    tm = 512
    gn, g2, gs2 = _rope_gains(kn_g)
    vec = lambda w: pl.BlockSpec((1, w), lambda i: (0, 0))
    return pl.pallas_call(
        _mla_kv_kernel,
        grid=(NT // tm,),
        in_specs=[
            pl.BlockSpec((tm, MLA_KVL), lambda i: (i, _COL_CKV // MLA_KVL)),
            pl.BlockSpec((tm, 256), lambda i: (i, _COL_KR // 256)),
            pl.BlockSpec((MLA_KVL, 3072), lambda i: (0, 0)),
            vec(MLA_KVL), vec(128), vec(128), vec(128),
            pl.BlockSpec((tm, 128), lambda i: (i, 0)),
            pl.BlockSpec((tm, 128), lambda i: (i, 0)),
        ],
        out_specs=[pl.BlockSpec((MLA_H, tm, MLA_HW), lambda i: (0, i, 0)),
                   pl.BlockSpec((MLA_H, tm, MLA_HW), lambda i: (0, i, 0))],
        out_shape=[jax.ShapeDtypeStruct((MLA_H, NT, MLA_HW), BF16),
                   jax.ShapeDtypeStruct((MLA_H, NT, MLA_HW), BF16)],
        compiler_params=_cp(("arbitrary",)),
        name="mla_kv_prep",
    )(proj, proj, w_ukv_r, kva_g.reshape(1, MLA_KVL), gn, g2, gs2, c2, s2)


MLA_TQ = 256


def _mla_attn_kernel(q_ref, kl_ref, kc_ref, vl_ref, vc_ref, o_ref, s0_ref, s1_ref, p0_ref, p1_ref):
    g = pl.program_id(0)

    @pl.when(g == 0)
    def _():
        s0_ref[...] = jnp.zeros_like(s0_ref)
        s1_ref[...] = jnp.zeros_like(s1_ref)
        p0_ref[...] = jnp.ones_like(p0_ref)
        p1_ref[...] = jnp.ones_like(p1_ref)

    n_kc = (L + S) // L
    rows = MLA_TQ // (n_kc - 1)

    def tie(x, dep):
        if dep is None:
            return x
        return jnp.concatenate([x[0:16, :] + dep, x[16:, :]], axis=0)

    def stages(s_new, s_old, p_new, p_old):
        acc, dep = None, None
        for c in range(n_kc):
            keys = slice(L * c, L * (c + 1))
            k = kc_ref[...] if c == 0 else kl_ref[L * (c - 1): L * c, :]
            v = vc_ref[...] if c == 0 else vl_ref[L * (c - 1): L * c, :]
            part = _dot(p_new[:, keys], v)
            acc = part if acc is None else acc + part
            s_new[:, keys] = _dot_nt(tie(q_ref[...], dep), k)
            if c < n_kc - 1:
                r = slice(rows * c, rows * (c + 1))
                s = s_old[r, :]
                p = jnp.exp(s - jnp.max(s, axis=-1, keepdims=True)).astype(BF16)
                p_old[r, :] = p
                dep = (p[0:16, 0:MLA_HW].astype(F32) * 0.0).astype(BF16)
        o_ref[...] = (acc[:, :MLA_V] / acc[:, MLA_V:MLA_V + 1]).astype(o_ref.dtype)

    @pl.when(g % 2 == 0)
    def _():
        stages(s0_ref, s1_ref, p0_ref, p1_ref)

    @pl.when(g % 2 == 1)
    def _():
        stages(s1_ref, s0_ref, p1_ref, p0_ref)


def _mla_ctx_kernel(q_ref, kc_ref, vc_ref, o_ref):
    s = _dot_nt(q_ref[...], kc_ref[...])
    p = jnp.exp(s - jnp.max(s, axis=-1, keepdims=True)).astype(BF16)
    acc = _dot(p, vc_ref[...])
    o_ref[...] = (acc[:, :MLA_V] / acc[:, MLA_V:MLA_V + 1]).astype(o_ref.dtype)


def mla_attention(q, k, v, need_ctx):
    nq = S // MLA_TQ
    n_steps = NB * MLA_H * nq

    def bhi(g):
        return g // (MLA_H * nq), (g // nq) % MLA_H, g % nq

    def cur(g):
        return bhi(jnp.minimum(g, n_steps - 1))

    def prev(g):
        return bhi(jnp.maximum(g - 2, 0))

    ob = pl.pallas_call(
        _mla_attn_kernel,
        grid=(n_steps + 2,),
        in_specs=[
            pl.BlockSpec((None, MLA_TQ, MLA_HW), lambda g: (cur(g)[1], cur(g)[0] * nq + cur(g)[2], 0)),
            pl.BlockSpec((None, S, MLA_HW), lambda g: (cur(g)[1], cur(g)[0], 0)),
            pl.BlockSpec((None, L, MLA_HW), lambda g: (cur(g)[1], CTX_BLK0 + cur(g)[0], 0)),
            pl.BlockSpec((None, S, MLA_HW), lambda g: (prev(g)[1], prev(g)[0], 0)),
            pl.BlockSpec((None, L, MLA_HW), lambda g: (prev(g)[1], CTX_BLK0 + prev(g)[0], 0)),
        ],
        out_specs=pl.BlockSpec((MLA_TQ, MLA_V), lambda g: (prev(g)[0] * nq + prev(g)[2], prev(g)[1])),
        out_shape=jax.ShapeDtypeStruct((NL, MLA_W), BF16),
        scratch_shapes=[pltpu.VMEM((MLA_TQ, L + S), F32), pltpu.VMEM((MLA_TQ, L + S), F32),
                        pltpu.VMEM((MLA_TQ, L + S), BF16), pltpu.VMEM((MLA_TQ, L + S), BF16)],
        compiler_params=_cp(("arbitrary",)),
        name="mla_attention",
    )(q, k, k, v, v)
    if not need_ctx:
        return ob, None
    ob_ctx = pl.pallas_call(
        _mla_ctx_kernel,
        grid=(NB, MLA_H),
        in_specs=[
            pl.BlockSpec((None, L, MLA_HW), lambda b, h: (h, CTX_BLK0 + b, 0)),
            pl.BlockSpec((None, L, MLA_HW), lambda b, h: (h, CTX_BLK0 + b, 0)),
            pl.BlockSpec((None, L, MLA_HW), lambda b, h: (h, CTX_BLK0 + b, 0)),
        ],
        out_specs=pl.BlockSpec((L, MLA_V), lambda b, h: (b, h)),
        out_shape=jax.ShapeDtypeStruct((NB * L, MLA_W), BF16),
        compiler_params=_cp(("arbitrary", "arbitrary")),
        name="mla_attention_ctx",
    )(q, k, v)
    return ob, ob_ctx


N_CH_C = L // CH
N_CH_L = S // CH
N_CH = N_CH_C + N_CH_L


def _gla_kernel(qf_ref, kf_ref, vf_ref, af_ref, qb_ref, kb_ref, vb_ref, ab_ref,
                wg_ref, bg_ref, of_ref, ob_ref, st_ref):
    t = pl.program_id(1)

    @pl.when(t == 0)
    def _():
        st_ref[...] = jnp.zeros_like(st_ref)

    row = lax.broadcasted_iota(jnp.int32, (CH, CH), 0)
    col = lax.broadcasted_iota(jnp.int32, (CH, CH), 1)
    dirs = (
        (qf_ref, kf_ref, vf_ref, af_ref, of_ref, col <= row, CH - 1),
        (qb_ref, kb_ref, vb_ref, ab_ref, ob_ref, col >= row, 0),
    )
    for d, (q_ref, k_ref, v_ref, a_ref, o_ref, keep, last) in enumerate(dirs):
        z = _dot(a_ref[...], wg_ref[d]) + bg_ref[d]
        g = (jnp.minimum(z, 0.0) - jnp.log(1.0 + jnp.exp(-jnp.abs(z)))) / GLA_TAU
        b = jnp.dot(keep.astype(F32), g, preferred_element_type=F32,
                    precision=lax.Precision.HIGHEST)
        b_end = b[last:last + 1, :]
        b_mid = b[CH // 2:CH // 2 + 1, :]
        q = q_ref[...].astype(F32) * (GLA_DK ** -0.5)
        k = k_ref[...].astype(F32)
        qa = (q * jnp.exp(b - b_mid)).astype(BF16)
        ka = (k * jnp.exp(b_mid - b)).astype(BF16)
        qd = (q * jnp.exp(b)).astype(BF16)
        ke = (k * jnp.exp(b_end - b)).astype(BF16)
        e_end = jnp.exp(b_end)
        v = v_ref[...]
        for h in range(GLA_H):
            ksl = slice(GLA_DK * h, GLA_DK * (h + 1))
            vsl = slice(GLA_DV * h, GLA_DV * (h + 1))
            att = jnp.where(keep, _dot_nt(qa[:, ksl], ka[:, ksl]), 0.0)
            st = st_ref[d, h]
            o = _dot_nt(qd[:, ksl], st.astype(BF16)) + _dot(att.astype(BF16), v[:, vsl])
            o_ref[:, vsl] = o.astype(o_ref.dtype)
            st_ref[d, h] = st * e_end[:, ksl] + _dot_tn(v[:, vsl], ke[:, ksl])


def gla_scan(proj, wg_f, bg_f, wg_b, bg_b):
    wg = jnp.zeros((2, 128, GLA_KW), F32)
    wg = wg.at[0, 0:GLA_RANK].set(wg_f).at[1, GLA_RANK:2 * GLA_RANK].set(wg_b).astype(BF16)
    bg = jnp.stack([bg_f, bg_b]).reshape(2, 1, GLA_KW)

    ctx0 = NL // CH

    def fwd(b, t):
        return jnp.where(t < N_CH_C, ctx0 + N_CH_C * b + t, N_CH_L * b + t - N_CH_C)

    def bwd(b, t):
        return jnp.where(t < N_CH_C, ctx0 + N_CH_C * b + N_CH_C - 1 - t, N_CH_L * b + N_CH - 1 - t)

    def specs(rowfn):
        return [
            pl.BlockSpec((CH, GLA_KW), lambda b, t: (rowfn(b, t), _COL_GQ // GLA_KW)),
            pl.BlockSpec((CH, GLA_KW), lambda b, t: (rowfn(b, t), _COL_GK // GLA_KW)),
            pl.BlockSpec((CH, GLA_W), lambda b, t: (rowfn(b, t), _COL_GV // GLA_W)),
            pl.BlockSpec((CH, 128), lambda b, t: (rowfn(b, t), _COL_GA // 128)),
        ]

    return pl.pallas_call(
        _gla_kernel,
        grid=(NB, N_CH),
        in_specs=specs(fwd) + specs(bwd) + [
            pl.BlockSpec((2, 128, GLA_KW), lambda b, t: (0, 0, 0)),
            pl.BlockSpec((2, 1, GLA_KW), lambda b, t: (0, 0, 0)),
        ],
        out_specs=[pl.BlockSpec((CH, GLA_W), lambda b, t: (fwd(b, t), 0)),
                   pl.BlockSpec((CH, GLA_W), lambda b, t: (bwd(b, t), 0))],
        out_shape=[jax.ShapeDtypeStruct((NT, GLA_W), BF16)] * 2,
        scratch_shapes=[pltpu.VMEM((2, GLA_H, GLA_DV, GLA_DK), F32)],
        compiler_params=_cp(("arbitrary", "arbitrary")),
        name="gla_scan",
    )(proj, proj, proj, proj, proj, proj, proj, proj, wg, bg)


def _gla_out_kernel(of_ref, ob_ref, g_ref, gn_ref, o_ref):
    for h in range(GLA_H):
        sl = slice(GLA_DV * h, GLA_DV * (h + 1))
        o = _rms_rows(of_ref[:, sl].astype(F32) + ob_ref[:, sl].astype(F32), gn_ref[...])
        gate = g_ref[:, sl].astype(F32)
        o_ref[:, sl] = (o * (gate * _sigmoid(gate))).astype(o_ref.dtype)


def gla_output(o_f, o_b, proj, on_g):
    tm = 512
    return pl.pallas_call(
        _gla_out_kernel,
        grid=(NT // tm,),
        in_specs=[
            pl.BlockSpec((tm, GLA_W), lambda i: (i, 0)),
            pl.BlockSpec((tm, GLA_W), lambda i: (i, 0)),
            pl.BlockSpec((tm, GLA_W), lambda i: (i, _COL_GG // GLA_W)),
            pl.BlockSpec((1, GLA_DV), lambda i: (0, 0)),
        ],
        out_specs=pl.BlockSpec((tm, GLA_W), lambda i: (i, 0)),
        out_shape=jax.ShapeDtypeStruct((NT, GLA_W), BF16),
        compiler_params=_cp(("arbitrary",)),
        name="gla_output",
    )(o_f, o_b, proj, on_g.reshape(1, GLA_DV))


OUT_TN = 1024


def _out_proj_kernel(with_ctx, layer, oa_ref, obl_ref, obc_ref, oc_ref, w_ref, xl_ref, xc_ref, g_ref, *rest):
    o_refs, (stage_ref, wb_ref, sem) = rest[:-3], rest[-3:]
    j, i = pl.program_id(0), pl.program_id(1)

    def tile(jj):
        cols = pl.ds(pl.multiple_of(jj * OUT_TN, OUT_TN), OUT_TN)
        return pltpu.make_async_copy(w_ref.at[layer, :, cols], stage_ref, sem)

    @pl.when(i == 0)
    def _():
        @pl.when(j == 0)
        def _():
            tile(0).start()

        tile(j).wait()
        wb_ref[...] = stage_ref[...].astype(BF16)

        @pl.when(j + 1 < D // OUT_TN)
        def _():
            tile(j + 1).start()

    a = jnp.concatenate([oa_ref[...], _stream_block(i, obl_ref, obc_ref), oc_ref[...]], axis=1)
    y = _stream_block(i, xl_ref, xc_ref) + g_ref[...] * _dot(a, wb_ref[...])
    if not with_ctx:
        o_refs[0][...] = y
        return

    @pl.when(i < NLB)
    def _():
        o_refs[0][...] = y

    @pl.when(i == NLB)
    def _():
        o_refs[1][...] = y


def out_projection(oa, ob, ob_ctx, oc, w_out, layer, xl, xc, mod, k_gate, with_ctx):
    tm, tn = CTX_ROWS, OUT_TN
    nj = D // tn
    lat = lambda j, i: (jnp.minimum(i, NLB - 1), j)
    out_specs = [pl.BlockSpec((tm, tn), lat)]
    out_shape = [jax.ShapeDtypeStruct((NL, D), F32)]
    if with_ctx:
        out_specs.append(pl.BlockSpec((tm, tn), lambda j, i: (0, j)))
        out_shape.append(jax.ShapeDtypeStruct((CTX_ROWS, D), F32))
    res = pl.pallas_call(
        functools.partial(_out_proj_kernel, with_ctx, layer),
        grid=(nj, NLB + (1 if with_ctx else 0)),
        in_specs=[
            pl.BlockSpec((tm, NA_W), lambda j, i: (i, 0)),
            pl.BlockSpec((tm, MLA_W), lambda j, i: (jnp.minimum(i, NLB - 1), 0)),
            pl.BlockSpec((tm, MLA_W), lambda j, i: (0, 0)),
            pl.BlockSpec((tm, GLA_W), lambda j, i: (i, 0)),
            pl.BlockSpec(memory_space=pl.ANY),
            pl.BlockSpec((tm, tn), lat),
            pl.BlockSpec((tm, tn), lambda j, i: (0, j)),
            pl.BlockSpec((None, 1, tn), lambda j, i: (_row_group(i, tm), 0, k_gate * nj + j)),
        ],
        out_specs=out_specs,
        out_shape=out_shape,
        scratch_shapes=[pltpu.VMEM((D, tn), F32), pltpu.VMEM((D, tn), BF16), pltpu.SemaphoreType.DMA(())],
        compiler_params=_cp(("arbitrary", "arbitrary")),
        name="out_projection",
    )(oa, ob, ob_ctx if with_ctx else ob, oc, w_out, xl, xc, mod)
    return (res[0], res[1]) if with_ctx else (res[0], None)


def _route(logits, router_bias):
    n = logits.shape[0]
    scores = jax.nn.sigmoid(logits)
    grouped = (scores + router_bias.astype(F32)).reshape(n, E_GROUPS, E_PER)

    def top2(a):
        idx = lax.broadcasted_iota(jnp.int32, a.shape, a.ndim - 1)
        i1 = jnp.argmax(a, axis=-1).astype(jnp.int32)
        rest = jnp.where(idx == i1[..., None], -jnp.inf, a)
        i2 = jnp.argmax(rest, axis=-1).astype(jnp.int32)
        return jnp.max(a, axis=-1), jnp.max(rest, axis=-1), i1, i2

    m1, m2, _, _ = top2(grouped)
    grp = jnp.argmax(m1 + m2, axis=-1).astype(jnp.int32)
    gsel = lax.broadcasted_iota(jnp.int32, (n, E_GROUPS, E_PER), 1) == grp[:, None, None]
    in_group = jnp.sum(jnp.where(gsel, grouped, 0.0), axis=1)
    _, _, l1, l2 = top2(in_group)
    expert_idx = grp[:, None] * E_PER + jnp.stack([l1, l2], axis=-1)
    esel = lax.broadcasted_iota(jnp.int32, (n, TOPK, E), 2) == expert_idx[:, :, None]
    w = jnp.sum(jnp.where(esel, scores[:, None, :], 0.0), axis=-1)
    return expert_idx, w / jnp.sum(w, axis=-1, keepdims=True)


def _dispatch_plan(expert_idx):
    nk = expert_idx.shape[0] * TOPK
    nb = _moe_blocks(expert_idx.shape[0])
    flat_e = expert_idx.reshape(nk)
    onehot = (flat_e[:, None] == jnp.arange(E, dtype=jnp.int32)[None, :]).astype(jnp.int32)
    csum = jnp.cumsum(onehot, axis=0)
    counts = csum[-1]
    rank = jnp.take_along_axis(csum, flat_e[:, None], axis=1)[:, 0] - 1
    padded = (counts + MOE_R - 1) // MOE_R * MOE_R
    pad_end = jnp.cumsum(padded)
    pad_start = pad_end - padded
    dest = (pad_start[flat_e] + rank).astype(jnp.int32)
    row_token = jnp.zeros((nb * MOE_R,), jnp.int32).at[dest].set(jnp.arange(nk, dtype=jnp.int32) // TOPK)
    blk0 = jnp.arange(nb, dtype=jnp.int32) * MOE_R
    block_expert = jnp.minimum(jnp.sum((pad_end[None, :] <= blk0[:, None]).astype(jnp.int32), axis=1), E - 1)
    n_active = (pad_end[-1] // MOE_R).astype(jnp.int32).reshape(1)
    n_valid = jnp.clip(counts[block_expert] - (blk0 - pad_start[block_expert]), 0, MOE_R).astype(jnp.int32)
    bidx = jnp.arange(nb, dtype=jnp.int32)
    prev_e = jnp.concatenate([jnp.full((1,), -1, jnp.int32), block_expert[:-1]])
    first = jnp.logical_and(bidx < n_active[0], block_expert != prev_e)
    later = jnp.where(first, bidx, nb)
    nxt_blk = jnp.concatenate([lax.cummin(later[::-1])[::-1][1:], jnp.full((1,), nb, jnp.int32)])
    nxt_e = jnp.where(nxt_blk < nb, block_expert[jnp.minimum(nxt_blk, nb - 1)], -1).astype(jnp.int32)
    plan = (block_expert, n_active, first.astype(jnp.int32), nxt_e)
    return dest, row_token, n_valid, plan


DH = D // 2


DMA_UNROLL = 8


def _for_rows(n, body):
    full = n // DMA_UNROLL

    def group(t, c):
        for u in range(DMA_UNROLL):
            body(t * DMA_UNROLL + u)
        return c

    def single(r, c):
        body(r)
        return c

    lax.fori_loop(0, full, group, 0)
    lax.fori_loop(full * DMA_UNROLL, n, single, 0)


def _gather_rows_kernel(tok_ref, nv_ref, h_ref, o_ref, sem):
    i = pl.program_id(0)
    base = i * MOE_R
    nv = nv_ref[i]

    @pl.when(nv < MOE_R)
    def _():
        o_ref[...] = jnp.zeros_like(o_ref)

    def copy(r):
        return pltpu.make_async_copy(h_ref.at[pl.ds(tok_ref[base + r], 1)], o_ref.at[pl.ds(r, 1)], sem)

    _for_rows(nv, lambda r: copy(r).start())
    _for_rows(nv, lambda r: copy(r).wait())


def gather_rows(row_token, n_valid, h):
    nb = n_valid.shape[0]
    return pl.pallas_call(
        _gather_rows_kernel,
        grid_spec=pltpu.PrefetchScalarGridSpec(
            num_scalar_prefetch=2,
            grid=(nb,),
            in_specs=[pl.BlockSpec(memory_space=pl.ANY)],
            out_specs=pl.BlockSpec((MOE_R, DH), lambda i, tok, nv: (i, 0)),
            scratch_shapes=[pltpu.SemaphoreType.DMA(())],
        ),
        out_shape=jax.ShapeDtypeStruct((nb * MOE_R, DH), jnp.uint32),
        compiler_params=_cp(("arbitrary",)),
        name="moe_gather_rows",
    )(row_token, n_valid, h)


def _stage_weights(layer, n_tiles, width, be_ref, first_ref, nxt_ref, active, mats):
    t = pl.program_id(0)
    i = pl.program_id(1)

    def copies(e, tt):
        cols = pl.ds(pl.multiple_of(tt * width, width), width)
        return [pltpu.make_async_copy(w.at[layer, e, :, cols], st, sm) for w, st, _, sm in mats]

    @pl.when(jnp.logical_and(active, first_ref[i] == 1))
    def _():
        @pl.when(jnp.logical_and(t == 0, i == 0))
        def _():
            for c in copies(be_ref[0], 0):
                c.start()

        for c in copies(be_ref[i], t):
            c.wait()
        for _, st, wb, _ in mats:
            wb[...] = st[...].astype(BF16)

        @pl.when(nxt_ref[i] >= 0)
        def _():
            for c in copies(nxt_ref[i], t):
                c.start()

        @pl.when(jnp.logical_and(nxt_ref[i] < 0, t + 1 < n_tiles))
        def _():
            for c in copies(be_ref[0], t + 1):
                c.start()


def _ffn_up_kernel(layer, be_ref, na_ref, first_ref, nxt_ref, x_ref, w1_ref, w3_ref, o_ref,
                   st1_ref, st3_ref, w1b_ref, w3b_ref, sem):
    i = pl.program_id(1)
    active = i < na_ref[0]
    _stage_weights(layer, FF // MOE_TF, MOE_TF, be_ref, first_ref, nxt_ref, active,
                   [(w1_ref, st1_ref, w1b_ref, sem.at[0]), (w3_ref, st3_ref, w3b_ref, sem.at[1])])

    @pl.when(active)
    def _():
        lo, hi = _unpack_pairs(x_ref[...])
        lo, hi = lo.astype(BF16), hi.astype(BF16)
        a = _dot(lo, w1b_ref[0:DH, :]) + _dot(hi, w1b_ref[DH:, :])
        b = _dot(lo, w3b_ref[0:DH, :]) + _dot(hi, w3b_ref[DH:, :])
        o_ref[...] = (a * _sigmoid(a) * b).astype(o_ref.dtype)

    @pl.when(jnp.logical_not(active))
    def _():
        o_ref[...] = jnp.zeros_like(o_ref)


def ffn_up(plan, xb, w1, w3, layer):
    nb = plan[0].shape[0]
    last = lambda i, na: jnp.minimum(i, na[0] - 1)
    return pl.pallas_call(
        functools.partial(_ffn_up_kernel, layer),
        grid_spec=pltpu.PrefetchScalarGridSpec(
            num_scalar_prefetch=4,
            grid=(FF // MOE_TF, nb),
            in_specs=[
                pl.BlockSpec((MOE_R, DH), lambda j, i, be, na, fi, nx: (last(i, na), 0)),
                pl.BlockSpec(memory_space=pl.ANY),
                pl.BlockSpec(memory_space=pl.ANY),
            ],
            out_specs=pl.BlockSpec((MOE_R, MOE_TF), lambda j, i, be, na, fi, nx: (i, j)),
            scratch_shapes=[pltpu.VMEM((D, MOE_TF), F32), pltpu.VMEM((D, MOE_TF), F32),
                            pltpu.VMEM((D, MOE_TF), BF16), pltpu.VMEM((D, MOE_TF), BF16),
                            pltpu.SemaphoreType.DMA((2,))],
        ),
        out_shape=jax.ShapeDtypeStruct((nb * MOE_R, FF), BF16),
        compiler_params=_cp(("arbitrary", "arbitrary")),
        name="moe_ffn_up",
    )(*plan, xb, w1, w3)


MOE_TN = 2048


def _ffn_down_kernel(layer, be_ref, na_ref, first_ref, nxt_ref, h_ref, w2_ref, o_ref, st2_ref, w2b_ref, sem):
    i = pl.program_id(1)
    active = i < na_ref[0]
    _stage_weights(layer, D // MOE_TN, MOE_TN, be_ref, first_ref, nxt_ref, active,
                   [(w2_ref, st2_ref, w2b_ref, sem.at[0])])

    @pl.when(active)
    def _():
        y = _dot(h_ref[...], w2b_ref[...])
        o_ref[...] = _pack_pairs(y[:, :MOE_TN // 2], y[:, MOE_TN // 2:])

    @pl.when(jnp.logical_not(active))
    def _():
        o_ref[...] = jnp.zeros_like(o_ref)


def ffn_down(plan, hmid, w2, layer):
    nb = plan[0].shape[0]
    last = lambda i, na: jnp.minimum(i, na[0] - 1)
    return pl.pallas_call(
        functools.partial(_ffn_down_kernel, layer),
        grid_spec=pltpu.PrefetchScalarGridSpec(
            num_scalar_prefetch=4,
            grid=(D // MOE_TN, nb),
            in_specs=[
                pl.BlockSpec((MOE_R, FF), lambda n, i, be, na, fi, nx: (last(i, na), 0)),
                pl.BlockSpec(memory_space=pl.ANY),
            ],
            out_specs=pl.BlockSpec((MOE_R, MOE_TN // 2), lambda n, i, be, na, fi, nx: (i, n)),
            scratch_shapes=[pltpu.VMEM((FF, MOE_TN), F32), pltpu.VMEM((FF, MOE_TN), BF16),
                            pltpu.SemaphoreType.DMA((1,))],
        ),
        out_shape=jax.ShapeDtypeStruct((nb * MOE_R, DH), jnp.uint32),
        compiler_params=_cp(("arbitrary", "arbitrary")),
        name="moe_ffn_down",
    )(*plan, hmid, w2)


CMB_T = 256
CMB_RC = 16


def _combine_kernel(with_ctx, pos_ref, yb_ref, gate_ref, xl_ref, xc_ref, g_ref, *rest):
    o_refs, (buf_ref, sem) = rest[:-2], rest[-2:]
    i = pl.program_id(0)
    n_lat = NL // CMB_T
    base = i * CMB_T

    def copy(r, k):
        p = pos_ref[(base + r) * TOPK + k]
        return pltpu.make_async_copy(yb_ref.at[pl.ds(p, 1)], buf_ref.at[k, pl.ds(r, 1)], sem.at[k])

    def start(r, c):
        copy(r, 0).start()
        copy(r, 1).start()
        return c

    def wait(r, c):
        copy(r, 0).wait()
        copy(r, 1).wait()
        return c

    lax.fori_loop(0, CMB_T, start, 0, unroll=DMA_UNROLL)
    lax.fori_loop(0, CMB_T, wait, 0, unroll=DMA_UNROLL)
    half = MOE_TN // 2

    def emit(x_ref, o_ref):
        def rows(t, c):
            r = pl.ds(pl.multiple_of(t * CMB_RC, CMB_RC), CMB_RC)
            g0, g1 = gate_ref[r, 0:1], gate_ref[r, 1:2]
            for n in range(D // MOE_TN):
                lo0, hi0 = _unpack_pairs(buf_ref[0, r, half * n: half * (n + 1)])
                lo1, hi1 = _unpack_pairs(buf_ref[1, r, half * n: half * (n + 1)])
                for part, y in enumerate((lo0 * g0 + lo1 * g1, hi0 * g0 + hi1 * g1)):
                    sl = slice(MOE_TN * n + half * part, MOE_TN * n + half * (part + 1))
                    o_ref[r, sl] = x_ref[r, sl] + g_ref[:, sl] * y
            return c

        lax.fori_loop(0, CMB_T // CMB_RC, rows, 0)

    if not with_ctx:
        emit(xl_ref, o_refs[0])
        return

    @pl.when(i < n_lat)
    def _():
        emit(xl_ref, o_refs[0])

    @pl.when(i >= n_lat)
    def _():
        emit(xc_ref, o_refs[1])


def moe_combine(dest, yb, gates, xl, xc, mod, k_gate, with_ctx):
    n_lat = NL // CMB_T
    lat = lambda i, pos: (jnp.minimum(i, n_lat - 1), 0)
    ctx = lambda i, pos: (jnp.maximum(i - n_lat, 0), 0)
    out_specs = [pl.BlockSpec((CMB_T, D), lat)]
    out_shape = [jax.ShapeDtypeStruct((NL, D), F32)]
    if with_ctx:
        out_specs.append(pl.BlockSpec((CMB_T, D), ctx))
        out_shape.append(jax.ShapeDtypeStruct((CTX_ROWS, D), F32))
    res = pl.pallas_call(
        functools.partial(_combine_kernel, with_ctx),
        grid_spec=pltpu.PrefetchScalarGridSpec(
            num_scalar_prefetch=1,
            grid=((NT if with_ctx else NL) // CMB_T,),
            in_specs=[
                pl.BlockSpec(memory_space=pl.ANY),
                pl.BlockSpec((CMB_T, TOPK), lambda i, pos: (i, 0)),
                pl.BlockSpec((CMB_T, D), lat),
                pl.BlockSpec((CMB_T, D), ctx),
                pl.BlockSpec((None, 1, D), lambda i, pos: (_row_group(i, CMB_T), 0, k_gate)),
            ],
            out_specs=out_specs,
            scratch_shapes=[pltpu.VMEM((TOPK, CMB_T, DH), jnp.uint32), pltpu.SemaphoreType.DMA((TOPK,))],
        ),
        out_shape=out_shape,
        compiler_params=_cp(("arbitrary",)),
        name="moe_combine",
    )(dest, yb, gates, xl, xc, mod)
    return (res[0], res[1]) if with_ctx else (res[0], None)


IN_W = 9312


W_RB = 512
W_NB = PW // W_RB
_W_SRC_KR = 4608
_W_SRC_GA = 9280


def _w_in_layout_kernel(w_ref, o_ref, buf_ref, sem):
    g = pl.program_id(0)

    def run(step, slot, op):
        l, k = step // W_NB, step % W_NB

        @pl.when(k < W_NB - 1)
        def _():
            r0 = pl.multiple_of(jnp.where(k < _W_SRC_KR // W_RB, k * W_RB, k * W_RB + MLA_ROPE), MLA_ROPE)
            op(pltpu.make_async_copy(w_ref.at[l, pl.ds(r0, W_RB)], buf_ref.at[slot], sem.at[slot]))

        @pl.when(k == W_NB - 1)
        def _():
            op(pltpu.make_async_copy(w_ref.at[l, pl.ds(_W_SRC_KR, MLA_ROPE)],
                                     buf_ref.at[slot, pl.ds(0, MLA_ROPE)], sem.at[slot]))
            op(pltpu.make_async_copy(w_ref.at[l, pl.ds(_W_SRC_GA, 2 * GLA_RANK)],
                                     buf_ref.at[slot, pl.ds(MLA_ROPE, 2 * GLA_RANK)], sem.at[slot]))

    start = lambda c: c.start()
    wait = lambda c: c.wait()

    @pl.when(g == 0)
    def _():
        run(0, 0, start)

    @pl.when(g + 1 < DEPTH * W_NB)
    def _():
        run(g + 1, (g + 1) % 2, start)

    run(g, g % 2, wait)
    k = g % W_NB

    @pl.when(k < W_NB - 1)
    def _():
        o_ref[...] = buf_ref[g % 2].astype(BF16)

    @pl.when(k == W_NB - 1)
    def _():
        x = buf_ref[g % 2, 0:MLA_ROPE, :]
        xs = jnp.concatenate([x[16:32], x[0:16], x[48:64], x[32:48]], axis=0)
        ga = buf_ref[g % 2, MLA_ROPE:MLA_ROPE + 2 * GLA_RANK, :]
        o_ref[...] = jnp.concatenate(
            [x, x, xs, xs, ga, jnp.zeros((W_RB - 4 * MLA_ROPE - 2 * GLA_RANK, D), F32)], axis=0).astype(BF16)


def _w_in_layout(w):
    w_t = jnp.swapaxes(w, 1, 2)
    return pl.pallas_call(
        _w_in_layout_kernel,
        grid=(DEPTH * W_NB,),
        in_specs=[pl.BlockSpec(memory_space=pl.ANY)],
        out_specs=pl.BlockSpec((None, W_RB, D), lambda g: (g // W_NB, g % W_NB, 0)),
        out_shape=jax.ShapeDtypeStruct((DEPTH, PW, D), BF16),
        scratch_shapes=[pltpu.VMEM((2, W_RB, D), F32), pltpu.SemaphoreType.DMA((2,))],
        compiler_params=_cp(("arbitrary",)),
        name="w_in_layout",
    )(w_t)


def _w_uq_layout(w):
    w3 = w.reshape(MLA_QL, MLA_H, MLA_QK)
    rope = w3[:, :, MLA_NOPE:]
    rope_s = jnp.concatenate([rope[..., 16:32], rope[..., 0:16], rope[..., 48:64], rope[..., 32:48]], axis=-1)
    parts = [w3[:, :, :MLA_NOPE], rope, rope_s]
    return jnp.concatenate([p.reshape(MLA_QL, -1) for p in parts], axis=1).astype(BF16)


def _w_ukv_layout(w):
    w3 = w.reshape(MLA_KVL, MLA_H, MLA_NOPE + MLA_V)
    parts = [w3[:, :, :MLA_NOPE], w3[:, :, MLA_NOPE:]]
    return jnp.concatenate([p.reshape(MLA_KVL, -1) for p in parts], axis=1).astype(BF16)


def token_mixing_layer(xl, xc, mod, layer, need_ctx, norm1, w_in_b, w_out, na_qn, na_kn, na_rpb, mla_qa, mla_kva,
                       w_uq, w_ukv, mla_qn, mla_kn, gwf, gbf, gwb, gbb, gla_on, c2, s2):
    h = norm_modulate(xl, xc, norm1.reshape(1, D), mod, 0, 1, BF16)
    proj = matmul(h, w_in_b, layer, NT // 4, 512, BF16)
    oa = na_attention(proj, na_rpb, na_qn, na_kn)
    q = mla_q_prep(proj, _w_uq_layout(w_uq), mla_qa, mla_qn, c2, s2)
    k, v = mla_kv_prep(proj, _w_ukv_layout(w_ukv), mla_kva, mla_kn, c2, s2)
    ob, ob_ctx = mla_attention(q, k, v, need_ctx)
    o_f, o_b = gla_scan(proj, gwf, gbf, gwb, gbb)
    oc = gla_output(o_f, o_b, proj, gla_on)
    return out_projection(oa, ob, ob_ctx, oc, w_out, layer, xl, xc, mod, 2, need_ctx)


def moe_layer(xl, xc, mod, norm2, w_router_p, router_bias, w1, w3, w2, layer, need_ctx):
    h, logits = norm_modulate(xl, xc, norm2.reshape(1, D), mod, 3, 4, None, w_router_p,
                              NT if need_ctx else NL)
    expert_idx, gates = _route(logits[:, :E], router_bias)
    dest, row_token, n_valid, plan = _dispatch_plan(expert_idx)
    xb = gather_rows(row_token, n_valid, h)
    hmid = ffn_up(plan, xb, w1, w3, layer)
    yb = ffn_down(plan, hmid, w2, layer)
    return moe_combine(dest, yb, gates, xl, xc, mod, 5, need_ctx)


def kernel(x, c, ctx, c_ctx, w_ada, b_ada, norm1, norm2, w_in, w_out, na_q_norm, na_k_norm, na_rpb, mla_qa_norm, mla_kva_norm, mla_w_uq, mla_w_ukv, mla_q_norm, mla_k_norm, gla_w_gate_f, gla_b_gate_f, gla_w_gate_b, gla_b_gate_b, gla_out_norm, w_router, router_bias, moe_w1, moe_w3, moe_w2):
    cond = jnp.concatenate([c, c_ctx[None, :], jnp.zeros((8 - NB - 1, D), F32)], axis=0)
    mods = ada_modulation(cond, w_ada, b_ada)
    xl, xc = x.reshape(NL, D), ctx.reshape(CTX_ROWS, D)
    c2, s2 = _rope_tables()
    w_router_p = jnp.concatenate([w_router, jnp.zeros((D, 128 - E), F32)], axis=1)
    w_in_b = _w_in_layout(w_in)
    for l in range(DEPTH):
        mod = mods[l].reshape(8, 1, 6 * D)
        need_ctx = l < DEPTH - 1
        xl, xc_new = token_mixing_layer(xl, xc, mod, l, need_ctx, norm1[l], w_in_b, w_out, na_q_norm[l],
                                        na_k_norm[l], na_rpb[l], mla_qa_norm[l], mla_kva_norm[l], mla_w_uq[l],
                                        mla_w_ukv[l], mla_q_norm[l], mla_k_norm[l], gla_w_gate_f[l],
                                        gla_b_gate_f[l], gla_w_gate_b[l], gla_b_gate_b[l], gla_out_norm[l], c2, s2)
        xc = xc_new if need_ctx else xc
        xl, xc_new = moe_layer(xl, xc, mod, norm2[l], w_router_p, router_bias, moe_w1, moe_w3, moe_w2, l, need_ctx)
        xc = xc_new if need_ctx else xc
    return xl.reshape(NB, S, D)
```

```python
import functools

import numpy as np
import jax
import jax.numpy as jnp
from jax import lax
from jax.experimental import pallas as pl
from jax.experimental.pallas import tpu as pltpu

F32 = jnp.float32
BF16 = jnp.bfloat16

D = 4096
NB = 2
S = 4096
L = 256
DEPTH = 2
GW = 64
EPS = 1e-6
NL = NB * S
NT = NL + NB * L
NA_H, NA_D = 8, 128
NA_W = NA_H * NA_D
NA_R, NA_C = 8, 16
MLA_H = 12
MLA_QL, MLA_KVL = 1024, 512
MLA_NOPE, MLA_ROPE, MLA_V = 128, 64, 128
MLA_QK = MLA_NOPE + MLA_ROPE
MLA_W = MLA_H * MLA_V
GLA_H, GLA_DK, GLA_DV = 6, 128, 256
GLA_KW = GLA_H * GLA_DK
GLA_W = GLA_H * GLA_DV
GLA_RANK = 16
GLA_TAU = 16.0
CH = 128
E = 16
E_GROUPS = 4
E_PER = E // E_GROUPS
TOPK = 2
FF = 1024
ROPE_BASE = 10000.0

_COL_NAQ, _COL_NAK, _COL_NAV = 0, 1024, 2048
_COL_CQ, _COL_CKV = 3072, 4096
_COL_GQ, _COL_GK, _COL_GV, _COL_GG = 4608, 5376, 6144, 7680
_COL_KR = 9216
_COL_GA = 9472
PW = 9728

BLK = 256
NBLK_L = S // BLK
CTX_BLK0 = NL // BLK
MOE_R = 256
MOE_TF = 512


def _moe_blocks(n_tokens):
    return (n_tokens * TOPK + E * (MOE_R - 1)) // MOE_R


NEG = -1e30
V7X_VMEM_BYTES = 64 * 1024 * 1024
VMEM_LIMIT = V7X_VMEM_BYTES - 8 * 1024 * 1024


def _cp(sem, vmem=VMEM_LIMIT):
    return pltpu.CompilerParams(dimension_semantics=sem, vmem_limit_bytes=vmem)


def _dot(a, b):
    return jnp.dot(a, b, preferred_element_type=F32)


def _dot_nt(a, b):
    return lax.dot_general(a, b, (((1,), (1,)), ((), ())), preferred_element_type=F32)


def _dot_tn(a, b):
    return lax.dot_general(a, b, (((0,), (0,)), ((), ())), preferred_element_type=F32)


def _sigmoid(x):
    return 1.0 / (1.0 + jnp.exp(-x))


def _pack_pairs(lo, hi):
    lo_w = lax.bitcast_convert_type(lo.astype(BF16).astype(F32), jnp.uint32)
    hi_w = lax.bitcast_convert_type(hi.astype(BF16).astype(F32), jnp.uint32)
    return lax.shift_right_logical(lo_w, jnp.uint32(16)) | (hi_w & jnp.uint32(0xFFFF0000))


def _unpack_pairs(w):
    lo = lax.bitcast_convert_type(lax.shift_left(w, jnp.uint32(16)), F32)
    hi = lax.bitcast_convert_type(w & jnp.uint32(0xFFFF0000), F32)
    return lo, hi


def _row_group(i, tm):
    r0 = i * tm
    return jnp.where(r0 >= NL, 2, r0 // S)


def _ada_kernel(s_ref, w_ref, b_ref, o_ref):
    s = s_ref[...]
    s = s * _sigmoid(s)
    o_ref[...] = _dot(s.astype(BF16), w_ref[...].astype(BF16)) + b_ref[...]


def ada_modulation(cond, w_ada, b_ada):
    tn = 512
    n = 6 * D
    return pl.pallas_call(
        _ada_kernel,
        grid=(DEPTH, n // tn),
        in_specs=[
            pl.BlockSpec((8, D), lambda l, j: (0, 0)),
            pl.BlockSpec((None, D, tn), lambda l, j: (l, 0, j)),
            pl.BlockSpec((None, 1, tn), lambda l, j: (l, 0, j)),
        ],
        out_specs=pl.BlockSpec((None, 8, tn), lambda l, j: (l, 0, j)),
        out_shape=jax.ShapeDtypeStruct((DEPTH, 8, n), F32),
        compiler_params=_cp(("arbitrary", "arbitrary")),
        name="ada_modulation",
    )(cond, w_ada, b_ada.reshape(DEPTH, 1, n))


def _normmod(x, g, sh, sc):
    y = x * lax.rsqrt(jnp.mean(x * x, axis=-1, keepdims=True) + EPS) * g
    return y * (1.0 + sc) + sh


CTX_ROWS = NB * L
NLB = NL // CTX_ROWS


def _stream_block(i, xl_ref, xc_ref):
    return jnp.where(i < NLB, xl_ref[...], xc_ref[...])


def _normmod_kernel(xl_ref, xc_ref, g_ref, sh_ref, sc_ref, h_ref):
    x = _stream_block(pl.program_id(0), xl_ref, xc_ref)
    h_ref[...] = _normmod(x, g_ref[...], sh_ref[...], sc_ref[...]).astype(h_ref.dtype)


def _normmod_router_kernel(xl_ref, xc_ref, g_ref, sh_ref, sc_ref, wr_ref, h_ref, lg_ref):
    x = _stream_block(pl.program_id(0), xl_ref, xc_ref)
    h = _normmod(x, g_ref[...], sh_ref[...], sc_ref[...])
    h_ref[...] = _pack_pairs(h[:, :D // 2], h[:, D // 2:])
    lg_ref[...] = _dot(h.astype(BF16), wr_ref[...].astype(BF16))


def norm_modulate(xl, xc, gain, mod, k_shift, k_scale, out_dtype, w_router=None, n_rows=NT):
    tm = CTX_ROWS
    in_specs = [
        pl.BlockSpec((tm, D), lambda i: (jnp.minimum(i, NLB - 1), 0)),
        pl.BlockSpec((tm, D), lambda i: (0, 0)),
        pl.BlockSpec((1, D), lambda i: (0, 0)),
        pl.BlockSpec((None, 1, D), lambda i: (_row_group(i, tm), 0, k_shift)),
        pl.BlockSpec((None, 1, D), lambda i: (_row_group(i, tm), 0, k_scale)),
    ]
    if w_router is None:
        return pl.pallas_call(
            _normmod_kernel, grid=(n_rows // tm,), in_specs=in_specs,
            out_specs=pl.BlockSpec((tm, D), lambda i: (i, 0)),
            out_shape=jax.ShapeDtypeStruct((n_rows, D), out_dtype),
            compiler_params=_cp(("arbitrary",)), name="norm_modulate",
        )(xl, xc, gain, mod, mod)
    return pl.pallas_call(
        _normmod_router_kernel, grid=(n_rows // tm,),
        in_specs=in_specs + [pl.BlockSpec((D, 128), lambda i: (0, 0))],
        out_specs=[pl.BlockSpec((tm, D // 2), lambda i: (i, 0)), pl.BlockSpec((tm, 128), lambda i: (i, 0))],
        out_shape=[jax.ShapeDtypeStruct((n_rows, D // 2), jnp.uint32), jax.ShapeDtypeStruct((n_rows, 128), F32)],
        compiler_params=_cp(("arbitrary",)), name="norm_modulate_router",
    )(xl, xc, gain, mod, mod, w_router)


def _mm_kernel(a_ref, b_ref, o_ref):
    o_ref[...] = _dot_nt(a_ref[...], b_ref[...]).astype(o_ref.dtype)


def matmul(a, b, layer, tm, tn, out_dtype):
    m, k = a.shape
    n = b.shape[1]
    return pl.pallas_call(
        _mm_kernel,
        grid=(m // tm, n // tn),
        in_specs=[pl.BlockSpec((tm, k), lambda i, j: (i, 0)),
                  pl.BlockSpec((None, tn, k), lambda i, j: (layer, j, 0))],
        out_specs=pl.BlockSpec((tm, tn), lambda i, j: (i, j)),
        out_shape=jax.ShapeDtypeStruct((m, n), out_dtype),
        compiler_params=_cp(("arbitrary", "arbitrary")),
        name="matmul",
    )(a, b)


def _qblk(b, i):
    return jnp.where(i == 0, CTX_BLK0 + b, b * NBLK_L + i - 1)


def _rms_rows(x, g):
    return x * lax.rsqrt(jnp.mean(x * x, axis=-1, keepdims=True) + EPS) * g


NA_KROWS = 12
NA_KW = NA_KROWS * GW


NA_HG = 4
NA_GW = NA_HG * NA_D


def _na_kernel(tiles, q_ref, kl_ref, kc_ref, vl_ref, vc_ref, pair_ref, qn_ref, kn_ref, o_ref, ks_ref, bias_ref):
    i = pl.program_id(2)
    heads = [slice(NA_D * h, NA_D * (h + 1)) for h in range(NA_HG)]

    @pl.when(i == 0)
    def _():
        for hs in heads:
            ks_ref[0:L, hs] = _rms_rows(kc_ref[:, hs].astype(F32), kn_ref[...]).astype(BF16)
            ks_ref[L:, hs] = _rms_rows(kl_ref[:, hs].astype(F32), kn_ref[...]).astype(BF16)

    left, ok_l, ok_r = tiles
    lane = lax.broadcasted_iota(jnp.int32, (GW, 2 * GW), 1)
    for p, first_step in enumerate((1, 2, NBLK_L)):
        @pl.when(i == first_step)
        def _(p=p):
            for h in range(NA_HG):
                for a in range(BLK // GW):
                    for m in range(NA_KROWS // 2):
                        dst = (h, slice(GW * a, GW * (a + 1)), slice(2 * GW * m, 2 * GW * (m + 1)))
                        if not (ok_l[p, a, m] or ok_r[p, a, m]):
                            bias_ref[dst] = jnp.full((GW, 2 * GW), NEG, F32)
                            continue
                        t = pair_ref[h, int(left[p, a, m]) + 1]
                        if not ok_l[p, a, m]:
                            t = jnp.where(lane >= GW, t, NEG)
                        if not ok_r[p, a, m]:
                            t = jnp.where(lane < GW, t, NEG)
                        bias_ref[dst] = t

    def query(hs):
        return (_rms_rows(q_ref[:, hs].astype(F32), qn_ref[...]) * (NA_D ** -0.5)).astype(BF16)

    @pl.when(i == 0)
    def _():
        for hs in heads:
            s_c = _dot_nt(query(hs), ks_ref[0:L, hs])
            p = jnp.exp(s_c - jnp.max(s_c, axis=-1, keepdims=True))
            l = jnp.sum(p, axis=-1, keepdims=True)
            o_ref[:, hs] = (_dot(p.astype(BF16), vc_ref[:, hs]) / l).astype(o_ref.dtype)

    @pl.when(i > 0)
    def _():
        row0 = jnp.clip(4 * (i - 1) - 4, 0, GW - NA_KROWS)
        k0 = pl.multiple_of(row0 * GW, BLK)
        for h, hs in enumerate(heads):
            q = query(hs)
            s_c = _dot_nt(q, ks_ref[0:L, hs])
            s_l = _dot_nt(q, ks_ref[pl.ds(L + k0, NA_KW), hs]) + bias_ref[h]
            m = jnp.maximum(jnp.max(s_c, axis=-1, keepdims=True), jnp.max(s_l, axis=-1, keepdims=True))
            p_c = jnp.exp(s_c - m)
            p_l = jnp.exp(s_l - m)
            l = jnp.sum(p_c, axis=-1, keepdims=True) + jnp.sum(p_l, axis=-1, keepdims=True)
            o = _dot(p_l.astype(BF16), vl_ref[pl.ds(k0, NA_KW), hs]) + _dot(p_c.astype(BF16), vc_ref[:, hs])
            o_ref[:, hs] = (o / l).astype(o_ref.dtype)


def _na_bias_index():
    dr = np.zeros((3, BLK, NA_KW), np.int32)
    dc = np.zeros((3, BLK, NA_KW), np.int32)
    ok = np.zeros((3, BLK, NA_KW), bool)
    rows = S // GW
    for p, blk in enumerate((0, 5, NBLK_L - 1)):
        ks = int(np.clip(4 * blk - 4, 0, GW - NA_KROWS))
        r = 4 * blk + np.arange(BLK) // GW
        c = np.arange(BLK) % GW
        rk = ks + np.arange(NA_KW) // GW
        ck = np.arange(NA_KW) % GW
        r0 = np.clip(r - NA_R // 2, 0, rows - NA_R)
        ws = np.clip(c - NA_C // 2, 0, GW - NA_C)
        row_ok = (rk[None, :] >= r0[:, None]) & (rk[None, :] < r0[:, None] + NA_R)
        col_ok = (ck[None, :] >= ws[:, None]) & (ck[None, :] < ws[:, None] + NA_C)
        ok[p] = row_ok & col_ok
        dr[p] = np.clip(rk[None, :] - r[:, None] + NA_R - 1, 0, 2 * NA_R - 2)
        dc[p] = np.clip(ck[None, :] - c[:, None], -(NA_C - 1), NA_C - 1) + NA_C - 1
    return dr, dc, ok


NA_NDR = 2 * NA_R


def _na_bias_tiles():
    dr, _, ok = _na_bias_index()
    qr, kr = BLK // GW, NA_KROWS
    dr_t = dr.reshape(3, qr, GW, kr, GW)[:, :, 0, :, 0]
    ok_t = ok.reshape(3, qr, GW, kr, GW).any(axis=(2, 4))
    rk0 = np.array([int(np.clip(4 * blk - 4, 0, GW - NA_KROWS)) for blk in (0, 5, NBLK_L - 1)])
    r = np.array([4 * blk for blk in (0, 5, NBLK_L - 1)])[:, None, None] + np.arange(qr)[None, :, None]
    left = (rk0[:, None, None] + 2 * np.arange(kr // 2)[None, None, :]) - r + NA_R - 1
    assert np.all(dr_t[:, :, 0::2][ok_t[:, :, 0::2]] == left[ok_t[:, :, 0::2]])
    return left, ok_t[:, :, 0::2], ok_t[:, :, 1::2]


def _na_bias_pairs(rpb):
    _, dc, ok = _na_bias_index()
    qr, kr = BLK // GW, NA_KROWS
    dc_t = dc[0].reshape(qr, GW, kr, GW)[0, :, 0, :].reshape(GW * GW)
    col_ok = ok.reshape(3, qr, GW, kr, GW).any(axis=(0, 1, 3))
    oh_c = jnp.asarray(np.eye(2 * NA_C - 1, dtype=np.float32)[:, dc_t])
    t1 = jnp.einsum('hrd,dx->hrx', rpb.astype(F32), oh_c, precision=lax.Precision.HIGHEST)
    t1 = jnp.where(col_ok.reshape(1, 1, GW * GW), t1, NEG).reshape(NA_H, 2 * NA_R - 1, GW, GW)
    t1 = jnp.pad(t1, ((0, 0), (1, 1), (0, 0), (0, 0)))
    return jnp.concatenate([t1[:, :-1], t1[:, 1:]], axis=-1)


def na_attention(proj, rpb, qn, kn):
    pairs = _na_bias_pairs(rpb)

    cq, ck, cv = _COL_NAQ // NA_GW, _COL_NAK // NA_GW, _COL_NAV // NA_GW
    return pl.pallas_call(
        functools.partial(_na_kernel, _na_bias_tiles()),
        grid=(NB, NA_H // NA_HG, NBLK_L + 1),
        in_specs=[
            pl.BlockSpec((BLK, NA_GW), lambda b, h, i: (_qblk(b, i), cq + h)),
            pl.BlockSpec((S, NA_GW), lambda b, h, i: (b, ck + h)),
            pl.BlockSpec((L, NA_GW), lambda b, h, i: (CTX_BLK0 + b, ck + h)),
            pl.BlockSpec((S, NA_GW), lambda b, h, i: (b, cv + h)),
            pl.BlockSpec((L, NA_GW), lambda b, h, i: (CTX_BLK0 + b, cv + h)),
            pl.BlockSpec((NA_HG, NA_NDR, GW, 2 * GW), lambda b, h, i: (h, 0, 0, 0)),
            pl.BlockSpec((1, NA_D), lambda b, h, i: (0, 0)),
            pl.BlockSpec((1, NA_D), lambda b, h, i: (0, 0)),
        ],
        out_specs=pl.BlockSpec((BLK, NA_GW), lambda b, h, i: (_qblk(b, i), h)),
        out_shape=jax.ShapeDtypeStruct((NT, NA_W), BF16),
        scratch_shapes=[pltpu.VMEM((L + S, NA_GW), BF16), pltpu.VMEM((NA_HG, BLK, NA_KW), F32)],
        compiler_params=_cp(("arbitrary", "arbitrary", "arbitrary")),
        name="na_attention",
    )(proj, proj, proj, proj, proj, pairs, qn.reshape(1, NA_D), kn.reshape(1, NA_D))


MLA_HW = 256


def _half_mask(h, width=128):
    lane = lax.broadcasted_iota(jnp.int32, (1, width), 1)
    return (lane < 64) if h % 2 == 0 else (lane >= 64)


def _mla_q_kernel(cq_ref, w_ref, qa_ref, gn_ref, g2_ref, gs2_ref, c2_ref, s2_ref, q_ref):
    x = _rms_rows(cq_ref[...].astype(F32), qa_ref[...]).astype(BF16)
    y = _dot(x, w_ref[...])
    nw = MLA_H * MLA_NOPE
    gc = g2_ref[...] * c2_ref[...]
    gs = gs2_ref[...] * s2_ref[...]
    for p in range(MLA_H // 2):
        r1 = y[:, nw + 128 * p: nw + 128 * (p + 1)]
        r2 = y[:, nw + 768 + 128 * p: nw + 768 + 128 * (p + 1)]
        rot = r1 * gc + r2 * gs
        sq = r1 * r1
        for h in (2 * p, 2 * p + 1):
            msk = _half_mask(h)
            nope = y[:, 128 * h: 128 * (h + 1)]
            ss = jnp.sum(nope * nope, axis=-1, keepdims=True) + jnp.sum(
                jnp.where(msk, sq, 0.0), axis=-1, keepdims=True)
            inv = lax.rsqrt(ss / MLA_QK + EPS) * (MLA_QK ** -0.5)
            q_ref[h, :, 0:128] = (nope * gn_ref[...] * inv).astype(BF16)
            q_ref[h, :, 128:256] = (jnp.where(msk, rot, 0.0) * inv).astype(BF16)


def _mla_kv_kernel(ckv_ref, kr_ref, w_ref, kva_ref, gn_ref, g2_ref, gs2_ref, c2_ref, s2_ref,
                   k_ref, v_ref):
    x = _rms_rows(ckv_ref[...].astype(F32), kva_ref[...]).astype(BF16)
    y = _dot(x, w_ref[...])
    kr = kr_ref[...].astype(F32)
    r1 = kr[:, 0:128]
    r2 = kr[:, 128:256]
    rot = r1 * (g2_ref[...] * c2_ref[...]) + r2 * (gs2_ref[...] * s2_ref[...])
    ss_r = jnp.sum(jnp.where(_half_mask(0), r1 * r1, 0.0), axis=-1, keepdims=True)
    nw = MLA_H * MLA_NOPE
    lane0 = lax.broadcasted_iota(jnp.int32, (kr.shape[0], 128), 1) == 0
    for h in range(MLA_H):
        nope = y[:, 128 * h: 128 * (h + 1)]
        ss = jnp.sum(nope * nope, axis=-1, keepdims=True) + ss_r
        inv = lax.rsqrt(ss / MLA_QK + EPS)
        k_ref[h, :, 0:128] = (nope * gn_ref[...] * inv).astype(BF16)
        k_ref[h, :, 128:256] = (jnp.where(_half_mask(h), rot, 0.0) * inv).astype(BF16)
        v_ref[h, :, 0:128] = y[:, nw + 128 * h: nw + 128 * (h + 1)].astype(BF16)
        v_ref[h, :, 128:256] = jnp.where(lane0, 1.0, 0.0).astype(BF16)


def _rope_tables():
    t = jnp.arange(S, dtype=jnp.int32)
    pos = (t // GW, t % GW)
    nf = MLA_ROPE // 4
    inv = ROPE_BASE ** (-jnp.arange(nf, dtype=F32) / nf)
    cs, sn = [], []
    for ax in range(2):
        ang = pos[ax].astype(F32)[:, None] * inv[None, :]
        c, s = jnp.cos(ang), jnp.sin(ang)
        cs += [c, c]
        sn += [-s, s]
    c64 = jnp.tile(jnp.concatenate(cs, axis=1), (NB, 1))
    s64 = jnp.tile(jnp.concatenate(sn, axis=1), (NB, 1))
    c64 = jnp.concatenate([c64, jnp.ones((NB * L, MLA_ROPE), F32)], axis=0)
    s64 = jnp.concatenate([s64, jnp.zeros((NB * L, MLA_ROPE), F32)], axis=0)
    return jnp.tile(c64, (1, 2)), jnp.tile(s64, (1, 2))


_ROPE_SWAP = np.concatenate([np.arange(16, 32), np.arange(0, 16), np.arange(48, 64), np.arange(32, 48)])


def _rope_gains(g):
    gr = g[MLA_NOPE:]
    return (g[:MLA_NOPE].reshape(1, 128), jnp.tile(gr, 2).reshape(1, 128),
            jnp.tile(gr[_ROPE_SWAP], 2).reshape(1, 128))


def mla_q_prep(proj, w_uq_r, qa_g, qn_g, c2, s2):
    tm = 512
    gn, g2, gs2 = _rope_gains(qn_g)
    vec = lambda w: pl.BlockSpec((1, w), lambda i: (0, 0))
    return pl.pallas_call(
        _mla_q_kernel,
        grid=(NT // tm,),
        in_specs=[
            pl.BlockSpec((tm, MLA_QL), lambda i: (i, _COL_CQ // MLA_QL)),
            pl.BlockSpec((MLA_QL, 3072), lambda i: (0, 0)),
            vec(MLA_QL), vec(128), vec(128), vec(128),
            pl.BlockSpec((tm, 128), lambda i: (i, 0)),
            pl.BlockSpec((tm, 128), lambda i: (i, 0)),
        ],
        out_specs=pl.BlockSpec((MLA_H, tm, MLA_HW), lambda i: (0, i, 0)),
        out_shape=jax.ShapeDtypeStruct((MLA_H, NT, MLA_HW), BF16),
        compiler_params=_cp(("arbitrary",)),
        name="mla_q_prep",
    )(proj, w_uq_r, qa_g.reshape(1, MLA_QL), gn, g2, gs2, c2, s2)


def mla_kv_prep(proj, w_ukv_r, kva_g, kn_g, c2, s2):
    tm = 512
    gn, g2, gs2 = _rope_gains(kn_g)
    vec = lambda w: pl.BlockSpec((1, w), lambda i: (0, 0))
    return pl.pallas_call(
        _mla_kv_kernel,
        grid=(NT // tm,),
        in_specs=[
            pl.BlockSpec((tm, MLA_KVL), lambda i: (i, _COL_CKV // MLA_KVL)),
            pl.BlockSpec((tm, 256), lambda i: (i, _COL_KR // 256)),
            pl.BlockSpec((MLA_KVL, 3072), lambda i: (0, 0)),
            vec(MLA_KVL), vec(128), vec(128), vec(128),
            pl.BlockSpec((tm, 128), lambda i: (i, 0)),
            pl.BlockSpec((tm, 128), lambda i: (i, 0)),
        ],
        out_specs=[pl.BlockSpec((MLA_H, tm, MLA_HW), lambda i: (0, i, 0)),
                   pl.BlockSpec((MLA_H, tm, MLA_HW), lambda i: (0, i, 0))],
        out_shape=[jax.ShapeDtypeStruct((MLA_H, NT, MLA_HW), BF16),
                   jax.ShapeDtypeStruct((MLA_H, NT, MLA_HW), BF16)],
        compiler_params=_cp(("arbitrary",)),
        name="mla_kv_prep",
    )(proj, proj, w_ukv_r, kva_g.reshape(1, MLA_KVL), gn, g2, gs2, c2, s2)


MLA_TQ = 256


def _mla_attn_kernel(q_ref, kl_ref, kc_ref, vl_ref, vc_ref, o_ref, s0_ref, s1_ref, p0_ref, p1_ref):
    g = pl.program_id(0)

    @pl.when(g == 0)
    def _():
        s0_ref[...] = jnp.zeros_like(s0_ref)
        s1_ref[...] = jnp.zeros_like(s1_ref)
        p0_ref[...] = jnp.ones_like(p0_ref)
        p1_ref[...] = jnp.ones_like(p1_ref)

    n_kc = (L + S) // L
    rows = MLA_TQ // (n_kc - 1)

    def tie(x, dep):
        if dep is None:
            return x
        return jnp.concatenate([x[0:16, :] + dep, x[16:, :]], axis=0)

    def stages(s_new, s_old, p_new, p_old):
        acc, dep = None, None
        for c in range(n_kc):
            keys = slice(L * c, L * (c + 1))
            k = kc_ref[...] if c == 0 else kl_ref[L * (c - 1): L * c, :]
            v = vc_ref[...] if c == 0 else vl_ref[L * (c - 1): L * c, :]
            part = _dot(p_new[:, keys], v)
            acc = part if acc is None else acc + part
            s_new[:, keys] = _dot_nt(tie(q_ref[...], dep), k)
            if c < n_kc - 1:
                r = slice(rows * c, rows * (c + 1))
                s = s_old[r, :]
                p = jnp.exp(s - jnp.max(s, axis=-1, keepdims=True)).astype(BF16)
                p_old[r, :] = p
                dep = (p[0:16, 0:MLA_HW].astype(F32) * 0.0).astype(BF16)
        o_ref[...] = (acc[:, :MLA_V] / acc[:, MLA_V:MLA_V + 1]).astype(o_ref.dtype)

    @pl.when(g % 2 == 0)
    def _():
        stages(s0_ref, s1_ref, p0_ref, p1_ref)

    @pl.when(g % 2 == 1)
    def _():
        stages(s1_ref, s0_ref, p1_ref, p0_ref)


def _mla_ctx_kernel(q_ref, kc_ref, vc_ref, o_ref):
    s = _dot_nt(q_ref[...], kc_ref[...])
    p = jnp.exp(s - jnp.max(s, axis=-1, keepdims=True)).astype(BF16)
    acc = _dot(p, vc_ref[...])
    o_ref[...] = (acc[:, :MLA_V] / acc[:, MLA_V:MLA_V + 1]).astype(o_ref.dtype)


def mla_attention(q, k, v, need_ctx):
    nq = S // MLA_TQ
    n_steps = NB * MLA_H * nq

    def bhi(g):
        return g // (MLA_H * nq), (g // nq) % MLA_H, g % nq

    def cur(g):
        return bhi(jnp.minimum(g, n_steps - 1))

    def prev(g):
        return bhi(jnp.maximum(g - 2, 0))

    ob = pl.pallas_call(
        _mla_attn_kernel,
        grid=(n_steps + 2,),
        in_specs=[
            pl.BlockSpec((None, MLA_TQ, MLA_HW), lambda g: (cur(g)[1], cur(g)[0] * nq + cur(g)[2], 0)),
            pl.BlockSpec((None, S, MLA_HW), lambda g: (cur(g)[1], cur(g)[0], 0)),
            pl.BlockSpec((None, L, MLA_HW), lambda g: (cur(g)[1], CTX_BLK0 + cur(g)[0], 0)),
            pl.BlockSpec((None, S, MLA_HW), lambda g: (prev(g)[1], prev(g)[0], 0)),
            pl.BlockSpec((None, L, MLA_HW), lambda g: (prev(g)[1], CTX_BLK0 + prev(g)[0], 0)),
        ],
        out_specs=pl.BlockSpec((MLA_TQ, MLA_V), lambda g: (prev(g)[0] * nq + prev(g)[2], prev(g)[1])),
        out_shape=jax.ShapeDtypeStruct((NL, MLA_W), BF16),
        scratch_shapes=[pltpu.VMEM((MLA_TQ, L + S), F32), pltpu.VMEM((MLA_TQ, L + S), F32),
                        pltpu.VMEM((MLA_TQ, L + S), BF16), pltpu.VMEM((MLA_TQ, L + S), BF16)],
        compiler_params=_cp(("arbitrary",)),
        name="mla_attention",
    )(q, k, k, v, v)
    if not need_ctx:
        return ob, None
    ob_ctx = pl.pallas_call(
        _mla_ctx_kernel,
        grid=(NB, MLA_H),
        in_specs=[
            pl.BlockSpec((None, L, MLA_HW), lambda b, h: (h, CTX_BLK0 + b, 0)),
            pl.BlockSpec((None, L, MLA_HW), lambda b, h: (h, CTX_BLK0 + b, 0)),
            pl.BlockSpec((None, L, MLA_HW), lambda b, h: (h, CTX_BLK0 + b, 0)),
        ],
        out_specs=pl.BlockSpec((L, MLA_V), lambda b, h: (b, h)),
        out_shape=jax.ShapeDtypeStruct((NB * L, MLA_W), BF16),
        compiler_params=_cp(("arbitrary", "arbitrary")),
        name="mla_attention_ctx",
    )(q, k, v)
    return ob, ob_ctx


N_CH_C = L // CH
N_CH_L = S // CH
N_CH = N_CH_C + N_CH_L


def _gla_kernel(qf_ref, kf_ref, vf_ref, af_ref, qb_ref, kb_ref, vb_ref, ab_ref,
                wg_ref, bg_ref, of_ref, ob_ref, st_ref):
    t = pl.program_id(1)

    @pl.when(t == 0)
    def _():
        st_ref[...] = jnp.zeros_like(st_ref)

    row = lax.broadcasted_iota(jnp.int32, (CH, CH), 0)
    col = lax.broadcasted_iota(jnp.int32, (CH, CH), 1)
    dirs = (
        (qf_ref, kf_ref, vf_ref, af_ref, of_ref, col <= row, CH - 1),
        (qb_ref, kb_ref, vb_ref, ab_ref, ob_ref, col >= row, 0),
    )
    for d, (q_ref, k_ref, v_ref, a_ref, o_ref, keep, last) in enumerate(dirs):
        z = _dot(a_ref[...], wg_ref[d]) + bg_ref[d]
        g = (jnp.minimum(z, 0.0) - jnp.log(1.0 + jnp.exp(-jnp.abs(z)))) / GLA_TAU
        b = jnp.dot(keep.astype(F32), g, preferred_element_type=F32,
                    precision=lax.Precision.HIGHEST)
        b_end = b[last:last + 1, :]
        b_mid = b[CH // 2:CH // 2 + 1, :]
        q = q_ref[...].astype(F32) * (GLA_DK ** -0.5)
        k = k_ref[...].astype(F32)
        qa = (q * jnp.exp(b - b_mid)).astype(BF16)
        ka = (k * jnp.exp(b_mid - b)).astype(BF16)
        qd = (q * jnp.exp(b)).astype(BF16)
        ke = (k * jnp.exp(b_end - b)).astype(BF16)
        e_end = jnp.exp(b_end)
        v = v_ref[...]
        for h in range(GLA_H):
            ksl = slice(GLA_DK * h, GLA_DK * (h + 1))
            vsl = slice(GLA_DV * h, GLA_DV * (h + 1))
            att = jnp.where(keep, _dot_nt(qa[:, ksl], ka[:, ksl]), 0.0)
            st = st_ref[d, h]
            o = _dot_nt(qd[:, ksl], st.astype(BF16)) + _dot(att.astype(BF16), v[:, vsl])
            o_ref[:, vsl] = o.astype(o_ref.dtype)
            st_ref[d, h] = st * e_end[:, ksl] + _dot_tn(v[:, vsl], ke[:, ksl])


def gla_scan(proj, wg_f, bg_f, wg_b, bg_b):
    wg = jnp.zeros((2, 128, GLA_KW), F32)
    wg = wg.at[0, 0:GLA_RANK].set(wg_f).at[1, GLA_RANK:2 * GLA_RANK].set(wg_b).astype(BF16)
    bg = jnp.stack([bg_f, bg_b]).reshape(2, 1, GLA_KW)

    ctx0 = NL // CH

    def fwd(b, t):
        return jnp.where(t < N_CH_C, ctx0 + N_CH_C * b + t, N_CH_L * b + t - N_CH_C)

    def bwd(b, t):
        return jnp.where(t < N_CH_C, ctx0 + N_CH_C * b + N_CH_C - 1 - t, N_CH_L * b + N_CH - 1 - t)

    def specs(rowfn):
        return [
            pl.BlockSpec((CH, GLA_KW), lambda b, t: (rowfn(b, t), _COL_GQ // GLA_KW)),
            pl.BlockSpec((CH, GLA_KW), lambda b, t: (rowfn(b, t), _COL_GK // GLA_KW)),
            pl.BlockSpec((CH, GLA_W), lambda b, t: (rowfn(b, t), _COL_GV // GLA_W)),
            pl.BlockSpec((CH, 128), lambda b, t: (rowfn(b, t), _COL_GA // 128)),
        ]

    return pl.pallas_call(
        _gla_kernel,
        grid=(NB, N_CH),
        in_specs=specs(fwd) + specs(bwd) + [
            pl.BlockSpec((2, 128, GLA_KW), lambda b, t: (0, 0, 0)),
            pl.BlockSpec((2, 1, GLA_KW), lambda b, t: (0, 0, 0)),
        ],
        out_specs=[pl.BlockSpec((CH, GLA_W), lambda b, t: (fwd(b, t), 0)),
                   pl.BlockSpec((CH, GLA_W), lambda b, t: (bwd(b, t), 0))],
        out_shape=[jax.ShapeDtypeStruct((NT, GLA_W), BF16)] * 2,
        scratch_shapes=[pltpu.VMEM((2, GLA_H, GLA_DV, GLA_DK), F32)],
        compiler_params=_cp(("arbitrary", "arbitrary")),
        name="gla_scan",
    )(proj, proj, proj, proj, proj, proj, proj, proj, wg, bg)


def _gla_out_kernel(of_ref, ob_ref, g_ref, gn_ref, o_ref):
    for h in range(GLA_H):
        sl = slice(GLA_DV * h, GLA_DV * (h + 1))
        o = _rms_rows(of_ref[:, sl].astype(F32) + ob_ref[:, sl].astype(F32), gn_ref[...])
        gate = g_ref[:, sl].astype(F32)
        o_ref[:, sl] = (o * (gate * _sigmoid(gate))).astype(o_ref.dtype)


def gla_output(o_f, o_b, proj, on_g):
    tm = 512
    return pl.pallas_call(
        _gla_out_kernel,
        grid=(NT // tm,),
        in_specs=[
            pl.BlockSpec((tm, GLA_W), lambda i: (i, 0)),
            pl.BlockSpec((tm, GLA_W), lambda i: (i, 0)),
            pl.BlockSpec((tm, GLA_W), lambda i: (i, _COL_GG // GLA_W)),
            pl.BlockSpec((1, GLA_DV), lambda i: (0, 0)),
        ],
        out_specs=pl.BlockSpec((tm, GLA_W), lambda i: (i, 0)),
        out_shape=jax.ShapeDtypeStruct((NT, GLA_W), BF16),
        compiler_params=_cp(("arbitrary",)),
        name="gla_output",
    )(o_f, o_b, proj, on_g.reshape(1, GLA_DV))


OUT_TN = 1024


def _out_proj_kernel(with_ctx, layer, oa_ref, obl_ref, obc_ref, oc_ref, w_ref, xl_ref, xc_ref, g_ref, *rest):
    o_refs, (stage_ref, wb_ref, sem) = rest[:-3], rest[-3:]
    j, i = pl.program_id(0), pl.program_id(1)

    def tile(jj):
        cols = pl.ds(pl.multiple_of(jj * OUT_TN, OUT_TN), OUT_TN)
        return pltpu.make_async_copy(w_ref.at[layer, :, cols], stage_ref, sem)

    @pl.when(i == 0)
    def _():
        @pl.when(j == 0)
        def _():
            tile(0).start()

        tile(j).wait()
        wb_ref[...] = stage_ref[...].astype(BF16)

        @pl.when(j + 1 < D // OUT_TN)
        def _():
            tile(j + 1).start()

    a = jnp.concatenate([oa_ref[...], _stream_block(i, obl_ref, obc_ref), oc_ref[...]], axis=1)
    y = _stream_block(i, xl_ref, xc_ref) + g_ref[...] * _dot(a, wb_ref[...])
    if not with_ctx:
        o_refs[0][...] = y
        return

    @pl.when(i < NLB)
    def _():
        o_refs[0][...] = y

    @pl.when(i == NLB)
    def _():
        o_refs[1][...] = y


def out_projection(oa, ob, ob_ctx, oc, w_out, layer, xl, xc, mod, k_gate, with_ctx):
    tm, tn = CTX_ROWS, OUT_TN
    nj = D // tn
    lat = lambda j, i: (jnp.minimum(i, NLB - 1), j)
    out_specs = [pl.BlockSpec((tm, tn), lat)]
    out_shape = [jax.ShapeDtypeStruct((NL, D), F32)]
    if with_ctx:
        out_specs.append(pl.BlockSpec((tm, tn), lambda j, i: (0, j)))
        out_shape.append(jax.ShapeDtypeStruct((CTX_ROWS, D), F32))
    res = pl.pallas_call(
        functools.partial(_out_proj_kernel, with_ctx, layer),
        grid=(nj, NLB + (1 if with_ctx else 0)),
        in_specs=[
            pl.BlockSpec((tm, NA_W), lambda j, i: (i, 0)),
            pl.BlockSpec((tm, MLA_W), lambda j, i: (jnp.minimum(i, NLB - 1), 0)),
            pl.BlockSpec((tm, MLA_W), lambda j, i: (0, 0)),
            pl.BlockSpec((tm, GLA_W), lambda j, i: (i, 0)),
            pl.BlockSpec(memory_space=pl.ANY),
            pl.BlockSpec((tm, tn), lat),
            pl.BlockSpec((tm, tn), lambda j, i: (0, j)),
            pl.BlockSpec((None, 1, tn), lambda j, i: (_row_group(i, tm), 0, k_gate * nj + j)),
        ],
        out_specs=out_specs,
        out_shape=out_shape,
        scratch_shapes=[pltpu.VMEM((D, tn), F32), pltpu.VMEM((D, tn), BF16), pltpu.SemaphoreType.DMA(())],
        compiler_params=_cp(("arbitrary", "arbitrary")),
        name="out_projection",
    )(oa, ob, ob_ctx if with_ctx else ob, oc, w_out, xl, xc, mod)
    return (res[0], res[1]) if with_ctx else (res[0], None)


def _route(logits, router_bias):
    n = logits.shape[0]
    scores = jax.nn.sigmoid(logits)
    grouped = (scores + router_bias.astype(F32)).reshape(n, E_GROUPS, E_PER)

    def top2(a):
        idx = lax.broadcasted_iota(jnp.int32, a.shape, a.ndim - 1)
        i1 = jnp.argmax(a, axis=-1).astype(jnp.int32)
        rest = jnp.where(idx == i1[..., None], -jnp.inf, a)
        i2 = jnp.argmax(rest, axis=-1).astype(jnp.int32)
        return jnp.max(a, axis=-1), jnp.max(rest, axis=-1), i1, i2

    m1, m2, _, _ = top2(grouped)
    grp = jnp.argmax(m1 + m2, axis=-1).astype(jnp.int32)
    gsel = lax.broadcasted_iota(jnp.int32, (n, E_GROUPS, E_PER), 1) == grp[:, None, None]
    in_group = jnp.sum(jnp.where(gsel, grouped, 0.0), axis=1)
    _, _, l1, l2 = top2(in_group)
    expert_idx = grp[:, None] * E_PER + jnp.stack([l1, l2], axis=-1)
    esel = lax.broadcasted_iota(jnp.int32, (n, TOPK, E), 2) == expert_idx[:, :, None]
    w = jnp.sum(jnp.where(esel, scores[:, None, :], 0.0), axis=-1)
    return expert_idx, w / jnp.sum(w, axis=-1, keepdims=True)


def _dispatch_plan(expert_idx):
    nk = expert_idx.shape[0] * TOPK
    nb = _moe_blocks(expert_idx.shape[0])
    flat_e = expert_idx.reshape(nk)
    onehot = (flat_e[:, None] == jnp.arange(E, dtype=jnp.int32)[None, :]).astype(jnp.int32)
    csum = jnp.cumsum(onehot, axis=0)
    counts = csum[-1]
    rank = jnp.take_along_axis(csum, flat_e[:, None], axis=1)[:, 0] - 1
    padded = (counts + MOE_R - 1) // MOE_R * MOE_R
    pad_end = jnp.cumsum(padded)
    pad_start = pad_end - padded
    dest = (pad_start[flat_e] + rank).astype(jnp.int32)
    row_token = jnp.zeros((nb * MOE_R,), jnp.int32).at[dest].set(jnp.arange(nk, dtype=jnp.int32) // TOPK)
    blk0 = jnp.arange(nb, dtype=jnp.int32) * MOE_R
    block_expert = jnp.minimum(jnp.sum((pad_end[None, :] <= blk0[:, None]).astype(jnp.int32), axis=1), E - 1)
    n_active = (pad_end[-1] // MOE_R).astype(jnp.int32).reshape(1)
    n_valid = jnp.clip(counts[block_expert] - (blk0 - pad_start[block_expert]), 0, MOE_R).astype(jnp.int32)
    bidx = jnp.arange(nb, dtype=jnp.int32)
    prev_e = jnp.concatenate([jnp.full((1,), -1, jnp.int32), block_expert[:-1]])
    first = jnp.logical_and(bidx < n_active[0], block_expert != prev_e)
    later = jnp.where(first, bidx, nb)
    nxt_blk = jnp.concatenate([lax.cummin(later[::-1])[::-1][1:], jnp.full((1,), nb, jnp.int32)])
    nxt_e = jnp.where(nxt_blk < nb, block_expert[jnp.minimum(nxt_blk, nb - 1)], -1).astype(jnp.int32)
    plan = (block_expert, n_active, first.astype(jnp.int32), nxt_e)
    return dest, row_token, n_valid, plan


DH = D // 2


DMA_UNROLL = 8


def _for_rows(n, body):
    full = n // DMA_UNROLL

    def group(t, c):
        for u in range(DMA_UNROLL):
            body(t * DMA_UNROLL + u)
        return c

    def single(r, c):
        body(r)
        return c

    lax.fori_loop(0, full, group, 0)
    lax.fori_loop(full * DMA_UNROLL, n, single, 0)


def _gather_rows_kernel(tok_ref, nv_ref, h_ref, o_ref, sem):
    i = pl.program_id(0)
    base = i * MOE_R
    nv = nv_ref[i]

    @pl.when(nv < MOE_R)
    def _():
        o_ref[...] = jnp.zeros_like(o_ref)

    def copy(r):
        return pltpu.make_async_copy(h_ref.at[pl.ds(tok_ref[base + r], 1)], o_ref.at[pl.ds(r, 1)], sem)

    _for_rows(nv, lambda r: copy(r).start())
    _for_rows(nv, lambda r: copy(r).wait())


def gather_rows(row_token, n_valid, h):
    nb = n_valid.shape[0]
    return pl.pallas_call(
        _gather_rows_kernel,
        grid_spec=pltpu.PrefetchScalarGridSpec(
            num_scalar_prefetch=2,
            grid=(nb,),
            in_specs=[pl.BlockSpec(memory_space=pl.ANY)],
            out_specs=pl.BlockSpec((MOE_R, DH), lambda i, tok, nv: (i, 0)),
            scratch_shapes=[pltpu.SemaphoreType.DMA(())],
        ),
        out_shape=jax.ShapeDtypeStruct((nb * MOE_R, DH), jnp.uint32),
        compiler_params=_cp(("arbitrary",)),
        name="moe_gather_rows",
    )(row_token, n_valid, h)


def _stage_weights(layer, n_tiles, width, be_ref, first_ref, nxt_ref, active, mats):
    t = pl.program_id(0)
    i = pl.program_id(1)

    def copies(e, tt):
        cols = pl.ds(pl.multiple_of(tt * width, width), width)
        return [pltpu.make_async_copy(w.at[layer, e, :, cols], st, sm) for w, st, _, sm in mats]

    @pl.when(jnp.logical_and(active, first_ref[i] == 1))
    def _():
        @pl.when(jnp.logical_and(t == 0, i == 0))
        def _():
            for c in copies(be_ref[0], 0):
                c.start()

        for c in copies(be_ref[i], t):
            c.wait()
        for _, st, wb, _ in mats:
            wb[...] = st[...].astype(BF16)

        @pl.when(nxt_ref[i] >= 0)
        def _():
            for c in copies(nxt_ref[i], t):
                c.start()

        @pl.when(jnp.logical_and(nxt_ref[i] < 0, t + 1 < n_tiles))
        def _():
            for c in copies(be_ref[0], t + 1):
                c.start()


def _ffn_up_kernel(layer, be_ref, na_ref, first_ref, nxt_ref, x_ref, w1_ref, w3_ref, o_ref,
                   st1_ref, st3_ref, w1b_ref, w3b_ref, sem):
    i = pl.program_id(1)
    active = i < na_ref[0]
    _stage_weights(layer, FF // MOE_TF, MOE_TF, be_ref, first_ref, nxt_ref, active,
                   [(w1_ref, st1_ref, w1b_ref, sem.at[0]), (w3_ref, st3_ref, w3b_ref, sem.at[1])])

    @pl.when(active)
    def _():
        lo, hi = _unpack_pairs(x_ref[...])
        lo, hi = lo.astype(BF16), hi.astype(BF16)
        a = _dot(lo, w1b_ref[0:DH, :]) + _dot(hi, w1b_ref[DH:, :])
        b = _dot(lo, w3b_ref[0:DH, :]) + _dot(hi, w3b_ref[DH:, :])
        o_ref[...] = (a * _sigmoid(a) * b).astype(o_ref.dtype)

    @pl.when(jnp.logical_not(active))
    def _():
        o_ref[...] = jnp.zeros_like(o_ref)


def ffn_up(plan, xb, w1, w3, layer):
    nb = plan[0].shape[0]
    last = lambda i, na: jnp.minimum(i, na[0] - 1)
    return pl.pallas_call(
        functools.partial(_ffn_up_kernel, layer),
        grid_spec=pltpu.PrefetchScalarGridSpec(
            num_scalar_prefetch=4,
            grid=(FF // MOE_TF, nb),
            in_specs=[
                pl.BlockSpec((MOE_R, DH), lambda j, i, be, na, fi, nx: (last(i, na), 0)),
                pl.BlockSpec(memory_space=pl.ANY),
                pl.BlockSpec(memory_space=pl.ANY),
            ],
            out_specs=pl.BlockSpec((MOE_R, MOE_TF), lambda j, i, be, na, fi, nx: (i, j)),
            scratch_shapes=[pltpu.VMEM((D, MOE_TF), F32), pltpu.VMEM((D, MOE_TF), F32),
                            pltpu.VMEM((D, MOE_TF), BF16), pltpu.VMEM((D, MOE_TF), BF16),
                            pltpu.SemaphoreType.DMA((2,))],
        ),
        out_shape=jax.ShapeDtypeStruct((nb * MOE_R, FF), BF16),
        compiler_params=_cp(("arbitrary", "arbitrary")),
        name="moe_ffn_up",
    )(*plan, xb, w1, w3)


MOE_TN = 4096


def _ffn_down_kernel(layer, be_ref, na_ref, first_ref, nxt_ref, h_ref, w2_ref, o_ref, st2_ref, w2b_ref, sem):
    i = pl.program_id(1)
    active = i < na_ref[0]
    _stage_weights(layer, D // MOE_TN, MOE_TN, be_ref, first_ref, nxt_ref, active,
                   [(w2_ref, st2_ref, w2b_ref, sem.at[0])])

    @pl.when(active)
    def _():
        y = _dot(h_ref[...], w2b_ref[...])
        o_ref[...] = _pack_pairs(y[:, :MOE_TN // 2], y[:, MOE_TN // 2:])

    @pl.when(jnp.logical_not(active))
    def _():
        o_ref[...] = jnp.zeros_like(o_ref)


def ffn_down(plan, hmid, w2, layer):
    nb = plan[0].shape[0]
    last = lambda i, na: jnp.minimum(i, na[0] - 1)
    return pl.pallas_call(
        functools.partial(_ffn_down_kernel, layer),
        grid_spec=pltpu.PrefetchScalarGridSpec(
            num_scalar_prefetch=4,
            grid=(D // MOE_TN, nb),
            in_specs=[
                pl.BlockSpec((MOE_R, FF), lambda n, i, be, na, fi, nx: (last(i, na), 0)),
                pl.BlockSpec(memory_space=pl.ANY),
            ],
            out_specs=pl.BlockSpec((MOE_R, MOE_TN // 2), lambda n, i, be, na, fi, nx: (i, n)),
            scratch_shapes=[pltpu.VMEM((FF, MOE_TN), F32), pltpu.VMEM((FF, MOE_TN), BF16),
                            pltpu.SemaphoreType.DMA((1,))],
        ),
        out_shape=jax.ShapeDtypeStruct((nb * MOE_R, DH), jnp.uint32),
        compiler_params=_cp(("arbitrary", "arbitrary")),
        name="moe_ffn_down",
    )(*plan, hmid, w2)


CMB_T = 256
CMB_RC = 16


def _combine_kernel(with_ctx, pos_ref, yb_ref, gate_ref, xl_ref, xc_ref, g_ref, *rest):
    o_refs, (buf_ref, sem) = rest[:-2], rest[-2:]
    i = pl.program_id(0)
    n_lat = NL // CMB_T
    base = i * CMB_T

    def copy(r, k):
        p = pos_ref[(base + r) * TOPK + k]
        return pltpu.make_async_copy(yb_ref.at[pl.ds(p, 1)], buf_ref.at[k, pl.ds(r, 1)], sem.at[k])

    def start(r, c):
        copy(r, 0).start()
        copy(r, 1).start()
        return c

    def wait(r, c):
        copy(r, 0).wait()
        copy(r, 1).wait()
        return c

    lax.fori_loop(0, CMB_T, start, 0, unroll=DMA_UNROLL)
    lax.fori_loop(0, CMB_T, wait, 0, unroll=DMA_UNROLL)
    half = MOE_TN // 2

    def emit(x_ref, o_ref):
        def rows(t, c):
            r = pl.ds(pl.multiple_of(t * CMB_RC, CMB_RC), CMB_RC)
            g0, g1 = gate_ref[r, 0:1], gate_ref[r, 1:2]
            for n in range(D // MOE_TN):
                lo0, hi0 = _unpack_pairs(buf_ref[0, r, half * n: half * (n + 1)])
                lo1, hi1 = _unpack_pairs(buf_ref[1, r, half * n: half * (n + 1)])
                for part, y in enumerate((lo0 * g0 + lo1 * g1, hi0 * g0 + hi1 * g1)):
                    sl = slice(MOE_TN * n + half * part, MOE_TN * n + half * (part + 1))
                    o_ref[r, sl] = x_ref[r, sl] + g_ref[:, sl] * y
            return c

        lax.fori_loop(0, CMB_T // CMB_RC, rows, 0)

    if not with_ctx:
        emit(xl_ref, o_refs[0])
        return

    @pl.when(i < n_lat)
    def _():
        emit(xl_ref, o_refs[0])

    @pl.when(i >= n_lat)
    def _():
        emit(xc_ref, o_refs[1])


def moe_combine(dest, yb, gates, xl, xc, mod, k_gate, with_ctx):
    n_lat = NL // CMB_T
    lat = lambda i, pos: (jnp.minimum(i, n_lat - 1), 0)
    ctx = lambda i, pos: (jnp.maximum(i - n_lat, 0), 0)
    out_specs = [pl.BlockSpec((CMB_T, D), lat)]
    out_shape = [jax.ShapeDtypeStruct((NL, D), F32)]
    if with_ctx:
        out_specs.append(pl.BlockSpec((CMB_T, D), ctx))
        out_shape.append(jax.ShapeDtypeStruct((CTX_ROWS, D), F32))
    res = pl.pallas_call(
        functools.partial(_combine_kernel, with_ctx),
        grid_spec=pltpu.PrefetchScalarGridSpec(
            num_scalar_prefetch=1,
            grid=((NT if with_ctx else NL) // CMB_T,),
            in_specs=[
                pl.BlockSpec(memory_space=pl.ANY),
                pl.BlockSpec((CMB_T, TOPK), lambda i, pos: (i, 0)),
                pl.BlockSpec((CMB_T, D), lat),
                pl.BlockSpec((CMB_T, D), ctx),
                pl.BlockSpec((None, 1, D), lambda i, pos: (_row_group(i, CMB_T), 0, k_gate)),
            ],
            out_specs=out_specs,
            scratch_shapes=[pltpu.VMEM((TOPK, CMB_T, DH), jnp.uint32), pltpu.SemaphoreType.DMA((TOPK,))],
        ),
        out_shape=out_shape,
        compiler_params=_cp(("arbitrary",)),
        name="moe_combine",
    )(dest, yb, gates, xl, xc, mod)
    return (res[0], res[1]) if with_ctx else (res[0], None)


IN_W = 9312


W_RB = 512
W_NB = PW // W_RB
_W_SRC_KR = 4608
_W_SRC_GA = 9280


def _w_in_layout_kernel(w_ref, o_ref, buf_ref, sem):
    g = pl.program_id(0)

    def run(step, slot, op):
        l, k = step // W_NB, step % W_NB

        @pl.when(k < W_NB - 1)
        def _():
            r0 = pl.multiple_of(jnp.where(k < _W_SRC_KR // W_RB, k * W_RB, k * W_RB + MLA_ROPE), MLA_ROPE)
            op(pltpu.make_async_copy(w_ref.at[l, pl.ds(r0, W_RB)], buf_ref.at[slot], sem.at[slot]))

        @pl.when(k == W_NB - 1)
        def _():
            op(pltpu.make_async_copy(w_ref.at[l, pl.ds(_W_SRC_KR, MLA_ROPE)],
                                     buf_ref.at[slot, pl.ds(0, MLA_ROPE)], sem.at[slot]))
            op(pltpu.make_async_copy(w_ref.at[l, pl.ds(_W_SRC_GA, 2 * GLA_RANK)],
                                     buf_ref.at[slot, pl.ds(MLA_ROPE, 2 * GLA_RANK)], sem.at[slot]))

    start = lambda c: c.start()
    wait = lambda c: c.wait()

    @pl.when(g == 0)
    def _():
        run(0, 0, start)

    @pl.when(g + 1 < DEPTH * W_NB)
    def _():
        run(g + 1, (g + 1) % 2, start)

    run(g, g % 2, wait)
    k = g % W_NB

    @pl.when(k < W_NB - 1)
    def _():
        o_ref[...] = buf_ref[g % 2].astype(BF16)

    @pl.when(k == W_NB - 1)
    def _():
        x = buf_ref[g % 2, 0:MLA_ROPE, :]
        xs = jnp.concatenate([x[16:32], x[0:16], x[48:64], x[32:48]], axis=0)
        ga = buf_ref[g % 2, MLA_ROPE:MLA_ROPE + 2 * GLA_RANK, :]
        o_ref[...] = jnp.concatenate(
            [x, x, xs, xs, ga, jnp.zeros((W_RB - 4 * MLA_ROPE - 2 * GLA_RANK, D), F32)], axis=0).astype(BF16)


def _w_in_layout(w):
    w_t = jnp.swapaxes(w, 1, 2)
    return pl.pallas_call(
        _w_in_layout_kernel,
        grid=(DEPTH * W_NB,),
        in_specs=[pl.BlockSpec(memory_space=pl.ANY)],
        out_specs=pl.BlockSpec((None, W_RB, D), lambda g: (g // W_NB, g % W_NB, 0)),
        out_shape=jax.ShapeDtypeStruct((DEPTH, PW, D), BF16),
        scratch_shapes=[pltpu.VMEM((2, W_RB, D), F32), pltpu.SemaphoreType.DMA((2,))],
        compiler_params=_cp(("arbitrary",)),
        name="w_in_layout",
    )(w_t)


def _w_uq_layout(w):
    w3 = w.reshape(MLA_QL, MLA_H, MLA_QK)
    rope = w3[:, :, MLA_NOPE:]
    rope_s = jnp.concatenate([rope[..., 16:32], rope[..., 0:16], rope[..., 48:64], rope[..., 32:48]], axis=-1)
    parts = [w3[:, :, :MLA_NOPE], rope, rope_s]
    return jnp.concatenate([p.reshape(MLA_QL, -1) for p in parts], axis=1).astype(BF16)


def _w_ukv_layout(w):
    w3 = w.reshape(MLA_KVL, MLA_H, MLA_NOPE + MLA_V)
    parts = [w3[:, :, :MLA_NOPE], w3[:, :, MLA_NOPE:]]
    return jnp.concatenate([p.reshape(MLA_KVL, -1) for p in parts], axis=1).astype(BF16)


def token_mixing_layer(xl, xc, mod, layer, need_ctx, norm1, w_in_b, w_out, na_qn, na_kn, na_rpb, mla_qa, mla_kva,
                       w_uq, w_ukv, mla_qn, mla_kn, gwf, gbf, gwb, gbb, gla_on, c2, s2):
    h = norm_modulate(xl, xc, norm1.reshape(1, D), mod, 0, 1, BF16)
    proj = matmul(h, w_in_b, layer, NT // 4, 512, BF16)
    oa = na_attention(proj, na_rpb, na_qn, na_kn)
    q = mla_q_prep(proj, _w_uq_layout(w_uq), mla_qa, mla_qn, c2, s2)
    k, v = mla_kv_prep(proj, _w_ukv_layout(w_ukv), mla_kva, mla_kn, c2, s2)
    ob, ob_ctx = mla_attention(q, k, v, need_ctx)
    o_f, o_b = gla_scan(proj, gwf, gbf, gwb, gbb)
    oc = gla_output(o_f, o_b, proj, gla_on)
    return out_projection(oa, ob, ob_ctx, oc, w_out, layer, xl, xc, mod, 2, need_ctx)


def moe_layer(xl, xc, mod, norm2, w_router_p, router_bias, w1, w3, w2, layer, need_ctx):
    h, logits = norm_modulate(xl, xc, norm2.reshape(1, D), mod, 3, 4, None, w_router_p,
                              NT if need_ctx else NL)
    expert_idx, gates = _route(logits[:, :E], router_bias)
    dest, row_token, n_valid, plan = _dispatch_plan(expert_idx)
    xb = gather_rows(row_token, n_valid, h)
    hmid = ffn_up(plan, xb, w1, w3, layer)
    yb = ffn_down(plan, hmid, w2, layer)
    return moe_combine(dest, yb, gates, xl, xc, mod, 5, need_ctx)


def kernel(x, c, ctx, c_ctx, w_ada, b_ada, norm1, norm2, w_in, w_out, na_q_norm, na_k_norm, na_rpb, mla_qa_norm, mla_kva_norm, mla_w_uq, mla_w_ukv, mla_q_norm, mla_k_norm, gla_w_gate_f, gla_b_gate_f, gla_w_gate_b, gla_b_gate_b, gla_out_norm, w_router, router_bias, moe_w1, moe_w3, moe_w2):
    cond = jnp.concatenate([c, c_ctx[None, :], jnp.zeros((8 - NB - 1, D), F32)], axis=0)
    mods = ada_modulation(cond, w_ada, b_ada)
    xl, xc = x.reshape(NL, D), ctx.reshape(CTX_ROWS, D)
    c2, s2 = _rope_tables()
    w_router_p = jnp.concatenate([w_router, jnp.zeros((D, 128 - E), F32)], axis=1)
    w_in_b = _w_in_layout(w_in)
    for l in range(DEPTH):
        mod = mods[l].reshape(8, 1, 6 * D)
        need_ctx = l < DEPTH - 1
        xl, xc_new = token_mixing_layer(xl, xc, mod, l, need_ctx, norm1[l], w_in_b, w_out, na_q_norm[l],
                                        na_k_norm[l], na_rpb[l], mla_qa_norm[l], mla_kva_norm[l], mla_w_uq[l],
                                        mla_w_ukv[l], mla_q_norm[l], mla_k_norm[l], gla_w_gate_f[l],
                                        gla_b_gate_f[l], gla_w_gate_b[l], gla_b_gate_b[l], gla_out_norm[l], c2, s2)
        xc = xc_new if need_ctx else xc
        xl, xc_new = moe_layer(xl, xc, mod, norm2[l], w_router_p, router_bias, moe_w1, moe_w3, moe_w2, l, need_ctx)
        xc = xc_new if need_ctx else xc
    return xl.reshape(NB, S, D)
```
